```python
import math
import jax, jax.numpy as jnp
from jax import lax
import numpy as np

D_MODEL = 1024
BATCH = 2
SEQ = 8192
DEPTH = 1
DEC_BATCH = 128
DEC_SEQ = 4
PAST_LEN = 2048
PAGE_SIZE = 128

D_ATTN = D_MODEL // 2
D_SSM = D_MODEL - D_ATTN
QK_DIM = 64
V_DIM = 2 * QK_DIM
N_HEADS = D_ATTN // V_DIM
N_BUCKETS = 32
MAX_DISTANCE = 128
SSM_GROUP = 16
N_SSM_GROUPS = D_SSM // SSM_GROUP
SSM_STATE = 64
Q_BLOCK = 128
EPS = 1e-6
NEG_INF = -1e30
QK_W = N_HEADS * 2 * QK_DIM
SPLITS = (QK_W, 2 * QK_W, 2 * QK_W + N_HEADS * V_DIM,
          2 * QK_W + N_HEADS * V_DIM + D_ATTN,
          2 * QK_W + N_HEADS * V_DIM + D_ATTN + D_SSM)
D_IN_PROJ = 2 * QK_W + N_HEADS * V_DIM + D_ATTN + 2 * D_SSM

kernel_name = "hymba_diffattn_s5_step"


def _rmsnorm(x, g):
    xf = x.astype(jnp.float32)
    return xf * lax.rsqrt(jnp.mean(xf * xf, axis=-1, keepdims=True) + EPS) * g.astype(jnp.float32)


def _lambda_init(layer):
    return 0.8 - 0.6 * math.exp(-0.3 * layer)


def _rel_bucket(q_pos, k_pos):
    n = jnp.maximum(q_pos[:, None] - k_pos[None, :], 0)
    max_exact = N_BUCKETS // 2
    nf = jnp.maximum(n, 1).astype(jnp.float32)
    large = max_exact + (jnp.log(nf / max_exact) / math.log(MAX_DISTANCE / max_exact)
                         * (N_BUCKETS - max_exact)).astype(jnp.int32)
    large = jnp.minimum(large, N_BUCKETS - 1)
    return jnp.where(n < max_exact, n, large)


def _diff_attend(q, k, v, q_pos, k_pos, rel_bias, lam):
    s = jnp.einsum('bqhcd,bkhcd->bhcqk', q, k) * (QK_DIM ** -0.5)
    bias = rel_bias.astype(jnp.float32)[_rel_bucket(q_pos, k_pos)]
    s = s + jnp.transpose(bias, (2, 0, 1))[None, :, None]
    mask = k_pos[None, :] <= q_pos[:, None]
    s = jnp.where(mask, s, NEG_INF)
    p = jax.nn.softmax(s, axis=-1)
    w = p[:, :, 0] - lam * p[:, :, 1]
    return jnp.einsum('bhqk,bkhe->bqhe', w, v)


def _in_proj(xn, w_in, q_g, k_g):
    b, L, _ = xn.shape
    h = xn @ w_in.astype(jnp.float32)
    q, k, v, ga, u, gs = jnp.split(h, SPLITS, axis=-1)
    q = _rmsnorm(q.reshape(b, L, N_HEADS, 2, QK_DIM), q_g)
    k = _rmsnorm(k.reshape(b, L, N_HEADS, 2, QK_DIM), k_g)
    v = v.reshape(b, L, N_HEADS, V_DIM)
    return q, k, v, ga, u, gs


def _attn_out(o, ga, subln_g, lam_init):
    b, L = o.shape[:2]
    o = _rmsnorm(o, subln_g) * (1.0 - lam_init)
    return o.reshape(b, L, D_ATTN) * jax.nn.silu(ga)


def _cplx_combine(e1, e2):
    a1r, a1i, b1r, b1i = e1
    a2r, a2i, b2r, b2i = e2
    return (a2r * a1r - a2i * a1i,
            a2r * a1i + a2i * a1r,
            a2r * b1r - a2i * b1i + b2r,
            a2r * b1i + a2i * b1r + b2i)


def _s5_branch(u, gs, h0_re, h0_im, a_re, a_im, log_dt, b_re, b_im, c_re, c_im,
               d_skip, w_glu, b_glu):
    bsz, L, _ = u.shape
    f32 = jnp.float32
    ug = u.reshape(bsz, L, N_SSM_GROUPS, SSM_GROUP)
    a_re = a_re.astype(f32); a_im = a_im.astype(f32)
    dt = jnp.exp(log_dt.astype(f32))[:, None]
    mag = jnp.exp(a_re * dt)
    abar_re = mag * jnp.cos(a_im * dt)
    abar_im = mag * jnp.sin(a_im * dt)
    nr = abar_re - 1.0
    den = a_re * a_re + a_im * a_im
    coef_re = (nr * a_re + abar_im * a_im) / den
    coef_im = (abar_im * a_re - nr * a_im) / den
    b_re = b_re.astype(f32); b_im = b_im.astype(f32)
    bbar_re = coef_re[..., None] * b_re - coef_im[..., None] * b_im
    bbar_im = coef_re[..., None] * b_im + coef_im[..., None] * b_re
    bu_re = jnp.einsum('blgc,gpc->blgp', ug, bbar_re)
    bu_im = jnp.einsum('blgc,gpc->blgp', ug, bbar_im)
    h0_re = h0_re.astype(f32); h0_im = h0_im.astype(f32)
    bu_re = bu_re.at[:, 0].add(abar_re * h0_re - abar_im * h0_im)
    bu_im = bu_im.at[:, 0].add(abar_re * h0_im + abar_im * h0_re)
    A_re = jnp.broadcast_to(abar_re, bu_re.shape)
    A_im = jnp.broadcast_to(abar_im, bu_im.shape)
    _, _, h_re, h_im = lax.associative_scan(_cplx_combine, (A_re, A_im, bu_re, bu_im), axis=1)
    y = (jnp.einsum('blgp,gcp->blgc', h_re, c_re.astype(f32))
         - jnp.einsum('blgp,gcp->blgc', h_im, c_im.astype(f32)))
    y = y.reshape(bsz, L, D_SSM) + d_skip.astype(f32) * u
    z = jax.nn.gelu(y)
    z = z * jax.nn.sigmoid(z @ w_glu.astype(f32) + b_glu.astype(f32))
    return z * jax.nn.silu(gs), h_re[:, -1], h_im[:, -1]


def setup_inputs(seed: int = 0) -> dict:
    key = jax.random.key(seed)
    ks = jax.random.split(key, 32)
    f32 = jnp.float32
    n_pages = PAST_LEN // PAGE_SIZE
    used = DEC_BATCH * n_pages
    n_pool = used + max(used // 4, 1)
    perm = jax.random.permutation(ks[0], n_pool)
    page_table = perm[:used].reshape(DEC_BATCH, n_pages).astype(jnp.int32)
    nrm = lambda k, s, sc: jax.random.normal(k, s, f32) * sc
    n_idx = jnp.arange(SSM_STATE, dtype=f32)
    a_re = -0.5 * jnp.exp(nrm(ks[1], (DEPTH, N_SSM_GROUPS, SSM_STATE), 0.01))
    a_im = math.pi * n_idx[None, None, :] + nrm(ks[2], (DEPTH, N_SSM_GROUPS, SSM_STATE), 0.01)
    log_dt = jax.random.uniform(ks[3], (DEPTH, N_SSM_GROUPS), f32, math.log(1e-3), math.log(1e-1))
    return {
        "x_prompt": nrm(ks[4], (BATCH, SEQ, D_MODEL), 1.0),
        "x_sample": nrm(ks[5], (DEC_BATCH, DEC_SEQ, D_MODEL), 1.0),
        "cache_k": nrm(ks[6], (DEPTH, n_pool, PAGE_SIZE, N_HEADS, 2 * QK_DIM), 1.0),
        "cache_v": nrm(ks[7], (DEPTH, n_pool, PAGE_SIZE, N_HEADS, V_DIM), 1.0),
        "state_ssm_re": nrm(ks[8], (DEPTH, DEC_BATCH, N_SSM_GROUPS, SSM_STATE), 0.1),
        "state_ssm_im": nrm(ks[9], (DEPTH, DEC_BATCH, N_SSM_GROUPS, SSM_STATE), 0.1),
        "page_table": page_table,
        "norm_g": 1.0 + nrm(ks[10], (DEPTH, D_MODEL), 0.01),
        "w_in": nrm(ks[11], (DEPTH, D_MODEL, D_IN_PROJ), D_MODEL ** -0.5),
        "q_norm_g": 1.0 + nrm(ks[12], (DEPTH, QK_DIM), 0.01),
        "k_norm_g": 1.0 + nrm(ks[13], (DEPTH, QK_DIM), 0.01),
        "lambda_q1": nrm(ks[14], (DEPTH, QK_DIM), 0.1),
        "lambda_k1": nrm(ks[15], (DEPTH, QK_DIM), 0.1),
        "lambda_q2": nrm(ks[16], (DEPTH, QK_DIM), 0.1),
        "lambda_k2": nrm(ks[17], (DEPTH, QK_DIM), 0.1),
        "subln_g": 1.0 + nrm(ks[18], (DEPTH, V_DIM), 0.01),
        "rel_bias": nrm(ks[19], (N_BUCKETS, N_HEADS), 0.5),
        "ssm_a_re": a_re,
        "ssm_a_im": a_im,
        "ssm_log_dt": log_dt,
        "ssm_b_re": nrm(ks[20], (DEPTH, N_SSM_GROUPS, SSM_STATE, SSM_GROUP), (2 * SSM_GROUP) ** -0.5),
        "ssm_b_im": nrm(ks[21], (DEPTH, N_SSM_GROUPS, SSM_STATE, SSM_GROUP), (2 * SSM_GROUP) ** -0.5),
        "ssm_c_re": nrm(ks[22], (DEPTH, N_SSM_GROUPS, SSM_GROUP, SSM_STATE), SSM_STATE ** -0.5),
        "ssm_c_im": nrm(ks[23], (DEPTH, N_SSM_GROUPS, SSM_GROUP, SSM_STATE), SSM_STATE ** -0.5),
        "ssm_d": nrm(ks[24], (DEPTH, D_SSM), 1.0),
        "w_glu": nrm(ks[25], (DEPTH, D_SSM, D_SSM), D_SSM ** -0.5),
        "b_glu": nrm(ks[26], (DEPTH, D_SSM), 0.01),
        "w_out": nrm(ks[27], (DEPTH, D_MODEL, D_MODEL), D_MODEL ** -0.5),
    }


def reference(x_prompt, x_sample, cache_k, cache_v, state_ssm_re, state_ssm_im, page_table,
              norm_g, w_in, q_norm_g, k_norm_g, lambda_q1, lambda_k1, lambda_q2, lambda_k2,
              subln_g, rel_bias, ssm_a_re, ssm_a_im, ssm_log_dt, ssm_b_re, ssm_b_im,
              ssm_c_re, ssm_c_im, ssm_d, w_glu, b_glu, w_out):
    f32 = jnp.float32
    n_pages = PAST_LEN // PAGE_SIZE
    n_blocks = SEQ // Q_BLOCK
    hp, hs = x_prompt, x_sample
    kp_l, vp_l, ks_l, vs_l = [], [], [], []
    srp_l, sip_l, srs_l, sis_l = [], [], [], []
    for l in range(DEPTH):
        lam_init = _lambda_init(l)
        lam = (jnp.exp(jnp.sum(lambda_q1[l].astype(f32) * lambda_k1[l].astype(f32)))
               - jnp.exp(jnp.sum(lambda_q2[l].astype(f32) * lambda_k2[l].astype(f32))) + lam_init)
        ssm_params = (ssm_a_re[l], ssm_a_im[l], ssm_log_dt[l], ssm_b_re[l], ssm_b_im[l],
                      ssm_c_re[l], ssm_c_im[l], ssm_d[l], w_glu[l], b_glu[l])

        xn = _rmsnorm(hp, norm_g[l])
        q, k, v, ga, u, gs = _in_proj(xn, w_in[l], q_norm_g[l], k_norm_g[l])
        k_pos = jnp.arange(SEQ)
        qb = jnp.moveaxis(q.reshape(BATCH, n_blocks, Q_BLOCK, N_HEADS, 2, QK_DIM), 1, 0)

        def _block(args, k=k, v=v, k_pos=k_pos, lam=lam):
            q_blk, bi = args
            q_pos = bi * Q_BLOCK + jnp.arange(Q_BLOCK)
            return _diff_attend(q_blk, k, v, q_pos, k_pos, rel_bias, lam)

        o = lax.map(_block, (qb, jnp.arange(n_blocks)))
        o = jnp.moveaxis(o, 0, 1).reshape(BATCH, SEQ, N_HEADS, V_DIM)
        o_a = _attn_out(o, ga, subln_g[l], lam_init)
        zero_h = jnp.zeros((BATCH, N_SSM_GROUPS, SSM_STATE), f32)
        o_s, hr_p, hi_p = _s5_branch(u, gs, zero_h, zero_h, *ssm_params)
        hp = (hp.astype(f32) + jnp.concatenate([o_a, o_s], axis=-1) @ w_out[l].astype(f32)).astype(x_prompt.dtype)
        kp_l.append(k.reshape(BATCH, SEQ, N_HEADS, 2 * QK_DIM).astype(cache_k.dtype))
        vp_l.append(v.astype(cache_v.dtype))
        srp_l.append(hr_p.astype(state_ssm_re.dtype)); sip_l.append(hi_p.astype(state_ssm_im.dtype))

        xn = _rmsnorm(hs, norm_g[l])
        q, k, v, ga, u, gs = _in_proj(xn, w_in[l], q_norm_g[l], k_norm_g[l])
        k_past = cache_k[l][page_table].reshape(DEC_BATCH, n_pages * PAGE_SIZE, N_HEADS, 2, QK_DIM).astype(f32)
        v_past = cache_v[l][page_table].reshape(DEC_BATCH, n_pages * PAGE_SIZE, N_HEADS, V_DIM).astype(f32)
        k_all = jnp.concatenate([k_past, k], axis=1)
        v_all = jnp.concatenate([v_past, v], axis=1)
        q_pos = PAST_LEN + jnp.arange(DEC_SEQ)
        k_pos_s = jnp.arange(PAST_LEN + DEC_SEQ)
        o = _diff_attend(q, k_all, v_all, q_pos, k_pos_s, rel_bias, lam)
        o_a = _attn_out(o, ga, subln_g[l], lam_init)
        o_s, hr_s, hi_s = _s5_branch(u, gs, state_ssm_re[l], state_ssm_im[l], *ssm_params)
        hs = (hs.astype(f32) + jnp.concatenate([o_a, o_s], axis=-1) @ w_out[l].astype(f32)).astype(x_sample.dtype)
        ks_l.append(k.reshape(DEC_BATCH, DEC_SEQ, N_HEADS, 2 * QK_DIM).astype(cache_k.dtype))
        vs_l.append(v.astype(cache_v.dtype))
        srs_l.append(hr_s.astype(state_ssm_re.dtype)); sis_l.append(hi_s.astype(state_ssm_im.dtype))

    y_prompt, y_sample = hp, hs
    k_prompt = jnp.stack(kp_l); v_prompt = jnp.stack(vp_l)
    k_sample = jnp.stack(ks_l); v_sample = jnp.stack(vs_l)
    ssm_re_prompt = jnp.stack(srp_l); ssm_im_prompt = jnp.stack(sip_l)
    ssm_re_sample = jnp.stack(srs_l); ssm_im_sample = jnp.stack(sis_l)
    return (y_prompt, y_sample, k_prompt, v_prompt, k_sample, v_sample,
            ssm_re_prompt, ssm_im_prompt, ssm_re_sample, ssm_im_sample)
```

```python
import functools
import math

import numpy as np
import jax
import jax.numpy as jnp
from jax import lax
from jax.experimental import pallas as pl
from jax.experimental.pallas import tpu as pltpu

F32 = jnp.float32
BF16 = jnp.bfloat16

QK_DIM = 64
V_DIM = 2 * QK_DIM
N_BUCKETS = 32
MAX_DISTANCE = 128
SSM_GROUP = 16
SSM_STATE = 64
EPS = 1e-6
NEG_INF = -1e30
LOG2E = math.log2(math.e)

LANES = 128
VMEM_LIMIT_BYTES = 56 * 1024 * 1024

ROW_TILE = 512
ATTN_TQ = 512
ATTN_TK = 512
SSM_CHUNK = 256
SSM_LANE_CHUNK = 512


def _lambda_init(layer):
    return 0.8 - 0.6 * math.exp(-0.3 * layer)


def _bucket_table(n_max):
    n = np.arange(n_max)
    max_exact = N_BUCKETS // 2
    nf = np.maximum(n, 1).astype(np.float32)
    large = max_exact + (np.log(nf / np.float32(max_exact)) / np.float32(math.log(MAX_DISTANCE / max_exact))
                         * np.float32(N_BUCKETS - max_exact)).astype(np.int32)
    large = np.minimum(large, N_BUCKETS - 1)
    return np.where(n < max_exact, n, large).astype(np.int32)


def _silu(x):
    return x * jax.nn.sigmoid(x)


def _gelu_tanh(x):
    return 0.5 * x * (1.0 + jnp.tanh(math.sqrt(2.0 / math.pi) * (x + 0.044715 * (x * x * x))))


def _split_bf16(x):
    hi = x.astype(BF16)
    lo = (x - hi.astype(F32)).astype(BF16)
    return hi, lo


def _inproj_kernel(x_ref, ng_ref, w_ref, gq_ref, gk_ref, gavg_ref,
                   q1_ref, q2_ref, kf_ref, kb_ref, vf_ref, va_ref, ga_ref, u_ref, gs_ref):
    x = x_ref[...]
    ms = jnp.mean(x * x, axis=-1, keepdims=True)
    xb = (x * lax.rsqrt(ms + EPS) * ng_ref[...]).astype(BF16)
    d_seg = q1_ref.shape[1]

    def seg(i):
        return jnp.dot(xb, w_ref[:, i * d_seg:(i + 1) * d_seg], preferred_element_type=F32)

    def group_norm(t, g):
        hi, lo = _split_bf16(t * t)
        msq = (jnp.dot(hi, gavg_ref[...], preferred_element_type=F32)
               + jnp.dot(lo, gavg_ref[...], preferred_element_type=F32))
        return t * lax.rsqrt(msq + EPS) * g

    q = group_norm(seg(0), gq_ref[...])
    lane = lax.broadcasted_iota(jnp.int32, q.shape, 1)
    first = (lane % V_DIM) < QK_DIM
    q1_ref[...] = jnp.where(first, q, 0.0).astype(BF16)
    q2_ref[...] = jnp.where(first, 0.0, q).astype(BF16)

    k = group_norm(seg(1), gk_ref[...])
    kf_ref[...] = k
    kb_ref[...] = k.astype(BF16)

    v = seg(2)
    vf_ref[...] = v
    vb = v.astype(BF16)
    ones = jnp.ones((v.shape[0], V_DIM), BF16)
    n_heads = d_seg // V_DIM
    pieces = []
    for h in range(n_heads):
        pieces += [vb[:, h * V_DIM:(h + 1) * V_DIM], ones]
    va_ref[...] = jnp.concatenate(pieces, axis=1)

    ga_ref[...] = seg(3)
    u_ref[...] = seg(4)
    gs_ref[...] = seg(5)


def _inproj(x, ng, w_bf, gq, gk, gavg):
    n, d_model = x.shape
    d_seg = gq.shape[1]
    tm = min(ROW_TILE, n)
    row = lambda w: pl.BlockSpec((tm, w), lambda i: (i, 0))
    full = lambda a: pl.BlockSpec(a.shape, lambda i: (0,) * a.ndim)
    out_shapes = [
        jax.ShapeDtypeStruct((n, d_seg), BF16),
        jax.ShapeDtypeStruct((n, d_seg), BF16),
        jax.ShapeDtypeStruct((n, d_seg), F32),
        jax.ShapeDtypeStruct((n, d_seg), BF16),
        jax.ShapeDtypeStruct((n, d_seg), F32),
        jax.ShapeDtypeStruct((n, 2 * d_seg), BF16),
        jax.ShapeDtypeStruct((n, d_seg), F32),
        jax.ShapeDtypeStruct((n, d_seg), F32),
        jax.ShapeDtypeStruct((n, d_seg), F32),
    ]
    out_specs = [row(s.shape[1]) for s in out_shapes]
    return pl.pallas_call(
        _inproj_kernel,
        grid=(n // tm,),
        in_specs=[row(d_model), full(ng), full(w_bf), full(gq), full(gk), full(gavg)],
        out_specs=out_specs,
        out_shape=out_shapes,
        compiler_params=pltpu.CompilerParams(
            dimension_semantics=("arbitrary",), vmem_limit_bytes=VMEM_LIMIT_BYTES),
        name="inproj",
    )(x, ng, w_bf, gq, gk, gavg)


def _diff_epilogue(o1, o2, lam, sg, ga):
    od = o1 - lam * o2
    ms = jnp.mean(od * od, axis=-1, keepdims=True)
    return od * lax.rsqrt(ms + EPS) * sg * _silu(ga)


def _attn_kernel(lam_ref, q1_ref, q2_ref, k_ref, v_ref, d_ref, ga_ref, sg_ref, o_ref, m_scr, acc_scr):
    tq = q1_ref.shape[0]
    tk = d_ref.shape[2]
    qi = pl.program_id(2)
    qs = jnp.concatenate([q1_ref[...], q2_ref[...]], axis=0)

    m_scr[...] = jnp.full(m_scr.shape, NEG_INF, F32)
    acc_scr[...] = jnp.zeros(acc_scr.shape, F32)

    def tile_step(kj, bias):
        start = pl.multiple_of(kj * tk, tk)
        k = k_ref[pl.ds(start, tk), :]
        s = lax.dot_general(qs, k, (((1,), (1,)), ((), ())), preferred_element_type=F32)
        if bias is not None:
            s = s + jnp.concatenate([bias, bias], axis=0)
        m_old = m_scr[...]
        m_new = jnp.maximum(m_old, jnp.max(s, axis=-1, keepdims=True))
        p = jnp.exp2(s - jnp.concatenate([m_new] * (tk // LANES), axis=1))
        alpha = jnp.exp2(m_old - m_new)
        pv = jnp.dot(p.astype(BF16), v_ref[pl.ds(start, tk), :], preferred_element_type=F32)
        acc_scr[...] = acc_scr[...] * jnp.concatenate([alpha] * (acc_scr.shape[1] // LANES), axis=1) + pv
        m_scr[...] = m_new

    def far_body(kj, carry):
        tile_step(kj, None)
        return carry

    lax.fori_loop(0, jnp.maximum(qi - 1, 0), far_body, 0)

    @pl.when(qi >= 1)
    def _():
        tile_step(qi - 1, d_ref[1])

    tile_step(qi, d_ref[0])

    acc = acc_scr[...]
    o = acc[:, :V_DIM] / acc[:, V_DIM:]
    out = _diff_epilogue(o[:tq], o[tq:], lam_ref[0], sg_ref[...], ga_ref[...])
    o_ref[...] = out.astype(o_ref.dtype)


def _prompt_attention(lam, q1, q2, kb, va, dtiles, ga, sg):
    b, l, d_attn = q1.shape
    n_heads = d_attn // V_DIM
    tq = ATTN_TQ
    return pl.pallas_call(
        _attn_kernel,
        grid=(b, n_heads, l // tq),
        in_specs=[
            pl.BlockSpec(memory_space=pltpu.SMEM),
            pl.BlockSpec((None, tq, V_DIM), lambda bi, h, qi: (bi, qi, h)),
            pl.BlockSpec((None, tq, V_DIM), lambda bi, h, qi: (bi, qi, h)),
            pl.BlockSpec((None, l, V_DIM), lambda bi, h, qi: (bi, 0, h)),
            pl.BlockSpec((None, l, 2 * V_DIM), lambda bi, h, qi: (bi, 0, h)),
            pl.BlockSpec((None, 2, tq, ATTN_TK), lambda bi, h, qi: (h, 0, 0, 0)),
            pl.BlockSpec((None, tq, V_DIM), lambda bi, h, qi: (bi, qi, h)),
            pl.BlockSpec((1, V_DIM), lambda bi, h, qi: (0, 0)),
        ],
        out_specs=pl.BlockSpec((None, tq, V_DIM), lambda bi, h, qi: (bi, qi, h)),
        out_shape=jax.ShapeDtypeStruct((b, l, d_attn), BF16),
        scratch_shapes=[pltpu.VMEM((2 * tq, LANES), F32), pltpu.VMEM((2 * tq, 2 * V_DIM), F32)],
        compiler_params=pltpu.CompilerParams(
            dimension_semantics=("arbitrary", "arbitrary", "arbitrary"),
            vmem_limit_bytes=VMEM_LIMIT_BYTES),
        name="prompt_attention",
    )(lam, q1, q2, kb, va, dtiles, ga, sg)


def _decode_kernel(pt_ref, lam_ref, q1_ref, q2_ref, kn_ref, vn_ref, ga_ref, bias_ref, sg_ref,
                   ck_hbm, cv_hbm, o_ref, kbuf, vbuf, knew, vnew, ksem, vsem):
    b = pl.program_id(0)
    nb = pl.num_programs(0)
    n_pages = kbuf.shape[1]
    page = kbuf.shape[2]
    dec_seq, d_attn = q1_ref.shape
    n_heads = d_attn // V_DIM
    rows_h = 2 * dec_seq
    slot = b % 2

    def k_copy(seq, s, j):
        return pltpu.make_async_copy(ck_hbm.at[pt_ref[seq, j]], kbuf.at[s, j], ksem.at[s])

    def v_copy(seq, s, j):
        return pltpu.make_async_copy(cv_hbm.at[pt_ref[seq, j]], vbuf.at[s, j], vsem.at[s])

    def start_fetch(seq, s):
        for j in range(n_pages):
            k_copy(seq, s, j).start()
            v_copy(seq, s, j).start()

    @pl.when(b == 0)
    def _():
        start_fetch(0, 0)
        knew[...] = jnp.zeros(knew.shape, knew.dtype)
        vnew[...] = jnp.zeros(vnew.shape, vnew.dtype)

    @pl.when(b + 1 < nb)
    def _():
        start_fetch(b + 1, 1 - slot)

    qs = jnp.concatenate([q1_ref[...], q2_ref[...]], axis=0)
    qbd = jnp.concatenate([qs] * n_heads, axis=0)
    r_head = lax.broadcasted_iota(jnp.int32, qbd.shape, 0) // rows_h
    l_head = lax.broadcasted_iota(jnp.int32, qbd.shape, 1) // V_DIM
    qbd = jnp.where(r_head == l_head, qbd, jnp.zeros_like(qbd))

    knew[0:dec_seq, :] = kn_ref[...].astype(BF16)
    vnew[0:dec_seq, :] = vn_ref[...].astype(BF16)

    for j in range(n_pages):
        k_copy(b, slot, j).wait()
        v_copy(b, slot, j).wait()

    nt = (((1,), (1,)), ((), ()))
    s_tiles = [lax.dot_general(qbd, kbuf[slot, j].astype(BF16), nt, preferred_element_type=F32)
               for j in range(n_pages)]
    s_tiles.append(lax.dot_general(qbd, knew[...], nt, preferred_element_type=F32))
    s = jnp.concatenate(s_tiles, axis=1) + bias_ref[...]
    m = jnp.max(s, axis=-1, keepdims=True)
    p = jnp.exp2(s - m)
    l_sum = jnp.sum(p, axis=-1, keepdims=True)
    pb = p.astype(BF16)
    acc = jnp.dot(pb[:, n_pages * page:], vnew[...], preferred_element_type=F32)
    for j in range(n_pages):
        acc = acc + jnp.dot(pb[:, j * page:(j + 1) * page], vbuf[slot, j].astype(BF16),
                            preferred_element_type=F32)
    o = acc / l_sum
    lam = lam_ref[0]
    outs = []
    for h in range(n_heads):
        oh = o[h * rows_h:(h + 1) * rows_h, h * V_DIM:(h + 1) * V_DIM]
        outs.append(_diff_epilogue(oh[:dec_seq], oh[dec_seq:], lam, sg_ref[...],
                                   ga_ref[:, h * V_DIM:(h + 1) * V_DIM]))
    o_ref[...] = jnp.concatenate(outs, axis=1).astype(o_ref.dtype)


def _decode_attention(page_table, lam, q1, q2, kn, vn, ga, bias, sg, cache_k, cache_v):
    n_seq, dec_seq, d_attn = q1.shape
    n_pages = page_table.shape[1]
    page = cache_k.shape[1]
    rows = bias.shape[0]
    seq_spec = pl.BlockSpec((None, dec_seq, d_attn), lambda bi, pt: (bi, 0, 0))
    grid_spec = pltpu.PrefetchScalarGridSpec(
        num_scalar_prefetch=1,
        grid=(n_seq,),
        in_specs=[
            pl.BlockSpec(memory_space=pltpu.SMEM),
            seq_spec, seq_spec, seq_spec, seq_spec, seq_spec,
            pl.BlockSpec(bias.shape, lambda bi, pt: (0, 0)),
            pl.BlockSpec((1, V_DIM), lambda bi, pt: (0, 0)),
            pl.BlockSpec(memory_space=pl.ANY),
            pl.BlockSpec(memory_space=pl.ANY),
        ],
        out_specs=seq_spec,
        scratch_shapes=[
            pltpu.VMEM((2, n_pages, page, d_attn), cache_k.dtype),
            pltpu.VMEM((2, n_pages, page, d_attn), cache_v.dtype),
            pltpu.VMEM((page, d_attn), BF16),
            pltpu.VMEM((page, d_attn), BF16),
            pltpu.SemaphoreType.DMA((2,)),
            pltpu.SemaphoreType.DMA((2,)),
        ],
    )
    del rows
    return pl.pallas_call(
        _decode_kernel,
        grid_spec=grid_spec,
        out_shape=jax.ShapeDtypeStruct((n_seq, dec_seq, d_attn), BF16),
        compiler_params=pltpu.CompilerParams(
            dimension_semantics=("arbitrary",), vmem_limit_bytes=VMEM_LIMIT_BYTES),
        name="decode_attention",
    )(page_table, lam, q1, q2, kn, vn, ga, bias, sg, cache_k, cache_v)


def _ssm_tail(y, u, gs, dskip_ref, wglu_ref, bglu_ref):
    z = _gelu_tanh(y + dskip_ref[...] * u)
    gate = jax.nn.sigmoid(jnp.dot(z.astype(BF16), wglu_ref[...], preferred_element_type=F32) + bglu_ref[...])
    return z * gate * _silu(gs)


def _ssm_prompt_kernel(u_ref, gs_ref, bw_ref, cw_ref, pin_re_ref, pin_im_ref, pout_re_ref, pout_im_ref,
                       abar_ref, tri_ref, dskip_ref, wglu_ref, bglu_ref, o_ref, hfin_ref, carry_scr):
    t = u_ref.shape[0]
    n_q = bw_ref.shape[0]
    sc = SSM_LANE_CHUNK
    c = pl.program_id(1)

    @pl.when(c == 0)
    def _():
        carry_scr[...] = jnp.zeros(carry_scr.shape, F32)

    u = u_ref[...]
    tri = tri_ref[...]
    ys = []
    for q in range(n_q):
        uq = u[:, q * LANES:(q + 1) * LANES].astype(BF16)
        bu = jnp.dot(uq, bw_ref[q], preferred_element_type=F32)
        br, bi = bu[:, :sc], bu[:, sc:]
        lanes = slice(q * sc, (q + 1) * sc)
        pir, pii = pin_re_ref[:, lanes], pin_im_ref[:, lanes]
        x = jnp.concatenate([br * pir - bi * pii, br * pii + bi * pir], axis=1)
        hi, lo = _split_bf16(x)
        cs = (jnp.dot(tri, hi, preferred_element_type=F32)
              + jnp.dot(tri, lo, preferred_element_type=F32)
              + carry_scr[:, 2 * q * sc:2 * (q + 1) * sc])
        sr, si = cs[:, :sc], cs[:, sc:]
        por, poi = pout_re_ref[:, lanes], pout_im_ref[:, lanes]
        hr = sr * por - si * poi
        hm = sr * poi + si * por
        lr, lm = hr[t - 1:t, :], hm[t - 1:t, :]
        ar, am = abar_ref[0:1, lanes], abar_ref[1:2, lanes]
        hfin_ref[:, 2 * q * sc:(2 * q + 1) * sc] = lr
        hfin_ref[:, (2 * q + 1) * sc:2 * (q + 1) * sc] = lm
        carry_scr[:, 2 * q * sc:(2 * q + 1) * sc] = ar * lr - am * lm
        carry_scr[:, (2 * q + 1) * sc:2 * (q + 1) * sc] = ar * lm + am * lr
        h = jnp.concatenate([hr, hm], axis=1).astype(BF16)
        ys.append(jnp.dot(h, cw_ref[q], preferred_element_type=F32))
    y = jnp.concatenate(ys, axis=1)
    o_ref[...] = _ssm_tail(y, u, gs_ref[...], dskip_ref, wglu_ref, bglu_ref).astype(o_ref.dtype)


def _ssm_prompt(u, gs, sp):
    b, l, d_ssm = u.shape
    t = SSM_CHUNK
    n_state2 = sp["bw"].shape[0] * sp["bw"].shape[2]
    row = pl.BlockSpec((None, t, d_ssm), lambda bi, ci: (bi, ci, 0))
    full = lambda a: pl.BlockSpec(a.shape, lambda bi, ci: (0,) * a.ndim)
    names = ["bw", "cw", "pin_re", "pin_im", "pout_re", "pout_im", "abar", "tri", "dskip", "wglu", "bglu"]
    return pl.pallas_call(
        _ssm_prompt_kernel,
        grid=(b, l // t),
        in_specs=[row, row] + [full(sp[n]) for n in names],
        out_specs=[row, pl.BlockSpec((None, 1, n_state2), lambda bi, ci: (bi, 0, 0))],
        out_shape=[jax.ShapeDtypeStruct((b, l, d_ssm), BF16),
                   jax.ShapeDtypeStruct((b, 1, n_state2), F32)],
        scratch_shapes=[pltpu.VMEM((1, n_state2), F32)],
        compiler_params=pltpu.CompilerParams(
            dimension_semantics=("arbitrary", "arbitrary"), vmem_limit_bytes=VMEM_LIMIT_BYTES),
        name="ssm_prompt",
    )(u, gs, *[sp[n] for n in names])


def _ssm_sample_kernel(u_ref, gs_ref, h0_ref, bw_ref, cw_ref, abar_ref, dskip_ref, wglu_ref, bglu_ref,
                       o_ref, hfin_ref, bu_scr, h_scr):
    n_seq = h0_ref.shape[0]
    dec_seq = u_ref.shape[0] // n_seq
    n_q = bw_ref.shape[0]
    sc = SSM_LANE_CHUNK
    n_lc = 2 * sc // LANES
    u = u_ref[...]
    ys = []
    for q in range(n_q):
        uq = u[:, q * LANES:(q + 1) * LANES].astype(BF16)
        bu = jnp.dot(uq, bw_ref[q], preferred_element_type=F32)
        for c in range(n_lc):
            bu_scr[c] = bu[:, c * LANES:(c + 1) * LANES]
        lanes = slice(q * sc, (q + 1) * sc)
        ar, am = abar_ref[0:1, lanes], abar_ref[1:2, lanes]
        hr = h0_ref[:, 2 * q * sc:(2 * q + 1) * sc]
        hm = h0_ref[:, (2 * q + 1) * sc:2 * (q + 1) * sc]
        for step in range(dec_seq):
            rows = pl.ds(step, n_seq, stride=dec_seq)
            b_all = jnp.concatenate([bu_scr[c, rows, :] for c in range(n_lc)], axis=1)
            br, bi = b_all[:, :sc], b_all[:, sc:]
            hr, hm = ar * hr - am * hm + br, ar * hm + am * hr + bi
            for c in range(n_lc // 2):
                h_scr[c, rows, :] = hr[:, c * LANES:(c + 1) * LANES]
                h_scr[n_lc // 2 + c, rows, :] = hm[:, c * LANES:(c + 1) * LANES]
        hfin_ref[:, 2 * q * sc:(2 * q + 1) * sc] = hr
        hfin_ref[:, (2 * q + 1) * sc:2 * (q + 1) * sc] = hm
        h_all = jnp.concatenate([h_scr[c] for c in range(n_lc)], axis=1)
        ys.append(jnp.dot(h_all.astype(BF16), cw_ref[q], preferred_element_type=F32))
    y = jnp.concatenate(ys, axis=1)
    o_ref[...] = _ssm_tail(y, u, gs_ref[...], dskip_ref, wglu_ref, bglu_ref).astype(o_ref.dtype)


def _ssm_sample(u, gs, h0, sp):
    n, d_ssm = u.shape
    n_seq, n_state2 = h0.shape
    names = ["bw", "cw", "abar", "dskip", "wglu", "bglu"]
    args = [u, gs, h0] + [sp[k] for k in names]
    full = lambda a: pl.BlockSpec(a.shape, lambda i: (0,) * a.ndim)
    return pl.pallas_call(
        _ssm_sample_kernel,
        grid=(1,),
        in_specs=[full(a) for a in args],
        out_specs=[pl.BlockSpec((n, d_ssm), lambda i: (0, 0)),
                   pl.BlockSpec((n_seq, n_state2), lambda i: (0, 0))],
        out_shape=[jax.ShapeDtypeStruct((n, d_ssm), BF16),
                   jax.ShapeDtypeStruct((n_seq, n_state2), F32)],
        scratch_shapes=[pltpu.VMEM((2 * SSM_LANE_CHUNK // LANES, n, LANES), F32),
                        pltpu.VMEM((2 * SSM_LANE_CHUNK // LANES, n, LANES), F32)],
        compiler_params=pltpu.CompilerParams(
            dimension_semantics=("arbitrary",), vmem_limit_bytes=VMEM_LIMIT_BYTES),
        name="ssm_sample",
    )(*args)


def _outproj_kernel(x_ref, oa_ref, os_ref, w_ref, y_ref):
    d_a = oa_ref.shape[1]
    y = x_ref[...] + jnp.dot(oa_ref[...], w_ref[:d_a, :], preferred_element_type=F32)
    y_ref[...] = y + jnp.dot(os_ref[...], w_ref[d_a:, :], preferred_element_type=F32)


def _outproj(x, oa, os_, w_bf):
    n, d_model = x.shape
    tm = min(ROW_TILE, n)
    row = lambda w: pl.BlockSpec((tm, w), lambda i: (i, 0))
    return pl.pallas_call(
        _outproj_kernel,
        grid=(n // tm,),
        in_specs=[row(d_model), row(oa.shape[1]), row(os_.shape[1]),
                  pl.BlockSpec(w_bf.shape, lambda i: (0, 0))],
        out_specs=row(d_model),
        out_shape=jax.ShapeDtypeStruct((n, d_model), F32),
        compiler_params=pltpu.CompilerParams(
            dimension_semantics=("arbitrary",), vmem_limit_bytes=VMEM_LIMIT_BYTES),
        name="outproj",
    )(x, oa, os_, w_bf)


def _ssm_params(a_re, a_im, log_dt, b_re, b_im, c_re, c_im, d_skip, w_glu, b_glu, chunk):
    n_groups, n_state = a_re.shape
    g_per_q = LANES // SSM_GROUP
    n_q = n_groups // g_per_q
    dt = jnp.exp(log_dt.astype(F32))[:, None]
    a_re = a_re.astype(F32)
    a_im = a_im.astype(F32)
    mag = jnp.exp(a_re * dt)
    abar_re = mag * jnp.cos(a_im * dt)
    abar_im = mag * jnp.sin(a_im * dt)
    nr = abar_re - 1.0
    den = a_re * a_re + a_im * a_im
    coef_re = (nr * a_re + abar_im * a_im) / den
    coef_im = (abar_im * a_re - nr * a_im) / den
    b_re = b_re.astype(F32)
    b_im = b_im.astype(F32)
    bbar_re = coef_re[..., None] * b_re - coef_im[..., None] * b_im
    bbar_im = coef_re[..., None] * b_im + coef_im[..., None] * b_re

    eye = jnp.eye(g_per_q, dtype=F32)

    def in_blocks(bb):
        bb = bb.reshape(n_q, g_per_q, n_state, SSM_GROUP)
        m = jnp.einsum("qgpc,gh->qgchp", bb, eye)
        return m.reshape(n_q, LANES, g_per_q * n_state)

    def out_blocks(cc):
        cc = cc.reshape(n_q, g_per_q, SSM_GROUP, n_state)
        m = jnp.einsum("qgcp,gh->qgphc", cc, eye)
        return m.reshape(n_q, g_per_q * n_state, LANES)

    bw = jnp.concatenate([in_blocks(bbar_re), in_blocks(bbar_im)], axis=2).astype(BF16)
    cw = jnp.concatenate([out_blocks(c_re.astype(F32)), -out_blocks(c_im.astype(F32))], axis=1).astype(BF16)

    ar = abar_re.reshape(1, -1)
    ai = abar_im.reshape(1, -1)
    sp = {
        "bw": bw, "cw": cw,
        "abar": jnp.concatenate([ar, ai], axis=0),
        "dskip": d_skip.astype(F32).reshape(1, -1),
        "wglu": w_glu.astype(BF16),
        "bglu": b_glu.astype(F32).reshape(1, -1),
    }
    if chunk:
        def cmul(x, y):
            return (x[0] * y[0] - x[1] * y[1], x[0] * y[1] + x[1] * y[0])
        ones = jnp.ones_like(ar)
        seq_r = jnp.concatenate([ones, jnp.broadcast_to(ar, (chunk - 1, ar.shape[1]))], axis=0)
        seq_i = jnp.concatenate([0.0 * ones, jnp.broadcast_to(ai, (chunk - 1, ai.shape[1]))], axis=0)
        pr, pi = lax.associative_scan(cmul, (seq_r, seq_i), axis=0)
        inv = 1.0 / (pr * pr + pi * pi)
        sp.update({"pout_re": pr, "pout_im": pi, "pin_re": pr * inv, "pin_im": -pi * inv,
                   "tri": jnp.tril(jnp.ones((chunk, chunk), F32)).astype(BF16)})
    return sp


def _state_to_lanes(h_re, h_im):
    b = h_re.shape[0]
    sc = SSM_LANE_CHUNK
    r = h_re.astype(F32).reshape(b, -1, 1, sc)
    i = h_im.astype(F32).reshape(b, -1, 1, sc)
    return jnp.concatenate([r, i], axis=2).reshape(b, -1)


def _lanes_to_state(h, n_groups, n_state):
    b = h.shape[0]
    h = h.reshape(b, -1, 2, SSM_LANE_CHUNK)
    return (h[:, :, 0, :].reshape(b, n_groups, n_state), h[:, :, 1, :].reshape(b, n_groups, n_state))


def _prompt_bias_tiles(bvec, c_far, tq, tk):
    i = np.arange(tq)[:, None]
    j = np.arange(tk)[None, :]
    tiles = []
    for t in range(2):
        dist = t * tk + i - j
        idx = np.clip(dist, 0, bvec.shape[1] - 1)
        vals = (bvec[:, idx] - c_far[:, None, None]) * LOG2E
        tiles.append(jnp.where(jnp.asarray(dist >= 0)[None], vals, NEG_INF))
    return jnp.stack(tiles, axis=1)


def _decode_bias(bvec, c_far, past_len, dec_seq, page, n_heads):
    n_cols = past_len + page
    qpos = past_len + np.arange(dec_seq)[:, None]
    kpos = np.arange(n_cols)[None, :]
    dist = qpos - kpos
    valid = (dist >= 0) & (kpos < past_len + dec_seq)
    idx = np.clip(dist, 0, bvec.shape[1] - 1)
    vals = (bvec[:, idx] - c_far[:, None, None]) * LOG2E
    vals = jnp.where(jnp.asarray(valid)[None], vals, NEG_INF)
    vals = jnp.broadcast_to(vals[:, None], (n_heads, 2, dec_seq, n_cols))
    return vals.reshape(n_heads * 2 * dec_seq, n_cols)


def kernel(x_prompt, x_sample, cache_k, cache_v, state_ssm_re, state_ssm_im, page_table,
           norm_g, w_in, q_norm_g, k_norm_g, lambda_q1, lambda_k1, lambda_q2, lambda_k2,
           subln_g, rel_bias, ssm_a_re, ssm_a_im, ssm_log_dt, ssm_b_re, ssm_b_im,
           ssm_c_re, ssm_c_im, ssm_d, w_glu, b_glu, w_out):
    batch, seq, d_model = x_prompt.shape
    dec_batch, dec_seq, _ = x_sample.shape
    depth, n_pool, page, n_heads, _ = cache_k.shape
    n_pages = page_table.shape[1]
    past_len = n_pages * page
    d_attn = n_heads * V_DIM
    n_groups, n_state = ssm_a_re.shape[1:]

    buckets = _bucket_table(max(2 * ATTN_TQ, past_len + dec_seq))
    far_from = int(np.max(np.nonzero(buckets < N_BUCKETS - 1)[0])) + 1
    assert far_from <= ATTN_TK and far_from <= page, "bias tiles assume far keys share the last bucket"

    rel_bias = rel_bias.astype(F32)
    bvec = rel_bias[buckets].T
    c_far = rel_bias[N_BUCKETS - 1]
    dtiles = _prompt_bias_tiles(bvec, c_far, ATTN_TQ, ATTN_TK)
    dbias = _decode_bias(bvec, c_far, past_len, dec_seq, page, n_heads)

    group_avg = jnp.asarray(np.kron(np.eye(d_attn // QK_DIM), np.full((QK_DIM, QK_DIM), 1.0 / QK_DIM)), BF16)
    n_rep = d_attn // QK_DIM

    hp = x_prompt.reshape(batch * seq, d_model)
    hs = x_sample.reshape(dec_batch * dec_seq, d_model)
    kp_l, vp_l, ks_l, vs_l = [], [], [], []
    srp_l, sip_l, srs_l, sis_l = [], [], [], []
    for l in range(depth):
        lam_init = _lambda_init(l)
        lam = (jnp.exp(jnp.sum(lambda_q1[l].astype(F32) * lambda_k1[l].astype(F32)))
               - jnp.exp(jnp.sum(lambda_q2[l].astype(F32) * lambda_k2[l].astype(F32))) + lam_init)
        lam = lam.reshape(1).astype(F32)
        ng = norm_g[l].astype(F32).reshape(1, d_model)
        w_bf = w_in[l].astype(BF16)
        gq = jnp.tile(q_norm_g[l].astype(F32), n_rep).reshape(1, d_attn) * (QK_DIM ** -0.5 * LOG2E)
        gk = jnp.tile(k_norm_g[l].astype(F32), n_rep).reshape(1, d_attn)
        sg = (subln_g[l].astype(F32) * (1.0 - lam_init)).reshape(1, V_DIM)
        wo_bf = w_out[l].astype(BF16)
        sp = _ssm_params(ssm_a_re[l], ssm_a_im[l], ssm_log_dt[l], ssm_b_re[l], ssm_b_im[l],
                         ssm_c_re[l], ssm_c_im[l], ssm_d[l], w_glu[l], b_glu[l], SSM_CHUNK)

        q1, q2, kf, kb, vf, va, ga, u, gs = _inproj(hp, ng, w_bf, gq, gk, group_avg)
        r3 = lambda a: a.reshape(batch, seq, a.shape[-1])
        o_a = _prompt_attention(lam, r3(q1), r3(q2), r3(kb), r3(va), dtiles, r3(ga), sg)
        o_s, hfin = _ssm_prompt(r3(u), r3(gs), sp)
        hp = _outproj(hp, o_a.reshape(batch * seq, d_attn), o_s.reshape(batch * seq, -1), wo_bf)
        kp_l.append(kf.reshape(batch, seq, n_heads, V_DIM).astype(cache_k.dtype))
        vp_l.append(vf.reshape(batch, seq, n_heads, V_DIM).astype(cache_v.dtype))
        hr_p, hi_p = _lanes_to_state(hfin.reshape(batch, -1), n_groups, n_state)
        srp_l.append(hr_p.astype(state_ssm_re.dtype))
        sip_l.append(hi_p.astype(state_ssm_im.dtype))

        q1, q2, kf, kb, vf, va, ga, u, gs = _inproj(hs, ng, w_bf, gq, gk, group_avg)
        s3 = lambda a: a.reshape(dec_batch, dec_seq, a.shape[-1])
        o_a = _decode_attention(page_table, lam, s3(q1), s3(q2), s3(kf), s3(vf), s3(ga), dbias, sg,
                                cache_k[l].reshape(n_pool, page, d_attn),
                                cache_v[l].reshape(n_pool, page, d_attn))
        h0 = _state_to_lanes(state_ssm_re[l], state_ssm_im[l])
        o_s, hfin = _ssm_sample(u, gs, h0, sp)
        hs = _outproj(hs, o_a.reshape(dec_batch * dec_seq, d_attn), o_s, wo_bf)
        ks_l.append(kf.reshape(dec_batch, dec_seq, n_heads, V_DIM).astype(cache_k.dtype))
        vs_l.append(vf.reshape(dec_batch, dec_seq, n_heads, V_DIM).astype(cache_v.dtype))
        hr_s, hi_s = _lanes_to_state(hfin, n_groups, n_state)
        srs_l.append(hr_s.astype(state_ssm_re.dtype))
        sis_l.append(hi_s.astype(state_ssm_im.dtype))

    y_prompt = hp.reshape(batch, seq, d_model).astype(x_prompt.dtype)
    y_sample = hs.reshape(dec_batch, dec_seq, d_model).astype(x_sample.dtype)
    return (y_prompt, y_sample, jnp.stack(kp_l), jnp.stack(vp_l), jnp.stack(ks_l), jnp.stack(vs_l),
            jnp.stack(srp_l), jnp.stack(sip_l), jnp.stack(srs_l), jnp.stack(sis_l))
```

```python
import functools
import math

import numpy as np
import jax
import jax.numpy as jnp
from jax import lax
from jax.experimental import pallas as pl
from jax.experimental.pallas import tpu as pltpu

F32 = jnp.float32
BF16 = jnp.bfloat16

QK_DIM = 64
V_DIM = 2 * QK_DIM
N_BUCKETS = 32
MAX_DISTANCE = 128
SSM_GROUP = 16
SSM_STATE = 64
EPS = 1e-6
NEG_INF = -1e30
LOG2E = math.log2(math.e)

LANES = 128
VMEM_LIMIT_BYTES = 56 * 1024 * 1024

ROW_TILE = 512
ATTN_TQ = 512
ATTN_TK = 512
SSM_CHUNK = 256
SSM_LANE_CHUNK = 512


def _lambda_init(layer):
    return 0.8 - 0.6 * math.exp(-0.3 * layer)


def _bucket_table(n_max):
    n = np.arange(n_max)
    max_exact = N_BUCKETS // 2
    nf = np.maximum(n, 1).astype(np.float32)
    large = max_exact + (np.log(nf / np.float32(max_exact)) / np.float32(math.log(MAX_DISTANCE / max_exact))
                         * np.float32(N_BUCKETS - max_exact)).astype(np.int32)
    large = np.minimum(large, N_BUCKETS - 1)
    return np.where(n < max_exact, n, large).astype(np.int32)


def _silu(x):
    return x * jax.nn.sigmoid(x)


def _gelu_tanh(x):
    return 0.5 * x * (1.0 + jnp.tanh(math.sqrt(2.0 / math.pi) * (x + 0.044715 * (x * x * x))))


def _split_bf16(x):
    hi = x.astype(BF16)
    lo = (x - hi.astype(F32)).astype(BF16)
    return hi, lo


def _store_head_rows(ref, val, n_heads):
    rows = val.shape[0]
    for h in range(n_heads):
        ref[pl.ds(h, rows, stride=n_heads), :] = val[:, h * V_DIM:(h + 1) * V_DIM]


def _inproj_kernel(x_ref, ng_ref, w_ref, gq_ref, gk_ref, gavg_ref, *out_refs, head_rows):
    x = x_ref[...]
    ms = jnp.mean(x * x, axis=-1, keepdims=True)
    xb = (x * lax.rsqrt(ms + EPS) * ng_ref[...]).astype(BF16)
    d_seg = gq_ref.shape[1]
    n_heads = d_seg // V_DIM

    def seg(i):
        return jnp.dot(xb, w_ref[:, i * d_seg:(i + 1) * d_seg], preferred_element_type=F32)

    def group_norm(t, g):
        hi, lo = _split_bf16(t * t)
        msq = (jnp.dot(hi, gavg_ref[...], preferred_element_type=F32)
               + jnp.dot(lo, gavg_ref[...], preferred_element_type=F32))
        return t * lax.rsqrt(msq + EPS) * g

    q = group_norm(seg(0), gq_ref[...])
    lane = lax.broadcasted_iota(jnp.int32, q.shape, 1)
    first = (lane % V_DIM) < QK_DIM
    qa = jnp.where(first, q, 0.0)
    qb = jnp.where(first, 0.0, q)
    k = group_norm(seg(1), gk_ref[...])
    v = seg(2)
    ga = seg(3)
    if head_rows:
        q1_ref, q2_ref, k4_ref, v4_ref, ga_ref, u_ref, gs_ref = out_refs
        _store_head_rows(q1_ref, qa, n_heads)
        _store_head_rows(q2_ref, qb, n_heads)
        _store_head_rows(ga_ref, ga, n_heads)
    else:
        q1_ref, q2_ref, k4_ref, kb_ref, v4_ref, va_ref, ga_ref, u_ref, gs_ref = out_refs
        q1_ref[...] = qa.astype(BF16)
        q2_ref[...] = qb.astype(BF16)
        kb_ref[...] = k.astype(BF16)
        vb = v.astype(BF16)
        ones = jnp.ones((v.shape[0], V_DIM), BF16)
        pieces = []
        for h in range(n_heads):
            pieces += [vb[:, h * V_DIM:(h + 1) * V_DIM], ones]
        va_ref[...] = jnp.concatenate(pieces, axis=1)
        ga_ref[...] = ga
    _store_head_rows(k4_ref, k, n_heads)
    _store_head_rows(v4_ref, v, n_heads)
    u_ref[...] = seg(4)
    gs_ref[...] = seg(5)


def _inproj(x, ng, w_bf, gq, gk, gavg, head_rows):
    n, d_model = x.shape
    d_seg = gq.shape[1]
    n_heads = d_seg // V_DIM
    tm = min(ROW_TILE, n)
    full = lambda a: pl.BlockSpec(a.shape, lambda i: (0,) * a.ndim)
    wide = lambda w, dt: (jax.ShapeDtypeStruct((n, w), dt), pl.BlockSpec((tm, w), lambda i: (i, 0)))
    tall = lambda dt: (jax.ShapeDtypeStruct((n * n_heads, V_DIM), dt),
                       pl.BlockSpec((tm * n_heads, V_DIM), lambda i: (i, 0)))
    if head_rows:
        outs = [tall(F32), tall(F32), tall(F32), tall(F32), tall(F32), wide(d_seg, F32), wide(d_seg, F32)]
    else:
        outs = [wide(d_seg, BF16), wide(d_seg, BF16), tall(F32), wide(d_seg, BF16), tall(F32),
                wide(2 * d_seg, BF16), wide(d_seg, F32), wide(d_seg, F32), wide(d_seg, F32)]
    return pl.pallas_call(
        functools.partial(_inproj_kernel, head_rows=head_rows),
        grid=(n // tm,),
        in_specs=[pl.BlockSpec((tm, d_model), lambda i: (i, 0)),
                  full(ng), full(w_bf), full(gq), full(gk), full(gavg)],
        out_specs=[o[1] for o in outs],
        out_shape=[o[0] for o in outs],
        compiler_params=pltpu.CompilerParams(
            dimension_semantics=("arbitrary",), vmem_limit_bytes=VMEM_LIMIT_BYTES),
        name="inproj_samples" if head_rows else "inproj_prompt",
    )(x, ng, w_bf, gq, gk, gavg)


def _diff_epilogue(o1, o2, lam, sg, ga):
    od = o1 - lam * o2
    ms = jnp.mean(od * od, axis=-1, keepdims=True)
    return od * lax.rsqrt(ms + EPS) * sg * _silu(ga)


def _attn_kernel(lam_ref, q1_ref, q2_ref, k_ref, v_ref, d_ref, ga_ref, sg_ref, o_ref, m_scr, acc_scr):
    tq = q1_ref.shape[0]
    tk = d_ref.shape[2]
    qi = pl.program_id(2)
    qs = jnp.concatenate([q1_ref[...], q2_ref[...]], axis=0)

    m_scr[...] = jnp.full(m_scr.shape, NEG_INF, F32)
    acc_scr[...] = jnp.zeros(acc_scr.shape, F32)

    def tile_step(kj, bias):
        start = pl.multiple_of(kj * tk, tk)
        k = k_ref[pl.ds(start, tk), :]
        s = lax.dot_general(qs, k, (((1,), (1,)), ((), ())), preferred_element_type=F32)
        if bias is not None:
            s = s + jnp.concatenate([bias, bias], axis=0)
        m_old = m_scr[...]
        m_new = jnp.maximum(m_old, jnp.max(s, axis=-1, keepdims=True))
        p = jnp.exp2(s - jnp.concatenate([m_new] * (tk // LANES), axis=1))
        alpha = jnp.exp2(m_old - m_new)
        pv = jnp.dot(p.astype(BF16), v_ref[pl.ds(start, tk), :], preferred_element_type=F32)
        acc_scr[...] = acc_scr[...] * jnp.concatenate([alpha] * (acc_scr.shape[1] // LANES), axis=1) + pv
        m_scr[...] = m_new

    def far_body(kj, carry):
        tile_step(kj, None)
        return carry

    lax.fori_loop(0, jnp.maximum(qi - 1, 0), far_body, 0)

    @pl.when(qi >= 1)
    def _():
        tile_step(qi - 1, d_ref[1])

    tile_step(qi, d_ref[0])

    acc = acc_scr[...]
    o = acc[:, :V_DIM] / acc[:, V_DIM:]
    out = _diff_epilogue(o[:tq], o[tq:], lam_ref[0], sg_ref[...], ga_ref[...])
    o_ref[...] = out.astype(o_ref.dtype)


def _prompt_attention(lam, q1, q2, kb, va, dtiles, ga, sg):
    b, l, d_attn = q1.shape
    n_heads = d_attn // V_DIM
    tq = ATTN_TQ
    return pl.pallas_call(
        _attn_kernel,
        grid=(b, n_heads, l // tq),
        in_specs=[
            pl.BlockSpec(memory_space=pltpu.SMEM),
            pl.BlockSpec((None, tq, V_DIM), lambda bi, h, qi: (bi, qi, h)),
            pl.BlockSpec((None, tq, V_DIM), lambda bi, h, qi: (bi, qi, h)),
            pl.BlockSpec((None, l, V_DIM), lambda bi, h, qi: (bi, 0, h)),
            pl.BlockSpec((None, l, 2 * V_DIM), lambda bi, h, qi: (bi, 0, h)),
            pl.BlockSpec((None, 2, tq, ATTN_TK), lambda bi, h, qi: (h, 0, 0, 0)),
            pl.BlockSpec((None, tq, V_DIM), lambda bi, h, qi: (bi, qi, h)),
            pl.BlockSpec((1, V_DIM), lambda bi, h, qi: (0, 0)),
        ],
        out_specs=pl.BlockSpec((None, tq, V_DIM), lambda bi, h, qi: (bi, qi, h)),
        out_shape=jax.ShapeDtypeStruct((b, l, d_attn), BF16),
        scratch_shapes=[pltpu.VMEM((2 * tq, LANES), F32), pltpu.VMEM((2 * tq, 2 * V_DIM), F32)],
        compiler_params=pltpu.CompilerParams(
            dimension_semantics=("arbitrary", "arbitrary", "arbitrary"),
            vmem_limit_bytes=VMEM_LIMIT_BYTES),
        name="prompt_attention",
    )(lam, q1, q2, kb, va, dtiles, ga, sg)


def _decode_kernel(pt_ref, lam_ref, q1_ref, q2_ref, kn_ref, vn_ref, ga_ref, bias_ref, sg_ref,
                   ck_hbm, cv_hbm, o_ref, kbuf, vbuf, knew, vnew, ksem, vsem):
    b = pl.program_id(0)
    nb = pl.num_programs(0)
    n_pages = kbuf.shape[1]
    page_rows = kbuf.shape[2]
    new_rows = q1_ref.shape[0]
    slot = b % 2

    def k_copy(seq, s, j):
        return pltpu.make_async_copy(ck_hbm.at[pt_ref[seq, j]], kbuf.at[s, j], ksem.at[s])

    def v_copy(seq, s, j):
        return pltpu.make_async_copy(cv_hbm.at[pt_ref[seq, j]], vbuf.at[s, j], vsem.at[s])

    def start_fetch(seq, s):
        for j in range(n_pages):
            k_copy(seq, s, j).start()
            v_copy(seq, s, j).start()

    @pl.when(b == 0)
    def _():
        start_fetch(0, 0)
        knew[...] = jnp.zeros(knew.shape, knew.dtype)
        vnew[...] = jnp.zeros(vnew.shape, vnew.dtype)

    @pl.when(b + 1 < nb)
    def _():
        start_fetch(b + 1, 1 - slot)

    qx = jnp.concatenate([q1_ref[...], q2_ref[...]], axis=0).astype(BF16)
    knew[0:new_rows, :] = kn_ref[...].astype(BF16)
    vnew[0:new_rows, :] = vn_ref[...].astype(BF16)

    for j in range(n_pages):
        k_copy(b, slot, j).wait()
        v_copy(b, slot, j).wait()

    nt = (((1,), (1,)), ((), ()))
    s_tiles = [lax.dot_general(qx, kbuf[slot, j].astype(BF16), nt, preferred_element_type=F32)
               for j in range(n_pages)]
    s_tiles.append(lax.dot_general(qx, knew[...], nt, preferred_element_type=F32))
    s = jnp.concatenate(s_tiles, axis=1) + bias_ref[...]
    m = jnp.max(s, axis=-1, keepdims=True)
    p = jnp.exp2(s - m)
    l_sum = jnp.sum(p, axis=-1, keepdims=True)
    pb = p.astype(BF16)
    acc = jnp.dot(pb[:, n_pages * page_rows:], vnew[...], preferred_element_type=F32)
    for j in range(n_pages):
        acc = acc + jnp.dot(pb[:, j * page_rows:(j + 1) * page_rows], vbuf[slot, j].astype(BF16),
                            preferred_element_type=F32)
    o = acc / l_sum
    o_ref[...] = _diff_epilogue(o[:new_rows], o[new_rows:], lam_ref[0], sg_ref[...], ga_ref[...])


def _decode_attention(page_table, lam, q1, q2, kn, vn, ga, bias, sg, cache_k, cache_v, new_rows):
    n_seq, n_pages = page_table.shape
    page_rows = cache_k.shape[1]
    seq_spec = pl.BlockSpec((new_rows, V_DIM), lambda bi, pt: (bi, 0))
    grid_spec = pltpu.PrefetchScalarGridSpec(
        num_scalar_prefetch=1,
        grid=(n_seq,),
        in_specs=[
            pl.BlockSpec(memory_space=pltpu.SMEM),
            seq_spec, seq_spec, seq_spec, seq_spec, seq_spec,
            pl.BlockSpec(bias.shape, lambda bi, pt: (0, 0)),
            pl.BlockSpec((1, V_DIM), lambda bi, pt: (0, 0)),
            pl.BlockSpec(memory_space=pl.ANY),
            pl.BlockSpec(memory_space=pl.ANY),
        ],
        out_specs=seq_spec,
        scratch_shapes=[
            pltpu.VMEM((2, n_pages, page_rows, V_DIM), cache_k.dtype),
            pltpu.VMEM((2, n_pages, page_rows, V_DIM), cache_v.dtype),
            pltpu.VMEM((page_rows, V_DIM), BF16),
            pltpu.VMEM((page_rows, V_DIM), BF16),
            pltpu.SemaphoreType.DMA((2,)),
            pltpu.SemaphoreType.DMA((2,)),
        ],
    )
    return pl.pallas_call(
        _decode_kernel,
        grid_spec=grid_spec,
        out_shape=jax.ShapeDtypeStruct((n_seq * new_rows, V_DIM), F32),
        compiler_params=pltpu.CompilerParams(
            dimension_semantics=("arbitrary",), vmem_limit_bytes=VMEM_LIMIT_BYTES),
        name="decode_attention",
    )(page_table, lam, q1, q2, kn, vn, ga, bias, sg, cache_k, cache_v)


def _ssm_tail(y, u, gs, dskip_ref, wglu_ref, bglu_ref):
    z = _gelu_tanh(y + dskip_ref[...] * u)
    gate = jax.nn.sigmoid(jnp.dot(z.astype(BF16), wglu_ref[...], preferred_element_type=F32) + bglu_ref[...])
    return z * gate * _silu(gs)


def _ssm_prompt_kernel(u_ref, gs_ref, bw_ref, cw_ref, pin_re_ref, pin_im_ref, pout_re_ref, pout_im_ref,
                       abar_ref, tri_ref, dskip_ref, wglu_ref, bglu_ref, o_ref, hfin_ref, carry_scr):
    t = u_ref.shape[0]
    n_q = bw_ref.shape[0]
    sc = SSM_LANE_CHUNK
    c = pl.program_id(1)

    @pl.when(c == 0)
    def _():
        carry_scr[...] = jnp.zeros(carry_scr.shape, F32)

    u = u_ref[...]
    tri = tri_ref[...]
    ys = []
    for q in range(n_q):
        uq = u[:, q * LANES:(q + 1) * LANES].astype(BF16)
        bu = jnp.dot(uq, bw_ref[q], preferred_element_type=F32)
        br, bi = bu[:, :sc], bu[:, sc:]
        lanes = slice(q * sc, (q + 1) * sc)
        pir, pii = pin_re_ref[:, lanes], pin_im_ref[:, lanes]
        x = jnp.concatenate([br * pir - bi * pii, br * pii + bi * pir], axis=1)
        hi, lo = _split_bf16(x)
        cs = (jnp.dot(tri, hi, preferred_element_type=F32)
              + jnp.dot(tri, lo, preferred_element_type=F32)
              + carry_scr[:, 2 * q * sc:2 * (q + 1) * sc])
        sr, si = cs[:, :sc], cs[:, sc:]
        por, poi = pout_re_ref[:, lanes], pout_im_ref[:, lanes]
        hr = sr * por - si * poi
        hm = sr * poi + si * por
        lr, lm = hr[t - 1:t, :], hm[t - 1:t, :]
        ar, am = abar_ref[0:1, lanes], abar_ref[1:2, lanes]
        hfin_ref[:, 2 * q * sc:(2 * q + 1) * sc] = lr
        hfin_ref[:, (2 * q + 1) * sc:2 * (q + 1) * sc] = lm
        carry_scr[:, 2 * q * sc:(2 * q + 1) * sc] = ar * lr - am * lm
        carry_scr[:, (2 * q + 1) * sc:2 * (q + 1) * sc] = ar * lm + am * lr
        h = jnp.concatenate([hr, hm], axis=1).astype(BF16)
        ys.append(jnp.dot(h, cw_ref[q], preferred_element_type=F32))
    y = jnp.concatenate(ys, axis=1)
    o_ref[...] = _ssm_tail(y, u, gs_ref[...], dskip_ref, wglu_ref, bglu_ref).astype(o_ref.dtype)


def _ssm_prompt(u, gs, sp):
    b, l, d_ssm = u.shape
    t = SSM_CHUNK
    n_state2 = sp["bw"].shape[0] * sp["bw"].shape[2]
    row = pl.BlockSpec((None, t, d_ssm), lambda bi, ci: (bi, ci, 0))
    full = lambda a: pl.BlockSpec(a.shape, lambda bi, ci: (0,) * a.ndim)
    names = ["bw", "cw", "pin_re", "pin_im", "pout_re", "pout_im", "abar", "tri", "dskip", "wglu", "bglu"]
    return pl.pallas_call(
        _ssm_prompt_kernel,
        grid=(b, l // t),
        in_specs=[row, row] + [full(sp[n]) for n in names],
        out_specs=[row, pl.BlockSpec((None, 1, n_state2), lambda bi, ci: (bi, 0, 0))],
        out_shape=[jax.ShapeDtypeStruct((b, l, d_ssm), BF16),
                   jax.ShapeDtypeStruct((b, 1, n_state2), F32)],
        scratch_shapes=[pltpu.VMEM((1, n_state2), F32)],
        compiler_params=pltpu.CompilerParams(
            dimension_semantics=("arbitrary", "arbitrary"), vmem_limit_bytes=VMEM_LIMIT_BYTES),
        name="ssm_prompt",
    )(u, gs, *[sp[n] for n in names])


def _ssm_sample_kernel(u_ref, gs_ref, h0_ref, bw_ref, cw_ref, abar_ref, dskip_ref, wglu_ref, bglu_ref,
                       o_ref, hfin_ref, bu_scr, h_scr):
    n_seq = h0_ref.shape[0]
    dec_seq = u_ref.shape[0] // n_seq
    n_q = bw_ref.shape[0]
    sc = SSM_LANE_CHUNK
    n_lc = 2 * sc // LANES
    u = u_ref[...]
    ys = []
    for q in range(n_q):
        uq = u[:, q * LANES:(q + 1) * LANES].astype(BF16)
        bu = jnp.dot(uq, bw_ref[q], preferred_element_type=F32)
        for c in range(n_lc):
            bu_scr[c] = bu[:, c * LANES:(c + 1) * LANES]
        lanes = slice(q * sc, (q + 1) * sc)
        ar, am = abar_ref[0:1, lanes], abar_ref[1:2, lanes]
        hr = h0_ref[:, 2 * q * sc:(2 * q + 1) * sc]
        hm = h0_ref[:, (2 * q + 1) * sc:2 * (q + 1) * sc]
        for step in range(dec_seq):
            rows = pl.ds(step, n_seq, stride=dec_seq)
            b_all = jnp.concatenate([bu_scr[c, rows, :] for c in range(n_lc)], axis=1)
            br, bi = b_all[:, :sc], b_all[:, sc:]
            hr, hm = ar * hr - am * hm + br, ar * hm + am * hr + bi
            for c in range(n_lc // 2):
                h_scr[c, rows, :] = hr[:, c * LANES:(c + 1) * LANES]
                h_scr[n_lc // 2 + c, rows, :] = hm[:, c * LANES:(c + 1) * LANES]
        hfin_ref[:, 2 * q * sc:(2 * q + 1) * sc] = hr
        hfin_ref[:, (2 * q + 1) * sc:2 * (q + 1) * sc] = hm
        h_all = jnp.concatenate([h_scr[c] for c in range(n_lc)], axis=1)
        ys.append(jnp.dot(h_all.astype(BF16), cw_ref[q], preferred_element_type=F32))
    y = jnp.concatenate(ys, axis=1)
    o_ref[...] = _ssm_tail(y, u, gs_ref[...], dskip_ref, wglu_ref, bglu_ref).astype(o_ref.dtype)


def _ssm_sample(u, gs, h0, sp):
    n, d_ssm = u.shape
    n_seq, n_state2 = h0.shape
    names = ["bw", "cw", "abar", "dskip", "wglu", "bglu"]
    args = [u, gs, h0] + [sp[k] for k in names]
    full = lambda a: pl.BlockSpec(a.shape, lambda i: (0,) * a.ndim)
    return pl.pallas_call(
        _ssm_sample_kernel,
        grid=(1,),
        in_specs=[full(a) for a in args],
        out_specs=[pl.BlockSpec((n, d_ssm), lambda i: (0, 0)),
                   pl.BlockSpec((n_seq, n_state2), lambda i: (0, 0))],
        out_shape=[jax.ShapeDtypeStruct((n, d_ssm), BF16),
                   jax.ShapeDtypeStruct((n_seq, n_state2), F32)],
        scratch_shapes=[pltpu.VMEM((2 * SSM_LANE_CHUNK // LANES, n, LANES), F32),
                        pltpu.VMEM((2 * SSM_LANE_CHUNK // LANES, n, LANES), F32)],
        compiler_params=pltpu.CompilerParams(
            dimension_semantics=("arbitrary",), vmem_limit_bytes=VMEM_LIMIT_BYTES),
        name="ssm_sample",
    )(*args)


def _outproj_kernel(x_ref, oa_ref, os_ref, w_ref, y_ref):
    d_a = oa_ref.shape[1]
    y = x_ref[...] + jnp.dot(oa_ref[...], w_ref[:d_a, :], preferred_element_type=F32)
    y_ref[...] = y + jnp.dot(os_ref[...], w_ref[d_a:, :], preferred_element_type=F32)


def _outproj(x, oa, os_, w_bf):
    n, d_model = x.shape
    tm = min(ROW_TILE, n)
    row = lambda w: pl.BlockSpec((tm, w), lambda i: (i, 0))
    return pl.pallas_call(
        _outproj_kernel,
        grid=(n // tm,),
        in_specs=[row(d_model), row(oa.shape[1]), row(os_.shape[1]),
                  pl.BlockSpec(w_bf.shape, lambda i: (0, 0))],
        out_specs=row(d_model),
        out_shape=jax.ShapeDtypeStruct((n, d_model), F32),
        compiler_params=pltpu.CompilerParams(
            dimension_semantics=("arbitrary",), vmem_limit_bytes=VMEM_LIMIT_BYTES),
        name="outproj",
    )(x, oa, os_, w_bf)


def _ssm_params(a_re, a_im, log_dt, b_re, b_im, c_re, c_im, d_skip, w_glu, b_glu, chunk):
    n_groups, n_state = a_re.shape
    g_per_q = LANES // SSM_GROUP
    n_q = n_groups // g_per_q
    dt = jnp.exp(log_dt.astype(F32))[:, None]
    a_re = a_re.astype(F32)
    a_im = a_im.astype(F32)
    mag = jnp.exp(a_re * dt)
    abar_re = mag * jnp.cos(a_im * dt)
    abar_im = mag * jnp.sin(a_im * dt)
    nr = abar_re - 1.0
    den = a_re * a_re + a_im * a_im
    coef_re = (nr * a_re + abar_im * a_im) / den
    coef_im = (abar_im * a_re - nr * a_im) / den
    b_re = b_re.astype(F32)
    b_im = b_im.astype(F32)
    bbar_re = coef_re[..., None] * b_re - coef_im[..., None] * b_im
    bbar_im = coef_re[..., None] * b_im + coef_im[..., None] * b_re

    eye = jnp.eye(g_per_q, dtype=F32)

    def in_blocks(bb):
        bb = bb.reshape(n_q, g_per_q, n_state, SSM_GROUP)
        m = jnp.einsum("qgpc,gh->qgchp", bb, eye)
        return m.reshape(n_q, LANES, g_per_q * n_state)

    def out_blocks(cc):
        cc = cc.reshape(n_q, g_per_q, SSM_GROUP, n_state)
        m = jnp.einsum("qgcp,gh->qgphc", cc, eye)
        return m.reshape(n_q, g_per_q * n_state, LANES)

    bw = jnp.concatenate([in_blocks(bbar_re), in_blocks(bbar_im)], axis=2).astype(BF16)
    cw = jnp.concatenate([out_blocks(c_re.astype(F32)), -out_blocks(c_im.astype(F32))], axis=1).astype(BF16)

    ar = abar_re.reshape(1, -1)
    ai = abar_im.reshape(1, -1)
    sp = {
        "bw": bw, "cw": cw,
        "abar": jnp.concatenate([ar, ai], axis=0),
        "dskip": d_skip.astype(F32).reshape(1, -1),
        "wglu": w_glu.astype(BF16),
        "bglu": b_glu.astype(F32).reshape(1, -1),
    }
    pr, pi = jnp.ones_like(ar), jnp.zeros_like(ar)
    cr, ci = ar, ai
    while pr.shape[0] < chunk:
        pr, pi = (jnp.concatenate([pr, pr * cr - pi * ci], axis=0),
                  jnp.concatenate([pi, pr * ci + pi * cr], axis=0))
        cr, ci = cr * cr - ci * ci, 2.0 * cr * ci
    assert pr.shape[0] == chunk
    inv = 1.0 / (pr * pr + pi * pi)
    sp.update({"pout_re": pr, "pout_im": pi, "pin_re": pr * inv, "pin_im": -pi * inv,
               "tri": jnp.asarray(np.tril(np.ones((chunk, chunk), np.float32)), BF16)})
    return sp


def _state_to_lanes(h_re, h_im):
    b = h_re.shape[0]
    sc = SSM_LANE_CHUNK
    r = h_re.astype(F32).reshape(b, -1, 1, sc)
    i = h_im.astype(F32).reshape(b, -1, 1, sc)
    return jnp.concatenate([r, i], axis=2).reshape(b, -1)


def _lanes_to_state(h, n_groups, n_state):
    b = h.shape[0]
    h = h.reshape(b, -1, 2, SSM_LANE_CHUNK)
    return (h[:, :, 0, :].reshape(b, n_groups, n_state), h[:, :, 1, :].reshape(b, n_groups, n_state))


def _toeplitz(v, n):
    h = v.shape[0]
    x = jnp.broadcast_to(v[:, None, :], (h, n, 2 * n)).reshape(h, 2 * n * n)
    return x[:, :n * (2 * n - 1)].reshape(h, n, 2 * n - 1)[:, :, :n]


def _prompt_bias_tiles(fvec, tq, tk):
    n = LANES
    h = fvec.shape[0]
    neg = jnp.full((h, n - 1), NEG_INF, F32)
    va = jnp.concatenate([fvec[:, 0:1], neg, jnp.zeros((h, 1), F32), fvec[:, 1:n][:, ::-1]], axis=1)
    vb = jnp.concatenate([fvec[:, 1:n + 1][:, ::-1], jnp.zeros((h, n), F32)], axis=1)
    blk_a = _toeplitz(va, n)
    blk_b = _toeplitz(vb, n)
    tiles = []
    for t in range(2):
        delta = (np.arange(tq // n)[:, None] - np.arange(tk // n)[None, :]) + t * (tk // n)
        dmap = jnp.asarray(np.kron(delta, np.ones((n, n), np.int32)))[None]
        ta = jnp.tile(blk_a, (1, tq // n, tk // n))
        tb = jnp.tile(blk_b, (1, tq // n, tk // n))
        tiles.append(jnp.where(dmap == 0, ta, jnp.where(dmap == 1, tb, jnp.where(dmap < 0, NEG_INF, 0.0))))
    return jnp.stack(tiles, axis=1)


def _decode_bias(fvec, n_pages, page, dec_seq, n_heads):
    h = fvec.shape[0]
    rows = []
    for i in range(dec_seq):
        last = fvec[:, i + 1:i + 1 + page][:, ::-1]
        new = jnp.concatenate([fvec[:, 0:i + 1][:, ::-1], jnp.full((h, page - i - 1), NEG_INF, F32)], axis=1)
        rows.append(jnp.concatenate([jnp.zeros((h, (n_pages - 1) * page), F32), last, new], axis=1))
    per_head = jnp.stack(rows, axis=0)
    same = jnp.asarray(np.eye(n_heads, dtype=bool))[None, :, None, :]
    full = jnp.where(same, per_head[:, :, :, None], NEG_INF)
    full = full.reshape(dec_seq * n_heads, -1)
    return jnp.concatenate([full, full], axis=0)


def kernel(x_prompt, x_sample, cache_k, cache_v, state_ssm_re, state_ssm_im, page_table,
           norm_g, w_in, q_norm_g, k_norm_g, lambda_q1, lambda_k1, lambda_q2, lambda_k2,
           subln_g, rel_bias, ssm_a_re, ssm_a_im, ssm_log_dt, ssm_b_re, ssm_b_im,
           ssm_c_re, ssm_c_im, ssm_d, w_glu, b_glu, w_out):
    batch, seq, d_model = x_prompt.shape
    dec_batch, dec_seq, _ = x_sample.shape
    depth, n_pool, page, n_heads, _ = cache_k.shape
    n_pages = page_table.shape[1]
    d_attn = n_heads * V_DIM
    n_groups, n_state = ssm_a_re.shape[1:]
    new_rows = dec_seq * n_heads

    buckets = _bucket_table(2 * LANES)
    far_from = int(np.max(np.nonzero(buckets < N_BUCKETS - 1)[0])) + 1
    assert far_from <= LANES and _bucket_table(seq + page * n_pages)[far_from:].min() == N_BUCKETS - 1
    assert page == LANES and dec_seq < LANES and ATTN_TQ == ATTN_TK

    rel_bias = rel_bias.astype(F32)
    fvec = (rel_bias[buckets].T - rel_bias[N_BUCKETS - 1][:, None]) * LOG2E
    fvec = jnp.where(jnp.asarray(np.arange(2 * LANES) < far_from)[None], fvec, 0.0)
    dtiles = _prompt_bias_tiles(fvec, ATTN_TQ, ATTN_TK)
    dbias = _decode_bias(fvec, n_pages, page, dec_seq, n_heads)

    group_avg = jnp.asarray(np.kron(np.eye(d_attn // QK_DIM), np.full((QK_DIM, QK_DIM), 1.0 / QK_DIM)), BF16)
    n_rep = d_attn // QK_DIM
    cache_k_rows = cache_k.reshape(depth * n_pool, page * n_heads, V_DIM)
    cache_v_rows = cache_v.reshape(depth * n_pool, page * n_heads, V_DIM)

    hp = x_prompt.reshape(batch * seq, d_model)
    hs = x_sample.reshape(dec_batch * dec_seq, d_model)
    kp_l, vp_l, ks_l, vs_l = [], [], [], []
    srp_l, sip_l, srs_l, sis_l = [], [], [], []
    for l in range(depth):
        lam_init = _lambda_init(l)
        lam = (jnp.exp(jnp.sum(lambda_q1[l].astype(F32) * lambda_k1[l].astype(F32)))
               - jnp.exp(jnp.sum(lambda_q2[l].astype(F32) * lambda_k2[l].astype(F32))) + lam_init)
        lam = lam.reshape(1).astype(F32)
        ng = norm_g[l].astype(F32).reshape(1, d_model)
        w_bf = w_in[l].astype(BF16)
        gq = jnp.tile(q_norm_g[l].astype(F32), n_rep).reshape(1, d_attn) * (QK_DIM ** -0.5 * LOG2E)
        gk = jnp.tile(k_norm_g[l].astype(F32), n_rep).reshape(1, d_attn)
        sg = (subln_g[l].astype(F32) * (1.0 - lam_init)).reshape(1, V_DIM)
        wo_bf = w_out[l].astype(BF16)
        sp = _ssm_params(ssm_a_re[l], ssm_a_im[l], ssm_log_dt[l], ssm_b_re[l], ssm_b_im[l],
                         ssm_c_re[l], ssm_c_im[l], ssm_d[l], w_glu[l], b_glu[l], SSM_CHUNK)

        q1, q2, k4, kb, v4, va, ga, u, gs = _inproj(hp, ng, w_bf, gq, gk, group_avg, head_rows=False)
        r3 = lambda a: a.reshape(batch, seq, a.shape[-1])
        o_a = _prompt_attention(lam, r3(q1), r3(q2), r3(kb), r3(va), dtiles, r3(ga), sg)
        o_s, hfin = _ssm_prompt(r3(u), r3(gs), sp)
        hp = _outproj(hp, o_a.reshape(batch * seq, d_attn), o_s.reshape(batch * seq, -1), wo_bf)
        kp_l.append(k4.reshape(batch, seq, n_heads, V_DIM).astype(cache_k.dtype))
        vp_l.append(v4.reshape(batch, seq, n_heads, V_DIM).astype(cache_v.dtype))
        hr_p, hi_p = _lanes_to_state(hfin.reshape(batch, -1), n_groups, n_state)
        srp_l.append(hr_p.astype(state_ssm_re.dtype))
        sip_l.append(hi_p.astype(state_ssm_im.dtype))

        q1, q2, k4, v4, ga, u, gs = _inproj(hs, ng, w_bf, gq, gk, group_avg, head_rows=True)
        o_a = _decode_attention(page_table + l * n_pool, lam, q1, q2, k4, v4, ga, dbias, sg,
                                cache_k_rows, cache_v_rows, new_rows)
        h0 = _state_to_lanes(state_ssm_re[l], state_ssm_im[l])
        o_s, hfin = _ssm_sample(u, gs, h0, sp)
        hs = _outproj(hs, o_a.reshape(dec_batch * dec_seq, d_attn).astype(BF16), o_s, wo_bf)
        ks_l.append(k4.reshape(dec_batch, dec_seq, n_heads, V_DIM).astype(cache_k.dtype))
        vs_l.append(v4.reshape(dec_batch, dec_seq, n_heads, V_DIM).astype(cache_v.dtype))
        hr_s, hi_s = _lanes_to_state(hfin, n_groups, n_state)
        srs_l.append(hr_s.astype(state_ssm_re.dtype))
        sis_l.append(hi_s.astype(state_ssm_im.dtype))

    y_prompt = hp.reshape(batch, seq, d_model).astype(x_prompt.dtype)
    y_sample = hs.reshape(dec_batch, dec_seq, d_model).astype(x_sample.dtype)
    return (y_prompt, y_sample, jnp.stack(kp_l), jnp.stack(vp_l), jnp.stack(ks_l), jnp.stack(vs_l),
            jnp.stack(srp_l), jnp.stack(sip_l), jnp.stack(srs_l), jnp.stack(sis_l))
```

```python
import functools
import math

import numpy as np
import jax
import jax.numpy as jnp
from jax import lax
from jax.experimental import pallas as pl
from jax.experimental.pallas import tpu as pltpu

F32 = jnp.float32
BF16 = jnp.bfloat16

QK_DIM = 64
V_DIM = 2 * QK_DIM
N_BUCKETS = 32
MAX_DISTANCE = 128
SSM_GROUP = 16
SSM_STATE = 64
EPS = 1e-6
NEG_INF = -1e30
LOG2E = math.log2(math.e)

LANES = 128
VMEM_LIMIT_BYTES = 56 * 1024 * 1024

ROW_TILE = 512
ATTN_TQ = 512
ATTN_TK = 512
SSM_CHUNK = 256
SSM_LANE_CHUNK = 512


def _lambda_init(layer):
    return 0.8 - 0.6 * math.exp(-0.3 * layer)


def _bucket_table(n_max):
    n = np.arange(n_max)
    max_exact = N_BUCKETS // 2
    nf = np.maximum(n, 1).astype(np.float32)
    large = max_exact + (np.log(nf / np.float32(max_exact)) / np.float32(math.log(MAX_DISTANCE / max_exact))
                         * np.float32(N_BUCKETS - max_exact)).astype(np.int32)
    large = np.minimum(large, N_BUCKETS - 1)
    return np.where(n < max_exact, n, large).astype(np.int32)


def _silu(x):
    return x * jax.nn.sigmoid(x)


def _gelu_tanh(x):
    return 0.5 * x * (1.0 + jnp.tanh(math.sqrt(2.0 / math.pi) * (x + 0.044715 * (x * x * x))))


def _split_bf16(x):
    hi = x.astype(BF16)
    lo = (x - hi.astype(F32)).astype(BF16)
    return hi, lo


def _store_head_rows(ref, val, n_heads):
    rows = val.shape[0]
    for h in range(n_heads):
        ref[pl.ds(h, rows, stride=n_heads), :] = val[:, h * V_DIM:(h + 1) * V_DIM]


def _inproj_kernel(x_ref, ng_ref, w_ref, gq_ref, gk_ref, gavg_ref, *out_refs, head_rows):
    x = x_ref[...]
    ms = jnp.mean(x * x, axis=-1, keepdims=True)
    xb = (x * lax.rsqrt(ms + EPS) * ng_ref[...]).astype(BF16)
    d_seg = gq_ref.shape[1]
    n_heads = d_seg // V_DIM

    def seg(i):
        return jnp.dot(xb, w_ref[:, i * d_seg:(i + 1) * d_seg], preferred_element_type=F32)

    def group_norm(t, g):
        hi, lo = _split_bf16(t * t)
        msq = (jnp.dot(hi, gavg_ref[...], preferred_element_type=F32)
               + jnp.dot(lo, gavg_ref[...], preferred_element_type=F32))
        return t * lax.rsqrt(msq + EPS) * g

    q = group_norm(seg(0), gq_ref[...])
    lane = lax.broadcasted_iota(jnp.int32, q.shape, 1)
    first = (lane % V_DIM) < QK_DIM
    qa = jnp.where(first, q, 0.0)
    qb = jnp.where(first, 0.0, q)
    k = group_norm(seg(1), gk_ref[...])
    v = seg(2)
    ga = seg(3)
    if head_rows:
        q1_ref, q2_ref, k4_ref, v4_ref, ga_ref, u_ref, gs_ref = out_refs
        _store_head_rows(q1_ref, qa, n_heads)
        _store_head_rows(q2_ref, qb, n_heads)
        _store_head_rows(ga_ref, ga, n_heads)
    else:
        q1_ref, q2_ref, k4_ref, kb_ref, v4_ref, va_ref, ga_ref, u_ref, gs_ref = out_refs
        q1_ref[...] = qa.astype(BF16)
        q2_ref[...] = qb.astype(BF16)
        kb_ref[...] = k.astype(BF16)
        vb = v.astype(BF16)
        ones = jnp.ones((v.shape[0], V_DIM), BF16)
        pieces = []
        for h in range(n_heads):
            pieces += [vb[:, h * V_DIM:(h + 1) * V_DIM], ones]
        va_ref[...] = jnp.concatenate(pieces, axis=1)
        ga_ref[...] = ga
    _store_head_rows(k4_ref, k, n_heads)
    _store_head_rows(v4_ref, v, n_heads)
    u_ref[...] = seg(4)
    gs_ref[...] = seg(5)


def _inproj(x, ng, w_bf, gq, gk, gavg, head_rows):
    n, d_model = x.shape
    d_seg = gq.shape[1]
    n_heads = d_seg // V_DIM
    tm = min(ROW_TILE, n)
    full = lambda a: pl.BlockSpec(a.shape, lambda i: (0,) * a.ndim)
    wide = lambda w, dt: (jax.ShapeDtypeStruct((n, w), dt), pl.BlockSpec((tm, w), lambda i: (i, 0)))
    tall = lambda dt: (jax.ShapeDtypeStruct((n * n_heads, V_DIM), dt),
                       pl.BlockSpec((tm * n_heads, V_DIM), lambda i: (i, 0)))
    if head_rows:
        outs = [tall(F32), tall(F32), tall(F32), tall(F32), tall(F32), wide(d_seg, F32), wide(d_seg, F32)]
    else:
        outs = [wide(d_seg, BF16), wide(d_seg, BF16), tall(F32), wide(d_seg, BF16), tall(F32),
                wide(2 * d_seg, BF16), wide(d_seg, F32), wide(d_seg, F32), wide(d_seg, F32)]
    return pl.pallas_call(
        functools.partial(_inproj_kernel, head_rows=head_rows),
        grid=(n // tm,),
        in_specs=[pl.BlockSpec((tm, d_model), lambda i: (i, 0)),
                  full(ng), full(w_bf), full(gq), full(gk), full(gavg)],
        out_specs=[o[1] for o in outs],
        out_shape=[o[0] for o in outs],
        compiler_params=pltpu.CompilerParams(
            dimension_semantics=("arbitrary",), vmem_limit_bytes=VMEM_LIMIT_BYTES),
        name="inproj_samples" if head_rows else "inproj_prompt",
    )(x, ng, w_bf, gq, gk, gavg)


def _diff_epilogue(o1, o2, lam, sg, ga):
    od = o1 - lam * o2
    ms = jnp.mean(od * od, axis=-1, keepdims=True)
    return od * lax.rsqrt(ms + EPS) * sg * _silu(ga)


def _attn_kernel(lam_ref, q1_ref, q2_ref, k_ref, v_ref, d_ref, ga_ref, sg_ref, o_ref, s_scr, m_scr, acc_scr):
    tq = q1_ref.shape[0]
    tk = d_ref.shape[2]
    qi = pl.program_id(2)
    qs = jnp.concatenate([q1_ref[...], q2_ref[...]], axis=0)

    m_scr[...] = jnp.full(m_scr.shape, NEG_INF, F32)
    acc_scr[...] = jnp.zeros(acc_scr.shape, F32)

    def produce(kj, bias, slot):
        start = pl.multiple_of(kj * tk, tk)
        s = lax.dot_general(qs, k_ref[pl.ds(start, tk), :], (((1,), (1,)), ((), ())),
                            preferred_element_type=F32)
        if bias is not None:
            s = s + jnp.concatenate([bias, bias], axis=0)
        s_scr[slot] = s

    def consume(kj, slot):
        start = pl.multiple_of(kj * tk, tk)
        s = s_scr[slot]
        m_old = m_scr[...]
        m_new = jnp.maximum(m_old, jnp.max(s, axis=-1, keepdims=True))
        p = jnp.exp2(s - jnp.concatenate([m_new] * (tk // LANES), axis=1))
        alpha = jnp.exp2(m_old - m_new)
        pv = jnp.dot(p.astype(BF16), v_ref[pl.ds(start, tk), :], preferred_element_type=F32)
        acc_scr[...] = acc_scr[...] * jnp.concatenate([alpha] * (acc_scr.shape[1] // LANES), axis=1) + pv
        m_scr[...] = m_new

    produce(qi, d_ref[0], 0)

    @pl.when(qi >= 1)
    def _():
        produce(qi - 1, d_ref[1], 1)
        consume(qi, 0)

    n_far = jnp.maximum(qi - 1, 0)

    def pair_body(i, carry):
        kj = qi - 2 - 2 * i
        produce(kj, None, 0)
        consume(kj + 1, 1)
        produce(kj - 1, None, 1)
        consume(kj, 0)
        return carry

    lax.fori_loop(0, n_far // 2, pair_body, 0)
    odd = n_far % 2 == 1

    @pl.when(odd)
    def _():
        produce(0, None, 0)
        consume(1, 1)
        consume(0, 0)

    @pl.when(jnp.logical_and(qi >= 1, jnp.logical_not(odd)))
    def _():
        consume(0, 1)

    @pl.when(qi == 0)
    def _():
        consume(0, 0)

    acc = acc_scr[...]
    o = acc[:, :V_DIM] / acc[:, V_DIM:]
    out = _diff_epilogue(o[:tq], o[tq:], lam_ref[0], sg_ref[...], ga_ref[...])
    o_ref[...] = out.astype(o_ref.dtype)


def _prompt_attention(lam, q1, q2, kb, va, dtiles, ga, sg):
    b, l, d_attn = q1.shape
    n_heads = d_attn // V_DIM
    tq = ATTN_TQ
    return pl.pallas_call(
        _attn_kernel,
        grid=(b, n_heads, l // tq),
        in_specs=[
            pl.BlockSpec(memory_space=pltpu.SMEM),
            pl.BlockSpec((None, tq, V_DIM), lambda bi, h, qi: (bi, qi, h)),
            pl.BlockSpec((None, tq, V_DIM), lambda bi, h, qi: (bi, qi, h)),
            pl.BlockSpec((None, l, V_DIM), lambda bi, h, qi: (bi, 0, h)),
            pl.BlockSpec((None, l, 2 * V_DIM), lambda bi, h, qi: (bi, 0, h)),
            pl.BlockSpec((None, 2, tq, ATTN_TK), lambda bi, h, qi: (h, 0, 0, 0)),
            pl.BlockSpec((None, tq, V_DIM), lambda bi, h, qi: (bi, qi, h)),
            pl.BlockSpec((1, V_DIM), lambda bi, h, qi: (0, 0)),
        ],
        out_specs=pl.BlockSpec((None, tq, V_DIM), lambda bi, h, qi: (bi, qi, h)),
        out_shape=jax.ShapeDtypeStruct((b, l, d_attn), BF16),
        scratch_shapes=[pltpu.VMEM((2, 2 * tq, ATTN_TK), F32), pltpu.VMEM((2 * tq, LANES), F32),
                        pltpu.VMEM((2 * tq, 2 * V_DIM), F32)],
        compiler_params=pltpu.CompilerParams(
            dimension_semantics=("arbitrary", "arbitrary", "arbitrary"),
            vmem_limit_bytes=VMEM_LIMIT_BYTES),
        name="prompt_attention",
    )(lam, q1, q2, kb, va, dtiles, ga, sg)


def _decode_kernel(pt_ref, lam_ref, q1_ref, q2_ref, kn_ref, vn_ref, ga_ref, bias_ref, sg_ref,
                   ck_hbm, cv_hbm, o_ref, kbuf, vbuf, knew, vnew, ksem, vsem):
    b = pl.program_id(0)
    nb = pl.num_programs(0)
    n_pages = kbuf.shape[1]
    page_rows = kbuf.shape[2]
    new_rows = q1_ref.shape[0]
    slot = b % 2

    def k_copy(seq, s, j):
        return pltpu.make_async_copy(ck_hbm.at[pt_ref[seq, j]], kbuf.at[s, j], ksem.at[s])

    def v_copy(seq, s, j):
        return pltpu.make_async_copy(cv_hbm.at[pt_ref[seq, j]], vbuf.at[s, j], vsem.at[s])

    def start_fetch(seq, s):
        for j in range(n_pages):
            k_copy(seq, s, j).start()
            v_copy(seq, s, j).start()

    @pl.when(b == 0)
    def _():
        start_fetch(0, 0)
        knew[...] = jnp.zeros(knew.shape, knew.dtype)
        vnew[...] = jnp.zeros(vnew.shape, vnew.dtype)

    @pl.when(b + 1 < nb)
    def _():
        start_fetch(b + 1, 1 - slot)

    qx = jnp.concatenate([q1_ref[...], q2_ref[...]], axis=0).astype(BF16)
    knew[0:new_rows, :] = kn_ref[...].astype(BF16)
    vnew[0:new_rows, :] = vn_ref[...].astype(BF16)

    for j in range(n_pages):
        k_copy(b, slot, j).wait()
        v_copy(b, slot, j).wait()

    nt = (((1,), (1,)), ((), ()))
    s_tiles = [lax.dot_general(qx, kbuf[slot, j].astype(BF16), nt, preferred_element_type=F32)
               for j in range(n_pages)]
    s_tiles.append(lax.dot_general(qx, knew[...], nt, preferred_element_type=F32))
    s = jnp.concatenate(s_tiles, axis=1) + bias_ref[...]
    m = jnp.max(s, axis=-1, keepdims=True)
    p = jnp.exp2(s - m)
    l_sum = jnp.sum(p, axis=-1, keepdims=True)
    pb = p.astype(BF16)
    acc = jnp.dot(pb[:, n_pages * page_rows:], vnew[...], preferred_element_type=F32)
    for j in range(n_pages):
        acc = acc + jnp.dot(pb[:, j * page_rows:(j + 1) * page_rows], vbuf[slot, j].astype(BF16),
                            preferred_element_type=F32)
    o = acc / l_sum
    o_ref[...] = _diff_epilogue(o[:new_rows], o[new_rows:], lam_ref[0], sg_ref[...], ga_ref[...])


def _decode_attention(page_table, lam, q1, q2, kn, vn, ga, bias, sg, cache_k, cache_v, new_rows):
    n_seq, n_pages = page_table.shape
    page_rows = cache_k.shape[1]
    seq_spec = pl.BlockSpec((new_rows, V_DIM), lambda bi, pt: (bi, 0))
    grid_spec = pltpu.PrefetchScalarGridSpec(
        num_scalar_prefetch=1,
        grid=(n_seq,),
        in_specs=[
            pl.BlockSpec(memory_space=pltpu.SMEM),
            seq_spec, seq_spec, seq_spec, seq_spec, seq_spec,
            pl.BlockSpec(bias.shape, lambda bi, pt: (0, 0)),
            pl.BlockSpec((1, V_DIM), lambda bi, pt: (0, 0)),
            pl.BlockSpec(memory_space=pl.ANY),
            pl.BlockSpec(memory_space=pl.ANY),
        ],
        out_specs=seq_spec,
        scratch_shapes=[
            pltpu.VMEM((2, n_pages, page_rows, V_DIM), cache_k.dtype),
            pltpu.VMEM((2, n_pages, page_rows, V_DIM), cache_v.dtype),
            pltpu.VMEM((page_rows, V_DIM), BF16),
            pltpu.VMEM((page_rows, V_DIM), BF16),
            pltpu.SemaphoreType.DMA((2,)),
            pltpu.SemaphoreType.DMA((2,)),
        ],
    )
    return pl.pallas_call(
        _decode_kernel,
        grid_spec=grid_spec,
        out_shape=jax.ShapeDtypeStruct((n_seq * new_rows, V_DIM), F32),
        compiler_params=pltpu.CompilerParams(
            dimension_semantics=("arbitrary",), vmem_limit_bytes=VMEM_LIMIT_BYTES),
        name="decode_attention",
    )(page_table, lam, q1, q2, kn, vn, ga, bias, sg, cache_k, cache_v)


def _ssm_tail(y, u, gs, dskip_ref, wglu_ref, bglu_ref):
    z = _gelu_tanh(y + dskip_ref[...] * u)
    gate = jax.nn.sigmoid(jnp.dot(z.astype(BF16), wglu_ref[...], preferred_element_type=F32) + bglu_ref[...])
    return z * gate * _silu(gs)


def _ssm_prompt_kernel(u_ref, gs_ref, bw_ref, cw_ref, pin_re_ref, pin_im_ref, pout_re_ref, pout_im_ref,
                       abar_ref, tri_ref, dskip_ref, wglu_ref, bglu_ref, o_ref, hfin_ref, carry_scr):
    t = u_ref.shape[0]
    n_q = bw_ref.shape[0]
    sc = SSM_LANE_CHUNK
    c = pl.program_id(1)

    @pl.when(c == 0)
    def _():
        carry_scr[...] = jnp.zeros(carry_scr.shape, F32)

    u = u_ref[...]
    tri = tri_ref[...]
    ys = []
    for q in range(n_q):
        uq = u[:, q * LANES:(q + 1) * LANES].astype(BF16)
        bu = jnp.dot(uq, bw_ref[q], preferred_element_type=F32)
        br, bi = bu[:, :sc], bu[:, sc:]
        lanes = slice(q * sc, (q + 1) * sc)
        pir, pii = pin_re_ref[:, lanes], pin_im_ref[:, lanes]
        x = jnp.concatenate([br * pir - bi * pii, br * pii + bi * pir], axis=1)
        hi, lo = _split_bf16(x)
        cs = (jnp.dot(tri, hi, preferred_element_type=F32)
              + jnp.dot(tri, lo, preferred_element_type=F32)
              + carry_scr[:, 2 * q * sc:2 * (q + 1) * sc])
        sr, si = cs[:, :sc], cs[:, sc:]
        por, poi = pout_re_ref[:, lanes], pout_im_ref[:, lanes]
        hr = sr * por - si * poi
        hm = sr * poi + si * por
        lr, lm = hr[t - 1:t, :], hm[t - 1:t, :]
        ar, am = abar_ref[0:1, lanes], abar_ref[1:2, lanes]
        hfin_ref[:, 2 * q * sc:(2 * q + 1) * sc] = lr
        hfin_ref[:, (2 * q + 1) * sc:2 * (q + 1) * sc] = lm
        carry_scr[:, 2 * q * sc:(2 * q + 1) * sc] = ar * lr - am * lm
        carry_scr[:, (2 * q + 1) * sc:2 * (q + 1) * sc] = ar * lm + am * lr
        h = jnp.concatenate([hr, hm], axis=1).astype(BF16)
        ys.append(jnp.dot(h, cw_ref[q], preferred_element_type=F32))
    y = jnp.concatenate(ys, axis=1)
    o_ref[...] = _ssm_tail(y, u, gs_ref[...], dskip_ref, wglu_ref, bglu_ref).astype(o_ref.dtype)


def _ssm_prompt(u, gs, sp):
    b, l, d_ssm = u.shape
    t = SSM_CHUNK
    n_state2 = sp["bw"].shape[0] * sp["bw"].shape[2]
    row = pl.BlockSpec((None, t, d_ssm), lambda bi, ci: (bi, ci, 0))
    full = lambda a: pl.BlockSpec(a.shape, lambda bi, ci: (0,) * a.ndim)
    names = ["bw", "cw", "pin_re", "pin_im", "pout_re", "pout_im", "abar", "tri", "dskip", "wglu", "bglu"]
    return pl.pallas_call(
        _ssm_prompt_kernel,
        grid=(b, l // t),
        in_specs=[row, row] + [full(sp[n]) for n in names],
        out_specs=[row, pl.BlockSpec((None, 1, n_state2), lambda bi, ci: (bi, 0, 0))],
        out_shape=[jax.ShapeDtypeStruct((b, l, d_ssm), BF16),
                   jax.ShapeDtypeStruct((b, 1, n_state2), F32)],
        scratch_shapes=[pltpu.VMEM((1, n_state2), F32)],
        compiler_params=pltpu.CompilerParams(
            dimension_semantics=("arbitrary", "arbitrary"), vmem_limit_bytes=VMEM_LIMIT_BYTES),
        name="ssm_prompt",
    )(u, gs, *[sp[n] for n in names])


def _ssm_sample_kernel(u_ref, gs_ref, h0_ref, bw_ref, cw_ref, abar_ref, dskip_ref, wglu_ref, bglu_ref,
                       o_ref, hfin_ref, bu_scr, h_scr):
    n_seq = h0_ref.shape[0]
    dec_seq = u_ref.shape[0] // n_seq
    n_q = bw_ref.shape[0]
    sc = SSM_LANE_CHUNK
    n_lc = 2 * sc // LANES
    u = u_ref[...]
    ys = []
    for q in range(n_q):
        uq = u[:, q * LANES:(q + 1) * LANES].astype(BF16)
        bu = jnp.dot(uq, bw_ref[q], preferred_element_type=F32)
        for c in range(n_lc):
            bu_scr[c] = bu[:, c * LANES:(c + 1) * LANES]
        lanes = slice(q * sc, (q + 1) * sc)
        ar, am = abar_ref[0:1, lanes], abar_ref[1:2, lanes]
        hr = h0_ref[:, 2 * q * sc:(2 * q + 1) * sc]
        hm = h0_ref[:, (2 * q + 1) * sc:2 * (q + 1) * sc]
        for step in range(dec_seq):
            rows = pl.ds(step, n_seq, stride=dec_seq)
            b_all = jnp.concatenate([bu_scr[c, rows, :] for c in range(n_lc)], axis=1)
            br, bi = b_all[:, :sc], b_all[:, sc:]
            hr, hm = ar * hr - am * hm + br, ar * hm + am * hr + bi
            for c in range(n_lc // 2):
                h_scr[c, rows, :] = hr[:, c * LANES:(c + 1) * LANES]
                h_scr[n_lc // 2 + c, rows, :] = hm[:, c * LANES:(c + 1) * LANES]
        hfin_ref[:, 2 * q * sc:(2 * q + 1) * sc] = hr
        hfin_ref[:, (2 * q + 1) * sc:2 * (q + 1) * sc] = hm
        h_all = jnp.concatenate([h_scr[c] for c in range(n_lc)], axis=1)
        ys.append(jnp.dot(h_all.astype(BF16), cw_ref[q], preferred_element_type=F32))
    y = jnp.concatenate(ys, axis=1)
    o_ref[...] = _ssm_tail(y, u, gs_ref[...], dskip_ref, wglu_ref, bglu_ref).astype(o_ref.dtype)


def _ssm_sample(u, gs, h0, sp):
    n, d_ssm = u.shape
    n_seq, n_state2 = h0.shape
    names = ["bw", "cw", "abar", "dskip", "wglu", "bglu"]
    args = [u, gs, h0] + [sp[k] for k in names]
    full = lambda a: pl.BlockSpec(a.shape, lambda i: (0,) * a.ndim)
    return pl.pallas_call(
        _ssm_sample_kernel,
        grid=(1,),
        in_specs=[full(a) for a in args],
        out_specs=[pl.BlockSpec((n, d_ssm), lambda i: (0, 0)),
                   pl.BlockSpec((n_seq, n_state2), lambda i: (0, 0))],
        out_shape=[jax.ShapeDtypeStruct((n, d_ssm), BF16),
                   jax.ShapeDtypeStruct((n_seq, n_state2), F32)],
        scratch_shapes=[pltpu.VMEM((2 * SSM_LANE_CHUNK // LANES, n, LANES), F32),
                        pltpu.VMEM((2 * SSM_LANE_CHUNK // LANES, n, LANES), F32)],
        compiler_params=pltpu.CompilerParams(
            dimension_semantics=("arbitrary",), vmem_limit_bytes=VMEM_LIMIT_BYTES),
        name="ssm_sample",
    )(*args)


def _outproj_kernel(x_ref, oa_ref, os_ref, w_ref, y_ref):
    d_a = oa_ref.shape[1]
    y = x_ref[...] + jnp.dot(oa_ref[...], w_ref[:d_a, :], preferred_element_type=F32)
    y_ref[...] = y + jnp.dot(os_ref[...], w_ref[d_a:, :], preferred_element_type=F32)


def _outproj(x, oa, os_, w_bf):
    n, d_model = x.shape
    tm = min(ROW_TILE, n)
    row = lambda w: pl.BlockSpec((tm, w), lambda i: (i, 0))
    return pl.pallas_call(
        _outproj_kernel,
        grid=(n // tm,),
        in_specs=[row(d_model), row(oa.shape[1]), row(os_.shape[1]),
                  pl.BlockSpec(w_bf.shape, lambda i: (0, 0))],
        out_specs=row(d_model),
        out_shape=jax.ShapeDtypeStruct((n, d_model), F32),
        compiler_params=pltpu.CompilerParams(
            dimension_semantics=("arbitrary",), vmem_limit_bytes=VMEM_LIMIT_BYTES),
        name="outproj",
    )(x, oa, os_, w_bf)


def _ssm_params(a_re, a_im, log_dt, b_re, b_im, c_re, c_im, d_skip, w_glu, b_glu, chunk):
    n_groups, n_state = a_re.shape
    g_per_q = LANES // SSM_GROUP
    n_q = n_groups // g_per_q
    dt = jnp.exp(log_dt.astype(F32))[:, None]
    a_re = a_re.astype(F32)
    a_im = a_im.astype(F32)
    mag = jnp.exp(a_re * dt)
    abar_re = mag * jnp.cos(a_im * dt)
    abar_im = mag * jnp.sin(a_im * dt)
    nr = abar_re - 1.0
    den = a_re * a_re + a_im * a_im
    coef_re = (nr * a_re + abar_im * a_im) / den
    coef_im = (abar_im * a_re - nr * a_im) / den
    b_re = b_re.astype(F32)
    b_im = b_im.astype(F32)
    bbar_re = coef_re[..., None] * b_re - coef_im[..., None] * b_im
    bbar_im = coef_re[..., None] * b_im + coef_im[..., None] * b_re

    eye = jnp.eye(g_per_q, dtype=F32)

    def in_blocks(bb):
        bb = bb.reshape(n_q, g_per_q, n_state, SSM_GROUP)
        m = jnp.einsum("qgpc,gh->qgchp", bb, eye)
        return m.reshape(n_q, LANES, g_per_q * n_state)

    def out_blocks(cc):
        cc = cc.reshape(n_q, g_per_q, SSM_GROUP, n_state)
        m = jnp.einsum("qgcp,gh->qgphc", cc, eye)
        return m.reshape(n_q, g_per_q * n_state, LANES)

    bw = jnp.concatenate([in_blocks(bbar_re), in_blocks(bbar_im)], axis=2).astype(BF16)
    cw = jnp.concatenate([out_blocks(c_re.astype(F32)), -out_blocks(c_im.astype(F32))], axis=1).astype(BF16)

    ar = abar_re.reshape(1, -1)
    ai = abar_im.reshape(1, -1)
    sp = {
        "bw": bw, "cw": cw,
        "abar": jnp.concatenate([ar, ai], axis=0),
        "dskip": d_skip.astype(F32).reshape(1, -1),
        "wglu": w_glu.astype(BF16),
        "bglu": b_glu.astype(F32).reshape(1, -1),
    }
    pr, pi = jnp.ones_like(ar), jnp.zeros_like(ar)
    cr, ci = ar, ai
    while pr.shape[0] < chunk:
        pr, pi = (jnp.concatenate([pr, pr * cr - pi * ci], axis=0),
                  jnp.concatenate([pi, pr * ci + pi * cr], axis=0))
        cr, ci = cr * cr - ci * ci, 2.0 * cr * ci
    assert pr.shape[0] == chunk
    inv = 1.0 / (pr * pr + pi * pi)
    sp.update({"pout_re": pr, "pout_im": pi, "pin_re": pr * inv, "pin_im": -pi * inv,
               "tri": jnp.asarray(np.tril(np.ones((chunk, chunk), np.float32)), BF16)})
    return sp


def _state_to_lanes(h_re, h_im):
    b = h_re.shape[0]
    sc = SSM_LANE_CHUNK
    r = h_re.astype(F32).reshape(b, -1, 1, sc)
    i = h_im.astype(F32).reshape(b, -1, 1, sc)
    return jnp.concatenate([r, i], axis=2).reshape(b, -1)


def _lanes_to_state(h, n_groups, n_state):
    b = h.shape[0]
    h = h.reshape(b, -1, 2, SSM_LANE_CHUNK)
    return (h[:, :, 0, :].reshape(b, n_groups, n_state), h[:, :, 1, :].reshape(b, n_groups, n_state))


def _toeplitz(v, n):
    h = v.shape[0]
    x = jnp.broadcast_to(v[:, None, :], (h, n, 2 * n)).reshape(h, 2 * n * n)
    return x[:, :n * (2 * n - 1)].reshape(h, n, 2 * n - 1)[:, :, :n]


def _prompt_bias_tiles(fvec, tq, tk):
    n = LANES
    h = fvec.shape[0]
    neg = jnp.full((h, n - 1), NEG_INF, F32)
    va = jnp.concatenate([fvec[:, 0:1], neg, jnp.zeros((h, 1), F32), fvec[:, 1:n][:, ::-1]], axis=1)
    vb = jnp.concatenate([fvec[:, 1:n + 1][:, ::-1], jnp.zeros((h, n), F32)], axis=1)
    blk_a = _toeplitz(va, n)
    blk_b = _toeplitz(vb, n)
    tiles = []
    for t in range(2):
        delta = (np.arange(tq // n)[:, None] - np.arange(tk // n)[None, :]) + t * (tk // n)
        dmap = jnp.asarray(np.kron(delta, np.ones((n, n), np.int32)))[None]
        ta = jnp.tile(blk_a, (1, tq // n, tk // n))
        tb = jnp.tile(blk_b, (1, tq // n, tk // n))
        tiles.append(jnp.where(dmap == 0, ta, jnp.where(dmap == 1, tb, jnp.where(dmap < 0, NEG_INF, 0.0))))
    return jnp.stack(tiles, axis=1)


def _decode_bias(fvec, n_pages, page, dec_seq, n_heads):
    h = fvec.shape[0]
    rows = []
    for i in range(dec_seq):
        last = fvec[:, i + 1:i + 1 + page][:, ::-1]
        new = jnp.concatenate([fvec[:, 0:i + 1][:, ::-1], jnp.full((h, page - i - 1), NEG_INF, F32)], axis=1)
        rows.append(jnp.concatenate([jnp.zeros((h, (n_pages - 1) * page), F32), last, new], axis=1))
    per_head = jnp.stack(rows, axis=0)
    same = jnp.asarray(np.eye(n_heads, dtype=bool))[None, :, None, :]
    full = jnp.where(same, per_head[:, :, :, None], NEG_INF)
    full = full.reshape(dec_seq * n_heads, -1)
    return jnp.concatenate([full, full], axis=0)


def kernel(x_prompt, x_sample, cache_k, cache_v, state_ssm_re, state_ssm_im, page_table,
           norm_g, w_in, q_norm_g, k_norm_g, lambda_q1, lambda_k1, lambda_q2, lambda_k2,
           subln_g, rel_bias, ssm_a_re, ssm_a_im, ssm_log_dt, ssm_b_re, ssm_b_im,
           ssm_c_re, ssm_c_im, ssm_d, w_glu, b_glu, w_out):
    batch, seq, d_model = x_prompt.shape
    dec_batch, dec_seq, _ = x_sample.shape
    depth, n_pool, page, n_heads, _ = cache_k.shape
    n_pages = page_table.shape[1]
    d_attn = n_heads * V_DIM
    n_groups, n_state = ssm_a_re.shape[1:]
    new_rows = dec_seq * n_heads

    buckets = _bucket_table(2 * LANES)
    far_from = int(np.max(np.nonzero(buckets < N_BUCKETS - 1)[0])) + 1
    assert far_from <= LANES and _bucket_table(seq + page * n_pages)[far_from:].min() == N_BUCKETS - 1
    assert page == LANES and dec_seq < LANES and ATTN_TQ == ATTN_TK

    rel_bias = rel_bias.astype(F32)
    fvec = (rel_bias[buckets].T - rel_bias[N_BUCKETS - 1][:, None]) * LOG2E
    fvec = jnp.where(jnp.asarray(np.arange(2 * LANES) < far_from)[None], fvec, 0.0)
    dtiles = _prompt_bias_tiles(fvec, ATTN_TQ, ATTN_TK)
    dbias = _decode_bias(fvec, n_pages, page, dec_seq, n_heads)

    group_avg = jnp.asarray(np.kron(np.eye(d_attn // QK_DIM), np.full((QK_DIM, QK_DIM), 1.0 / QK_DIM)), BF16)
    n_rep = d_attn // QK_DIM
    cache_k_rows = cache_k.reshape(depth * n_pool, page * n_heads, V_DIM)
    cache_v_rows = cache_v.reshape(depth * n_pool, page * n_heads, V_DIM)

    hp = x_prompt.reshape(batch * seq, d_model)
    hs = x_sample.reshape(dec_batch * dec_seq, d_model)
    kp_l, vp_l, ks_l, vs_l = [], [], [], []
    srp_l, sip_l, srs_l, sis_l = [], [], [], []
    for l in range(depth):
        lam_init = _lambda_init(l)
        lam = (jnp.exp(jnp.sum(lambda_q1[l].astype(F32) * lambda_k1[l].astype(F32)))
               - jnp.exp(jnp.sum(lambda_q2[l].astype(F32) * lambda_k2[l].astype(F32))) + lam_init)
        lam = lam.reshape(1).astype(F32)
        ng = norm_g[l].astype(F32).reshape(1, d_model)
        w_bf = w_in[l].astype(BF16)
        gq = jnp.tile(q_norm_g[l].astype(F32), n_rep).reshape(1, d_attn) * (QK_DIM ** -0.5 * LOG2E)
        gk = jnp.tile(k_norm_g[l].astype(F32), n_rep).reshape(1, d_attn)
        sg = (subln_g[l].astype(F32) * (1.0 - lam_init)).reshape(1, V_DIM)
        wo_bf = w_out[l].astype(BF16)
        sp = _ssm_params(ssm_a_re[l], ssm_a_im[l], ssm_log_dt[l], ssm_b_re[l], ssm_b_im[l],
                         ssm_c_re[l], ssm_c_im[l], ssm_d[l], w_glu[l], b_glu[l], SSM_CHUNK)

        q1, q2, k4, kb, v4, va, ga, u, gs = _inproj(hp, ng, w_bf, gq, gk, group_avg, head_rows=False)
        r3 = lambda a: a.reshape(batch, seq, a.shape[-1])
        o_a = _prompt_attention(lam, r3(q1), r3(q2), r3(kb), r3(va), dtiles, r3(ga), sg)
        o_s, hfin = _ssm_prompt(r3(u), r3(gs), sp)
        hp = _outproj(hp, o_a.reshape(batch * seq, d_attn), o_s.reshape(batch * seq, -1), wo_bf)
        kp_l.append(k4.reshape(batch, seq, n_heads, V_DIM).astype(cache_k.dtype))
        vp_l.append(v4.reshape(batch, seq, n_heads, V_DIM).astype(cache_v.dtype))
        hr_p, hi_p = _lanes_to_state(hfin.reshape(batch, -1), n_groups, n_state)
        srp_l.append(hr_p.astype(state_ssm_re.dtype))
        sip_l.append(hi_p.astype(state_ssm_im.dtype))

        q1, q2, k4, v4, ga, u, gs = _inproj(hs, ng, w_bf, gq, gk, group_avg, head_rows=True)
        o_a = _decode_attention(page_table + l * n_pool, lam, q1, q2, k4, v4, ga, dbias, sg,
                                cache_k_rows, cache_v_rows, new_rows)
        h0 = _state_to_lanes(state_ssm_re[l], state_ssm_im[l])
        o_s, hfin = _ssm_sample(u, gs, h0, sp)
        hs = _outproj(hs, o_a.reshape(dec_batch * dec_seq, d_attn).astype(BF16), o_s, wo_bf)
        ks_l.append(k4.reshape(dec_batch, dec_seq, n_heads, V_DIM).astype(cache_k.dtype))
        vs_l.append(v4.reshape(dec_batch, dec_seq, n_heads, V_DIM).astype(cache_v.dtype))
        hr_s, hi_s = _lanes_to_state(hfin, n_groups, n_state)
        srs_l.append(hr_s.astype(state_ssm_re.dtype))
        sis_l.append(hi_s.astype(state_ssm_im.dtype))

    y_prompt = hp.reshape(batch, seq, d_model).astype(x_prompt.dtype)
    y_sample = hs.reshape(dec_batch, dec_seq, d_model).astype(x_sample.dtype)
    return (y_prompt, y_sample, jnp.stack(kp_l), jnp.stack(vp_l), jnp.stack(ks_l), jnp.stack(vs_l),
            jnp.stack(srp_l), jnp.stack(sip_l), jnp.stack(srs_l), jnp.stack(sis_l))
```

```python
import functools
import math

import numpy as np
import jax
import jax.numpy as jnp
from jax import lax
from jax.experimental import pallas as pl
from jax.experimental.pallas import tpu as pltpu

F32 = jnp.float32
BF16 = jnp.bfloat16

QK_DIM = 64
V_DIM = 2 * QK_DIM
N_BUCKETS = 32
MAX_DISTANCE = 128
SSM_GROUP = 16
SSM_STATE = 64
EPS = 1e-6
NEG_INF = -1e30
LOG2E = math.log2(math.e)

LANES = 128
VMEM_LIMIT_BYTES = 56 * 1024 * 1024

ROW_TILE = 512
ATTN_TQ = 512
ATTN_TK = 512
SSM_CHUNK = 256
SSM_LANE_CHUNK = 512


def _lambda_init(layer):
    return 0.8 - 0.6 * math.exp(-0.3 * layer)


def _bucket_table(n_max):
    n = np.arange(n_max)
    max_exact = N_BUCKETS // 2
    nf = np.maximum(n, 1).astype(np.float32)
    large = max_exact + (np.log(nf / np.float32(max_exact)) / np.float32(math.log(MAX_DISTANCE / max_exact))
                         * np.float32(N_BUCKETS - max_exact)).astype(np.int32)
    large = np.minimum(large, N_BUCKETS - 1)
    return np.where(n < max_exact, n, large).astype(np.int32)


def _silu(x):
    return x * jax.nn.sigmoid(x)


def _gelu_tanh(x):
    return 0.5 * x * (1.0 + jnp.tanh(math.sqrt(2.0 / math.pi) * (x + 0.044715 * (x * x * x))))


def _store_head_rows(ref, val, n_heads):
    rows = val.shape[0]
    for h in range(n_heads):
        ref[pl.ds(h, rows, stride=n_heads), :] = val[:, h * V_DIM:(h + 1) * V_DIM]


def _inproj_kernel(x_ref, ng_ref, w_ref, gq_ref, gk_ref, gavg_ref, *out_refs, head_rows):
    x = x_ref[...]
    ms = jnp.mean(x * x, axis=-1, keepdims=True)
    xb = (x * lax.rsqrt(ms + EPS) * ng_ref[...]).astype(BF16)
    d_seg = gq_ref.shape[1]
    n_heads = d_seg // V_DIM

    def seg(i):
        return jnp.dot(xb, w_ref[:, i * d_seg:(i + 1) * d_seg], preferred_element_type=F32)

    def group_norm(t, g):
        msq = jnp.dot((t * t).astype(BF16), gavg_ref[...], preferred_element_type=F32)
        return t * lax.rsqrt(msq + EPS) * g

    q = group_norm(seg(0), gq_ref[...])
    lane = lax.broadcasted_iota(jnp.int32, q.shape, 1)
    first = (lane % V_DIM) < QK_DIM
    qa = jnp.where(first, q, 0.0)
    qb = jnp.where(first, 0.0, q)
    k = group_norm(seg(1), gk_ref[...])
    v = seg(2)
    ga = seg(3)
    if head_rows:
        q1_ref, q2_ref, k4_ref, v4_ref, ga_ref, u_ref, gs_ref = out_refs
        _store_head_rows(q1_ref, qa, n_heads)
        _store_head_rows(q2_ref, qb, n_heads)
        _store_head_rows(ga_ref, ga, n_heads)
    else:
        q1_ref, q2_ref, k4_ref, kb_ref, v4_ref, va_ref, ga_ref, u_ref, gs_ref = out_refs
        q1_ref[...] = qa.astype(BF16)
        q2_ref[...] = qb.astype(BF16)
        kb_ref[...] = k.astype(BF16)
        vb = v.astype(BF16)
        ones = jnp.ones((v.shape[0], V_DIM), BF16)
        pieces = []
        for h in range(n_heads):
            pieces += [vb[:, h * V_DIM:(h + 1) * V_DIM], ones]
        va_ref[...] = jnp.concatenate(pieces, axis=1)
        ga_ref[...] = ga
    _store_head_rows(k4_ref, k, n_heads)
    _store_head_rows(v4_ref, v, n_heads)
    u_ref[...] = seg(4)
    gs_ref[...] = seg(5)


def _inproj(x, ng, w_bf, gq, gk, gavg, head_rows):
    n, d_model = x.shape
    d_seg = gq.shape[1]
    n_heads = d_seg // V_DIM
    tm = min(ROW_TILE, n)
    full = lambda a: pl.BlockSpec(a.shape, lambda i: (0,) * a.ndim)
    wide = lambda w, dt: (jax.ShapeDtypeStruct((n, w), dt), pl.BlockSpec((tm, w), lambda i: (i, 0)))
    tall = lambda dt: (jax.ShapeDtypeStruct((n * n_heads, V_DIM), dt),
                       pl.BlockSpec((tm * n_heads, V_DIM), lambda i: (i, 0)))
    if head_rows:
        outs = [tall(F32), tall(F32), tall(F32), tall(F32), tall(F32), wide(d_seg, F32), wide(d_seg, F32)]
    else:
        outs = [wide(d_seg, BF16), wide(d_seg, BF16), tall(F32), wide(d_seg, BF16), tall(F32),
                wide(2 * d_seg, BF16), wide(d_seg, F32), wide(d_seg, F32), wide(d_seg, F32)]
    return pl.pallas_call(
        functools.partial(_inproj_kernel, head_rows=head_rows),
        grid=(n // tm,),
        in_specs=[pl.BlockSpec((tm, d_model), lambda i: (i, 0)),
                  full(ng), full(w_bf), full(gq), full(gk), full(gavg)],
        out_specs=[o[1] for o in outs],
        out_shape=[o[0] for o in outs],
        compiler_params=pltpu.CompilerParams(
            dimension_semantics=("arbitrary",), vmem_limit_bytes=VMEM_LIMIT_BYTES),
        name="inproj_samples" if head_rows else "inproj_prompt",
    )(x, ng, w_bf, gq, gk, gavg)


def _diff_epilogue(o1, o2, lam, sg, ga):
    od = o1 - lam * o2
    ms = jnp.mean(od * od, axis=-1, keepdims=True)
    return od * lax.rsqrt(ms + EPS) * sg * _silu(ga)


def _attn_kernel(lam_ref, q1_ref, q2_ref, k_ref, v_ref, d_ref, ga_ref, sg_ref, o_ref, s_scr, m_scr, acc_scr):
    tq = q1_ref.shape[0]
    tk = d_ref.shape[2]
    qi = pl.program_id(2)
    qs = jnp.concatenate([q1_ref[...], q2_ref[...]], axis=0)

    m_scr[...] = jnp.full(m_scr.shape, NEG_INF, F32)
    acc_scr[...] = jnp.zeros(acc_scr.shape, F32)

    def produce(kj, bias, slot):
        start = pl.multiple_of(kj * tk, tk)
        s = lax.dot_general(qs, k_ref[pl.ds(start, tk), :], (((1,), (1,)), ((), ())),
                            preferred_element_type=F32)
        if bias is not None:
            s = s + jnp.concatenate([bias, bias], axis=0)
        s_scr[slot] = s

    def consume(kj, slot):
        start = pl.multiple_of(kj * tk, tk)
        s = s_scr[slot]
        m_old = m_scr[...]
        m_new = jnp.maximum(m_old, jnp.max(s, axis=-1, keepdims=True))
        p = jnp.exp2(s - jnp.concatenate([m_new] * (tk // LANES), axis=1))
        alpha = jnp.exp2(m_old - m_new)
        pv = jnp.dot(p.astype(BF16), v_ref[pl.ds(start, tk), :], preferred_element_type=F32)
        acc_scr[...] = acc_scr[...] * jnp.concatenate([alpha] * (acc_scr.shape[1] // LANES), axis=1) + pv
        m_scr[...] = m_new

    produce(qi, d_ref[0], 0)

    @pl.when(qi >= 1)
    def _():
        produce(qi - 1, d_ref[1], 1)
        consume(qi, 0)

    n_far = jnp.maximum(qi - 1, 0)

    def pair_body(i, carry):
        kj = qi - 2 - 2 * i
        produce(kj, None, 0)
        consume(kj + 1, 1)
        produce(kj - 1, None, 1)
        consume(kj, 0)
        return carry

    lax.fori_loop(0, n_far // 2, pair_body, 0)
    odd = n_far % 2 == 1

    @pl.when(odd)
    def _():
        produce(0, None, 0)
        consume(1, 1)
        consume(0, 0)

    @pl.when(jnp.logical_and(qi >= 1, jnp.logical_not(odd)))
    def _():
        consume(0, 1)

    @pl.when(qi == 0)
    def _():
        consume(0, 0)

    acc = acc_scr[...]
    o = acc[:, :V_DIM] / acc[:, V_DIM:]
    out = _diff_epilogue(o[:tq], o[tq:], lam_ref[0], sg_ref[...], ga_ref[...])
    o_ref[...] = out.astype(o_ref.dtype)


def _prompt_attention(lam, q1, q2, kb, va, dtiles, ga, sg):
    b, l, d_attn = q1.shape
    n_heads = d_attn // V_DIM
    tq = ATTN_TQ
    return pl.pallas_call(
        _attn_kernel,
        grid=(b, n_heads, l // tq),
        in_specs=[
            pl.BlockSpec(memory_space=pltpu.SMEM),
            pl.BlockSpec((None, tq, V_DIM), lambda bi, h, qi: (bi, qi, h)),
            pl.BlockSpec((None, tq, V_DIM), lambda bi, h, qi: (bi, qi, h)),
            pl.BlockSpec((None, l, V_DIM), lambda bi, h, qi: (bi, 0, h)),
            pl.BlockSpec((None, l, 2 * V_DIM), lambda bi, h, qi: (bi, 0, h)),
            pl.BlockSpec((None, 2, tq, ATTN_TK), lambda bi, h, qi: (h, 0, 0, 0)),
            pl.BlockSpec((None, tq, V_DIM), lambda bi, h, qi: (bi, qi, h)),
            pl.BlockSpec((1, V_DIM), lambda bi, h, qi: (0, 0)),
        ],
        out_specs=pl.BlockSpec((None, tq, V_DIM), lambda bi, h, qi: (bi, qi, h)),
        out_shape=jax.ShapeDtypeStruct((b, l, d_attn), BF16),
        scratch_shapes=[pltpu.VMEM((2, 2 * tq, ATTN_TK), F32), pltpu.VMEM((2 * tq, LANES), F32),
                        pltpu.VMEM((2 * tq, 2 * V_DIM), F32)],
        compiler_params=pltpu.CompilerParams(
            dimension_semantics=("arbitrary", "arbitrary", "arbitrary"),
            vmem_limit_bytes=VMEM_LIMIT_BYTES),
        name="prompt_attention",
    )(lam, q1, q2, kb, va, dtiles, ga, sg)


def _decode_kernel(pt_ref, lam_ref, q1_ref, q2_ref, kn_ref, vn_ref, ga_ref, hm_ref, near_ref, sg_ref,
                   ck_hbm, cv_hbm, o_ref, kbuf, vbuf, knew, vnew, ksem, vsem):
    b = pl.program_id(0)
    nb = pl.num_programs(0)
    n_pages = kbuf.shape[1]
    page_rows = kbuf.shape[2]
    new_rows = q1_ref.shape[0]
    slot = b % 2

    def k_copy(seq, s, j):
        return pltpu.make_async_copy(ck_hbm.at[pt_ref[seq, j]], kbuf.at[s, j], ksem.at[s])

    def v_copy(seq, s, j):
        return pltpu.make_async_copy(cv_hbm.at[pt_ref[seq, j]], vbuf.at[s, j], vsem.at[s])

    def start_fetch(seq, s):
        for j in range(n_pages):
            k_copy(seq, s, j).start()
            v_copy(seq, s, j).start()

    @pl.when(b == 0)
    def _():
        start_fetch(0, 0)
        knew[...] = jnp.zeros(knew.shape, knew.dtype)
        vnew[...] = jnp.zeros(vnew.shape, vnew.dtype)

    @pl.when(b + 1 < nb)
    def _():
        start_fetch(b + 1, 1 - slot)

    qx = jnp.concatenate([q1_ref[...], q2_ref[...]], axis=0).astype(BF16)
    knew[0:new_rows, :] = kn_ref[...].astype(BF16)
    vnew[0:new_rows, :] = vn_ref[...].astype(BF16)

    for j in range(n_pages):
        k_copy(b, slot, j).wait()
        v_copy(b, slot, j).wait()

    nt = (((1,), (1,)), ((), ()))
    s_tiles = [lax.dot_general(qx, kbuf[slot, j].astype(BF16), nt, preferred_element_type=F32)
               for j in range(n_pages)]
    s_tiles.append(lax.dot_general(qx, knew[...], nt, preferred_element_type=F32))
    head_mask = hm_ref[...]
    s = jnp.concatenate([t + head_mask for t in s_tiles[:n_pages - 1]]
                        + [jnp.concatenate(s_tiles[n_pages - 1:], axis=1) + near_ref[...]],
                        axis=1)
    m = jnp.max(s, axis=-1, keepdims=True)
    p = jnp.exp2(s - m)
    l_sum = jnp.sum(p, axis=-1, keepdims=True)
    pb = p.astype(BF16)
    acc = jnp.dot(pb[:, n_pages * page_rows:], vnew[...], preferred_element_type=F32)
    for j in range(n_pages):
        acc = acc + jnp.dot(pb[:, j * page_rows:(j + 1) * page_rows], vbuf[slot, j].astype(BF16),
                            preferred_element_type=F32)
    o = acc / l_sum
    o_ref[...] = _diff_epilogue(o[:new_rows], o[new_rows:], lam_ref[0], sg_ref[...], ga_ref[...])


def _decode_attention(page_table, lam, q1, q2, kn, vn, ga, head_mask, near_bias, sg, cache_k, cache_v,
                      new_rows):
    n_seq, n_pages = page_table.shape
    page_rows = cache_k.shape[1]
    seq_spec = pl.BlockSpec((new_rows, V_DIM), lambda bi, pt: (bi, 0))
    grid_spec = pltpu.PrefetchScalarGridSpec(
        num_scalar_prefetch=1,
        grid=(n_seq,),
        in_specs=[
            pl.BlockSpec(memory_space=pltpu.SMEM),
            seq_spec, seq_spec, seq_spec, seq_spec, seq_spec,
            pl.BlockSpec(head_mask.shape, lambda bi, pt: (0, 0)),
            pl.BlockSpec(near_bias.shape, lambda bi, pt: (0, 0)),
            pl.BlockSpec((1, V_DIM), lambda bi, pt: (0, 0)),
            pl.BlockSpec(memory_space=pl.ANY),
            pl.BlockSpec(memory_space=pl.ANY),
        ],
        out_specs=seq_spec,
        scratch_shapes=[
            pltpu.VMEM((2, n_pages, page_rows, V_DIM), cache_k.dtype),
            pltpu.VMEM((2, n_pages, page_rows, V_DIM), cache_v.dtype),
            pltpu.VMEM((page_rows, V_DIM), BF16),
            pltpu.VMEM((page_rows, V_DIM), BF16),
            pltpu.SemaphoreType.DMA((2,)),
            pltpu.SemaphoreType.DMA((2,)),
        ],
    )
    return pl.pallas_call(
        _decode_kernel,
        grid_spec=grid_spec,
        out_shape=jax.ShapeDtypeStruct((n_seq * new_rows, V_DIM), F32),
        compiler_params=pltpu.CompilerParams(
            dimension_semantics=("arbitrary",), vmem_limit_bytes=VMEM_LIMIT_BYTES),
        name="decode_attention",
    )(page_table, lam, q1, q2, kn, vn, ga, head_mask, near_bias, sg, cache_k, cache_v)


def _ssm_tail(y, u, gs, dskip_ref, wglu_ref, bglu_ref):
    z = _gelu_tanh(y + dskip_ref[...] * u)
    gate = jax.nn.sigmoid(jnp.dot(z.astype(BF16), wglu_ref[...], preferred_element_type=F32) + bglu_ref[...])
    return z * gate * _silu(gs)


def _ssm_prompt_kernel(u_ref, gs_ref, bw_ref, cw_ref, pin_re_ref, pin_im_ref, pout_re_ref, pout_im_ref,
                       abar_ref, tri_ref, dskip_ref, wglu_ref, bglu_ref, o_ref, hfin_ref, carry_scr):
    t = u_ref.shape[0]
    n_q = bw_ref.shape[0]
    sc = SSM_LANE_CHUNK
    c = pl.program_id(1)

    @pl.when(c == 0)
    def _():
        carry_scr[...] = jnp.zeros(carry_scr.shape, F32)

    u = u_ref[...]
    tri = tri_ref[...]
    ys = []
    for q in range(n_q):
        uq = u[:, q * LANES:(q + 1) * LANES].astype(BF16)
        bu = jnp.dot(uq, bw_ref[q], preferred_element_type=F32)
        br, bi = bu[:, :sc], bu[:, sc:]
        lanes = slice(q * sc, (q + 1) * sc)
        pir, pii = pin_re_ref[:, lanes], pin_im_ref[:, lanes]
        x = jnp.concatenate([br * pir - bi * pii, br * pii + bi * pir], axis=1)
        cs = (jnp.dot(tri, x.astype(BF16), preferred_element_type=F32)
              + carry_scr[:, 2 * q * sc:2 * (q + 1) * sc])
        sr, si = cs[:, :sc], cs[:, sc:]
        por, poi = pout_re_ref[:, lanes], pout_im_ref[:, lanes]
        hr = sr * por - si * poi
        hm = sr * poi + si * por
        lr, lm = hr[t - 1:t, :], hm[t - 1:t, :]
        ar, am = abar_ref[0:1, lanes], abar_ref[1:2, lanes]
        hfin_ref[:, 2 * q * sc:(2 * q + 1) * sc] = lr
        hfin_ref[:, (2 * q + 1) * sc:2 * (q + 1) * sc] = lm
        carry_scr[:, 2 * q * sc:(2 * q + 1) * sc] = ar * lr - am * lm
        carry_scr[:, (2 * q + 1) * sc:2 * (q + 1) * sc] = ar * lm + am * lr
        h = jnp.concatenate([hr, hm], axis=1).astype(BF16)
        ys.append(jnp.dot(h, cw_ref[q], preferred_element_type=F32))
    y = jnp.concatenate(ys, axis=1)
    o_ref[...] = _ssm_tail(y, u, gs_ref[...], dskip_ref, wglu_ref, bglu_ref).astype(o_ref.dtype)


def _ssm_prompt(u, gs, sp):
    b, l, d_ssm = u.shape
    t = SSM_CHUNK
    n_state2 = sp["bw"].shape[0] * sp["bw"].shape[2]
    row = pl.BlockSpec((None, t, d_ssm), lambda bi, ci: (bi, ci, 0))
    full = lambda a: pl.BlockSpec(a.shape, lambda bi, ci: (0,) * a.ndim)
    names = ["bw", "cw", "pin_re", "pin_im", "pout_re", "pout_im", "abar", "tri", "dskip", "wglu", "bglu"]
    return pl.pallas_call(
        _ssm_prompt_kernel,
        grid=(b, l // t),
        in_specs=[row, row] + [full(sp[n]) for n in names],
        out_specs=[row, pl.BlockSpec((None, 1, n_state2), lambda bi, ci: (bi, 0, 0))],
        out_shape=[jax.ShapeDtypeStruct((b, l, d_ssm), BF16),
                   jax.ShapeDtypeStruct((b, 1, n_state2), F32)],
        scratch_shapes=[pltpu.VMEM((1, n_state2), F32)],
        compiler_params=pltpu.CompilerParams(
            dimension_semantics=("arbitrary", "arbitrary"), vmem_limit_bytes=VMEM_LIMIT_BYTES),
        name="ssm_prompt",
    )(u, gs, *[sp[n] for n in names])


def _ssm_sample_kernel(u_ref, gs_ref, h0_ref, bw_ref, cw_ref, abar_ref, dskip_ref, wglu_ref, bglu_ref,
                       o_ref, hfin_ref, bu_scr, h_scr):
    n_seq = h0_ref.shape[0]
    dec_seq = u_ref.shape[0] // n_seq
    n_q = bw_ref.shape[0]
    sc = SSM_LANE_CHUNK
    n_lc = 2 * sc // LANES
    u = u_ref[...]
    ys = []
    for q in range(n_q):
        uq = u[:, q * LANES:(q + 1) * LANES].astype(BF16)
        bu = jnp.dot(uq, bw_ref[q], preferred_element_type=F32)
        for c in range(n_lc):
            bu_scr[c] = bu[:, c * LANES:(c + 1) * LANES]
        lanes = slice(q * sc, (q + 1) * sc)
        ar, am = abar_ref[0:1, lanes], abar_ref[1:2, lanes]
        hr = h0_ref[:, 2 * q * sc:(2 * q + 1) * sc]
        hm = h0_ref[:, (2 * q + 1) * sc:2 * (q + 1) * sc]
        for step in range(dec_seq):
            rows = pl.ds(step, n_seq, stride=dec_seq)
            b_all = jnp.concatenate([bu_scr[c, rows, :] for c in range(n_lc)], axis=1)
            br, bi = b_all[:, :sc], b_all[:, sc:]
            hr, hm = ar * hr - am * hm + br, ar * hm + am * hr + bi
            for c in range(n_lc // 2):
                h_scr[c, rows, :] = hr[:, c * LANES:(c + 1) * LANES]
                h_scr[n_lc // 2 + c, rows, :] = hm[:, c * LANES:(c + 1) * LANES]
        hfin_ref[:, 2 * q * sc:(2 * q + 1) * sc] = hr
        hfin_ref[:, (2 * q + 1) * sc:2 * (q + 1) * sc] = hm
        h_all = jnp.concatenate([h_scr[c] for c in range(n_lc)], axis=1)
        ys.append(jnp.dot(h_all.astype(BF16), cw_ref[q], preferred_element_type=F32))
    y = jnp.concatenate(ys, axis=1)
    o_ref[...] = _ssm_tail(y, u, gs_ref[...], dskip_ref, wglu_ref, bglu_ref).astype(o_ref.dtype)


def _ssm_sample(u, gs, h0, sp):
    n, d_ssm = u.shape
    n_seq, n_state2 = h0.shape
    names = ["bw", "cw", "abar", "dskip", "wglu", "bglu"]
    args = [u, gs, h0] + [sp[k] for k in names]
    full = lambda a: pl.BlockSpec(a.shape, lambda i: (0,) * a.ndim)
    return pl.pallas_call(
        _ssm_sample_kernel,
        grid=(1,),
        in_specs=[full(a) for a in args],
        out_specs=[pl.BlockSpec((n, d_ssm), lambda i: (0, 0)),
                   pl.BlockSpec((n_seq, n_state2), lambda i: (0, 0))],
        out_shape=[jax.ShapeDtypeStruct((n, d_ssm), BF16),
                   jax.ShapeDtypeStruct((n_seq, n_state2), F32)],
        scratch_shapes=[pltpu.VMEM((2 * SSM_LANE_CHUNK // LANES, n, LANES), F32),
                        pltpu.VMEM((2 * SSM_LANE_CHUNK // LANES, n, LANES), F32)],
        compiler_params=pltpu.CompilerParams(
            dimension_semantics=("arbitrary",), vmem_limit_bytes=VMEM_LIMIT_BYTES),
        name="ssm_sample",
    )(*args)


def _outproj_kernel(x_ref, oa_ref, os_ref, w_ref, y_ref):
    d_a = oa_ref.shape[1]
    y = x_ref[...] + jnp.dot(oa_ref[...], w_ref[:d_a, :], preferred_element_type=F32)
    y_ref[...] = y + jnp.dot(os_ref[...], w_ref[d_a:, :], preferred_element_type=F32)


def _outproj(x, oa, os_, w_bf):
    n, d_model = x.shape
    tm = min(ROW_TILE, n)
    row = lambda w: pl.BlockSpec((tm, w), lambda i: (i, 0))
    return pl.pallas_call(
        _outproj_kernel,
        grid=(n // tm,),
        in_specs=[row(d_model), row(oa.shape[1]), row(os_.shape[1]),
                  pl.BlockSpec(w_bf.shape, lambda i: (0, 0))],
        out_specs=row(d_model),
        out_shape=jax.ShapeDtypeStruct((n, d_model), F32),
        compiler_params=pltpu.CompilerParams(
            dimension_semantics=("arbitrary",), vmem_limit_bytes=VMEM_LIMIT_BYTES),
        name="outproj",
    )(x, oa, os_, w_bf)


def _ssm_params(a_re, a_im, log_dt, b_re, b_im, c_re, c_im, d_skip, w_glu, b_glu, chunk):
    n_groups, n_state = a_re.shape
    g_per_q = LANES // SSM_GROUP
    n_q = n_groups // g_per_q
    dt = jnp.exp(log_dt.astype(F32))[:, None]
    a_re = a_re.astype(F32)
    a_im = a_im.astype(F32)
    mag = jnp.exp(a_re * dt)
    abar_re = mag * jnp.cos(a_im * dt)
    abar_im = mag * jnp.sin(a_im * dt)
    nr = abar_re - 1.0
    den = a_re * a_re + a_im * a_im
    coef_re = (nr * a_re + abar_im * a_im) / den
    coef_im = (abar_im * a_re - nr * a_im) / den
    b_re = b_re.astype(F32)
    b_im = b_im.astype(F32)
    bbar_re = coef_re[..., None] * b_re - coef_im[..., None] * b_im
    bbar_im = coef_re[..., None] * b_im + coef_im[..., None] * b_re

    eye = jnp.eye(g_per_q, dtype=F32)

    def in_blocks(bb):
        bb = bb.reshape(n_q, g_per_q, n_state, SSM_GROUP)
        m = jnp.einsum("qgpc,gh->qgchp", bb, eye)
        return m.reshape(n_q, LANES, g_per_q * n_state)

    def out_blocks(cc):
        cc = cc.reshape(n_q, g_per_q, SSM_GROUP, n_state)
        m = jnp.einsum("qgcp,gh->qgphc", cc, eye)
        return m.reshape(n_q, g_per_q * n_state, LANES)

    bw = jnp.concatenate([in_blocks(bbar_re), in_blocks(bbar_im)], axis=2).astype(BF16)
    cw = jnp.concatenate([out_blocks(c_re.astype(F32)), -out_blocks(c_im.astype(F32))], axis=1).astype(BF16)

    ar = abar_re.reshape(1, -1)
    ai = abar_im.reshape(1, -1)
    sp = {
        "bw": bw, "cw": cw,
        "abar": jnp.concatenate([ar, ai], axis=0),
        "dskip": d_skip.astype(F32).reshape(1, -1),
        "wglu": w_glu.astype(BF16),
        "bglu": b_glu.astype(F32).reshape(1, -1),
    }
    pr, pi = jnp.ones_like(ar), jnp.zeros_like(ar)
    cr, ci = ar, ai
    while pr.shape[0] < chunk:
        pr, pi = (jnp.concatenate([pr, pr * cr - pi * ci], axis=0),
                  jnp.concatenate([pi, pr * ci + pi * cr], axis=0))
        cr, ci = cr * cr - ci * ci, 2.0 * cr * ci
    assert pr.shape[0] == chunk
    inv = 1.0 / (pr * pr + pi * pi)
    sp.update({"pout_re": pr, "pout_im": pi, "pin_re": pr * inv, "pin_im": -pi * inv,
               "tri": jnp.asarray(np.tril(np.ones((chunk, chunk), np.float32)), BF16)})
    return sp


def _state_to_lanes(h_re, h_im):
    b = h_re.shape[0]
    sc = SSM_LANE_CHUNK
    r = h_re.astype(F32).reshape(b, -1, 1, sc)
    i = h_im.astype(F32).reshape(b, -1, 1, sc)
    return jnp.concatenate([r, i], axis=2).reshape(b, -1)


def _lanes_to_state(h, n_groups, n_state):
    b = h.shape[0]
    h = h.reshape(b, -1, 2, SSM_LANE_CHUNK)
    return (h[:, :, 0, :].reshape(b, n_groups, n_state), h[:, :, 1, :].reshape(b, n_groups, n_state))


def _toeplitz(v, n):
    h = v.shape[0]
    x = jnp.broadcast_to(v[:, None, :], (h, n, 2 * n)).reshape(h, 2 * n * n)
    return x[:, :n * (2 * n - 1)].reshape(h, n, 2 * n - 1)[:, :, :n]


def _prompt_bias_tiles(fvec, tq, tk):
    n = LANES
    h = fvec.shape[0]
    neg = jnp.full((h, n - 1), NEG_INF, F32)
    va = jnp.concatenate([fvec[:, 0:1], neg, jnp.zeros((h, 1), F32), fvec[:, 1:n][:, ::-1]], axis=1)
    vb = jnp.concatenate([fvec[:, 1:n + 1][:, ::-1], jnp.zeros((h, n), F32)], axis=1)
    blk_a = _toeplitz(va, n)
    blk_b = _toeplitz(vb, n)
    tiles = []
    for t in range(2):
        delta = (np.arange(tq // n)[:, None] - np.arange(tk // n)[None, :]) + t * (tk // n)
        dmap = jnp.asarray(np.kron(delta, np.ones((n, n), np.int32)))[None]
        ta = jnp.tile(blk_a, (1, tq // n, tk // n))
        tb = jnp.tile(blk_b, (1, tq // n, tk // n))
        tiles.append(jnp.where(dmap == 0, ta, jnp.where(dmap == 1, tb, jnp.where(dmap < 0, NEG_INF, 0.0))))
    return jnp.stack(tiles, axis=1)


def _decode_bias(fvec, page, dec_seq, n_heads):
    h = fvec.shape[0]
    rows = []
    for i in range(dec_seq):
        last = fvec[:, i + 1:i + 1 + page][:, ::-1]
        new = jnp.concatenate([fvec[:, 0:i + 1][:, ::-1], jnp.full((h, page - i - 1), NEG_INF, F32)], axis=1)
        rows.append(jnp.concatenate([last, new], axis=1))
    per_head = jnp.stack(rows, axis=0)
    same = np.eye(n_heads, dtype=bool)[None, :, None, :]
    near = jnp.where(jnp.asarray(same), per_head[:, :, :, None], NEG_INF)
    near = near.reshape(dec_seq * n_heads, -1)
    mask = np.where(np.broadcast_to(same, (dec_seq, n_heads, page, n_heads)), 0.0, NEG_INF)
    mask = mask.reshape(dec_seq * n_heads, -1).astype(np.float32)
    return jnp.asarray(np.concatenate([mask, mask], axis=0)), jnp.concatenate([near, near], axis=0)


def kernel(x_prompt, x_sample, cache_k, cache_v, state_ssm_re, state_ssm_im, page_table,
           norm_g, w_in, q_norm_g, k_norm_g, lambda_q1, lambda_k1, lambda_q2, lambda_k2,
           subln_g, rel_bias, ssm_a_re, ssm_a_im, ssm_log_dt, ssm_b_re, ssm_b_im,
           ssm_c_re, ssm_c_im, ssm_d, w_glu, b_glu, w_out):
    batch, seq, d_model = x_prompt.shape
    dec_batch, dec_seq, _ = x_sample.shape
    depth, n_pool, page, n_heads, _ = cache_k.shape
    n_pages = page_table.shape[1]
    d_attn = n_heads * V_DIM
    n_groups, n_state = ssm_a_re.shape[1:]
    new_rows = dec_seq * n_heads

    buckets = _bucket_table(2 * LANES)
    far_from = int(np.max(np.nonzero(buckets < N_BUCKETS - 1)[0])) + 1
    assert far_from <= LANES and _bucket_table(seq + page * n_pages)[far_from:].min() == N_BUCKETS - 1
    assert page == LANES and dec_seq < LANES and ATTN_TQ == ATTN_TK

    rel_bias = rel_bias.astype(F32)
    fvec = (rel_bias[buckets].T - rel_bias[N_BUCKETS - 1][:, None]) * LOG2E
    fvec = jnp.where(jnp.asarray(np.arange(2 * LANES) < far_from)[None], fvec, 0.0)
    dtiles = _prompt_bias_tiles(fvec, ATTN_TQ, ATTN_TK)
    head_mask, near_bias = _decode_bias(fvec, page, dec_seq, n_heads)

    group_avg = jnp.asarray(np.kron(np.eye(d_attn // QK_DIM), np.full((QK_DIM, QK_DIM), 1.0 / QK_DIM)), BF16)
    n_rep = d_attn // QK_DIM
    cache_k_rows = cache_k.reshape(depth * n_pool, page * n_heads, V_DIM)
    cache_v_rows = cache_v.reshape(depth * n_pool, page * n_heads, V_DIM)

    hp = x_prompt.reshape(batch * seq, d_model)
    hs = x_sample.reshape(dec_batch * dec_seq, d_model)
    kp_l, vp_l, ks_l, vs_l = [], [], [], []
    srp_l, sip_l, srs_l, sis_l = [], [], [], []
    for l in range(depth):
        lam_init = _lambda_init(l)
        lam = (jnp.exp(jnp.sum(lambda_q1[l].astype(F32) * lambda_k1[l].astype(F32)))
               - jnp.exp(jnp.sum(lambda_q2[l].astype(F32) * lambda_k2[l].astype(F32))) + lam_init)
        lam = lam.reshape(1).astype(F32)
        ng = norm_g[l].astype(F32).reshape(1, d_model)
        w_bf = w_in[l].astype(BF16)
        gq = jnp.tile(q_norm_g[l].astype(F32), n_rep).reshape(1, d_attn) * (QK_DIM ** -0.5 * LOG2E)
        gk = jnp.tile(k_norm_g[l].astype(F32), n_rep).reshape(1, d_attn)
        sg = (subln_g[l].astype(F32) * (1.0 - lam_init)).reshape(1, V_DIM)
        wo_bf = w_out[l].astype(BF16)
        sp = _ssm_params(ssm_a_re[l], ssm_a_im[l], ssm_log_dt[l], ssm_b_re[l], ssm_b_im[l],
                         ssm_c_re[l], ssm_c_im[l], ssm_d[l], w_glu[l], b_glu[l], SSM_CHUNK)

        q1, q2, k4, kb, v4, va, ga, u, gs = _inproj(hp, ng, w_bf, gq, gk, group_avg, head_rows=False)
        r3 = lambda a: a.reshape(batch, seq, a.shape[-1])
        o_a = _prompt_attention(lam, r3(q1), r3(q2), r3(kb), r3(va), dtiles, r3(ga), sg)
        o_s, hfin = _ssm_prompt(r3(u), r3(gs), sp)
        hp = _outproj(hp, o_a.reshape(batch * seq, d_attn), o_s.reshape(batch * seq, -1), wo_bf)
        kp_l.append(k4.reshape(batch, seq, n_heads, V_DIM).astype(cache_k.dtype))
        vp_l.append(v4.reshape(batch, seq, n_heads, V_DIM).astype(cache_v.dtype))
        hr_p, hi_p = _lanes_to_state(hfin.reshape(batch, -1), n_groups, n_state)
        srp_l.append(hr_p.astype(state_ssm_re.dtype))
        sip_l.append(hi_p.astype(state_ssm_im.dtype))

        q1, q2, k4, v4, ga, u, gs = _inproj(hs, ng, w_bf, gq, gk, group_avg, head_rows=True)
        o_a = _decode_attention(page_table + l * n_pool, lam, q1, q2, k4, v4, ga, head_mask, near_bias, sg,
                                cache_k_rows, cache_v_rows, new_rows)
        h0 = _state_to_lanes(state_ssm_re[l], state_ssm_im[l])
        o_s, hfin = _ssm_sample(u, gs, h0, sp)
        hs = _outproj(hs, o_a.reshape(dec_batch * dec_seq, d_attn).astype(BF16), o_s, wo_bf)
        ks_l.append(k4.reshape(dec_batch, dec_seq, n_heads, V_DIM).astype(cache_k.dtype))
        vs_l.append(v4.reshape(dec_batch, dec_seq, n_heads, V_DIM).astype(cache_v.dtype))
        hr_s, hi_s = _lanes_to_state(hfin, n_groups, n_state)
        srs_l.append(hr_s.astype(state_ssm_re.dtype))
        sis_l.append(hi_s.astype(state_ssm_im.dtype))

    y_prompt = hp.reshape(batch, seq, d_model).astype(x_prompt.dtype)
    y_sample = hs.reshape(dec_batch, dec_seq, d_model).astype(x_sample.dtype)
    return (y_prompt, y_sample, jnp.stack(kp_l), jnp.stack(vp_l), jnp.stack(ks_l), jnp.stack(vs_l),
            jnp.stack(srp_l), jnp.stack(sip_l), jnp.stack(srs_l), jnp.stack(sis_l))
```

```python
import functools
import math

import numpy as np
import jax
import jax.numpy as jnp
from jax import lax
from jax.experimental import pallas as pl
from jax.experimental.pallas import tpu as pltpu

F32 = jnp.float32
BF16 = jnp.bfloat16

QK_DIM = 64
V_DIM = 2 * QK_DIM
N_BUCKETS = 32
MAX_DISTANCE = 128
SSM_GROUP = 16
SSM_STATE = 64
EPS = 1e-6
NEG_INF = -1e30
LOG2E = math.log2(math.e)

LANES = 128
VMEM_LIMIT_BYTES = 56 * 1024 * 1024

ROW_TILE = 512
ATTN_TQ = 512
ATTN_TK = 512
SSM_CHUNK = 256
SSM_LANE_CHUNK = 512


def _lambda_init(layer):
    return 0.8 - 0.6 * math.exp(-0.3 * layer)


def _bucket_table(n_max):
    n = np.arange(n_max)
    max_exact = N_BUCKETS // 2
    nf = np.maximum(n, 1).astype(np.float32)
    large = max_exact + (np.log(nf / np.float32(max_exact)) / np.float32(math.log(MAX_DISTANCE / max_exact))
                         * np.float32(N_BUCKETS - max_exact)).astype(np.int32)
    large = np.minimum(large, N_BUCKETS - 1)
    return np.where(n < max_exact, n, large).astype(np.int32)


def _silu(x):
    return x * jax.nn.sigmoid(x)


def _gelu_tanh(x):
    return 0.5 * x * (1.0 + jnp.tanh(math.sqrt(2.0 / math.pi) * (x + 0.044715 * (x * x * x))))


def _store_head_rows(ref, val, n_heads):
    rows = val.shape[0]
    for h in range(n_heads):
        ref[pl.ds(h, rows, stride=n_heads), :] = val[:, h * V_DIM:(h + 1) * V_DIM]


def _inproj_kernel(x_ref, ng_ref, w_ref, gq_ref, gk_ref, gavg_ref, *out_refs, head_rows):
    x = x_ref[...]
    ms = jnp.mean(x * x, axis=-1, keepdims=True)
    xb = (x * lax.rsqrt(ms + EPS) * ng_ref[...]).astype(BF16)
    d_seg = gq_ref.shape[1]
    n_heads = d_seg // V_DIM

    def seg(i):
        return jnp.dot(xb, w_ref[:, i * d_seg:(i + 1) * d_seg], preferred_element_type=F32)

    def group_norm(t, g):
        msq = jnp.dot((t * t).astype(BF16), gavg_ref[...], preferred_element_type=F32)
        return t * lax.rsqrt(msq + EPS) * g

    q = group_norm(seg(0), gq_ref[...])
    lane = lax.broadcasted_iota(jnp.int32, q.shape, 1)
    first = (lane % V_DIM) < QK_DIM
    qa = jnp.where(first, q, 0.0)
    qb = jnp.where(first, 0.0, q)
    k = group_norm(seg(1), gk_ref[...])
    v = seg(2)
    ga = seg(3)
    if head_rows:
        q1_ref, q2_ref, k4_ref, v4_ref, ga_ref, u_ref, gs_ref = out_refs
        _store_head_rows(q1_ref, qa, n_heads)
        _store_head_rows(q2_ref, qb, n_heads)
        _store_head_rows(ga_ref, ga, n_heads)
    else:
        q1_ref, q2_ref, k4_ref, kb_ref, v4_ref, va_ref, ga_ref, u_ref, gs_ref = out_refs
        q1_ref[...] = qa.astype(BF16)
        q2_ref[...] = qb.astype(BF16)
        kb_ref[...] = k.astype(BF16)
        vb = v.astype(BF16)
        ones = jnp.ones((v.shape[0], V_DIM), BF16)
        pieces = []
        for h in range(n_heads):
            pieces += [vb[:, h * V_DIM:(h + 1) * V_DIM], ones]
        va_ref[...] = jnp.concatenate(pieces, axis=1)
        ga_ref[...] = ga
    _store_head_rows(k4_ref, k, n_heads)
    _store_head_rows(v4_ref, v, n_heads)
    u_ref[...] = seg(4)
    gs_ref[...] = seg(5)


def _inproj(x, ng, w_bf, gq, gk, gavg, head_rows):
    n, d_model = x.shape
    d_seg = gq.shape[1]
    n_heads = d_seg // V_DIM
    tm = min(ROW_TILE, n)
    full = lambda a: pl.BlockSpec(a.shape, lambda i: (0,) * a.ndim)
    wide = lambda w, dt: (jax.ShapeDtypeStruct((n, w), dt), pl.BlockSpec((tm, w), lambda i: (i, 0)))
    tall = lambda dt: (jax.ShapeDtypeStruct((n * n_heads, V_DIM), dt),
                       pl.BlockSpec((tm * n_heads, V_DIM), lambda i: (i, 0)))
    if head_rows:
        outs = [tall(F32), tall(F32), tall(F32), tall(F32), tall(F32), wide(d_seg, F32), wide(d_seg, F32)]
    else:
        outs = [wide(d_seg, BF16), wide(d_seg, BF16), tall(F32), wide(d_seg, BF16), tall(F32),
                wide(2 * d_seg, BF16), wide(d_seg, F32), wide(d_seg, F32), wide(d_seg, F32)]
    return pl.pallas_call(
        functools.partial(_inproj_kernel, head_rows=head_rows),
        grid=(n // tm,),
        in_specs=[pl.BlockSpec((tm, d_model), lambda i: (i, 0)),
                  full(ng), full(w_bf), full(gq), full(gk), full(gavg)],
        out_specs=[o[1] for o in outs],
        out_shape=[o[0] for o in outs],
        compiler_params=pltpu.CompilerParams(
            dimension_semantics=("arbitrary",), vmem_limit_bytes=VMEM_LIMIT_BYTES),
        name="inproj_samples" if head_rows else "inproj_prompt",
    )(x, ng, w_bf, gq, gk, gavg)


def _diff_epilogue(o1, o2, lam, sg, ga):
    od = o1 - lam * o2
    ms = jnp.mean(od * od, axis=-1, keepdims=True)
    return od * lax.rsqrt(ms + EPS) * sg * _silu(ga)


def _attn_kernel(lam_ref, q1_ref, q2_ref, k_ref, v_ref, ab_ref, ga_ref, sg_ref, o_ref,
                 s_scr, d_scr, m_scr, acc_scr):
    tq = q1_ref.shape[0]
    tk = s_scr.shape[2]
    qi = pl.program_id(2)

    @pl.when(qi == 0)
    def _():
        for t in range(2):
            for a in range(tq // LANES):
                for b in range(tk // LANES):
                    delta = a - b + t * (tk // LANES)
                    if delta in (0, 1):
                        blk = ab_ref[delta]
                    else:
                        blk = jnp.full((LANES, LANES), NEG_INF if delta < 0 else 0.0, F32)
                    d_scr[t, a * LANES:(a + 1) * LANES, b * LANES:(b + 1) * LANES] = blk

    qs = jnp.concatenate([q1_ref[...], q2_ref[...]], axis=0)

    m_scr[...] = jnp.full(m_scr.shape, NEG_INF, F32)
    acc_scr[...] = jnp.zeros(acc_scr.shape, F32)

    def produce(kj, bias, slot):
        start = pl.multiple_of(kj * tk, tk)
        s = lax.dot_general(qs, k_ref[pl.ds(start, tk), :], (((1,), (1,)), ((), ())),
                            preferred_element_type=F32)
        if bias is not None:
            s = s + jnp.concatenate([bias, bias], axis=0)
        s_scr[slot] = s

    def consume(kj, slot):
        start = pl.multiple_of(kj * tk, tk)
        s = s_scr[slot]
        m_old = m_scr[...]
        m_new = jnp.maximum(m_old, jnp.max(s, axis=-1, keepdims=True))
        p = jnp.exp2(s - jnp.concatenate([m_new] * (tk // LANES), axis=1))
        alpha = jnp.exp2(m_old - m_new)
        pv = jnp.dot(p.astype(BF16), v_ref[pl.ds(start, tk), :], preferred_element_type=F32)
        acc_scr[...] = acc_scr[...] * jnp.concatenate([alpha] * (acc_scr.shape[1] // LANES), axis=1) + pv
        m_scr[...] = m_new

    produce(qi, d_scr[0], 0)

    @pl.when(qi >= 1)
    def _():
        produce(qi - 1, d_scr[1], 1)
        consume(qi, 0)

    n_far = jnp.maximum(qi - 1, 0)

    def pair(kj):
        produce(kj, None, 0)
        consume(kj + 1, 1)
        produce(kj - 1, None, 1)
        consume(kj, 0)

    def quad_body(i, carry):
        kj = qi - 2 - 4 * i
        pair(kj)
        pair(kj - 2)
        return carry

    n_quad = n_far // 4
    lax.fori_loop(0, n_quad, quad_body, 0)

    @pl.when(n_far % 4 >= 2)
    def _():
        pair(qi - 2 - 4 * n_quad)

    odd = n_far % 2 == 1

    @pl.when(odd)
    def _():
        produce(0, None, 0)
        consume(1, 1)
        consume(0, 0)

    @pl.when(jnp.logical_and(qi >= 1, jnp.logical_not(odd)))
    def _():
        consume(0, 1)

    @pl.when(qi == 0)
    def _():
        consume(0, 0)

    acc = acc_scr[...]
    o = acc[:, :V_DIM] / acc[:, V_DIM:]
    out = _diff_epilogue(o[:tq], o[tq:], lam_ref[0], sg_ref[...], ga_ref[...])
    o_ref[...] = out.astype(o_ref.dtype)


def _prompt_attention(lam, q1, q2, kb, va, dtiles, ga, sg):
    b, l, d_attn = q1.shape
    n_heads = d_attn // V_DIM
    tq = ATTN_TQ
    return pl.pallas_call(
        _attn_kernel,
        grid=(b, n_heads, l // tq),
        in_specs=[
            pl.BlockSpec(memory_space=pltpu.SMEM),
            pl.BlockSpec((None, tq, V_DIM), lambda bi, h, qi: (bi, qi, h)),
            pl.BlockSpec((None, tq, V_DIM), lambda bi, h, qi: (bi, qi, h)),
            pl.BlockSpec((None, l, V_DIM), lambda bi, h, qi: (bi, 0, h)),
            pl.BlockSpec((None, l, 2 * V_DIM), lambda bi, h, qi: (bi, 0, h)),
            pl.BlockSpec((None, 2, LANES, LANES), lambda bi, h, qi: (h, 0, 0, 0)),
            pl.BlockSpec((None, tq, V_DIM), lambda bi, h, qi: (bi, qi, h)),
            pl.BlockSpec((1, V_DIM), lambda bi, h, qi: (0, 0)),
        ],
        out_specs=pl.BlockSpec((None, tq, V_DIM), lambda bi, h, qi: (bi, qi, h)),
        out_shape=jax.ShapeDtypeStruct((b, l, d_attn), BF16),
        scratch_shapes=[pltpu.VMEM((2, 2 * tq, ATTN_TK), F32), pltpu.VMEM((2, tq, ATTN_TK), F32),
                        pltpu.VMEM((2 * tq, LANES), F32),
                        pltpu.VMEM((2 * tq, 2 * V_DIM), F32)],
        compiler_params=pltpu.CompilerParams(
            dimension_semantics=("arbitrary", "arbitrary", "arbitrary"),
            vmem_limit_bytes=VMEM_LIMIT_BYTES),
        name="prompt_attention",
    )(lam, q1, q2, kb, va, dtiles, ga, sg)


def _decode_kernel(pt_ref, lam_ref, q1_ref, q2_ref, kn_ref, vn_ref, ga_ref, hm_ref, near_ref, sg_ref,
                   ck_hbm, cv_hbm, o_ref, kbuf, vbuf, knew, vnew, ksem, vsem):
    b = pl.program_id(0)
    nb = pl.num_programs(0)
    n_pages = kbuf.shape[1]
    page_rows = kbuf.shape[2]
    new_rows = q1_ref.shape[0]
    slot = b % 2

    def k_copy(seq, s, j):
        return pltpu.make_async_copy(ck_hbm.at[pt_ref[seq, j]], kbuf.at[s, j], ksem.at[s])

    def v_copy(seq, s, j):
        return pltpu.make_async_copy(cv_hbm.at[pt_ref[seq, j]], vbuf.at[s, j], vsem.at[s])

    def start_fetch(seq, s):
        for j in range(n_pages):
            k_copy(seq, s, j).start()
            v_copy(seq, s, j).start()

    @pl.when(b == 0)
    def _():
        start_fetch(0, 0)
        knew[...] = jnp.zeros(knew.shape, knew.dtype)
        vnew[...] = jnp.zeros(vnew.shape, vnew.dtype)

    @pl.when(b + 1 < nb)
    def _():
        start_fetch(b + 1, 1 - slot)

    qx = jnp.concatenate([q1_ref[...], q2_ref[...]], axis=0).astype(BF16)
    knew[0:new_rows, :] = kn_ref[...].astype(BF16)
    vnew[0:new_rows, :] = vn_ref[...].astype(BF16)

    for j in range(n_pages):
        k_copy(b, slot, j).wait()
        v_copy(b, slot, j).wait()

    nt = (((1,), (1,)), ((), ()))
    s_tiles = [lax.dot_general(qx, kbuf[slot, j].astype(BF16), nt, preferred_element_type=F32)
               for j in range(n_pages)]
    s_tiles.append(lax.dot_general(qx, knew[...], nt, preferred_element_type=F32))
    head_mask = hm_ref[...]
    s = jnp.concatenate([t + head_mask for t in s_tiles[:n_pages - 1]]
                        + [jnp.concatenate(s_tiles[n_pages - 1:], axis=1) + near_ref[...]],
                        axis=1)
    m = jnp.max(s, axis=-1, keepdims=True)
    p = jnp.exp2(s - m)
    l_sum = jnp.sum(p, axis=-1, keepdims=True)
    pb = p.astype(BF16)
    acc = jnp.dot(pb[:, n_pages * page_rows:], vnew[...], preferred_element_type=F32)
    for j in range(n_pages):
        acc = acc + jnp.dot(pb[:, j * page_rows:(j + 1) * page_rows], vbuf[slot, j].astype(BF16),
                            preferred_element_type=F32)
    o = acc / l_sum
    o_ref[...] = _diff_epilogue(o[:new_rows], o[new_rows:], lam_ref[0], sg_ref[...], ga_ref[...])


def _decode_attention(page_table, lam, q1, q2, kn, vn, ga, head_mask, near_bias, sg, cache_k, cache_v,
                      new_rows):
    n_seq, n_pages = page_table.shape
    page_rows = cache_k.shape[1]
    seq_spec = pl.BlockSpec((new_rows, V_DIM), lambda bi, pt: (bi, 0))
    grid_spec = pltpu.PrefetchScalarGridSpec(
        num_scalar_prefetch=1,
        grid=(n_seq,),
        in_specs=[
            pl.BlockSpec(memory_space=pltpu.SMEM),
            seq_spec, seq_spec, seq_spec, seq_spec, seq_spec,
            pl.BlockSpec(head_mask.shape, lambda bi, pt: (0, 0)),
            pl.BlockSpec(near_bias.shape, lambda bi, pt: (0, 0)),
            pl.BlockSpec((1, V_DIM), lambda bi, pt: (0, 0)),
            pl.BlockSpec(memory_space=pl.ANY),
            pl.BlockSpec(memory_space=pl.ANY),
        ],
        out_specs=seq_spec,
        scratch_shapes=[
            pltpu.VMEM((2, n_pages, page_rows, V_DIM), cache_k.dtype),
            pltpu.VMEM((2, n_pages, page_rows, V_DIM), cache_v.dtype),
            pltpu.VMEM((page_rows, V_DIM), BF16),
            pltpu.VMEM((page_rows, V_DIM), BF16),
            pltpu.SemaphoreType.DMA((2,)),
            pltpu.SemaphoreType.DMA((2,)),
        ],
    )
    return pl.pallas_call(
        _decode_kernel,
        grid_spec=grid_spec,
        out_shape=jax.ShapeDtypeStruct((n_seq * new_rows, V_DIM), F32),
        compiler_params=pltpu.CompilerParams(
            dimension_semantics=("arbitrary",), vmem_limit_bytes=VMEM_LIMIT_BYTES),
        name="decode_attention",
    )(page_table, lam, q1, q2, kn, vn, ga, head_mask, near_bias, sg, cache_k, cache_v)


def _ssm_tail(y, u, gs, dskip_ref, wglu_ref, bglu_ref):
    z = _gelu_tanh(y + dskip_ref[...] * u)
    gate = jax.nn.sigmoid(jnp.dot(z.astype(BF16), wglu_ref[...], preferred_element_type=F32) + bglu_ref[...])
    return z * gate * _silu(gs)


def _ssm_prompt_kernel(u_ref, gs_ref, bw_ref, cw_ref, pin_re_ref, pin_im_ref, pout_re_ref, pout_im_ref,
                       abar_ref, tri_ref, dskip_ref, wglu_ref, bglu_ref, o_ref, hfin_ref, carry_scr):
    t = u_ref.shape[0]
    n_q = bw_ref.shape[0]
    sc = SSM_LANE_CHUNK
    c = pl.program_id(1)

    @pl.when(c == 0)
    def _():
        carry_scr[...] = jnp.zeros(carry_scr.shape, F32)

    u = u_ref[...]
    tri = tri_ref[...]
    ys = []
    for q in range(n_q):
        uq = u[:, q * LANES:(q + 1) * LANES].astype(BF16)
        bu = jnp.dot(uq, bw_ref[q], preferred_element_type=F32)
        br, bi = bu[:, :sc], bu[:, sc:]
        lanes = slice(q * sc, (q + 1) * sc)
        pir, pii = pin_re_ref[:, lanes], pin_im_ref[:, lanes]
        x = jnp.concatenate([br * pir - bi * pii, br * pii + bi * pir], axis=1)
        cs = (jnp.dot(tri, x.astype(BF16), preferred_element_type=F32)
              + carry_scr[:, 2 * q * sc:2 * (q + 1) * sc])
        sr, si = cs[:, :sc], cs[:, sc:]
        por, poi = pout_re_ref[:, lanes], pout_im_ref[:, lanes]
        hr = sr * por - si * poi
        hm = sr * poi + si * por
        lr, lm = hr[t - 1:t, :], hm[t - 1:t, :]
        ar, am = abar_ref[0:1, lanes], abar_ref[1:2, lanes]
        hfin_ref[:, 2 * q * sc:(2 * q + 1) * sc] = lr
        hfin_ref[:, (2 * q + 1) * sc:2 * (q + 1) * sc] = lm
        carry_scr[:, 2 * q * sc:(2 * q + 1) * sc] = ar * lr - am * lm
        carry_scr[:, (2 * q + 1) * sc:2 * (q + 1) * sc] = ar * lm + am * lr
        h = jnp.concatenate([hr, hm], axis=1).astype(BF16)
        ys.append(jnp.dot(h, cw_ref[q], preferred_element_type=F32))
    y = jnp.concatenate(ys, axis=1)
    o_ref[...] = _ssm_tail(y, u, gs_ref[...], dskip_ref, wglu_ref, bglu_ref).astype(o_ref.dtype)


def _ssm_prompt(u, gs, sp):
    b, l, d_ssm = u.shape
    t = SSM_CHUNK
    n_state2 = sp["bw"].shape[0] * sp["bw"].shape[2]
    row = pl.BlockSpec((None, t, d_ssm), lambda bi, ci: (bi, ci, 0))
    full = lambda a: pl.BlockSpec(a.shape, lambda bi, ci: (0,) * a.ndim)
    names = ["bw", "cw", "pin_re", "pin_im", "pout_re", "pout_im", "abar", "tri", "dskip", "wglu", "bglu"]
    return pl.pallas_call(
        _ssm_prompt_kernel,
        grid=(b, l // t),
        in_specs=[row, row] + [full(sp[n]) for n in names],
        out_specs=[row, pl.BlockSpec((None, 1, n_state2), lambda bi, ci: (bi, 0, 0))],
        out_shape=[jax.ShapeDtypeStruct((b, l, d_ssm), BF16),
                   jax.ShapeDtypeStruct((b, 1, n_state2), F32)],
        scratch_shapes=[pltpu.VMEM((1, n_state2), F32)],
        compiler_params=pltpu.CompilerParams(
            dimension_semantics=("arbitrary", "arbitrary"), vmem_limit_bytes=VMEM_LIMIT_BYTES),
        name="ssm_prompt",
    )(u, gs, *[sp[n] for n in names])


def _ssm_sample_kernel(u_ref, gs_ref, h0_ref, bw_ref, cw_ref, abar_ref, dskip_ref, wglu_ref, bglu_ref,
                       o_ref, hfin_ref, bu_scr, h_scr):
    n_seq = h0_ref.shape[0]
    dec_seq = u_ref.shape[0] // n_seq
    n_q = bw_ref.shape[0]
    sc = SSM_LANE_CHUNK
    n_lc = 2 * sc // LANES
    u = u_ref[...]
    ys = []
    for q in range(n_q):
        uq = u[:, q * LANES:(q + 1) * LANES].astype(BF16)
        bu = jnp.dot(uq, bw_ref[q], preferred_element_type=F32)
        for c in range(n_lc):
            bu_scr[c] = bu[:, c * LANES:(c + 1) * LANES]
        lanes = slice(q * sc, (q + 1) * sc)
        ar, am = abar_ref[0:1, lanes], abar_ref[1:2, lanes]
        hr = h0_ref[:, 2 * q * sc:(2 * q + 1) * sc]
        hm = h0_ref[:, (2 * q + 1) * sc:2 * (q + 1) * sc]
        for step in range(dec_seq):
            rows = pl.ds(step, n_seq, stride=dec_seq)
            b_all = jnp.concatenate([bu_scr[c, rows, :] for c in range(n_lc)], axis=1)
            br, bi = b_all[:, :sc], b_all[:, sc:]
            hr, hm = ar * hr - am * hm + br, ar * hm + am * hr + bi
            for c in range(n_lc // 2):
                h_scr[c, rows, :] = hr[:, c * LANES:(c + 1) * LANES]
                h_scr[n_lc // 2 + c, rows, :] = hm[:, c * LANES:(c + 1) * LANES]
        hfin_ref[:, 2 * q * sc:(2 * q + 1) * sc] = hr
        hfin_ref[:, (2 * q + 1) * sc:2 * (q + 1) * sc] = hm
        h_all = jnp.concatenate([h_scr[c] for c in range(n_lc)], axis=1)
        ys.append(jnp.dot(h_all.astype(BF16), cw_ref[q], preferred_element_type=F32))
    y = jnp.concatenate(ys, axis=1)
    o_ref[...] = _ssm_tail(y, u, gs_ref[...], dskip_ref, wglu_ref, bglu_ref).astype(o_ref.dtype)


def _ssm_sample(u, gs, h0, sp):
    n, d_ssm = u.shape
    n_seq, n_state2 = h0.shape
    names = ["bw", "cw", "abar", "dskip", "wglu", "bglu"]
    args = [u, gs, h0] + [sp[k] for k in names]
    full = lambda a: pl.BlockSpec(a.shape, lambda i: (0,) * a.ndim)
    return pl.pallas_call(
        _ssm_sample_kernel,
        grid=(1,),
        in_specs=[full(a) for a in args],
        out_specs=[pl.BlockSpec((n, d_ssm), lambda i: (0, 0)),
                   pl.BlockSpec((n_seq, n_state2), lambda i: (0, 0))],
        out_shape=[jax.ShapeDtypeStruct((n, d_ssm), BF16),
                   jax.ShapeDtypeStruct((n_seq, n_state2), F32)],
        scratch_shapes=[pltpu.VMEM((2 * SSM_LANE_CHUNK // LANES, n, LANES), F32),
                        pltpu.VMEM((2 * SSM_LANE_CHUNK // LANES, n, LANES), F32)],
        compiler_params=pltpu.CompilerParams(
            dimension_semantics=("arbitrary",), vmem_limit_bytes=VMEM_LIMIT_BYTES),
        name="ssm_sample",
    )(*args)


def _outproj_kernel(x_ref, oa_ref, os_ref, w_ref, y_ref):
    d_a = oa_ref.shape[1]
    y = x_ref[...] + jnp.dot(oa_ref[...], w_ref[:d_a, :], preferred_element_type=F32)
    y_ref[...] = y + jnp.dot(os_ref[...], w_ref[d_a:, :], preferred_element_type=F32)


def _outproj(x, oa, os_, w_bf):
    n, d_model = x.shape
    tm = min(ROW_TILE, n)
    row = lambda w: pl.BlockSpec((tm, w), lambda i: (i, 0))
    return pl.pallas_call(
        _outproj_kernel,
        grid=(n // tm,),
        in_specs=[row(d_model), row(oa.shape[1]), row(os_.shape[1]),
                  pl.BlockSpec(w_bf.shape, lambda i: (0, 0))],
        out_specs=row(d_model),
        out_shape=jax.ShapeDtypeStruct((n, d_model), F32),
        compiler_params=pltpu.CompilerParams(
            dimension_semantics=("arbitrary",), vmem_limit_bytes=VMEM_LIMIT_BYTES),
        name="outproj",
    )(x, oa, os_, w_bf)


def _ssm_params(a_re, a_im, log_dt, b_re, b_im, c_re, c_im, d_skip, w_glu, b_glu, chunk):
    n_groups, n_state = a_re.shape
    g_per_q = LANES // SSM_GROUP
    n_q = n_groups // g_per_q
    dt = jnp.exp(log_dt.astype(F32))[:, None]
    a_re = a_re.astype(F32)
    a_im = a_im.astype(F32)
    mag = jnp.exp(a_re * dt)
    abar_re = mag * jnp.cos(a_im * dt)
    abar_im = mag * jnp.sin(a_im * dt)
    nr = abar_re - 1.0
    den = a_re * a_re + a_im * a_im
    coef_re = (nr * a_re + abar_im * a_im) / den
    coef_im = (abar_im * a_re - nr * a_im) / den
    b_re = b_re.astype(F32)
    b_im = b_im.astype(F32)
    bbar_re = coef_re[..., None] * b_re - coef_im[..., None] * b_im
    bbar_im = coef_re[..., None] * b_im + coef_im[..., None] * b_re

    eye = jnp.eye(g_per_q, dtype=F32)

    def in_blocks(bb):
        bb = bb.reshape(n_q, g_per_q, n_state, SSM_GROUP)
        m = jnp.einsum("qgpc,gh->qgchp", bb, eye)
        return m.reshape(n_q, LANES, g_per_q * n_state)

    def out_blocks(cc):
        cc = cc.reshape(n_q, g_per_q, SSM_GROUP, n_state)
        m = jnp.einsum("qgcp,gh->qgphc", cc, eye)
        return m.reshape(n_q, g_per_q * n_state, LANES)

    bw = jnp.concatenate([in_blocks(bbar_re), in_blocks(bbar_im)], axis=2).astype(BF16)
    cw = jnp.concatenate([out_blocks(c_re.astype(F32)), -out_blocks(c_im.astype(F32))], axis=1).astype(BF16)

    ar = abar_re.reshape(1, -1)
    ai = abar_im.reshape(1, -1)
    sp = {
        "bw": bw, "cw": cw,
        "abar": jnp.concatenate([ar, ai], axis=0),
        "dskip": d_skip.astype(F32).reshape(1, -1),
        "wglu": w_glu.astype(BF16),
        "bglu": b_glu.astype(F32).reshape(1, -1),
    }
    t = jnp.asarray(np.arange(chunk, dtype=np.float32)[:, None])
    log_mag = t * (a_re * dt).reshape(1, -1)
    angle = t * (a_im * dt).reshape(1, -1)
    cos_t, sin_t = jnp.cos(angle), jnp.sin(angle)
    grow, decay = jnp.exp(-log_mag), jnp.exp(log_mag)
    sp.update({"pout_re": decay * cos_t, "pout_im": decay * sin_t,
               "pin_re": grow * cos_t, "pin_im": -(grow * sin_t),
               "tri": jnp.asarray(np.tril(np.ones((chunk, chunk), np.float32)), BF16)})
    return sp


def _state_to_lanes(h_re, h_im):
    b = h_re.shape[0]
    sc = SSM_LANE_CHUNK
    r = h_re.astype(F32).reshape(b, -1, 1, sc)
    i = h_im.astype(F32).reshape(b, -1, 1, sc)
    return jnp.concatenate([r, i], axis=2).reshape(b, -1)


def _lanes_to_state(h, n_groups, n_state):
    b = h.shape[0]
    h = h.reshape(b, -1, 2, SSM_LANE_CHUNK)
    return (h[:, :, 0, :].reshape(b, n_groups, n_state), h[:, :, 1, :].reshape(b, n_groups, n_state))


def _toeplitz(v, n):
    h = v.shape[0]
    x = jnp.broadcast_to(v[:, None, :], (h, n, 2 * n)).reshape(h, 2 * n * n)
    return x[:, :n * (2 * n - 1)].reshape(h, n, 2 * n - 1)[:, :, :n]


def _prompt_bias_blocks(fvec):
    n = LANES
    h = fvec.shape[0]
    neg = jnp.full((h, n - 1), NEG_INF, F32)
    va = jnp.concatenate([fvec[:, 0:1], neg, jnp.zeros((h, 1), F32), fvec[:, 1:n][:, ::-1]], axis=1)
    vb = jnp.concatenate([fvec[:, 1:n + 1][:, ::-1], jnp.zeros((h, n), F32)], axis=1)
    return jnp.stack([_toeplitz(va, n), _toeplitz(vb, n)], axis=1)


def _decode_bias(fvec, page, dec_seq, n_heads):
    h = fvec.shape[0]
    rows = []
    for i in range(dec_seq):
        last = fvec[:, i + 1:i + 1 + page][:, ::-1]
        new = jnp.concatenate([fvec[:, 0:i + 1][:, ::-1], jnp.full((h, page - i - 1), NEG_INF, F32)], axis=1)
        rows.append(jnp.concatenate([last, new], axis=1))
    per_head = jnp.stack(rows, axis=0)
    same = np.eye(n_heads, dtype=bool)[None, :, None, :]
    near = jnp.where(jnp.asarray(same), per_head[:, :, :, None], NEG_INF)
    near = near.reshape(dec_seq * n_heads, -1)
    mask = np.where(np.broadcast_to(same, (dec_seq, n_heads, page, n_heads)), 0.0, NEG_INF)
    mask = mask.reshape(dec_seq * n_heads, -1).astype(np.float32)
    return jnp.asarray(np.concatenate([mask, mask], axis=0)), jnp.concatenate([near, near], axis=0)


def kernel(x_prompt, x_sample, cache_k, cache_v, state_ssm_re, state_ssm_im, page_table,
           norm_g, w_in, q_norm_g, k_norm_g, lambda_q1, lambda_k1, lambda_q2, lambda_k2,
           subln_g, rel_bias, ssm_a_re, ssm_a_im, ssm_log_dt, ssm_b_re, ssm_b_im,
           ssm_c_re, ssm_c_im, ssm_d, w_glu, b_glu, w_out):
    batch, seq, d_model = x_prompt.shape
    dec_batch, dec_seq, _ = x_sample.shape
    depth, n_pool, page, n_heads, _ = cache_k.shape
    n_pages = page_table.shape[1]
    d_attn = n_heads * V_DIM
    n_groups, n_state = ssm_a_re.shape[1:]
    new_rows = dec_seq * n_heads

    buckets = _bucket_table(2 * LANES)
    far_from = int(np.max(np.nonzero(buckets < N_BUCKETS - 1)[0])) + 1
    assert far_from <= LANES and _bucket_table(seq + page * n_pages)[far_from:].min() == N_BUCKETS - 1
    assert page == LANES and dec_seq < LANES and ATTN_TQ == ATTN_TK

    rel_bias = rel_bias.astype(F32)
    fvec = (rel_bias[buckets].T - rel_bias[N_BUCKETS - 1][:, None]) * LOG2E
    fvec = jnp.where(jnp.asarray(np.arange(2 * LANES) < far_from)[None], fvec, 0.0)
    dtiles = _prompt_bias_blocks(fvec)
    head_mask, near_bias = _decode_bias(fvec, page, dec_seq, n_heads)

    group_avg = jnp.asarray(np.kron(np.eye(d_attn // QK_DIM), np.full((QK_DIM, QK_DIM), 1.0 / QK_DIM)), BF16)
    n_rep = d_attn // QK_DIM
    cache_k_rows = cache_k.reshape(depth * n_pool, page * n_heads, V_DIM)
    cache_v_rows = cache_v.reshape(depth * n_pool, page * n_heads, V_DIM)

    hp = x_prompt.reshape(batch * seq, d_model)
    hs = x_sample.reshape(dec_batch * dec_seq, d_model)
    kp_l, vp_l, ks_l, vs_l = [], [], [], []
    srp_l, sip_l, srs_l, sis_l = [], [], [], []
    for l in range(depth):
        lam_init = _lambda_init(l)
        lam = (jnp.exp(jnp.sum(lambda_q1[l].astype(F32) * lambda_k1[l].astype(F32)))
               - jnp.exp(jnp.sum(lambda_q2[l].astype(F32) * lambda_k2[l].astype(F32))) + lam_init)
        lam = lam.reshape(1).astype(F32)
        ng = norm_g[l].astype(F32).reshape(1, d_model)
        w_bf = w_in[l].astype(BF16)
        gq = jnp.tile(q_norm_g[l].astype(F32), n_rep).reshape(1, d_attn) * (QK_DIM ** -0.5 * LOG2E)
        gk = jnp.tile(k_norm_g[l].astype(F32), n_rep).reshape(1, d_attn)
        sg = (subln_g[l].astype(F32) * (1.0 - lam_init)).reshape(1, V_DIM)
        wo_bf = w_out[l].astype(BF16)
        sp = _ssm_params(ssm_a_re[l], ssm_a_im[l], ssm_log_dt[l], ssm_b_re[l], ssm_b_im[l],
                         ssm_c_re[l], ssm_c_im[l], ssm_d[l], w_glu[l], b_glu[l], SSM_CHUNK)

        q1, q2, k4, kb, v4, va, ga, u, gs = _inproj(hp, ng, w_bf, gq, gk, group_avg, head_rows=False)
        r3 = lambda a: a.reshape(batch, seq, a.shape[-1])
        o_a = _prompt_attention(lam, r3(q1), r3(q2), r3(kb), r3(va), dtiles, r3(ga), sg)
        o_s, hfin = _ssm_prompt(r3(u), r3(gs), sp)
        hp = _outproj(hp, o_a.reshape(batch * seq, d_attn), o_s.reshape(batch * seq, -1), wo_bf)
        kp_l.append(k4.reshape(batch, seq, n_heads, V_DIM).astype(cache_k.dtype))
        vp_l.append(v4.reshape(batch, seq, n_heads, V_DIM).astype(cache_v.dtype))
        hr_p, hi_p = _lanes_to_state(hfin.reshape(batch, -1), n_groups, n_state)
        srp_l.append(hr_p.astype(state_ssm_re.dtype))
        sip_l.append(hi_p.astype(state_ssm_im.dtype))

        q1, q2, k4, v4, ga, u, gs = _inproj(hs, ng, w_bf, gq, gk, group_avg, head_rows=True)
        o_a = _decode_attention(page_table + l * n_pool, lam, q1, q2, k4, v4, ga, head_mask, near_bias, sg,
                                cache_k_rows, cache_v_rows, new_rows)
        h0 = _state_to_lanes(state_ssm_re[l], state_ssm_im[l])
        o_s, hfin = _ssm_sample(u, gs, h0, sp)
        hs = _outproj(hs, o_a.reshape(dec_batch * dec_seq, d_attn).astype(BF16), o_s, wo_bf)
        ks_l.append(k4.reshape(dec_batch, dec_seq, n_heads, V_DIM).astype(cache_k.dtype))
        vs_l.append(v4.reshape(dec_batch, dec_seq, n_heads, V_DIM).astype(cache_v.dtype))
        hr_s, hi_s = _lanes_to_state(hfin, n_groups, n_state)
        srs_l.append(hr_s.astype(state_ssm_re.dtype))
        sis_l.append(hi_s.astype(state_ssm_im.dtype))

    y_prompt = hp.reshape(batch, seq, d_model).astype(x_prompt.dtype)
    y_sample = hs.reshape(dec_batch, dec_seq, d_model).astype(x_sample.dtype)
    return (y_prompt, y_sample, jnp.stack(kp_l), jnp.stack(vp_l), jnp.stack(ks_l), jnp.stack(vs_l),
            jnp.stack(srp_l), jnp.stack(sip_l), jnp.stack(srs_l), jnp.stack(sis_l))
```

```python
import functools
import math

import numpy as np
import jax
import jax.numpy as jnp
from jax import lax
from jax.experimental import pallas as pl
from jax.experimental.pallas import tpu as pltpu

F32 = jnp.float32
BF16 = jnp.bfloat16

QK_DIM = 64
V_DIM = 2 * QK_DIM
N_BUCKETS = 32
MAX_DISTANCE = 128
SSM_GROUP = 16
SSM_STATE = 64
EPS = 1e-6
NEG_INF = -1e30
LOG2E = math.log2(math.e)
GELU_C0 = math.sqrt(2.0 / math.pi)
GELU_C1 = GELU_C0 * 0.044715

LANES = 128
VMEM_LIMIT_BYTES = 56 * 1024 * 1024

ROW_TILE = 512
ATTN_TQ = 512
ATTN_TK = 512
SSM_MICRO = 4
SSM_STEP = 1024
SSM_SCAN_ROWS = 32
SSM_LANE_CHUNK = 512


def _lambda_init(layer):
    return 0.8 - 0.6 * math.exp(-0.3 * layer)


def _bucket_table(n_max):
    n = np.arange(n_max)
    max_exact = N_BUCKETS // 2
    nf = np.maximum(n, 1).astype(np.float32)
    large = max_exact + (np.log(nf / np.float32(max_exact)) / np.float32(math.log(MAX_DISTANCE / max_exact))
                         * np.float32(N_BUCKETS - max_exact)).astype(np.int32)
    large = np.minimum(large, N_BUCKETS - 1)
    return np.where(n < max_exact, n, large).astype(np.int32)


def _silu(x):
    return (0.5 * x) * (1.0 + jnp.tanh(0.5 * x))


def _store_head_rows(ref, val, n_heads):
    rows = val.shape[0]
    for h in range(n_heads):
        ref[pl.ds(h, rows, stride=n_heads), :] = val[:, h * V_DIM:(h + 1) * V_DIM]


def _inproj_kernel(x_ref, ng_ref, w_ref, gq_ref, gk_ref, gavg_ref, *out_refs, head_rows):
    x = x_ref[...]
    ms = jnp.mean(x * x, axis=-1, keepdims=True)
    xb = (x * lax.rsqrt(ms + EPS) * ng_ref[...]).astype(BF16)
    d_seg = gq_ref.shape[1]
    n_heads = d_seg // V_DIM

    def seg(i):
        return jnp.dot(xb, w_ref[:, i * d_seg:(i + 1) * d_seg], preferred_element_type=F32)

    def group_norm(t, g):
        msq = jnp.dot((t * t).astype(BF16), gavg_ref[...], preferred_element_type=F32)
        return t * lax.rsqrt(msq + EPS) * g

    q = group_norm(seg(0), gq_ref[...])
    lane = lax.broadcasted_iota(jnp.int32, q.shape, 1)
    first = (lane % V_DIM) < QK_DIM
    qa = jnp.where(first, q, 0.0)
    qb = jnp.where(first, 0.0, q)
    k = group_norm(seg(1), gk_ref[...])
    v = seg(2)
    ga = seg(3)
    if head_rows:
        q1_ref, q2_ref, k4_ref, v4_ref, ga_ref, u_ref, gs_ref = out_refs
        _store_head_rows(q1_ref, qa, n_heads)
        _store_head_rows(q2_ref, qb, n_heads)
        _store_head_rows(ga_ref, ga, n_heads)
    else:
        q1_ref, q2_ref, k4_ref, kb_ref, v4_ref, va_ref, ga_ref, u_ref, gs_ref = out_refs
        q1_ref[...] = qa.astype(BF16)
        q2_ref[...] = qb.astype(BF16)
        kb_ref[...] = k.astype(BF16)
        vb = v.astype(BF16)
        ones = jnp.ones((v.shape[0], V_DIM), BF16)
        pieces = []
        for h in range(n_heads):
            pieces += [vb[:, h * V_DIM:(h + 1) * V_DIM], ones]
        va_ref[...] = jnp.concatenate(pieces, axis=1)
        ga_ref[...] = ga
    _store_head_rows(k4_ref, k, n_heads)
    _store_head_rows(v4_ref, v, n_heads)
    if head_rows:
        u_ref[...] = seg(4)
    else:
        _store_head_rows(u_ref, seg(4), d_seg // LANES)
    gs_ref[...] = seg(5)


def _inproj(x, ng, w_bf, gq, gk, gavg, head_rows):
    n, d_model = x.shape
    d_seg = gq.shape[1]
    n_heads = d_seg // V_DIM
    tm = min(ROW_TILE, n)
    full = lambda a: pl.BlockSpec(a.shape, lambda i: (0,) * a.ndim)
    wide = lambda w, dt: (jax.ShapeDtypeStruct((n, w), dt), pl.BlockSpec((tm, w), lambda i: (i, 0)))
    tall = lambda dt: (jax.ShapeDtypeStruct((n * n_heads, V_DIM), dt),
                       pl.BlockSpec((tm * n_heads, V_DIM), lambda i: (i, 0)))
    if head_rows:
        outs = [tall(F32), tall(F32), tall(F32), tall(F32), tall(F32), wide(d_seg, F32), wide(d_seg, F32)]
    else:
        outs = [wide(d_seg, BF16), wide(d_seg, BF16), tall(F32), wide(d_seg, BF16), tall(F32),
                wide(2 * d_seg, BF16), wide(d_seg, F32), tall(F32), wide(d_seg, F32)]
    return pl.pallas_call(
        functools.partial(_inproj_kernel, head_rows=head_rows),
        grid=(n // tm,),
        in_specs=[pl.BlockSpec((tm, d_model), lambda i: (i, 0)),
                  full(ng), full(w_bf), full(gq), full(gk), full(gavg)],
        out_specs=[o[1] for o in outs],
        out_shape=[o[0] for o in outs],
        compiler_params=pltpu.CompilerParams(
            dimension_semantics=("arbitrary",), vmem_limit_bytes=VMEM_LIMIT_BYTES),
        name="inproj_samples" if head_rows else "inproj_prompt",
    )(x, ng, w_bf, gq, gk, gavg)


def _diff_epilogue(o1, o2, lam, sg, ga):
    od = o1 - lam * o2
    ms = jnp.mean(od * od, axis=-1, keepdims=True)
    return od * lax.rsqrt(ms + EPS) * sg * _silu(ga)


def _attn_kernel(lam_ref, q1_ref, q2_ref, k_ref, v_ref, ab_ref, ga_ref, sg_ref, o_ref,
                 s_scr, d_scr, m_scr, acc_scr):
    tq = q1_ref.shape[0]
    tk = s_scr.shape[2]
    qi = pl.program_id(2)

    @pl.when(qi == 0)
    def _():
        for t in range(2):
            for a in range(tq // LANES):
                for b in range(tk // LANES):
                    delta = a - b + t * (tk // LANES)
                    if delta in (0, 1):
                        blk = ab_ref[delta]
                    else:
                        blk = jnp.full((LANES, LANES), NEG_INF if delta < 0 else 0.0, F32)
                    d_scr[t, a * LANES:(a + 1) * LANES, b * LANES:(b + 1) * LANES] = blk

    qs = jnp.concatenate([q1_ref[...], q2_ref[...]], axis=0)

    m_scr[...] = jnp.full(m_scr.shape, NEG_INF, F32)
    acc_scr[...] = jnp.zeros(acc_scr.shape, F32)

    def produce(kj, bias, slot):
        start = pl.multiple_of(kj * tk, tk)
        s = lax.dot_general(qs, k_ref[pl.ds(start, tk), :], (((1,), (1,)), ((), ())),
                            preferred_element_type=F32)
        if bias is not None:
            s = s + jnp.concatenate([bias, bias], axis=0)
        s_scr[slot] = s

    def consume(kj, slot):
        start = pl.multiple_of(kj * tk, tk)
        s = s_scr[slot]
        m_old = m_scr[...]
        m_new = jnp.maximum(m_old, jnp.max(s, axis=-1, keepdims=True))
        p = jnp.exp2(s - jnp.concatenate([m_new] * (tk // LANES), axis=1))
        alpha = jnp.exp2(m_old - m_new)
        pv = jnp.dot(p.astype(BF16), v_ref[pl.ds(start, tk), :], preferred_element_type=F32)
        acc_scr[...] = acc_scr[...] * jnp.concatenate([alpha] * (acc_scr.shape[1] // LANES), axis=1) + pv
        m_scr[...] = m_new

    produce(qi, d_scr[0], 0)

    @pl.when(qi >= 1)
    def _():
        produce(qi - 1, d_scr[1], 1)
        consume(qi, 0)

    n_far = jnp.maximum(qi - 1, 0)

    def pair(kj):
        produce(kj, None, 0)
        consume(kj + 1, 1)
        produce(kj - 1, None, 1)
        consume(kj, 0)

    def quad_body(i, carry):
        kj = qi - 2 - 4 * i
        pair(kj)
        pair(kj - 2)
        return carry

    n_quad = n_far // 4
    lax.fori_loop(0, n_quad, quad_body, 0)

    @pl.when(n_far % 4 >= 2)
    def _():
        pair(qi - 2 - 4 * n_quad)

    odd = n_far % 2 == 1

    @pl.when(odd)
    def _():
        produce(0, None, 0)
        consume(1, 1)
        consume(0, 0)

    @pl.when(jnp.logical_and(qi >= 1, jnp.logical_not(odd)))
    def _():
        consume(0, 1)

    @pl.when(qi == 0)
    def _():
        consume(0, 0)

    acc = acc_scr[...]
    o = acc[:, :V_DIM] / acc[:, V_DIM:]
    out = _diff_epilogue(o[:tq], o[tq:], lam_ref[0], sg_ref[...], ga_ref[...])
    o_ref[...] = out.astype(o_ref.dtype)


def _prompt_attention(lam, q1, q2, kb, va, dtiles, ga, sg):
    b, l, d_attn = q1.shape
    n_heads = d_attn // V_DIM
    tq = ATTN_TQ
    return pl.pallas_call(
        _attn_kernel,
        grid=(b, n_heads, l // tq),
        in_specs=[
            pl.BlockSpec(memory_space=pltpu.SMEM),
            pl.BlockSpec((None, tq, V_DIM), lambda bi, h, qi: (bi, qi, h)),
            pl.BlockSpec((None, tq, V_DIM), lambda bi, h, qi: (bi, qi, h)),
            pl.BlockSpec((None, l, V_DIM), lambda bi, h, qi: (bi, 0, h)),
            pl.BlockSpec((None, l, 2 * V_DIM), lambda bi, h, qi: (bi, 0, h)),
            pl.BlockSpec((None, 2, LANES, LANES), lambda bi, h, qi: (h, 0, 0, 0)),
            pl.BlockSpec((None, tq, V_DIM), lambda bi, h, qi: (bi, qi, h)),
            pl.BlockSpec((1, V_DIM), lambda bi, h, qi: (0, 0)),
        ],
        out_specs=pl.BlockSpec((None, tq, V_DIM), lambda bi, h, qi: (bi, qi, h)),
        out_shape=jax.ShapeDtypeStruct((b, l, d_attn), BF16),
        scratch_shapes=[pltpu.VMEM((2, 2 * tq, ATTN_TK), F32), pltpu.VMEM((2, tq, ATTN_TK), F32),
                        pltpu.VMEM((2 * tq, LANES), F32),
                        pltpu.VMEM((2 * tq, 2 * V_DIM), F32)],
        compiler_params=pltpu.CompilerParams(
            dimension_semantics=("arbitrary", "arbitrary", "arbitrary"),
            vmem_limit_bytes=VMEM_LIMIT_BYTES),
        name="prompt_attention",
    )(lam, q1, q2, kb, va, dtiles, ga, sg)


def _decode_kernel(pt_ref, lam_ref, q1_ref, q2_ref, kn_ref, vn_ref, ga_ref, hm_ref, near_ref, sg_ref,
                   ck_hbm, cv_hbm, o_ref, kbuf, vbuf, knew, vnew, ksem, vsem):
    b = pl.program_id(0)
    nb = pl.num_programs(0)
    n_pages = kbuf.shape[1]
    page_rows = kbuf.shape[2]
    new_rows = q1_ref.shape[0]
    slot = b % 2

    def k_copy(seq, s, j):
        return pltpu.make_async_copy(ck_hbm.at[pt_ref[seq, j]], kbuf.at[s, j], ksem.at[s])

    def v_copy(seq, s, j):
        return pltpu.make_async_copy(cv_hbm.at[pt_ref[seq, j]], vbuf.at[s, j], vsem.at[s])

    def start_fetch(seq, s):
        for j in range(n_pages):
            k_copy(seq, s, j).start()
            v_copy(seq, s, j).start()

    @pl.when(b == 0)
    def _():
        start_fetch(0, 0)
        knew[...] = jnp.zeros(knew.shape, knew.dtype)
        vnew[...] = jnp.zeros(vnew.shape, vnew.dtype)

    @pl.when(b + 1 < nb)
    def _():
        start_fetch(b + 1, 1 - slot)

    qx = jnp.concatenate([q1_ref[...], q2_ref[...]], axis=0).astype(BF16)
    knew[0:new_rows, :] = kn_ref[...].astype(BF16)
    vnew[0:new_rows, :] = vn_ref[...].astype(BF16)

    for j in range(n_pages):
        k_copy(b, slot, j).wait()
        v_copy(b, slot, j).wait()

    nt = (((1,), (1,)), ((), ()))
    s_tiles = [lax.dot_general(qx, kbuf[slot, j].astype(BF16), nt, preferred_element_type=F32)
               for j in range(n_pages)]
    s_tiles.append(lax.dot_general(qx, knew[...], nt, preferred_element_type=F32))
    head_mask = hm_ref[...]
    s = jnp.concatenate([t + head_mask for t in s_tiles[:n_pages - 1]]
                        + [jnp.concatenate(s_tiles[n_pages - 1:], axis=1) + near_ref[...]],
                        axis=1)
    m = jnp.max(s, axis=-1, keepdims=True)
    p = jnp.exp2(s - m)
    l_sum = jnp.sum(p, axis=-1, keepdims=True)
    pb = p.astype(BF16)
    acc = jnp.dot(pb[:, n_pages * page_rows:], vnew[...], preferred_element_type=F32)
    for j in range(n_pages):
        acc = acc + jnp.dot(pb[:, j * page_rows:(j + 1) * page_rows], vbuf[slot, j].astype(BF16),
                            preferred_element_type=F32)
    o = acc / l_sum
    o_ref[...] = _diff_epilogue(o[:new_rows], o[new_rows:], lam_ref[0], sg_ref[...], ga_ref[...])


def _decode_attention(page_table, lam, q1, q2, kn, vn, ga, head_mask, near_bias, sg, cache_k, cache_v,
                      new_rows):
    n_seq, n_pages = page_table.shape
    page_rows = cache_k.shape[1]
    seq_spec = pl.BlockSpec((new_rows, V_DIM), lambda bi, pt: (bi, 0))
    grid_spec = pltpu.PrefetchScalarGridSpec(
        num_scalar_prefetch=1,
        grid=(n_seq,),
        in_specs=[
            pl.BlockSpec(memory_space=pltpu.SMEM),
            seq_spec, seq_spec, seq_spec, seq_spec, seq_spec,
            pl.BlockSpec(head_mask.shape, lambda bi, pt: (0, 0)),
            pl.BlockSpec(near_bias.shape, lambda bi, pt: (0, 0)),
            pl.BlockSpec((1, V_DIM), lambda bi, pt: (0, 0)),
            pl.BlockSpec(memory_space=pl.ANY),
            pl.BlockSpec(memory_space=pl.ANY),
        ],
        out_specs=seq_spec,
        scratch_shapes=[
            pltpu.VMEM((2, n_pages, page_rows, V_DIM), cache_k.dtype),
            pltpu.VMEM((2, n_pages, page_rows, V_DIM), cache_v.dtype),
            pltpu.VMEM((page_rows, V_DIM), BF16),
            pltpu.VMEM((page_rows, V_DIM), BF16),
            pltpu.SemaphoreType.DMA((2,)),
            pltpu.SemaphoreType.DMA((2,)),
        ],
    )
    return pl.pallas_call(
        _decode_kernel,
        grid_spec=grid_spec,
        out_shape=jax.ShapeDtypeStruct((n_seq * new_rows, V_DIM), F32),
        compiler_params=pltpu.CompilerParams(
            dimension_semantics=("arbitrary",), vmem_limit_bytes=VMEM_LIMIT_BYTES),
        name="decode_attention",
    )(page_table, lam, q1, q2, kn, vn, ga, head_mask, near_bias, sg, cache_k, cache_v)


def _ssm_tail(y, u, gs, dskip_ref, wglu_half_ref, bglu_half_ref):
    x = y + dskip_ref[...] * u
    inner = x * (GELU_C1 * (x * x) + GELU_C0)
    z = (0.5 * x) * (1.0 + jnp.tanh(inner))
    t_glu = jnp.tanh(jnp.dot(z.astype(BF16), wglu_half_ref[...], preferred_element_type=F32) + bglu_half_ref[...])
    t_gate = jnp.tanh(0.5 * gs)
    return ((z * gs) * 0.25) * (1.0 + t_glu) * (1.0 + t_gate)


def _ssm_prompt_kernel(u4_ref, gs_ref, w_ref, m_ref, v_ref, pin_re_ref, pin_im_ref, pout_re_ref, pout_im_ref,
                       amic_ref, tril_ref, dskip_ref, wglu_ref, bglu_ref, o_ref, hfin_ref,
                       carry_scr, hp_scr, y4_scr):
    n_q = w_ref.shape[0]
    mic = SSM_MICRO
    sc = SSM_LANE_CHUNK
    ts = gs_ref.shape[0]
    r = ts // mic
    rb = tril_ref.shape[0]
    c = pl.program_id(1)

    @pl.when(c == 0)
    def _():
        carry_scr[...] = jnp.zeros(carry_scr.shape, F32)

    tril = tril_ref[...]
    for q in range(n_q):
        re_l = slice(2 * q * sc, (2 * q + 1) * sc)
        im_l = slice((2 * q + 1) * sc, 2 * (q + 1) * sc)
        st_l = slice(q * sc, (q + 1) * sc)
        x = jnp.concatenate([u4_ref[pl.ds(n_q * s + q, r, stride=n_q * mic), :] for s in range(mic)],
                            axis=1).astype(BF16)
        e = jnp.dot(x, w_ref[q], preferred_element_type=F32)
        ar, am = amic_ref[0:1, st_l], amic_ref[1:2, st_l]
        pir, pii = pin_re_ref[:, st_l], pin_im_ref[:, st_l]
        por, poi = pout_re_ref[:, st_l], pout_im_ref[:, st_l]
        h_r, h_m = carry_scr[:, re_l], carry_scr[:, im_l]
        for blk in range(r // rb):
            rows = slice(blk * rb, (blk + 1) * rb)
            er, em = e[rows, :sc], e[rows, sc:]
            xs = jnp.concatenate([er * pir - em * pii, er * pii + em * pir], axis=1)
            s_ex = jnp.dot(tril, xs.astype(BF16), preferred_element_type=F32)
            sr = s_ex[:, :sc] + (ar * h_r - am * h_m)
            sm = s_ex[:, sc:] + (ar * h_m + am * h_r)
            hp_r = sr * por - sm * poi
            hp_m = sr * poi + sm * por
            hp_scr[rows, :sc] = hp_r
            hp_scr[rows, sc:] = hp_m
            lr, lm = hp_r[rb - 1:rb, :], hp_m[rb - 1:rb, :]
            h_r = ar * lr - am * lm + er[rb - 1:rb, :]
            h_m = ar * lm + am * lr + em[rb - 1:rb, :]
        carry_scr[:, re_l] = h_r
        carry_scr[:, im_l] = h_m
        hfin_ref[:, re_l] = h_r
        hfin_ref[:, im_l] = h_m
        y = (jnp.dot(x, m_ref[q], preferred_element_type=F32)
             + jnp.dot(hp_scr[...].astype(BF16), v_ref[q], preferred_element_type=F32))
        for s in range(mic):
            y4_scr[pl.ds(n_q * s + q, r, stride=n_q * mic), :] = y[:, s * LANES:(s + 1) * LANES]
    y = jnp.concatenate([y4_scr[pl.ds(q, ts, stride=n_q), :] for q in range(n_q)], axis=1)
    u = jnp.concatenate([u4_ref[pl.ds(q, ts, stride=n_q), :] for q in range(n_q)], axis=1)
    o_ref[...] = _ssm_tail(y, u, gs_ref[...], dskip_ref, wglu_ref, bglu_ref).astype(o_ref.dtype)


def _ssm_prompt(u4, gs, sp):
    b, l, d_ssm = gs.shape
    n_q = d_ssm // LANES
    ts = SSM_STEP
    n_state2 = 2 * sp["amic"].shape[1]
    row = pl.BlockSpec((None, ts, d_ssm), lambda bi, ci: (bi, ci, 0))
    row4 = pl.BlockSpec((None, ts * n_q, LANES), lambda bi, ci: (bi, ci, 0))
    full = lambda a: pl.BlockSpec(a.shape, lambda bi, ci: (0,) * a.ndim)
    names = ["w", "m", "v", "pin_re", "pin_im", "pout_re", "pout_im", "amic", "tril", "dskip", "wglu", "bglu"]
    return pl.pallas_call(
        _ssm_prompt_kernel,
        grid=(b, l // ts),
        in_specs=[row4, row] + [full(sp[n]) for n in names],
        out_specs=[row, pl.BlockSpec((None, 1, n_state2), lambda bi, ci: (bi, 0, 0))],
        out_shape=[jax.ShapeDtypeStruct((b, l, d_ssm), BF16),
                   jax.ShapeDtypeStruct((b, 1, n_state2), F32)],
        scratch_shapes=[pltpu.VMEM((1, n_state2), F32),
                        pltpu.VMEM((ts // SSM_MICRO, 2 * SSM_LANE_CHUNK), F32),
                        pltpu.VMEM((ts * n_q, LANES), F32)],
        compiler_params=pltpu.CompilerParams(
            dimension_semantics=("arbitrary", "arbitrary"), vmem_limit_bytes=VMEM_LIMIT_BYTES),
        name="ssm_prompt",
    )(u4, gs, *[sp[n] for n in names])


def _ssm_sample_kernel(u_ref, gs_ref, h0_ref, bw_ref, cw_ref, abar_ref, dskip_ref, wglu_ref, bglu_ref,
                       o_ref, hfin_ref, bu_scr, h_scr):
    n_seq = h0_ref.shape[0]
    dec_seq = u_ref.shape[0] // n_seq
    n_q = bw_ref.shape[0]
    sc = SSM_LANE_CHUNK
    n_lc = 2 * sc // LANES
    u = u_ref[...]
    ys = []
    for q in range(n_q):
        uq = u[:, q * LANES:(q + 1) * LANES].astype(BF16)
        bu = jnp.dot(uq, bw_ref[q], preferred_element_type=F32)
        for c in range(n_lc):
            bu_scr[c] = bu[:, c * LANES:(c + 1) * LANES]
        lanes = slice(q * sc, (q + 1) * sc)
        ar, am = abar_ref[0:1, lanes], abar_ref[1:2, lanes]
        hr = h0_ref[:, 2 * q * sc:(2 * q + 1) * sc]
        hm = h0_ref[:, (2 * q + 1) * sc:2 * (q + 1) * sc]
        for step in range(dec_seq):
            rows = pl.ds(step, n_seq, stride=dec_seq)
            b_all = jnp.concatenate([bu_scr[c, rows, :] for c in range(n_lc)], axis=1)
            br, bi = b_all[:, :sc], b_all[:, sc:]
            hr, hm = ar * hr - am * hm + br, ar * hm + am * hr + bi
            for c in range(n_lc // 2):
                h_scr[c, rows, :] = hr[:, c * LANES:(c + 1) * LANES]
                h_scr[n_lc // 2 + c, rows, :] = hm[:, c * LANES:(c + 1) * LANES]
        hfin_ref[:, 2 * q * sc:(2 * q + 1) * sc] = hr
        hfin_ref[:, (2 * q + 1) * sc:2 * (q + 1) * sc] = hm
        h_all = jnp.concatenate([h_scr[c] for c in range(n_lc)], axis=1)
        ys.append(jnp.dot(h_all.astype(BF16), cw_ref[q], preferred_element_type=F32))
    y = jnp.concatenate(ys, axis=1)
    o_ref[...] = _ssm_tail(y, u, gs_ref[...], dskip_ref, wglu_ref, bglu_ref).astype(o_ref.dtype)


def _ssm_sample(u, gs, h0, sp):
    n, d_ssm = u.shape
    n_seq, n_state2 = h0.shape
    names = ["bw", "cw", "abar", "dskip", "wglu", "bglu"]
    args = [u, gs, h0] + [sp[k] for k in names]
    full = lambda a: pl.BlockSpec(a.shape, lambda i: (0,) * a.ndim)
    return pl.pallas_call(
        _ssm_sample_kernel,
        grid=(1,),
        in_specs=[full(a) for a in args],
        out_specs=[pl.BlockSpec((n, d_ssm), lambda i: (0, 0)),
                   pl.BlockSpec((n_seq, n_state2), lambda i: (0, 0))],
        out_shape=[jax.ShapeDtypeStruct((n, d_ssm), BF16),
                   jax.ShapeDtypeStruct((n_seq, n_state2), F32)],
        scratch_shapes=[pltpu.VMEM((2 * SSM_LANE_CHUNK // LANES, n, LANES), F32),
                        pltpu.VMEM((2 * SSM_LANE_CHUNK // LANES, n, LANES), F32)],
        compiler_params=pltpu.CompilerParams(
            dimension_semantics=("arbitrary",), vmem_limit_bytes=VMEM_LIMIT_BYTES),
        name="ssm_sample",
    )(*args)


def _outproj_kernel(x_ref, oa_ref, os_ref, w_ref, y_ref):
    d_a = oa_ref.shape[1]
    y = x_ref[...] + jnp.dot(oa_ref[...], w_ref[:d_a, :], preferred_element_type=F32)
    y_ref[...] = y + jnp.dot(os_ref[...], w_ref[d_a:, :], preferred_element_type=F32)


def _outproj(x, oa, os_, w_bf):
    n, d_model = x.shape
    tm = min(ROW_TILE, n)
    row = lambda w: pl.BlockSpec((tm, w), lambda i: (i, 0))
    return pl.pallas_call(
        _outproj_kernel,
        grid=(n // tm,),
        in_specs=[row(d_model), row(oa.shape[1]), row(os_.shape[1]),
                  pl.BlockSpec(w_bf.shape, lambda i: (0, 0))],
        out_specs=row(d_model),
        out_shape=jax.ShapeDtypeStruct((n, d_model), F32),
        compiler_params=pltpu.CompilerParams(
            dimension_semantics=("arbitrary",), vmem_limit_bytes=VMEM_LIMIT_BYTES),
        name="outproj",
    )(x, oa, os_, w_bf)


def _ssm_params(a_re, a_im, log_dt, b_re, b_im, c_re, c_im, d_skip, w_glu, b_glu):
    n_groups, n_state = a_re.shape
    g_per_q = LANES // SSM_GROUP
    n_q = n_groups // g_per_q
    dt = jnp.exp(log_dt.astype(F32))[:, None]
    a_re = a_re.astype(F32)
    a_im = a_im.astype(F32)
    mag = jnp.exp(a_re * dt)
    abar_re = mag * jnp.cos(a_im * dt)
    abar_im = mag * jnp.sin(a_im * dt)
    nr = abar_re - 1.0
    den = a_re * a_re + a_im * a_im
    coef_re = (nr * a_re + abar_im * a_im) / den
    coef_im = (abar_im * a_re - nr * a_im) / den
    b_re = b_re.astype(F32)
    b_im = b_im.astype(F32)
    bbar_re = coef_re[..., None] * b_re - coef_im[..., None] * b_im
    bbar_im = coef_re[..., None] * b_im + coef_im[..., None] * b_re

    eye = jnp.eye(g_per_q, dtype=F32)

    def in_blocks(bb):
        bb = bb.reshape(n_q, g_per_q, n_state, SSM_GROUP)
        m = jnp.einsum("qgpc,gh->qgchp", bb, eye)
        return m.reshape(n_q, LANES, g_per_q * n_state)

    def out_blocks(cc):
        cc = cc.reshape(n_q, g_per_q, SSM_GROUP, n_state)
        m = jnp.einsum("qgcp,gh->qgphc", cc, eye)
        return m.reshape(n_q, g_per_q * n_state, LANES)

    bw = jnp.concatenate([in_blocks(bbar_re), in_blocks(bbar_im)], axis=2).astype(BF16)
    cw = jnp.concatenate([out_blocks(c_re.astype(F32)), -out_blocks(c_im.astype(F32))], axis=1).astype(BF16)

    ar = abar_re.reshape(1, -1)
    ai = abar_im.reshape(1, -1)
    sp = {
        "bw": bw, "cw": cw,
        "abar": jnp.concatenate([ar, ai], axis=0),
        "dskip": d_skip.astype(F32).reshape(1, -1),
        "wglu": (0.5 * w_glu.astype(F32)).astype(BF16),
        "bglu": 0.5 * b_glu.astype(F32).reshape(1, -1),
    }

    mic = SSM_MICRO
    lr_step = a_re * dt
    th_step = a_im * dt

    def power(t):
        t = jnp.asarray(np.asarray(t, np.float32))[:, None, None]
        mag = jnp.exp(t * lr_step)
        return mag * jnp.cos(t * th_step), mag * jnp.sin(t * th_step)

    pw_r, pw_i = power(np.arange(mic + 1))
    eye = jnp.asarray(np.eye(g_per_q, dtype=np.float32))

    def rows_in(t):
        t = t.reshape(mic, n_q, g_per_q, n_state, SSM_GROUP)
        t = t[:, :, :, None, :, :] * eye[None, None, :, :, None, None]
        return jnp.transpose(t, (1, 0, 2, 5, 3, 4)).reshape(n_q, mic * LANES, g_per_q * n_state)

    def cols_out(t):
        t = t.reshape(mic, n_q, g_per_q, SSM_GROUP, n_state)
        t = t[:, :, :, None, :, :] * eye[None, None, :, :, None, None]
        return jnp.transpose(t, (1, 2, 5, 0, 3, 4)).reshape(n_q, g_per_q * n_state, mic * LANES)

    zr, zi = power(np.arange(mic - 1, -1, -1))
    wb_r = zr[..., None] * bbar_re[None] - zi[..., None] * bbar_im[None]
    wb_i = zr[..., None] * bbar_im[None] + zi[..., None] * bbar_re[None]
    w = jnp.concatenate([rows_in(wb_r), rows_in(wb_i)], axis=2).astype(BF16)

    c_re = c_re.astype(F32)
    c_im = c_im.astype(F32)
    pr1, pi1 = pw_r[1:mic + 1][:, :, None, :], pw_i[1:mic + 1][:, :, None, :]
    v_r = c_re[None] * pr1 - c_im[None] * pi1
    v_i = c_re[None] * pi1 + c_im[None] * pr1
    v = jnp.concatenate([cols_out(v_r), cols_out(-v_i)], axis=1).astype(BF16)

    tb_r = pw_r[:mic, :, :, None] * bbar_re[None] - pw_i[:mic, :, :, None] * bbar_im[None]
    tb_i = pw_r[:mic, :, :, None] * bbar_im[None] + pw_i[:mic, :, :, None] * bbar_re[None]
    taps = (jnp.sum(c_re[None, :, None, :, :] * jnp.swapaxes(tb_r, 2, 3)[:, :, :, None, :], axis=-1)
            - jnp.sum(c_im[None, :, None, :, :] * jnp.swapaxes(tb_i, 2, 3)[:, :, :, None, :], axis=-1))
    zero = jnp.zeros_like(taps[0])
    grid = jnp.stack([jnp.stack([taps[t - s] if t >= s else zero for t in range(mic)], axis=0)
                      for s in range(mic)], axis=0)
    grid = grid.reshape(mic, mic, n_q, g_per_q, SSM_GROUP, SSM_GROUP)
    grid = grid[:, :, :, :, None, :, :] * eye[None, None, None, :, :, None, None]
    m = jnp.transpose(grid, (2, 0, 3, 5, 1, 4, 6)).reshape(n_q, mic * LANES, mic * LANES).astype(BF16)

    rb = SSM_SCAN_ROWS
    k = np.arange(rb)
    in_r, in_i = power(-mic * k)
    out_r, out_i = power(mic * (k - 1))
    flat = lambda a: a.reshape(a.shape[0], -1)
    sp.update({"w": w, "m": m, "v": v,
               "pin_re": flat(in_r), "pin_im": flat(in_i), "pout_re": flat(out_r), "pout_im": flat(out_i),
               "amic": jnp.concatenate([flat(pw_r[mic:mic + 1]), flat(pw_i[mic:mic + 1])], axis=0),
               "tril": jnp.asarray(np.tril(np.ones((rb, rb), np.float32), -1), BF16)})
    return sp


def _state_to_lanes(h_re, h_im):
    b = h_re.shape[0]
    sc = SSM_LANE_CHUNK
    r = h_re.astype(F32).reshape(b, -1, 1, sc)
    i = h_im.astype(F32).reshape(b, -1, 1, sc)
    return jnp.concatenate([r, i], axis=2).reshape(b, -1)


def _lanes_to_state(h, n_groups, n_state):
    b = h.shape[0]
    h = h.reshape(b, -1, 2, SSM_LANE_CHUNK)
    return (h[:, :, 0, :].reshape(b, n_groups, n_state), h[:, :, 1, :].reshape(b, n_groups, n_state))


def _toeplitz(v, n):
    h = v.shape[0]
    x = jnp.broadcast_to(v[:, None, :], (h, n, 2 * n)).reshape(h, 2 * n * n)
    return x[:, :n * (2 * n - 1)].reshape(h, n, 2 * n - 1)[:, :, :n]


def _prompt_bias_blocks(fvec):
    n = LANES
    h = fvec.shape[0]
    neg = jnp.full((h, n - 1), NEG_INF, F32)
    va = jnp.concatenate([fvec[:, 0:1], neg, jnp.zeros((h, 1), F32), fvec[:, 1:n][:, ::-1]], axis=1)
    vb = jnp.concatenate([fvec[:, 1:n + 1][:, ::-1], jnp.zeros((h, n), F32)], axis=1)
    return jnp.stack([_toeplitz(va, n), _toeplitz(vb, n)], axis=1)


def _decode_bias(fvec, page, dec_seq, n_heads):
    h = fvec.shape[0]
    rows = []
    for i in range(dec_seq):
        last = fvec[:, i + 1:i + 1 + page][:, ::-1]
        new = jnp.concatenate([fvec[:, 0:i + 1][:, ::-1], jnp.full((h, page - i - 1), NEG_INF, F32)], axis=1)
        rows.append(jnp.concatenate([last, new], axis=1))
    per_head = jnp.stack(rows, axis=0)
    same = np.eye(n_heads, dtype=bool)[None, :, None, :]
    near = jnp.where(jnp.asarray(same), per_head[:, :, :, None], NEG_INF)
    near = near.reshape(dec_seq * n_heads, -1)
    mask = np.where(np.broadcast_to(same, (dec_seq, n_heads, page, n_heads)), 0.0, NEG_INF)
    mask = mask.reshape(dec_seq * n_heads, -1).astype(np.float32)
    return jnp.asarray(np.concatenate([mask, mask], axis=0)), jnp.concatenate([near, near], axis=0)


def kernel(x_prompt, x_sample, cache_k, cache_v, state_ssm_re, state_ssm_im, page_table,
           norm_g, w_in, q_norm_g, k_norm_g, lambda_q1, lambda_k1, lambda_q2, lambda_k2,
           subln_g, rel_bias, ssm_a_re, ssm_a_im, ssm_log_dt, ssm_b_re, ssm_b_im,
           ssm_c_re, ssm_c_im, ssm_d, w_glu, b_glu, w_out):
    batch, seq, d_model = x_prompt.shape
    dec_batch, dec_seq, _ = x_sample.shape
    depth, n_pool, page, n_heads, _ = cache_k.shape
    n_pages = page_table.shape[1]
    d_attn = n_heads * V_DIM
    n_groups, n_state = ssm_a_re.shape[1:]
    new_rows = dec_seq * n_heads

    buckets = _bucket_table(2 * LANES)
    far_from = int(np.max(np.nonzero(buckets < N_BUCKETS - 1)[0])) + 1
    assert far_from <= LANES and _bucket_table(seq + page * n_pages)[far_from:].min() == N_BUCKETS - 1
    assert page == LANES and dec_seq < LANES and ATTN_TQ == ATTN_TK

    rel_bias = rel_bias.astype(F32)
    fvec = (rel_bias[buckets].T - rel_bias[N_BUCKETS - 1][:, None]) * LOG2E
    fvec = jnp.where(jnp.asarray(np.arange(2 * LANES) < far_from)[None], fvec, 0.0)
    dtiles = _prompt_bias_blocks(fvec)
    head_mask, near_bias = _decode_bias(fvec, page, dec_seq, n_heads)

    group_avg = jnp.asarray(np.kron(np.eye(d_attn // QK_DIM), np.full((QK_DIM, QK_DIM), 1.0 / QK_DIM)), BF16)
    n_rep = d_attn // QK_DIM
    cache_k_rows = cache_k.reshape(depth * n_pool, page * n_heads, V_DIM)
    cache_v_rows = cache_v.reshape(depth * n_pool, page * n_heads, V_DIM)

    hp = x_prompt.reshape(batch * seq, d_model)
    hs = x_sample.reshape(dec_batch * dec_seq, d_model)
    kp_l, vp_l, ks_l, vs_l = [], [], [], []
    srp_l, sip_l, srs_l, sis_l = [], [], [], []
    for l in range(depth):
        lam_init = _lambda_init(l)
        lam = (jnp.exp(jnp.sum(lambda_q1[l].astype(F32) * lambda_k1[l].astype(F32)))
               - jnp.exp(jnp.sum(lambda_q2[l].astype(F32) * lambda_k2[l].astype(F32))) + lam_init)
        lam = lam.reshape(1).astype(F32)
        ng = norm_g[l].astype(F32).reshape(1, d_model)
        w_bf = w_in[l].astype(BF16)
        gq = jnp.tile(q_norm_g[l].astype(F32), n_rep).reshape(1, d_attn) * (QK_DIM ** -0.5 * LOG2E)
        gk = jnp.tile(k_norm_g[l].astype(F32), n_rep).reshape(1, d_attn)
        sg = (subln_g[l].astype(F32) * (1.0 - lam_init)).reshape(1, V_DIM)
        wo_bf = w_out[l].astype(BF16)
        sp = _ssm_params(ssm_a_re[l], ssm_a_im[l], ssm_log_dt[l], ssm_b_re[l], ssm_b_im[l],
                         ssm_c_re[l], ssm_c_im[l], ssm_d[l], w_glu[l], b_glu[l])

        q1, q2, k4, kb, v4, va, ga, u4, gs = _inproj(hp, ng, w_bf, gq, gk, group_avg, head_rows=False)
        r3 = lambda a: a.reshape(batch, seq, a.shape[-1])
        o_a = _prompt_attention(lam, r3(q1), r3(q2), r3(kb), r3(va), dtiles, r3(ga), sg)
        o_s, hfin = _ssm_prompt(u4.reshape(batch, -1, LANES), r3(gs), sp)
        hp = _outproj(hp, o_a.reshape(batch * seq, d_attn), o_s.reshape(batch * seq, -1), wo_bf)
        kp_l.append(k4.reshape(batch, seq, n_heads, V_DIM).astype(cache_k.dtype))
        vp_l.append(v4.reshape(batch, seq, n_heads, V_DIM).astype(cache_v.dtype))
        hr_p, hi_p = _lanes_to_state(hfin.reshape(batch, -1), n_groups, n_state)
        srp_l.append(hr_p.astype(state_ssm_re.dtype))
        sip_l.append(hi_p.astype(state_ssm_im.dtype))

        q1, q2, k4, v4, ga, u, gs = _inproj(hs, ng, w_bf, gq, gk, group_avg, head_rows=True)
        o_a = _decode_attention(page_table + l * n_pool, lam, q1, q2, k4, v4, ga, head_mask, near_bias, sg,
                                cache_k_rows, cache_v_rows, new_rows)
        h0 = _state_to_lanes(state_ssm_re[l], state_ssm_im[l])
        o_s, hfin = _ssm_sample(u, gs, h0, sp)
        hs = _outproj(hs, o_a.reshape(dec_batch * dec_seq, d_attn).astype(BF16), o_s, wo_bf)
        ks_l.append(k4.reshape(dec_batch, dec_seq, n_heads, V_DIM).astype(cache_k.dtype))
        vs_l.append(v4.reshape(dec_batch, dec_seq, n_heads, V_DIM).astype(cache_v.dtype))
        hr_s, hi_s = _lanes_to_state(hfin, n_groups, n_state)
        srs_l.append(hr_s.astype(state_ssm_re.dtype))
        sis_l.append(hi_s.astype(state_ssm_im.dtype))

    y_prompt = hp.reshape(batch, seq, d_model).astype(x_prompt.dtype)
    y_sample = hs.reshape(dec_batch, dec_seq, d_model).astype(x_sample.dtype)
    return (y_prompt, y_sample, jnp.stack(kp_l), jnp.stack(vp_l), jnp.stack(ks_l), jnp.stack(vs_l),
            jnp.stack(srp_l), jnp.stack(sip_l), jnp.stack(srs_l), jnp.stack(sis_l))
```

```python
import functools
import math

import numpy as np
import jax
import jax.numpy as jnp
from jax import lax
from jax.experimental import pallas as pl
from jax.experimental.pallas import tpu as pltpu

F32 = jnp.float32
BF16 = jnp.bfloat16

QK_DIM = 64
V_DIM = 2 * QK_DIM
N_BUCKETS = 32
MAX_DISTANCE = 128
SSM_GROUP = 16
SSM_STATE = 64
EPS = 1e-6
NEG_INF = -1e30
LOG2E = math.log2(math.e)
GELU_C0 = math.sqrt(2.0 / math.pi)
GELU_C1 = GELU_C0 * 0.044715

LANES = 128
VMEM_LIMIT_BYTES = 56 * 1024 * 1024

ROW_TILE = 512
ATTN_TQ = 512
ATTN_TK = 512
SSM_MICRO = 4
SSM_STEP = 1024
SSM_SCAN_ROWS = 32
SSM_LANE_CHUNK = 512


def _lambda_init(layer):
    return 0.8 - 0.6 * math.exp(-0.3 * layer)


def _bucket_table(n_max):
    n = np.arange(n_max)
    max_exact = N_BUCKETS // 2
    nf = np.maximum(n, 1).astype(np.float32)
    large = max_exact + (np.log(nf / np.float32(max_exact)) / np.float32(math.log(MAX_DISTANCE / max_exact))
                         * np.float32(N_BUCKETS - max_exact)).astype(np.int32)
    large = np.minimum(large, N_BUCKETS - 1)
    return np.where(n < max_exact, n, large).astype(np.int32)


def _silu(x):
    return (0.5 * x) * (1.0 + jnp.tanh(0.5 * x))


def _store_head_rows(ref, val, n_heads):
    rows = val.shape[0]
    for h in range(n_heads):
        ref[pl.ds(h, rows, stride=n_heads), :] = val[:, h * V_DIM:(h + 1) * V_DIM]


def _inproj_kernel(x_ref, ng_ref, w_ref, gq_ref, gk_ref, gavg_ref, *out_refs, head_rows):
    x = x_ref[...]
    ms = jnp.mean(x * x, axis=-1, keepdims=True)
    xb = (x * lax.rsqrt(ms + EPS) * ng_ref[...]).astype(BF16)
    d_seg = gq_ref.shape[1]
    n_heads = d_seg // V_DIM

    def seg(i):
        return jnp.dot(xb, w_ref[:, i * d_seg:(i + 1) * d_seg], preferred_element_type=F32)

    def group_norm(t, g):
        msq = jnp.dot((t * t).astype(BF16), gavg_ref[...], preferred_element_type=F32)
        return t * lax.rsqrt(msq + EPS) * g

    q = group_norm(seg(0), gq_ref[...])
    lane = lax.broadcasted_iota(jnp.int32, q.shape, 1)
    first = (lane % V_DIM) < QK_DIM
    qa = jnp.where(first, q, 0.0)
    qb = jnp.where(first, 0.0, q)
    k = group_norm(seg(1), gk_ref[...])
    v = seg(2)
    ga = seg(3)
    if head_rows:
        q1_ref, q2_ref, k4_ref, v4_ref, ga_ref, u_ref, gs_ref = out_refs
        _store_head_rows(q1_ref, qa, n_heads)
        _store_head_rows(q2_ref, qb, n_heads)
        _store_head_rows(ga_ref, ga, n_heads)
    else:
        q1_ref, q2_ref, k4_ref, kb_ref, v4_ref, va_ref, ga_ref, u_ref, gs_ref = out_refs
        q1_ref[...] = qa.astype(BF16)
        q2_ref[...] = qb.astype(BF16)
        kb_ref[...] = k.astype(BF16)
        vb = v.astype(BF16)
        ones = jnp.ones((v.shape[0], V_DIM), BF16)
        pieces = []
        for h in range(n_heads):
            pieces += [vb[:, h * V_DIM:(h + 1) * V_DIM], ones]
        va_ref[...] = jnp.concatenate(pieces, axis=1)
        ga_ref[...] = ga
    _store_head_rows(k4_ref, k, n_heads)
    _store_head_rows(v4_ref, v, n_heads)
    if head_rows:
        u_ref[...] = seg(4)
    else:
        _store_head_rows(u_ref, seg(4), d_seg // LANES)
    gs_ref[...] = seg(5)


def _inproj(x, ng, w_bf, gq, gk, gavg, head_rows):
    n, d_model = x.shape
    d_seg = gq.shape[1]
    n_heads = d_seg // V_DIM
    tm = min(ROW_TILE, n)
    full = lambda a: pl.BlockSpec(a.shape, lambda i: (0,) * a.ndim)
    wide = lambda w, dt: (jax.ShapeDtypeStruct((n, w), dt), pl.BlockSpec((tm, w), lambda i: (i, 0)))
    tall = lambda dt: (jax.ShapeDtypeStruct((n * n_heads, V_DIM), dt),
                       pl.BlockSpec((tm * n_heads, V_DIM), lambda i: (i, 0)))
    if head_rows:
        outs = [tall(F32), tall(F32), tall(F32), tall(F32), tall(F32), wide(d_seg, F32), wide(d_seg, F32)]
    else:
        outs = [wide(d_seg, BF16), wide(d_seg, BF16), tall(F32), wide(d_seg, BF16), tall(F32),
                wide(2 * d_seg, BF16), wide(d_seg, F32), tall(F32), wide(d_seg, F32)]
    return pl.pallas_call(
        functools.partial(_inproj_kernel, head_rows=head_rows),
        grid=(n // tm,),
        in_specs=[pl.BlockSpec((tm, d_model), lambda i: (i, 0)),
                  full(ng), full(w_bf), full(gq), full(gk), full(gavg)],
        out_specs=[o[1] for o in outs],
        out_shape=[o[0] for o in outs],
        compiler_params=pltpu.CompilerParams(
            dimension_semantics=("arbitrary",), vmem_limit_bytes=VMEM_LIMIT_BYTES),
        name="inproj_samples" if head_rows else "inproj_prompt",
    )(x, ng, w_bf, gq, gk, gavg)


def _diff_epilogue(o1, o2, lam, sg, ga):
    od = o1 - lam * o2
    ms = jnp.mean(od * od, axis=-1, keepdims=True)
    return od * lax.rsqrt(ms + EPS) * sg * _silu(ga)


def _attn_kernel(lam_ref, q1_ref, q2_ref, k_ref, v_ref, ab_ref, ga_ref, sg_ref, o_ref,
                 s_scr, d_scr, m_scr, acc_scr):
    tq = q1_ref.shape[0]
    tk = s_scr.shape[2]
    qi = pl.program_id(2)

    @pl.when(qi == 0)
    def _():
        for t in range(2):
            for a in range(tq // LANES):
                for b in range(tk // LANES):
                    delta = a - b + t * (tk // LANES)
                    if delta in (0, 1):
                        blk = ab_ref[delta]
                    else:
                        blk = jnp.full((LANES, LANES), NEG_INF if delta < 0 else 0.0, F32)
                    d_scr[t, a * LANES:(a + 1) * LANES, b * LANES:(b + 1) * LANES] = blk

    qs = jnp.concatenate([q1_ref[...], q2_ref[...]], axis=0)

    m_scr[...] = jnp.full(m_scr.shape, NEG_INF, F32)
    acc_scr[...] = jnp.zeros(acc_scr.shape, F32)

    def produce(kj, bias, slot):
        start = pl.multiple_of(kj * tk, tk)
        s = lax.dot_general(qs, k_ref[pl.ds(start, tk), :], (((1,), (1,)), ((), ())),
                            preferred_element_type=F32)
        if bias is not None:
            s = s + jnp.concatenate([bias, bias], axis=0)
        s_scr[slot] = s

    def consume(kj, slot):
        start = pl.multiple_of(kj * tk, tk)
        s = s_scr[slot]
        m_old = m_scr[...]
        m_new = jnp.maximum(m_old, jnp.max(s, axis=-1, keepdims=True))
        p = jnp.exp2(s - jnp.concatenate([m_new] * (tk // LANES), axis=1))
        alpha = jnp.exp2(m_old - m_new)
        pv = jnp.dot(p.astype(BF16), v_ref[pl.ds(start, tk), :], preferred_element_type=F32)
        acc_scr[...] = acc_scr[...] * jnp.concatenate([alpha] * (acc_scr.shape[1] // LANES), axis=1) + pv
        m_scr[...] = m_new

    produce(qi, d_scr[0], 0)

    @pl.when(qi >= 1)
    def _():
        produce(qi - 1, d_scr[1], 1)
        consume(qi, 0)

    n_far = jnp.maximum(qi - 1, 0)

    def pair(kj):
        produce(kj, None, 0)
        consume(kj + 1, 1)
        produce(kj - 1, None, 1)
        consume(kj, 0)

    def quad_body(i, carry):
        kj = qi - 2 - 4 * i
        pair(kj)
        pair(kj - 2)
        return carry

    n_quad = n_far // 4
    lax.fori_loop(0, n_quad, quad_body, 0)

    @pl.when(n_far % 4 >= 2)
    def _():
        pair(qi - 2 - 4 * n_quad)

    odd = n_far % 2 == 1

    @pl.when(odd)
    def _():
        produce(0, None, 0)
        consume(1, 1)
        consume(0, 0)

    @pl.when(jnp.logical_and(qi >= 1, jnp.logical_not(odd)))
    def _():
        consume(0, 1)

    @pl.when(qi == 0)
    def _():
        consume(0, 0)

    acc = acc_scr[...]
    o = acc[:, :V_DIM] / acc[:, V_DIM:]
    out = _diff_epilogue(o[:tq], o[tq:], lam_ref[0], sg_ref[...], ga_ref[...])
    o_ref[...] = out.astype(o_ref.dtype)


def _prompt_attention(lam, q1, q2, kb, va, dtiles, ga, sg):
    b, l, d_attn = q1.shape
    n_heads = d_attn // V_DIM
    tq = ATTN_TQ
    return pl.pallas_call(
        _attn_kernel,
        grid=(b, n_heads, l // tq),
        in_specs=[
            pl.BlockSpec(memory_space=pltpu.SMEM),
            pl.BlockSpec((None, tq, V_DIM), lambda bi, h, qi: (bi, qi, h)),
            pl.BlockSpec((None, tq, V_DIM), lambda bi, h, qi: (bi, qi, h)),
            pl.BlockSpec((None, l, V_DIM), lambda bi, h, qi: (bi, 0, h)),
            pl.BlockSpec((None, l, 2 * V_DIM), lambda bi, h, qi: (bi, 0, h)),
            pl.BlockSpec((None, 2, LANES, LANES), lambda bi, h, qi: (h, 0, 0, 0)),
            pl.BlockSpec((None, tq, V_DIM), lambda bi, h, qi: (bi, qi, h)),
            pl.BlockSpec((1, V_DIM), lambda bi, h, qi: (0, 0)),
        ],
        out_specs=pl.BlockSpec((None, tq, V_DIM), lambda bi, h, qi: (bi, qi, h)),
        out_shape=jax.ShapeDtypeStruct((b, l, d_attn), BF16),
        scratch_shapes=[pltpu.VMEM((2, 2 * tq, ATTN_TK), F32), pltpu.VMEM((2, tq, ATTN_TK), F32),
                        pltpu.VMEM((2 * tq, LANES), F32),
                        pltpu.VMEM((2 * tq, 2 * V_DIM), F32)],
        compiler_params=pltpu.CompilerParams(
            dimension_semantics=("arbitrary", "arbitrary", "arbitrary"),
            vmem_limit_bytes=VMEM_LIMIT_BYTES),
        name="prompt_attention",
    )(lam, q1, q2, kb, va, dtiles, ga, sg)


def _decode_kernel(pt_ref, lam_ref, q1_ref, q2_ref, kn_ref, vn_ref, ga_ref, hm_ref, near_ref, sg_ref,
                   ck_hbm, cv_hbm, o_ref, kbuf, vbuf, knew, vnew, ksem, vsem):
    b = pl.program_id(0)
    nb = pl.num_programs(0)
    n_pages = kbuf.shape[1]
    page_rows = kbuf.shape[2]
    new_rows = q1_ref.shape[0]
    slot = b % 2

    def k_copy(seq, s, j):
        return pltpu.make_async_copy(ck_hbm.at[pt_ref[seq, j]], kbuf.at[s, j], ksem.at[s])

    def v_copy(seq, s, j):
        return pltpu.make_async_copy(cv_hbm.at[pt_ref[seq, j]], vbuf.at[s, j], vsem.at[s])

    def start_fetch(seq, s):
        for j in range(n_pages):
            k_copy(seq, s, j).start()
            v_copy(seq, s, j).start()

    @pl.when(b == 0)
    def _():
        start_fetch(0, 0)
        knew[...] = jnp.zeros(knew.shape, knew.dtype)
        vnew[...] = jnp.zeros(vnew.shape, vnew.dtype)

    @pl.when(b + 1 < nb)
    def _():
        start_fetch(b + 1, 1 - slot)

    qx = jnp.concatenate([q1_ref[...], q2_ref[...]], axis=0).astype(BF16)
    knew[0:new_rows, :] = kn_ref[...].astype(BF16)
    vnew[0:new_rows, :] = vn_ref[...].astype(BF16)

    for j in range(n_pages):
        k_copy(b, slot, j).wait()
        v_copy(b, slot, j).wait()

    nt = (((1,), (1,)), ((), ()))
    s_tiles = [lax.dot_general(qx, kbuf[slot, j].astype(BF16), nt, preferred_element_type=F32)
               for j in range(n_pages)]
    s_tiles.append(lax.dot_general(qx, knew[...], nt, preferred_element_type=F32))
    head_mask = hm_ref[...]
    s = jnp.concatenate([t + head_mask for t in s_tiles[:n_pages - 1]]
                        + [jnp.concatenate(s_tiles[n_pages - 1:], axis=1) + near_ref[...]],
                        axis=1)
    m = jnp.max(s, axis=-1, keepdims=True)
    p = jnp.exp2(s - m)
    l_sum = jnp.sum(p, axis=-1, keepdims=True)
    pb = p.astype(BF16)
    acc = jnp.dot(pb[:, n_pages * page_rows:], vnew[...], preferred_element_type=F32)
    for j in range(n_pages):
        acc = acc + jnp.dot(pb[:, j * page_rows:(j + 1) * page_rows], vbuf[slot, j].astype(BF16),
                            preferred_element_type=F32)
    o = acc / l_sum
    o_ref[...] = _diff_epilogue(o[:new_rows], o[new_rows:], lam_ref[0], sg_ref[...], ga_ref[...])


def _decode_attention(page_table, lam, q1, q2, kn, vn, ga, head_mask, near_bias, sg, cache_k, cache_v,
                      new_rows):
    n_seq, n_pages = page_table.shape
    page_rows = cache_k.shape[1]
    seq_spec = pl.BlockSpec((new_rows, V_DIM), lambda bi, pt: (bi, 0))
    grid_spec = pltpu.PrefetchScalarGridSpec(
        num_scalar_prefetch=1,
        grid=(n_seq,),
        in_specs=[
            pl.BlockSpec(memory_space=pltpu.SMEM),
            seq_spec, seq_spec, seq_spec, seq_spec, seq_spec,
            pl.BlockSpec(head_mask.shape, lambda bi, pt: (0, 0)),
            pl.BlockSpec(near_bias.shape, lambda bi, pt: (0, 0)),
            pl.BlockSpec((1, V_DIM), lambda bi, pt: (0, 0)),
            pl.BlockSpec(memory_space=pl.ANY),
            pl.BlockSpec(memory_space=pl.ANY),
        ],
        out_specs=seq_spec,
        scratch_shapes=[
            pltpu.VMEM((2, n_pages, page_rows, V_DIM), cache_k.dtype),
            pltpu.VMEM((2, n_pages, page_rows, V_DIM), cache_v.dtype),
            pltpu.VMEM((page_rows, V_DIM), BF16),
            pltpu.VMEM((page_rows, V_DIM), BF16),
            pltpu.SemaphoreType.DMA((2,)),
            pltpu.SemaphoreType.DMA((2,)),
        ],
    )
    return pl.pallas_call(
        _decode_kernel,
        grid_spec=grid_spec,
        out_shape=jax.ShapeDtypeStruct((n_seq * new_rows, V_DIM), F32),
        compiler_params=pltpu.CompilerParams(
            dimension_semantics=("arbitrary",), vmem_limit_bytes=VMEM_LIMIT_BYTES),
        name="decode_attention",
    )(page_table, lam, q1, q2, kn, vn, ga, head_mask, near_bias, sg, cache_k, cache_v)


def _ssm_tail(y, u, gs, dskip_ref, wglu_half_ref, bglu_half_ref):
    x = y + dskip_ref[...] * u
    inner = x * (GELU_C1 * (x * x) + GELU_C0)
    z = (0.5 * x) * (1.0 + jnp.tanh(inner))
    t_glu = jnp.tanh(jnp.dot(z.astype(BF16), wglu_half_ref[...], preferred_element_type=F32) + bglu_half_ref[...])
    t_gate = jnp.tanh(0.5 * gs)
    return ((z * gs) * 0.25) * (1.0 + t_glu) * (1.0 + t_gate)


def _ssm_prompt_kernel(u4_ref, gs_ref, w_ref, m_ref, v_ref, pin_re_ref, pin_im_ref, pout_re_ref, pout_im_ref,
                       aux_ref, scan_ref, dskip_ref, wglu_ref, bglu_ref, o_ref, hfin_ref,
                       carry_scr, y4_scr):
    n_q = w_ref.shape[0]
    mic = SSM_MICRO
    sc = SSM_LANE_CHUNK
    ts = gs_ref.shape[0]
    r = ts // mic
    rb = pin_re_ref.shape[0]
    n_blk = r // rb
    tile_rows = lambda a: jnp.concatenate([a] * n_blk, axis=0)
    c = pl.program_id(1)

    @pl.when(c == 0)
    def _():
        carry_scr[...] = jnp.zeros(carry_scr.shape, F32)

    for q in range(n_q):
        re_l = slice(2 * q * sc, (2 * q + 1) * sc)
        im_l = slice((2 * q + 1) * sc, 2 * (q + 1) * sc)
        st_l = slice(q * sc, (q + 1) * sc)
        x = jnp.concatenate([u4_ref[pl.ds(n_q * s + q, r, stride=n_q * mic), :] for s in range(mic)],
                            axis=1).astype(BF16)
        e = jnp.dot(x, w_ref[q], preferred_element_type=F32)
        er, em = e[:, :sc], e[:, sc:]
        pw = lambda ref, i: ref[i:i + 1, st_l]
        pir, pii = tile_rows(pin_re_ref[:, st_l]), tile_rows(pin_im_ref[:, st_l])
        xs = jnp.concatenate([er * pir - em * pii, er * pii + em * pir], axis=1).astype(BF16)
        cs = jnp.dot(scan_ref[...], xs, preferred_element_type=F32)
        tot_r, tot_m = cs[r:r + n_blk, :sc], cs[r:r + n_blk, sc:]
        t_r = tot_r * pw(aux_ref, 2) - tot_m * pw(aux_ref, 3)
        t_m = tot_r * pw(aux_ref, 3) + tot_m * pw(aux_ref, 2)
        ar, am = pw(aux_ref, 0), pw(aux_ref, 1)
        br, bm = pw(aux_ref, 4), pw(aux_ref, 5)
        h_r, h_m = carry_scr[:, re_l], carry_scr[:, im_l]
        base_r, base_m = [], []
        for blk in range(n_blk):
            base_r.append(jnp.broadcast_to(ar * h_r - am * h_m, (rb, sc)))
            base_m.append(jnp.broadcast_to(ar * h_m + am * h_r, (rb, sc)))
            h_r, h_m = (br * h_r - bm * h_m + t_r[blk:blk + 1, :], br * h_m + bm * h_r + t_m[blk:blk + 1, :])
        carry_scr[:, re_l] = h_r
        carry_scr[:, im_l] = h_m
        hfin_ref[:, re_l] = h_r
        hfin_ref[:, im_l] = h_m
        sr = cs[:r, :sc] + jnp.concatenate(base_r, axis=0)
        sm = cs[:r, sc:] + jnp.concatenate(base_m, axis=0)
        por, poi = tile_rows(pout_re_ref[:, st_l]), tile_rows(pout_im_ref[:, st_l])
        hp = jnp.concatenate([sr * por - sm * poi, sr * poi + sm * por], axis=1).astype(BF16)
        y = (jnp.dot(x, m_ref[q], preferred_element_type=F32)
             + jnp.dot(hp, v_ref[q], preferred_element_type=F32))
        for s in range(mic):
            y4_scr[pl.ds(n_q * s + q, r, stride=n_q * mic), :] = y[:, s * LANES:(s + 1) * LANES]
    y = jnp.concatenate([y4_scr[pl.ds(q, ts, stride=n_q), :] for q in range(n_q)], axis=1)
    u = jnp.concatenate([u4_ref[pl.ds(q, ts, stride=n_q), :] for q in range(n_q)], axis=1)
    o_ref[...] = _ssm_tail(y, u, gs_ref[...], dskip_ref, wglu_ref, bglu_ref).astype(o_ref.dtype)


def _ssm_prompt(u4, gs, sp):
    b, l, d_ssm = gs.shape
    n_q = d_ssm // LANES
    ts = SSM_STEP
    n_state2 = 2 * sp["aux"].shape[1]
    row = pl.BlockSpec((None, ts, d_ssm), lambda bi, ci: (bi, ci, 0))
    row4 = pl.BlockSpec((None, ts * n_q, LANES), lambda bi, ci: (bi, ci, 0))
    full = lambda a: pl.BlockSpec(a.shape, lambda bi, ci: (0,) * a.ndim)
    names = ["w", "m", "v", "pin_re", "pin_im", "pout_re", "pout_im", "aux", "scan", "dskip", "wglu", "bglu"]
    return pl.pallas_call(
        _ssm_prompt_kernel,
        grid=(b, l // ts),
        in_specs=[row4, row] + [full(sp[n]) for n in names],
        out_specs=[row, pl.BlockSpec((None, 1, n_state2), lambda bi, ci: (bi, 0, 0))],
        out_shape=[jax.ShapeDtypeStruct((b, l, d_ssm), BF16),
                   jax.ShapeDtypeStruct((b, 1, n_state2), F32)],
        scratch_shapes=[pltpu.VMEM((1, n_state2), F32), pltpu.VMEM((ts * n_q, LANES), F32)],
        compiler_params=pltpu.CompilerParams(
            dimension_semantics=("arbitrary", "arbitrary"), vmem_limit_bytes=VMEM_LIMIT_BYTES),
        name="ssm_prompt",
    )(u4, gs, *[sp[n] for n in names])


def _ssm_sample_kernel(u_ref, gs_ref, h0_ref, bw_ref, cw_ref, abar_ref, dskip_ref, wglu_ref, bglu_ref,
                       o_ref, hfin_ref, bu_scr, h_scr):
    n_seq = h0_ref.shape[0]
    dec_seq = u_ref.shape[0] // n_seq
    n_q = bw_ref.shape[0]
    sc = SSM_LANE_CHUNK
    n_lc = 2 * sc // LANES
    u = u_ref[...]
    ys = []
    for q in range(n_q):
        uq = u[:, q * LANES:(q + 1) * LANES].astype(BF16)
        bu = jnp.dot(uq, bw_ref[q], preferred_element_type=F32)
        for c in range(n_lc):
            bu_scr[c] = bu[:, c * LANES:(c + 1) * LANES]
        lanes = slice(q * sc, (q + 1) * sc)
        ar, am = abar_ref[0:1, lanes], abar_ref[1:2, lanes]
        hr = h0_ref[:, 2 * q * sc:(2 * q + 1) * sc]
        hm = h0_ref[:, (2 * q + 1) * sc:2 * (q + 1) * sc]
        for step in range(dec_seq):
            rows = pl.ds(step, n_seq, stride=dec_seq)
            b_all = jnp.concatenate([bu_scr[c, rows, :] for c in range(n_lc)], axis=1)
            br, bi = b_all[:, :sc], b_all[:, sc:]
            hr, hm = ar * hr - am * hm + br, ar * hm + am * hr + bi
            for c in range(n_lc // 2):
                h_scr[c, rows, :] = hr[:, c * LANES:(c + 1) * LANES]
                h_scr[n_lc // 2 + c, rows, :] = hm[:, c * LANES:(c + 1) * LANES]
        hfin_ref[:, 2 * q * sc:(2 * q + 1) * sc] = hr
        hfin_ref[:, (2 * q + 1) * sc:2 * (q + 1) * sc] = hm
        h_all = jnp.concatenate([h_scr[c] for c in range(n_lc)], axis=1)
        ys.append(jnp.dot(h_all.astype(BF16), cw_ref[q], preferred_element_type=F32))
    y = jnp.concatenate(ys, axis=1)
    o_ref[...] = _ssm_tail(y, u, gs_ref[...], dskip_ref, wglu_ref, bglu_ref).astype(o_ref.dtype)


def _ssm_sample(u, gs, h0, sp):
    n, d_ssm = u.shape
    n_seq, n_state2 = h0.shape
    names = ["bw", "cw", "abar", "dskip", "wglu", "bglu"]
    args = [u, gs, h0] + [sp[k] for k in names]
    full = lambda a: pl.BlockSpec(a.shape, lambda i: (0,) * a.ndim)
    return pl.pallas_call(
        _ssm_sample_kernel,
        grid=(1,),
        in_specs=[full(a) for a in args],
        out_specs=[pl.BlockSpec((n, d_ssm), lambda i: (0, 0)),
                   pl.BlockSpec((n_seq, n_state2), lambda i: (0, 0))],
        out_shape=[jax.ShapeDtypeStruct((n, d_ssm), BF16),
                   jax.ShapeDtypeStruct((n_seq, n_state2), F32)],
        scratch_shapes=[pltpu.VMEM((2 * SSM_LANE_CHUNK // LANES, n, LANES), F32),
                        pltpu.VMEM((2 * SSM_LANE_CHUNK // LANES, n, LANES), F32)],
        compiler_params=pltpu.CompilerParams(
            dimension_semantics=("arbitrary",), vmem_limit_bytes=VMEM_LIMIT_BYTES),
        name="ssm_sample",
    )(*args)


def _outproj_kernel(x_ref, oa_ref, os_ref, w_ref, y_ref):
    d_a = oa_ref.shape[1]
    y = x_ref[...] + jnp.dot(oa_ref[...], w_ref[:d_a, :], preferred_element_type=F32)
    y_ref[...] = y + jnp.dot(os_ref[...], w_ref[d_a:, :], preferred_element_type=F32)


def _outproj(x, oa, os_, w_bf):
    n, d_model = x.shape
    tm = min(ROW_TILE, n)
    row = lambda w: pl.BlockSpec((tm, w), lambda i: (i, 0))
    return pl.pallas_call(
        _outproj_kernel,
        grid=(n // tm,),
        in_specs=[row(d_model), row(oa.shape[1]), row(os_.shape[1]),
                  pl.BlockSpec(w_bf.shape, lambda i: (0, 0))],
        out_specs=row(d_model),
        out_shape=jax.ShapeDtypeStruct((n, d_model), F32),
        compiler_params=pltpu.CompilerParams(
            dimension_semantics=("arbitrary",), vmem_limit_bytes=VMEM_LIMIT_BYTES),
        name="outproj",
    )(x, oa, os_, w_bf)


def _ssm_params(a_re, a_im, log_dt, b_re, b_im, c_re, c_im, d_skip, w_glu, b_glu):
    n_groups, n_state = a_re.shape
    g_per_q = LANES // SSM_GROUP
    n_q = n_groups // g_per_q
    dt = jnp.exp(log_dt.astype(F32))[:, None]
    a_re = a_re.astype(F32)
    a_im = a_im.astype(F32)
    mag = jnp.exp(a_re * dt)
    abar_re = mag * jnp.cos(a_im * dt)
    abar_im = mag * jnp.sin(a_im * dt)
    nr = abar_re - 1.0
    den = a_re * a_re + a_im * a_im
    coef_re = (nr * a_re + abar_im * a_im) / den
    coef_im = (abar_im * a_re - nr * a_im) / den
    b_re = b_re.astype(F32)
    b_im = b_im.astype(F32)
    bbar_re = coef_re[..., None] * b_re - coef_im[..., None] * b_im
    bbar_im = coef_re[..., None] * b_im + coef_im[..., None] * b_re

    same_group = (np.arange(g_per_q * SSM_GROUP)[:, None] // SSM_GROUP
                  == np.arange(g_per_q * n_state)[None, :] // n_state)

    def lane_tile(a, reps):
        w = a.shape[-1]
        return jnp.matmul(a, jnp.asarray(np.tile(np.eye(w, dtype=np.float32), (1, reps))),
                          precision=lax.Precision.HIGHEST)

    def rows_in(t):
        n = t.shape[0]
        t = lane_tile(t.reshape(n, n_q, g_per_q * n_state, SSM_GROUP), g_per_q)
        t = jnp.swapaxes(t * jnp.asarray(same_group.T, F32), -1, -2)
        return jnp.swapaxes(t, 0, 1).reshape(n_q, n * LANES, g_per_q * n_state)

    def cols_out(t):
        n = t.shape[0]
        t = lane_tile(t.reshape(n, n_q, g_per_q * SSM_GROUP, n_state), g_per_q)
        t = jnp.swapaxes(t * jnp.asarray(same_group, F32), -1, -2)
        return jnp.transpose(t, (1, 2, 0, 3)).reshape(n_q, g_per_q * n_state, n * LANES)

    bw = jnp.concatenate([rows_in(bbar_re[None]), rows_in(bbar_im[None])], axis=2).astype(BF16)
    cw = jnp.concatenate([cols_out(c_re.astype(F32)[None]), cols_out(-c_im.astype(F32)[None])],
                         axis=1).astype(BF16)

    ar = abar_re.reshape(1, -1)
    ai = abar_im.reshape(1, -1)
    sp = {
        "bw": bw, "cw": cw,
        "abar": jnp.concatenate([ar, ai], axis=0),
        "dskip": d_skip.astype(F32).reshape(1, -1),
        "wglu": (0.5 * w_glu.astype(F32)).astype(BF16),
        "bglu": 0.5 * b_glu.astype(F32).reshape(1, -1),
    }

    mic = SSM_MICRO
    lr_step = a_re * dt
    th_step = a_im * dt

    def power(t):
        t = jnp.asarray(np.asarray(t, np.float32))[:, None, None]
        mag = jnp.exp(t * lr_step)
        return mag * jnp.cos(t * th_step), mag * jnp.sin(t * th_step)

    pw_r, pw_i = power(np.arange(mic + 1))
    zr, zi = power(np.arange(mic - 1, -1, -1))
    wb_r = zr[..., None] * bbar_re[None] - zi[..., None] * bbar_im[None]
    wb_i = zr[..., None] * bbar_im[None] + zi[..., None] * bbar_re[None]
    w = jnp.concatenate([rows_in(wb_r), rows_in(wb_i)], axis=2).astype(BF16)

    c_re = c_re.astype(F32)
    c_im = c_im.astype(F32)
    pr1, pi1 = pw_r[1:mic + 1][:, :, None, :], pw_i[1:mic + 1][:, :, None, :]
    v_r = c_re[None] * pr1 - c_im[None] * pi1
    v_i = c_re[None] * pi1 + c_im[None] * pr1
    v = jnp.concatenate([cols_out(v_r), cols_out(-v_i)], axis=1).astype(BF16)

    tb_r = pw_r[:mic, :, :, None] * bbar_re[None] - pw_i[:mic, :, :, None] * bbar_im[None]
    tb_i = pw_r[:mic, :, :, None] * bbar_im[None] + pw_i[:mic, :, :, None] * bbar_re[None]
    taps = (jnp.sum(c_re[None, :, None, :, :] * jnp.swapaxes(tb_r, 2, 3)[:, :, :, None, :], axis=-1)
            - jnp.sum(c_im[None, :, None, :, :] * jnp.swapaxes(tb_i, 2, 3)[:, :, :, None, :], axis=-1))
    zero = jnp.zeros_like(taps[0])
    grid = jnp.stack([jnp.stack([taps[t - s] if t >= s else zero for t in range(mic)], axis=0)
                      for s in range(mic)], axis=0)
    grid = lane_tile(grid.reshape(mic, mic, n_q, LANES, SSM_GROUP), g_per_q)
    grid = grid * jnp.asarray(same_group[:, ::n_state // SSM_GROUP], F32)
    m = jnp.transpose(grid, (2, 0, 3, 1, 4)).reshape(n_q, mic * LANES, mic * LANES).astype(BF16)

    rb = SSM_SCAN_ROWS
    k = np.arange(rb)
    in_r, in_i = power(-mic * k)
    out_r, out_i = power(mic * (k - 1))
    flat = lambda a: a.reshape(a.shape[0], -1)
    aux_r, aux_i = power(np.array([mic, mic * (rb - 1), mic * rb]))
    aux = jnp.stack([flat(aux_r), flat(aux_i)], axis=1).reshape(6, -1)
    n_rows = SSM_STEP // mic
    blk = np.arange(n_rows) // rb
    strict = (blk[:, None] == blk[None, :]) & (np.arange(n_rows)[:, None] > np.arange(n_rows)[None, :])
    sums = np.arange(n_rows // rb)[:, None] == blk[None, :]
    pad = np.zeros((-(n_rows + n_rows // rb) % 16, n_rows), bool)
    scan = jnp.asarray(np.concatenate([strict, sums, pad], axis=0).astype(np.float32), BF16)
    sp.update({"w": w, "m": m, "v": v, "aux": aux, "scan": scan,
               "pin_re": flat(in_r), "pin_im": flat(in_i), "pout_re": flat(out_r), "pout_im": flat(out_i)})
    return sp


def _state_to_lanes(h_re, h_im):
    b = h_re.shape[0]
    sc = SSM_LANE_CHUNK
    r = h_re.astype(F32).reshape(b, -1, 1, sc)
    i = h_im.astype(F32).reshape(b, -1, 1, sc)
    return jnp.concatenate([r, i], axis=2).reshape(b, -1)


def _lanes_to_state(h, n_groups, n_state):
    b = h.shape[0]
    h = h.reshape(b, -1, 2, SSM_LANE_CHUNK)
    return (h[:, :, 0, :].reshape(b, n_groups, n_state), h[:, :, 1, :].reshape(b, n_groups, n_state))


def _toeplitz(v, n):
    h = v.shape[0]
    x = jnp.broadcast_to(v[:, None, :], (h, n, 2 * n)).reshape(h, 2 * n * n)
    return x[:, :n * (2 * n - 1)].reshape(h, n, 2 * n - 1)[:, :, :n]


def _prompt_bias_blocks(fvec):
    n = LANES
    h = fvec.shape[0]
    neg = jnp.full((h, n - 1), NEG_INF, F32)
    va = jnp.concatenate([fvec[:, 0:1], neg, jnp.zeros((h, 1), F32), fvec[:, 1:n][:, ::-1]], axis=1)
    vb = jnp.concatenate([fvec[:, 1:n + 1][:, ::-1], jnp.zeros((h, n), F32)], axis=1)
    return jnp.stack([_toeplitz(va, n), _toeplitz(vb, n)], axis=1)


def _decode_bias(fvec, page, dec_seq, n_heads):
    h = fvec.shape[0]
    rows = []
    for i in range(dec_seq):
        last = fvec[:, i + 1:i + 1 + page][:, ::-1]
        new = jnp.concatenate([fvec[:, 0:i + 1][:, ::-1], jnp.full((h, page - i - 1), NEG_INF, F32)], axis=1)
        rows.append(jnp.concatenate([last, new], axis=1))
    per_head = jnp.stack(rows, axis=0)
    same = np.eye(n_heads, dtype=bool)[None, :, None, :]
    near = jnp.where(jnp.asarray(same), per_head[:, :, :, None], NEG_INF)
    near = near.reshape(dec_seq * n_heads, -1)
    mask = np.where(np.broadcast_to(same, (dec_seq, n_heads, page, n_heads)), 0.0, NEG_INF)
    mask = mask.reshape(dec_seq * n_heads, -1).astype(np.float32)
    return jnp.asarray(np.concatenate([mask, mask], axis=0)), jnp.concatenate([near, near], axis=0)


def kernel(x_prompt, x_sample, cache_k, cache_v, state_ssm_re, state_ssm_im, page_table,
           norm_g, w_in, q_norm_g, k_norm_g, lambda_q1, lambda_k1, lambda_q2, lambda_k2,
           subln_g, rel_bias, ssm_a_re, ssm_a_im, ssm_log_dt, ssm_b_re, ssm_b_im,
           ssm_c_re, ssm_c_im, ssm_d, w_glu, b_glu, w_out):
    batch, seq, d_model = x_prompt.shape
    dec_batch, dec_seq, _ = x_sample.shape
    depth, n_pool, page, n_heads, _ = cache_k.shape
    n_pages = page_table.shape[1]
    d_attn = n_heads * V_DIM
    n_groups, n_state = ssm_a_re.shape[1:]
    new_rows = dec_seq * n_heads

    buckets = _bucket_table(2 * LANES)
    far_from = int(np.max(np.nonzero(buckets < N_BUCKETS - 1)[0])) + 1
    assert far_from <= LANES and _bucket_table(seq + page * n_pages)[far_from:].min() == N_BUCKETS - 1
    assert page == LANES and dec_seq < LANES and ATTN_TQ == ATTN_TK

    rel_bias = rel_bias.astype(F32)
    fvec = (rel_bias[buckets].T - rel_bias[N_BUCKETS - 1][:, None]) * LOG2E
    fvec = jnp.where(jnp.asarray(np.arange(2 * LANES) < far_from)[None], fvec, 0.0)
    dtiles = _prompt_bias_blocks(fvec)
    head_mask, near_bias = _decode_bias(fvec, page, dec_seq, n_heads)

    group_avg = jnp.asarray(np.kron(np.eye(d_attn // QK_DIM), np.full((QK_DIM, QK_DIM), 1.0 / QK_DIM)), BF16)
    n_rep = d_attn // QK_DIM
    cache_k_rows = cache_k.reshape(depth * n_pool, page * n_heads, V_DIM)
    cache_v_rows = cache_v.reshape(depth * n_pool, page * n_heads, V_DIM)

    hp = x_prompt.reshape(batch * seq, d_model)
    hs = x_sample.reshape(dec_batch * dec_seq, d_model)
    kp_l, vp_l, ks_l, vs_l = [], [], [], []
    srp_l, sip_l, srs_l, sis_l = [], [], [], []
    for l in range(depth):
        lam_init = _lambda_init(l)
        lam = (jnp.exp(jnp.sum(lambda_q1[l].astype(F32) * lambda_k1[l].astype(F32)))
               - jnp.exp(jnp.sum(lambda_q2[l].astype(F32) * lambda_k2[l].astype(F32))) + lam_init)
        lam = lam.reshape(1).astype(F32)
        ng = norm_g[l].astype(F32).reshape(1, d_model)
        w_bf = w_in[l].astype(BF16)
        gq = jnp.tile(q_norm_g[l].astype(F32), n_rep).reshape(1, d_attn) * (QK_DIM ** -0.5 * LOG2E)
        gk = jnp.tile(k_norm_g[l].astype(F32), n_rep).reshape(1, d_attn)
        sg = (subln_g[l].astype(F32) * (1.0 - lam_init)).reshape(1, V_DIM)
        wo_bf = w_out[l].astype(BF16)
        sp = _ssm_params(ssm_a_re[l], ssm_a_im[l], ssm_log_dt[l], ssm_b_re[l], ssm_b_im[l],
                         ssm_c_re[l], ssm_c_im[l], ssm_d[l], w_glu[l], b_glu[l])

        q1, q2, k4, kb, v4, va, ga, u4, gs = _inproj(hp, ng, w_bf, gq, gk, group_avg, head_rows=False)
        r3 = lambda a: a.reshape(batch, seq, a.shape[-1])
        o_a = _prompt_attention(lam, r3(q1), r3(q2), r3(kb), r3(va), dtiles, r3(ga), sg)
        o_s, hfin = _ssm_prompt(u4.reshape(batch, -1, LANES), r3(gs), sp)
        hp = _outproj(hp, o_a.reshape(batch * seq, d_attn), o_s.reshape(batch * seq, -1), wo_bf)
        kp_l.append(k4.reshape(batch, seq, n_heads, V_DIM).astype(cache_k.dtype))
        vp_l.append(v4.reshape(batch, seq, n_heads, V_DIM).astype(cache_v.dtype))
        hr_p, hi_p = _lanes_to_state(hfin.reshape(batch, -1), n_groups, n_state)
        srp_l.append(hr_p.astype(state_ssm_re.dtype))
        sip_l.append(hi_p.astype(state_ssm_im.dtype))

        q1, q2, k4, v4, ga, u, gs = _inproj(hs, ng, w_bf, gq, gk, group_avg, head_rows=True)
        o_a = _decode_attention(page_table + l * n_pool, lam, q1, q2, k4, v4, ga, head_mask, near_bias, sg,
                                cache_k_rows, cache_v_rows, new_rows)
        h0 = _state_to_lanes(state_ssm_re[l], state_ssm_im[l])
        o_s, hfin = _ssm_sample(u, gs, h0, sp)
        hs = _outproj(hs, o_a.reshape(dec_batch * dec_seq, d_attn).astype(BF16), o_s, wo_bf)
        ks_l.append(k4.reshape(dec_batch, dec_seq, n_heads, V_DIM).astype(cache_k.dtype))
        vs_l.append(v4.reshape(dec_batch, dec_seq, n_heads, V_DIM).astype(cache_v.dtype))
        hr_s, hi_s = _lanes_to_state(hfin, n_groups, n_state)
        srs_l.append(hr_s.astype(state_ssm_re.dtype))
        sis_l.append(hi_s.astype(state_ssm_im.dtype))

    y_prompt = hp.reshape(batch, seq, d_model).astype(x_prompt.dtype)
    y_sample = hs.reshape(dec_batch, dec_seq, d_model).astype(x_sample.dtype)
    return (y_prompt, y_sample, jnp.stack(kp_l), jnp.stack(vp_l), jnp.stack(ks_l), jnp.stack(vs_l),
            jnp.stack(srp_l), jnp.stack(sip_l), jnp.stack(srs_l), jnp.stack(sis_l))
```

```python
import functools
import math

import numpy as np
import jax
import jax.numpy as jnp
from jax import lax
from jax.experimental import pallas as pl
from jax.experimental.pallas import tpu as pltpu

F32 = jnp.float32
BF16 = jnp.bfloat16

QK_DIM = 64
V_DIM = 2 * QK_DIM
N_BUCKETS = 32
MAX_DISTANCE = 128
SSM_GROUP = 16
SSM_STATE = 64
EPS = 1e-6
NEG_INF = -1e30
LOG2E = math.log2(math.e)
GELU_C0 = math.sqrt(2.0 / math.pi)
GELU_C1 = GELU_C0 * 0.044715

LANES = 128
VMEM_LIMIT_BYTES = 56 * 1024 * 1024

ROW_TILE = 512
ATTN_TQ = 512
ATTN_TK = 512
ATTN_UNROLL = 4
SSM_MICRO = 4
SSM_STEP = 1024
SSM_SCAN_ROWS = 32
SSM_LANE_CHUNK = 512


def _lambda_init(layer):
    return 0.8 - 0.6 * math.exp(-0.3 * layer)


def _bucket_table(n_max):
    n = np.arange(n_max)
    max_exact = N_BUCKETS // 2
    nf = np.maximum(n, 1).astype(np.float32)
    large = max_exact + (np.log(nf / np.float32(max_exact)) / np.float32(math.log(MAX_DISTANCE / max_exact))
                         * np.float32(N_BUCKETS - max_exact)).astype(np.int32)
    large = np.minimum(large, N_BUCKETS - 1)
    return np.where(n < max_exact, n, large).astype(np.int32)


def _silu(x):
    return (0.5 * x) * (1.0 + jnp.tanh(0.5 * x))


def _store_head_rows(ref, val, n_heads):
    rows = val.shape[0]
    for h in range(n_heads):
        ref[pl.ds(h, rows, stride=n_heads), :] = val[:, h * V_DIM:(h + 1) * V_DIM]


def _inproj_kernel(x_ref, ng_ref, w_ref, gq_ref, gk_ref, gavg_ref, *out_refs, head_rows):
    x = x_ref[...]
    ms = jnp.mean(x * x, axis=-1, keepdims=True)
    xb = (x * lax.rsqrt(ms + EPS) * ng_ref[...]).astype(BF16)
    d_seg = gq_ref.shape[1]
    n_heads = d_seg // V_DIM

    def seg(i):
        return jnp.dot(xb, w_ref[:, i * d_seg:(i + 1) * d_seg], preferred_element_type=F32)

    def group_norm(t, g):
        msq = jnp.dot((t * t).astype(BF16), gavg_ref[...], preferred_element_type=F32)
        return t * lax.rsqrt(msq + EPS) * g

    q = group_norm(seg(0), gq_ref[...])
    lane = lax.broadcasted_iota(jnp.int32, q.shape, 1)
    first = (lane % V_DIM) < QK_DIM
    qa = jnp.where(first, q, 0.0)
    qb = jnp.where(first, 0.0, q)
    k = group_norm(seg(1), gk_ref[...])
    v = seg(2)
    ga = seg(3)
    if head_rows:
        q1_ref, q2_ref, k4_ref, v4_ref, ga_ref, u_ref, gs_ref = out_refs
        _store_head_rows(q1_ref, qa, n_heads)
        _store_head_rows(q2_ref, qb, n_heads)
        _store_head_rows(ga_ref, ga, n_heads)
    else:
        q1_ref, q2_ref, k4_ref, kb_ref, v4_ref, va_ref, ga_ref, u_ref, gs_ref = out_refs
        q1_ref[...] = qa.astype(BF16)
        q2_ref[...] = qb.astype(BF16)
        kb_ref[...] = k.astype(BF16)
        vb = v.astype(BF16)
        ones = jnp.ones((v.shape[0], V_DIM), BF16)
        pieces = []
        for h in range(n_heads):
            pieces += [vb[:, h * V_DIM:(h + 1) * V_DIM], ones]
        va_ref[...] = jnp.concatenate(pieces, axis=1)
        ga_ref[...] = ga
    _store_head_rows(k4_ref, k, n_heads)
    _store_head_rows(v4_ref, v, n_heads)
    if head_rows:
        u_ref[...] = seg(4)
    else:
        _store_head_rows(u_ref, seg(4), d_seg // LANES)
    gs_ref[...] = seg(5)


def _inproj(x, ng, w_bf, gq, gk, gavg, head_rows):
    n, d_model = x.shape
    d_seg = gq.shape[1]
    n_heads = d_seg // V_DIM
    tm = min(ROW_TILE, n)
    full = lambda a: pl.BlockSpec(a.shape, lambda i: (0,) * a.ndim)
    wide = lambda w, dt: (jax.ShapeDtypeStruct((n, w), dt), pl.BlockSpec((tm, w), lambda i: (i, 0)))
    tall = lambda dt: (jax.ShapeDtypeStruct((n * n_heads, V_DIM), dt),
                       pl.BlockSpec((tm * n_heads, V_DIM), lambda i: (i, 0)))
    if head_rows:
        outs = [tall(F32), tall(F32), tall(F32), tall(F32), tall(F32), wide(d_seg, F32), wide(d_seg, F32)]
    else:
        outs = [wide(d_seg, BF16), wide(d_seg, BF16), tall(F32), wide(d_seg, BF16), tall(F32),
                wide(2 * d_seg, BF16), wide(d_seg, F32), tall(F32), wide(d_seg, F32)]
    return pl.pallas_call(
        functools.partial(_inproj_kernel, head_rows=head_rows),
        grid=(n // tm,),
        in_specs=[pl.BlockSpec((tm, d_model), lambda i: (i, 0)),
                  full(ng), full(w_bf), full(gq), full(gk), full(gavg)],
        out_specs=[o[1] for o in outs],
        out_shape=[o[0] for o in outs],
        compiler_params=pltpu.CompilerParams(
            dimension_semantics=("arbitrary",), vmem_limit_bytes=VMEM_LIMIT_BYTES),
        name="inproj_samples" if head_rows else "inproj_prompt",
    )(x, ng, w_bf, gq, gk, gavg)


def _diff_epilogue(o1, o2, lam, sg, ga):
    od = o1 - lam * o2
    ms = jnp.mean(od * od, axis=-1, keepdims=True)
    return od * lax.rsqrt(ms + EPS) * sg * _silu(ga)


def _attn_kernel(tasks_ref, lam_ref, q1_ref, q2_ref, k_ref, v_ref, ab_ref, ga_ref, sg_ref, o_ref,
                 s_scr, d_scr, m_scr, acc_scr):
    tq = s_scr.shape[1] // 2
    tk = s_scr.shape[2]
    p = pl.program_id(2)
    n_steps = pl.num_programs(2)
    n_tasks = tasks_ref.shape[1]
    unroll = min(ATTN_UNROLL, n_tasks - 1)

    @pl.when(p == 0)
    def _():
        for t in range(2):
            for a in range(tq // LANES):
                for b in range(tk // LANES):
                    delta = a - b + t * (tk // LANES)
                    if delta in (0, 1):
                        blk = ab_ref[delta]
                    else:
                        blk = jnp.full((LANES, LANES), NEG_INF if delta < 0 else 0.0, F32)
                    d_scr[t, a * LANES:(a + 1) * LANES, b * LANES:(b + 1) * LANES] = blk
        d_scr[2] = jnp.zeros((tq, tk), F32)

    m_scr[...] = jnp.full(m_scr.shape, NEG_INF, F32)
    acc_scr[...] = jnp.zeros(acc_scr.shape, F32)

    def task(kind, i):
        return tasks_ref[kind * n_steps + p, i]

    def produce(i, slot):
        q0 = pl.multiple_of(task(0, i) * tq, tq)
        k0 = pl.multiple_of(task(1, i) * tk, tk)
        qs = jnp.concatenate([q1_ref[pl.ds(q0, tq), :], q2_ref[pl.ds(q0, tq), :]], axis=0)
        s = lax.dot_general(qs, k_ref[pl.ds(k0, tk), :], (((1,), (1,)), ((), ())),
                            preferred_element_type=F32)
        bias = d_scr[task(2, i)]
        s_scr[slot] = s + jnp.concatenate([bias, bias], axis=0)

    def consume(i, slot):
        k0 = pl.multiple_of(task(1, i) * tk, tk)
        a = task(3, i)
        s = s_scr[slot]
        m_old = m_scr[a]
        m_new = jnp.maximum(m_old, jnp.max(s, axis=-1, keepdims=True))
        pr = jnp.exp2(s - jnp.concatenate([m_new] * (tk // LANES), axis=1))
        alpha = jnp.exp2(m_old - m_new)
        pv = jnp.dot(pr.astype(BF16), v_ref[pl.ds(k0, tk), :], preferred_element_type=F32)
        acc_scr[a] = acc_scr[a] * jnp.concatenate([alpha] * (acc_scr.shape[2] // LANES), axis=1) + pv
        m_scr[a] = m_new

    produce(0, 0)

    def body(j, carry):
        for u in range(unroll):
            i = j * unroll + u
            produce(i + 1, (u + 1) % 2)
            consume(i, u % 2)
        return carry

    lax.fori_loop(0, (n_tasks - 1) // unroll, body, 0)
    consume(n_tasks - 1, (n_tasks - 1) % 2)

    for a, i in ((0, 0), (1, n_tasks - 1)):
        q0 = pl.multiple_of(task(0, i) * tq, tq)
        acc = acc_scr[a]
        o = acc[:, :V_DIM] / acc[:, V_DIM:]
        out = _diff_epilogue(o[:tq], o[tq:], lam_ref[0], sg_ref[...], ga_ref[pl.ds(q0, tq), :])
        o_ref[pl.ds(q0, tq), :] = out.astype(o_ref.dtype)


def _attn_tasks(n_q):
    n_steps = n_q // 2
    tab = np.zeros((4, n_steps, n_q + 1), np.int32)
    for p in range(n_steps):
        i = 0
        for acc, qt in enumerate((p, n_q - 1 - p)):
            for t in range(qt + 1):
                tab[:, p, i] = (qt, qt - t, min(t, 2), acc)
                i += 1
        assert i == n_q + 1
    return tab.reshape(4 * n_steps, n_q + 1)


def _prompt_attention(lam, q1, q2, kb, va, dtiles, ga, sg):
    b, l, d_attn = q1.shape
    n_heads = d_attn // V_DIM
    tq, tk = ATTN_TQ, ATTN_TK
    n_q = l // tq
    assert tq == tk and n_q % 2 == 0 and n_q % min(ATTN_UNROLL, n_q) == 0 and min(ATTN_UNROLL, n_q) % 2 == 0
    tasks = jnp.asarray(_attn_tasks(n_q))
    whole = lambda w: pl.BlockSpec((None, l, w), lambda bi, h, pi, tasks: (bi, 0, h))
    grid_spec = pltpu.PrefetchScalarGridSpec(
        num_scalar_prefetch=1,
        grid=(b, n_heads, n_q // 2),
        in_specs=[
            pl.BlockSpec(memory_space=pltpu.SMEM),
            whole(V_DIM), whole(V_DIM), whole(V_DIM), whole(2 * V_DIM),
            pl.BlockSpec((None, 2, LANES, LANES), lambda bi, h, pi, tasks: (h, 0, 0, 0)),
            whole(V_DIM),
            pl.BlockSpec((1, V_DIM), lambda bi, h, pi, tasks: (0, 0)),
        ],
        out_specs=whole(V_DIM),
        scratch_shapes=[pltpu.VMEM((2, 2 * tq, tk), F32), pltpu.VMEM((3, tq, tk), F32),
                        pltpu.VMEM((2, 2 * tq, LANES), F32),
                        pltpu.VMEM((2, 2 * tq, 2 * V_DIM), F32)],
    )
    return pl.pallas_call(
        _attn_kernel,
        grid_spec=grid_spec,
        out_shape=jax.ShapeDtypeStruct((b, l, d_attn), BF16),
        compiler_params=pltpu.CompilerParams(
            dimension_semantics=("arbitrary", "arbitrary", "arbitrary"),
            vmem_limit_bytes=VMEM_LIMIT_BYTES),
        name="prompt_attention",
    )(tasks, lam, q1, q2, kb, va, dtiles, ga, sg)


def _decode_kernel(pt_ref, lam_ref, q1_ref, q2_ref, kn_ref, vn_ref, ga_ref, hm_ref, near_ref, sg_ref,
                   ck_hbm, cv_hbm, o_ref, kbuf, vbuf, knew, vnew, ksem, vsem):
    b = pl.program_id(0)
    nb = pl.num_programs(0)
    n_pages = kbuf.shape[1]
    page_rows = kbuf.shape[2]
    new_rows = q1_ref.shape[0]
    slot = b % 2

    def k_copy(seq, s, j):
        return pltpu.make_async_copy(ck_hbm.at[pt_ref[seq, j]], kbuf.at[s, j], ksem.at[s])

    def v_copy(seq, s, j):
        return pltpu.make_async_copy(cv_hbm.at[pt_ref[seq, j]], vbuf.at[s, j], vsem.at[s])

    def start_fetch(seq, s):
        for j in range(n_pages):
            k_copy(seq, s, j).start()
            v_copy(seq, s, j).start()

    @pl.when(b == 0)
    def _():
        start_fetch(0, 0)
        knew[...] = jnp.zeros(knew.shape, knew.dtype)
        vnew[...] = jnp.zeros(vnew.shape, vnew.dtype)

    @pl.when(b + 1 < nb)
    def _():
        start_fetch(b + 1, 1 - slot)

    qx = jnp.concatenate([q1_ref[...], q2_ref[...]], axis=0).astype(BF16)
    knew[0:new_rows, :] = kn_ref[...].astype(BF16)
    vnew[0:new_rows, :] = vn_ref[...].astype(BF16)

    for j in range(n_pages):
        k_copy(b, slot, j).wait()
        v_copy(b, slot, j).wait()

    nt = (((1,), (1,)), ((), ()))
    s_tiles = [lax.dot_general(qx, kbuf[slot, j].astype(BF16), nt, preferred_element_type=F32)
               for j in range(n_pages)]
    s_tiles.append(lax.dot_general(qx, knew[...], nt, preferred_element_type=F32))
    head_mask = hm_ref[...]
    s = jnp.concatenate([t + head_mask for t in s_tiles[:n_pages - 1]]
                        + [jnp.concatenate(s_tiles[n_pages - 1:], axis=1) + near_ref[...]],
                        axis=1)
    m = jnp.max(s, axis=-1, keepdims=True)
    p = jnp.exp2(s - m)
    l_sum = jnp.sum(p, axis=-1, keepdims=True)
    pb = p.astype(BF16)
    acc = jnp.dot(pb[:, n_pages * page_rows:], vnew[...], preferred_element_type=F32)
    for j in range(n_pages):
        acc = acc + jnp.dot(pb[:, j * page_rows:(j + 1) * page_rows], vbuf[slot, j].astype(BF16),
                            preferred_element_type=F32)
    o = acc / l_sum
    o_ref[...] = _diff_epilogue(o[:new_rows], o[new_rows:], lam_ref[0], sg_ref[...], ga_ref[...])


def _decode_attention(page_table, lam, q1, q2, kn, vn, ga, head_mask, near_bias, sg, cache_k, cache_v,
                      new_rows):
    n_seq, n_pages = page_table.shape
    page_rows = cache_k.shape[1]
    seq_spec = pl.BlockSpec((new_rows, V_DIM), lambda bi, pt: (bi, 0))
    grid_spec = pltpu.PrefetchScalarGridSpec(
        num_scalar_prefetch=1,
        grid=(n_seq,),
        in_specs=[
            pl.BlockSpec(memory_space=pltpu.SMEM),
            seq_spec, seq_spec, seq_spec, seq_spec, seq_spec,
            pl.BlockSpec(head_mask.shape, lambda bi, pt: (0, 0)),
            pl.BlockSpec(near_bias.shape, lambda bi, pt: (0, 0)),
            pl.BlockSpec((1, V_DIM), lambda bi, pt: (0, 0)),
            pl.BlockSpec(memory_space=pl.ANY),
            pl.BlockSpec(memory_space=pl.ANY),
        ],
        out_specs=seq_spec,
        scratch_shapes=[
            pltpu.VMEM((2, n_pages, page_rows, V_DIM), cache_k.dtype),
            pltpu.VMEM((2, n_pages, page_rows, V_DIM), cache_v.dtype),
            pltpu.VMEM((page_rows, V_DIM), BF16),
            pltpu.VMEM((page_rows, V_DIM), BF16),
            pltpu.SemaphoreType.DMA((2,)),
            pltpu.SemaphoreType.DMA((2,)),
        ],
    )
    return pl.pallas_call(
        _decode_kernel,
        grid_spec=grid_spec,
        out_shape=jax.ShapeDtypeStruct((n_seq * new_rows, V_DIM), F32),
        compiler_params=pltpu.CompilerParams(
            dimension_semantics=("arbitrary",), vmem_limit_bytes=VMEM_LIMIT_BYTES),
        name="decode_attention",
    )(page_table, lam, q1, q2, kn, vn, ga, head_mask, near_bias, sg, cache_k, cache_v)


def _ssm_tail(y, u, gs, dskip_ref, wglu_half_ref, bglu_half_ref):
    x = y + dskip_ref[...] * u
    inner = x * (GELU_C1 * (x * x) + GELU_C0)
    z = (0.5 * x) * (1.0 + jnp.tanh(inner))
    t_glu = jnp.tanh(jnp.dot(z.astype(BF16), wglu_half_ref[...], preferred_element_type=F32) + bglu_half_ref[...])
    t_gate = jnp.tanh(0.5 * gs)
    return ((z * gs) * 0.25) * (1.0 + t_glu) * (1.0 + t_gate)


def _ssm_prompt_kernel(u4_ref, gs_ref, w_ref, m_ref, v_ref, pin_re_ref, pin_im_ref, pout_re_ref, pout_im_ref,
                       aux_ref, scan_ref, dskip_ref, wglu_ref, bglu_ref, o_ref, hfin_ref,
                       carry_scr, y4_scr):
    n_q = w_ref.shape[0]
    mic = SSM_MICRO
    sc = SSM_LANE_CHUNK
    ts = gs_ref.shape[0]
    r = ts // mic
    rb = pin_re_ref.shape[0]
    n_blk = r // rb
    tile_rows = lambda a: jnp.concatenate([a] * n_blk, axis=0)
    c = pl.program_id(1)

    @pl.when(c == 0)
    def _():
        carry_scr[...] = jnp.zeros(carry_scr.shape, F32)

    for q in range(n_q):
        re_l = slice(2 * q * sc, (2 * q + 1) * sc)
        im_l = slice((2 * q + 1) * sc, 2 * (q + 1) * sc)
        st_l = slice(q * sc, (q + 1) * sc)
        x = jnp.concatenate([u4_ref[pl.ds(n_q * s + q, r, stride=n_q * mic), :] for s in range(mic)],
                            axis=1).astype(BF16)
        e = jnp.dot(x, w_ref[q], preferred_element_type=F32)
        er, em = e[:, :sc], e[:, sc:]
        pw = lambda ref, i: ref[i:i + 1, st_l]
        pir, pii = tile_rows(pin_re_ref[:, st_l]), tile_rows(pin_im_ref[:, st_l])
        xs = jnp.concatenate([er * pir - em * pii, er * pii + em * pir], axis=1).astype(BF16)
        cs = jnp.dot(scan_ref[...], xs, preferred_element_type=F32)
        tot_r, tot_m = cs[r:r + n_blk, :sc], cs[r:r + n_blk, sc:]
        t_r = tot_r * pw(aux_ref, 2) - tot_m * pw(aux_ref, 3)
        t_m = tot_r * pw(aux_ref, 3) + tot_m * pw(aux_ref, 2)
        ar, am = pw(aux_ref, 0), pw(aux_ref, 1)
        br, bm = pw(aux_ref, 4), pw(aux_ref, 5)
        h_r, h_m = carry_scr[:, re_l], carry_scr[:, im_l]
        base_r, base_m = [], []
        for blk in range(n_blk):
            base_r.append(jnp.broadcast_to(ar * h_r - am * h_m, (rb, sc)))
            base_m.append(jnp.broadcast_to(ar * h_m + am * h_r, (rb, sc)))
            h_r, h_m = (br * h_r - bm * h_m + t_r[blk:blk + 1, :], br * h_m + bm * h_r + t_m[blk:blk + 1, :])
        carry_scr[:, re_l] = h_r
        carry_scr[:, im_l] = h_m
        hfin_ref[:, re_l] = h_r
        hfin_ref[:, im_l] = h_m
        sr = cs[:r, :sc] + jnp.concatenate(base_r, axis=0)
        sm = cs[:r, sc:] + jnp.concatenate(base_m, axis=0)
        por, poi = tile_rows(pout_re_ref[:, st_l]), tile_rows(pout_im_ref[:, st_l])
        hp = jnp.concatenate([sr * por - sm * poi, sr * poi + sm * por], axis=1).astype(BF16)
        y = (jnp.dot(x, m_ref[q], preferred_element_type=F32)
             + jnp.dot(hp, v_ref[q], preferred_element_type=F32))
        for s in range(mic):
            y4_scr[pl.ds(n_q * s + q, r, stride=n_q * mic), :] = y[:, s * LANES:(s + 1) * LANES]
    y = jnp.concatenate([y4_scr[pl.ds(q, ts, stride=n_q), :] for q in range(n_q)], axis=1)
    u = jnp.concatenate([u4_ref[pl.ds(q, ts, stride=n_q), :] for q in range(n_q)], axis=1)
    o_ref[...] = _ssm_tail(y, u, gs_ref[...], dskip_ref, wglu_ref, bglu_ref).astype(o_ref.dtype)


def _ssm_prompt(u4, gs, sp):
    b, l, d_ssm = gs.shape
    n_q = d_ssm // LANES
    ts = SSM_STEP
    n_state2 = 2 * sp["aux"].shape[1]
    row = pl.BlockSpec((None, ts, d_ssm), lambda bi, ci: (bi, ci, 0))
    row4 = pl.BlockSpec((None, ts * n_q, LANES), lambda bi, ci: (bi, ci, 0))
    full = lambda a: pl.BlockSpec(a.shape, lambda bi, ci: (0,) * a.ndim)
    names = ["w", "m", "v", "pin_re", "pin_im", "pout_re", "pout_im", "aux", "scan", "dskip", "wglu", "bglu"]
    return pl.pallas_call(
        _ssm_prompt_kernel,
        grid=(b, l // ts),
        in_specs=[row4, row] + [full(sp[n]) for n in names],
        out_specs=[row, pl.BlockSpec((None, 1, n_state2), lambda bi, ci: (bi, 0, 0))],
        out_shape=[jax.ShapeDtypeStruct((b, l, d_ssm), BF16),
                   jax.ShapeDtypeStruct((b, 1, n_state2), F32)],
        scratch_shapes=[pltpu.VMEM((1, n_state2), F32), pltpu.VMEM((ts * n_q, LANES), F32)],
        compiler_params=pltpu.CompilerParams(
            dimension_semantics=("arbitrary", "arbitrary"), vmem_limit_bytes=VMEM_LIMIT_BYTES),
        name="ssm_prompt",
    )(u4, gs, *[sp[n] for n in names])


def _ssm_sample_kernel(u_ref, gs_ref, h0_ref, bw_ref, cw_ref, abar_ref, dskip_ref, wglu_ref, bglu_ref,
                       o_ref, hfin_ref, bu_scr, h_scr):
    n_seq = h0_ref.shape[0]
    dec_seq = u_ref.shape[0] // n_seq
    n_q = bw_ref.shape[0]
    sc = SSM_LANE_CHUNK
    n_lc = 2 * sc // LANES
    u = u_ref[...]
    ys = []
    for q in range(n_q):
        uq = u[:, q * LANES:(q + 1) * LANES].astype(BF16)
        bu = jnp.dot(uq, bw_ref[q], preferred_element_type=F32)
        for c in range(n_lc):
            bu_scr[c] = bu[:, c * LANES:(c + 1) * LANES]
        lanes = slice(q * sc, (q + 1) * sc)
        ar, am = abar_ref[0:1, lanes], abar_ref[1:2, lanes]
        hr = h0_ref[:, 2 * q * sc:(2 * q + 1) * sc]
        hm = h0_ref[:, (2 * q + 1) * sc:2 * (q + 1) * sc]
        for step in range(dec_seq):
            rows = pl.ds(step, n_seq, stride=dec_seq)
            b_all = jnp.concatenate([bu_scr[c, rows, :] for c in range(n_lc)], axis=1)
            br, bi = b_all[:, :sc], b_all[:, sc:]
            hr, hm = ar * hr - am * hm + br, ar * hm + am * hr + bi
            for c in range(n_lc // 2):
                h_scr[c, rows, :] = hr[:, c * LANES:(c + 1) * LANES]
                h_scr[n_lc // 2 + c, rows, :] = hm[:, c * LANES:(c + 1) * LANES]
        hfin_ref[:, 2 * q * sc:(2 * q + 1) * sc] = hr
        hfin_ref[:, (2 * q + 1) * sc:2 * (q + 1) * sc] = hm
        h_all = jnp.concatenate([h_scr[c] for c in range(n_lc)], axis=1)
        ys.append(jnp.dot(h_all.astype(BF16), cw_ref[q], preferred_element_type=F32))
    y = jnp.concatenate(ys, axis=1)
    o_ref[...] = _ssm_tail(y, u, gs_ref[...], dskip_ref, wglu_ref, bglu_ref).astype(o_ref.dtype)


def _ssm_sample(u, gs, h0, sp):
    n, d_ssm = u.shape
    n_seq, n_state2 = h0.shape
    names = ["bw", "cw", "abar", "dskip", "wglu", "bglu"]
    args = [u, gs, h0] + [sp[k] for k in names]
    full = lambda a: pl.BlockSpec(a.shape, lambda i: (0,) * a.ndim)
    return pl.pallas_call(
        _ssm_sample_kernel,
        grid=(1,),
        in_specs=[full(a) for a in args],
        out_specs=[pl.BlockSpec((n, d_ssm), lambda i: (0, 0)),
                   pl.BlockSpec((n_seq, n_state2), lambda i: (0, 0))],
        out_shape=[jax.ShapeDtypeStruct((n, d_ssm), BF16),
                   jax.ShapeDtypeStruct((n_seq, n_state2), F32)],
        scratch_shapes=[pltpu.VMEM((2 * SSM_LANE_CHUNK // LANES, n, LANES), F32),
                        pltpu.VMEM((2 * SSM_LANE_CHUNK // LANES, n, LANES), F32)],
        compiler_params=pltpu.CompilerParams(
            dimension_semantics=("arbitrary",), vmem_limit_bytes=VMEM_LIMIT_BYTES),
        name="ssm_sample",
    )(*args)


def _outproj_kernel(x_ref, oa_ref, os_ref, w_ref, y_ref):
    d_a = oa_ref.shape[1]
    y = x_ref[...] + jnp.dot(oa_ref[...], w_ref[:d_a, :], preferred_element_type=F32)
    y_ref[...] = y + jnp.dot(os_ref[...], w_ref[d_a:, :], preferred_element_type=F32)


def _outproj(x, oa, os_, w_bf):
    n, d_model = x.shape
    tm = min(ROW_TILE, n)
    row = lambda w: pl.BlockSpec((tm, w), lambda i: (i, 0))
    return pl.pallas_call(
        _outproj_kernel,
        grid=(n // tm,),
        in_specs=[row(d_model), row(oa.shape[1]), row(os_.shape[1]),
                  pl.BlockSpec(w_bf.shape, lambda i: (0, 0))],
        out_specs=row(d_model),
        out_shape=jax.ShapeDtypeStruct((n, d_model), F32),
        compiler_params=pltpu.CompilerParams(
            dimension_semantics=("arbitrary",), vmem_limit_bytes=VMEM_LIMIT_BYTES),
        name="outproj",
    )(x, oa, os_, w_bf)


def _ssm_params(a_re, a_im, log_dt, b_re, b_im, c_re, c_im, d_skip, w_glu, b_glu):
    n_groups, n_state = a_re.shape
    g_per_q = LANES // SSM_GROUP
    n_q = n_groups // g_per_q
    dt = jnp.exp(log_dt.astype(F32))[:, None]
    a_re = a_re.astype(F32)
    a_im = a_im.astype(F32)
    mag = jnp.exp(a_re * dt)
    abar_re = mag * jnp.cos(a_im * dt)
    abar_im = mag * jnp.sin(a_im * dt)
    nr = abar_re - 1.0
    den = a_re * a_re + a_im * a_im
    coef_re = (nr * a_re + abar_im * a_im) / den
    coef_im = (abar_im * a_re - nr * a_im) / den
    b_re = b_re.astype(F32)
    b_im = b_im.astype(F32)
    bbar_re = coef_re[..., None] * b_re - coef_im[..., None] * b_im
    bbar_im = coef_re[..., None] * b_im + coef_im[..., None] * b_re

    same_group = (np.arange(g_per_q * SSM_GROUP)[:, None] // SSM_GROUP
                  == np.arange(g_per_q * n_state)[None, :] // n_state)

    def lane_tile(a, reps):
        w = a.shape[-1]
        return jnp.matmul(a, jnp.asarray(np.tile(np.eye(w, dtype=np.float32), (1, reps))),
                          precision=lax.Precision.HIGHEST)

    def rows_in(t):
        n = t.shape[0]
        t = lane_tile(t.reshape(n, n_q, g_per_q * n_state, SSM_GROUP), g_per_q)
        t = jnp.swapaxes(t * jnp.asarray(same_group.T, F32), -1, -2)
        return jnp.swapaxes(t, 0, 1).reshape(n_q, n * LANES, g_per_q * n_state)

    def cols_out(t):
        n = t.shape[0]
        t = lane_tile(t.reshape(n, n_q, g_per_q * SSM_GROUP, n_state), g_per_q)
        t = jnp.swapaxes(t * jnp.asarray(same_group, F32), -1, -2)
        return jnp.transpose(t, (1, 2, 0, 3)).reshape(n_q, g_per_q * n_state, n * LANES)

    bw = jnp.concatenate([rows_in(bbar_re[None]), rows_in(bbar_im[None])], axis=2).astype(BF16)
    cw = jnp.concatenate([cols_out(c_re.astype(F32)[None]), cols_out(-c_im.astype(F32)[None])],
                         axis=1).astype(BF16)

    ar = abar_re.reshape(1, -1)
    ai = abar_im.reshape(1, -1)
    sp = {
        "bw": bw, "cw": cw,
        "abar": jnp.concatenate([ar, ai], axis=0),
        "dskip": d_skip.astype(F32).reshape(1, -1),
        "wglu": (0.5 * w_glu.astype(F32)).astype(BF16),
        "bglu": 0.5 * b_glu.astype(F32).reshape(1, -1),
    }

    mic = SSM_MICRO
    lr_step = a_re * dt
    th_step = a_im * dt

    def power(t):
        t = jnp.asarray(np.asarray(t, np.float32))[:, None, None]
        mag = jnp.exp(t * lr_step)
        return mag * jnp.cos(t * th_step), mag * jnp.sin(t * th_step)

    pw_r, pw_i = power(np.arange(mic + 1))
    zr, zi = power(np.arange(mic - 1, -1, -1))
    wb_r = zr[..., None] * bbar_re[None] - zi[..., None] * bbar_im[None]
    wb_i = zr[..., None] * bbar_im[None] + zi[..., None] * bbar_re[None]
    w = jnp.concatenate([rows_in(wb_r), rows_in(wb_i)], axis=2).astype(BF16)

    c_re = c_re.astype(F32)
    c_im = c_im.astype(F32)
    pr1, pi1 = pw_r[1:mic + 1][:, :, None, :], pw_i[1:mic + 1][:, :, None, :]
    v_r = c_re[None] * pr1 - c_im[None] * pi1
    v_i = c_re[None] * pi1 + c_im[None] * pr1
    v = jnp.concatenate([cols_out(v_r), cols_out(-v_i)], axis=1).astype(BF16)

    tb_r = pw_r[:mic, :, :, None] * bbar_re[None] - pw_i[:mic, :, :, None] * bbar_im[None]
    tb_i = pw_r[:mic, :, :, None] * bbar_im[None] + pw_i[:mic, :, :, None] * bbar_re[None]
    taps = (jnp.sum(c_re[None, :, None, :, :] * jnp.swapaxes(tb_r, 2, 3)[:, :, :, None, :], axis=-1)
            - jnp.sum(c_im[None, :, None, :, :] * jnp.swapaxes(tb_i, 2, 3)[:, :, :, None, :], axis=-1))
    zero = jnp.zeros_like(taps[0])
    grid = jnp.stack([jnp.stack([taps[t - s] if t >= s else zero for t in range(mic)], axis=0)
                      for s in range(mic)], axis=0)
    grid = lane_tile(grid.reshape(mic, mic, n_q, LANES, SSM_GROUP), g_per_q)
    grid = grid * jnp.asarray(same_group[:, ::n_state // SSM_GROUP], F32)
    m = jnp.transpose(grid, (2, 0, 3, 1, 4)).reshape(n_q, mic * LANES, mic * LANES).astype(BF16)

    rb = SSM_SCAN_ROWS
    k = np.arange(rb)
    in_r, in_i = power(-mic * k)
    out_r, out_i = power(mic * (k - 1))
    flat = lambda a: a.reshape(a.shape[0], -1)
    aux_r, aux_i = power(np.array([mic, mic * (rb - 1), mic * rb]))
    aux = jnp.stack([flat(aux_r), flat(aux_i)], axis=1).reshape(6, -1)
    n_rows = SSM_STEP // mic
    blk = np.arange(n_rows) // rb
    strict = (blk[:, None] == blk[None, :]) & (np.arange(n_rows)[:, None] > np.arange(n_rows)[None, :])
    sums = np.arange(n_rows // rb)[:, None] == blk[None, :]
    pad = np.zeros((-(n_rows + n_rows // rb) % 16, n_rows), bool)
    scan = jnp.asarray(np.concatenate([strict, sums, pad], axis=0).astype(np.float32), BF16)
    sp.update({"w": w, "m": m, "v": v, "aux": aux, "scan": scan,
               "pin_re": flat(in_r), "pin_im": flat(in_i), "pout_re": flat(out_r), "pout_im": flat(out_i)})
    return sp


def _state_to_lanes(h_re, h_im):
    b = h_re.shape[0]
    sc = SSM_LANE_CHUNK
    r = h_re.astype(F32).reshape(b, -1, 1, sc)
    i = h_im.astype(F32).reshape(b, -1, 1, sc)
    return jnp.concatenate([r, i], axis=2).reshape(b, -1)


def _lanes_to_state(h, n_groups, n_state):
    b = h.shape[0]
    h = h.reshape(b, -1, 2, SSM_LANE_CHUNK)
    return (h[:, :, 0, :].reshape(b, n_groups, n_state), h[:, :, 1, :].reshape(b, n_groups, n_state))


def _toeplitz(v, n):
    h = v.shape[0]
    x = jnp.broadcast_to(v[:, None, :], (h, n, 2 * n)).reshape(h, 2 * n * n)
    return x[:, :n * (2 * n - 1)].reshape(h, n, 2 * n - 1)[:, :, :n]


def _prompt_bias_blocks(fvec):
    n = LANES
    h = fvec.shape[0]
    neg = jnp.full((h, n - 1), NEG_INF, F32)
    va = jnp.concatenate([fvec[:, 0:1], neg, jnp.zeros((h, 1), F32), fvec[:, 1:n][:, ::-1]], axis=1)
    vb = jnp.concatenate([fvec[:, 1:n + 1][:, ::-1], jnp.zeros((h, n), F32)], axis=1)
    return jnp.stack([_toeplitz(va, n), _toeplitz(vb, n)], axis=1)


def _decode_bias(fvec, page, dec_seq, n_heads):
    h = fvec.shape[0]
    rows = []
    for i in range(dec_seq):
        last = fvec[:, i + 1:i + 1 + page][:, ::-1]
        new = jnp.concatenate([fvec[:, 0:i + 1][:, ::-1], jnp.full((h, page - i - 1), NEG_INF, F32)], axis=1)
        rows.append(jnp.concatenate([last, new], axis=1))
    per_head = jnp.stack(rows, axis=0)
    same = np.eye(n_heads, dtype=bool)[None, :, None, :]
    near = jnp.where(jnp.asarray(same), per_head[:, :, :, None], NEG_INF)
    near = near.reshape(dec_seq * n_heads, -1)
    mask = np.where(np.broadcast_to(same, (dec_seq, n_heads, page, n_heads)), 0.0, NEG_INF)
    mask = mask.reshape(dec_seq * n_heads, -1).astype(np.float32)
    return jnp.asarray(np.concatenate([mask, mask], axis=0)), jnp.concatenate([near, near], axis=0)


def kernel(x_prompt, x_sample, cache_k, cache_v, state_ssm_re, state_ssm_im, page_table,
           norm_g, w_in, q_norm_g, k_norm_g, lambda_q1, lambda_k1, lambda_q2, lambda_k2,
           subln_g, rel_bias, ssm_a_re, ssm_a_im, ssm_log_dt, ssm_b_re, ssm_b_im,
           ssm_c_re, ssm_c_im, ssm_d, w_glu, b_glu, w_out):
    batch, seq, d_model = x_prompt.shape
    dec_batch, dec_seq, _ = x_sample.shape
    depth, n_pool, page, n_heads, _ = cache_k.shape
    n_pages = page_table.shape[1]
    d_attn = n_heads * V_DIM
    n_groups, n_state = ssm_a_re.shape[1:]
    new_rows = dec_seq * n_heads

    buckets = _bucket_table(2 * LANES)
    far_from = int(np.max(np.nonzero(buckets < N_BUCKETS - 1)[0])) + 1
    assert far_from <= LANES and _bucket_table(seq + page * n_pages)[far_from:].min() == N_BUCKETS - 1
    assert page == LANES and dec_seq < LANES and ATTN_TQ == ATTN_TK

    rel_bias = rel_bias.astype(F32)
    fvec = (rel_bias[buckets].T - rel_bias[N_BUCKETS - 1][:, None]) * LOG2E
    fvec = jnp.where(jnp.asarray(np.arange(2 * LANES) < far_from)[None], fvec, 0.0)
    dtiles = _prompt_bias_blocks(fvec)
    head_mask, near_bias = _decode_bias(fvec, page, dec_seq, n_heads)

    group_avg = jnp.asarray(np.kron(np.eye(d_attn // QK_DIM), np.full((QK_DIM, QK_DIM), 1.0 / QK_DIM)), BF16)
    n_rep = d_attn // QK_DIM
    cache_k_rows = cache_k.reshape(depth * n_pool, page * n_heads, V_DIM)
    cache_v_rows = cache_v.reshape(depth * n_pool, page * n_heads, V_DIM)

    hp = x_prompt.reshape(batch * seq, d_model)
    hs = x_sample.reshape(dec_batch * dec_seq, d_model)
    kp_l, vp_l, ks_l, vs_l = [], [], [], []
    srp_l, sip_l, srs_l, sis_l = [], [], [], []
    for l in range(depth):
        lam_init = _lambda_init(l)
        lam = (jnp.exp(jnp.sum(lambda_q1[l].astype(F32) * lambda_k1[l].astype(F32)))
               - jnp.exp(jnp.sum(lambda_q2[l].astype(F32) * lambda_k2[l].astype(F32))) + lam_init)
        lam = lam.reshape(1).astype(F32)
        ng = norm_g[l].astype(F32).reshape(1, d_model)
        w_bf = w_in[l].astype(BF16)
        gq = jnp.tile(q_norm_g[l].astype(F32), n_rep).reshape(1, d_attn) * (QK_DIM ** -0.5 * LOG2E)
        gk = jnp.tile(k_norm_g[l].astype(F32), n_rep).reshape(1, d_attn)
        sg = (subln_g[l].astype(F32) * (1.0 - lam_init)).reshape(1, V_DIM)
        wo_bf = w_out[l].astype(BF16)
        sp = _ssm_params(ssm_a_re[l], ssm_a_im[l], ssm_log_dt[l], ssm_b_re[l], ssm_b_im[l],
                         ssm_c_re[l], ssm_c_im[l], ssm_d[l], w_glu[l], b_glu[l])

        q1, q2, k4, kb, v4, va, ga, u4, gs = _inproj(hp, ng, w_bf, gq, gk, group_avg, head_rows=False)
        r3 = lambda a: a.reshape(batch, seq, a.shape[-1])
        o_a = _prompt_attention(lam, r3(q1), r3(q2), r3(kb), r3(va), dtiles, r3(ga), sg)
        o_s, hfin = _ssm_prompt(u4.reshape(batch, -1, LANES), r3(gs), sp)
        hp = _outproj(hp, o_a.reshape(batch * seq, d_attn), o_s.reshape(batch * seq, -1), wo_bf)
        kp_l.append(k4.reshape(batch, seq, n_heads, V_DIM).astype(cache_k.dtype))
        vp_l.append(v4.reshape(batch, seq, n_heads, V_DIM).astype(cache_v.dtype))
        hr_p, hi_p = _lanes_to_state(hfin.reshape(batch, -1), n_groups, n_state)
        srp_l.append(hr_p.astype(state_ssm_re.dtype))
        sip_l.append(hi_p.astype(state_ssm_im.dtype))

        q1, q2, k4, v4, ga, u, gs = _inproj(hs, ng, w_bf, gq, gk, group_avg, head_rows=True)
        o_a = _decode_attention(page_table + l * n_pool, lam, q1, q2, k4, v4, ga, head_mask, near_bias, sg,
                                cache_k_rows, cache_v_rows, new_rows)
        h0 = _state_to_lanes(state_ssm_re[l], state_ssm_im[l])
        o_s, hfin = _ssm_sample(u, gs, h0, sp)
        hs = _outproj(hs, o_a.reshape(dec_batch * dec_seq, d_attn).astype(BF16), o_s, wo_bf)
        ks_l.append(k4.reshape(dec_batch, dec_seq, n_heads, V_DIM).astype(cache_k.dtype))
        vs_l.append(v4.reshape(dec_batch, dec_seq, n_heads, V_DIM).astype(cache_v.dtype))
        hr_s, hi_s = _lanes_to_state(hfin, n_groups, n_state)
        srs_l.append(hr_s.astype(state_ssm_re.dtype))
        sis_l.append(hi_s.astype(state_ssm_im.dtype))

    y_prompt = hp.reshape(batch, seq, d_model).astype(x_prompt.dtype)
    y_sample = hs.reshape(dec_batch, dec_seq, d_model).astype(x_sample.dtype)
    return (y_prompt, y_sample, jnp.stack(kp_l), jnp.stack(vp_l), jnp.stack(ks_l), jnp.stack(vs_l),
            jnp.stack(srp_l), jnp.stack(sip_l), jnp.stack(srs_l), jnp.stack(sis_l))
```

```python
import functools
import math

import numpy as np
import jax
import jax.numpy as jnp
from jax import lax
from jax.experimental import pallas as pl
from jax.experimental.pallas import tpu as pltpu

F32 = jnp.float32
BF16 = jnp.bfloat16

QK_DIM = 64
V_DIM = 2 * QK_DIM
N_BUCKETS = 32
MAX_DISTANCE = 128
SSM_GROUP = 16
SSM_STATE = 64
EPS = 1e-6
NEG_INF = -1e30
LOG2E = math.log2(math.e)
GELU_C0 = math.sqrt(2.0 / math.pi)
GELU_C1 = GELU_C0 * 0.044715

LANES = 128
VMEM_LIMIT_BYTES = 56 * 1024 * 1024

ROW_TILE = 512
ATTN_TQ = 512
ATTN_TK = 512
ATTN_UNROLL = 8
SSM_MICRO = 4
SSM_STEP = 1024
SSM_SCAN_ROWS = 32
SSM_LANE_CHUNK = 512


def _lambda_init(layer):
    return 0.8 - 0.6 * math.exp(-0.3 * layer)


def _bucket_table(n_max):
    n = np.arange(n_max)
    max_exact = N_BUCKETS // 2
    nf = np.maximum(n, 1).astype(np.float32)
    large = max_exact + (np.log(nf / np.float32(max_exact)) / np.float32(math.log(MAX_DISTANCE / max_exact))
                         * np.float32(N_BUCKETS - max_exact)).astype(np.int32)
    large = np.minimum(large, N_BUCKETS - 1)
    return np.where(n < max_exact, n, large).astype(np.int32)


def _silu(x):
    return (0.5 * x) * (1.0 + jnp.tanh(0.5 * x))


def _store_head_rows(ref, val, n_heads):
    rows = val.shape[0]
    for h in range(n_heads):
        ref[pl.ds(h, rows, stride=n_heads), :] = val[:, h * V_DIM:(h + 1) * V_DIM]


def _inproj_kernel(x_ref, ng_ref, w_ref, gq_ref, gk_ref, gavg_ref, *out_refs, head_rows):
    x = x_ref[...]
    ms = jnp.mean(x * x, axis=-1, keepdims=True)
    xb = (x * lax.rsqrt(ms + EPS) * ng_ref[...]).astype(BF16)
    d_seg = gq_ref.shape[1]
    n_heads = d_seg // V_DIM

    def seg(i):
        return jnp.dot(xb, w_ref[:, i * d_seg:(i + 1) * d_seg], preferred_element_type=F32)

    def group_norm(t, g):
        msq = jnp.dot((t * t).astype(BF16), gavg_ref[...], preferred_element_type=F32)
        return t * lax.rsqrt(msq + EPS) * g

    q = group_norm(seg(0), gq_ref[...])
    lane = lax.broadcasted_iota(jnp.int32, q.shape, 1)
    first = (lane % V_DIM) < QK_DIM
    qa = jnp.where(first, q, 0.0)
    qb = jnp.where(first, 0.0, q)
    k = group_norm(seg(1), gk_ref[...])
    v = seg(2)
    ga = seg(3)
    if head_rows:
        q1_ref, q2_ref, k4_ref, v4_ref, ga_ref, u_ref, gs_ref = out_refs
        _store_head_rows(q1_ref, qa, n_heads)
        _store_head_rows(q2_ref, qb, n_heads)
        _store_head_rows(ga_ref, ga, n_heads)
    else:
        q1_ref, q2_ref, k4_ref, kb_ref, v4_ref, va_ref, ga_ref, u_ref, gs_ref = out_refs
        q1_ref[...] = qa.astype(BF16)
        q2_ref[...] = qb.astype(BF16)
        kb_ref[...] = k.astype(BF16)
        vb = v.astype(BF16)
        ones = jnp.ones((v.shape[0], V_DIM), BF16)
        pieces = []
        for h in range(n_heads):
            pieces += [vb[:, h * V_DIM:(h + 1) * V_DIM], ones]
        va_ref[...] = jnp.concatenate(pieces, axis=1)
        ga_ref[...] = ga
    _store_head_rows(k4_ref, k, n_heads)
    _store_head_rows(v4_ref, v, n_heads)
    if head_rows:
        u_ref[...] = seg(4)
    else:
        _store_head_rows(u_ref, seg(4), d_seg // LANES)
    gs_ref[...] = seg(5)


def _inproj(x, ng, w_bf, gq, gk, gavg, head_rows):
    n, d_model = x.shape
    d_seg = gq.shape[1]
    n_heads = d_seg // V_DIM
    tm = min(ROW_TILE, n)
    full = lambda a: pl.BlockSpec(a.shape, lambda i: (0,) * a.ndim)
    wide = lambda w, dt: (jax.ShapeDtypeStruct((n, w), dt), pl.BlockSpec((tm, w), lambda i: (i, 0)))
    tall = lambda dt: (jax.ShapeDtypeStruct((n * n_heads, V_DIM), dt),
                       pl.BlockSpec((tm * n_heads, V_DIM), lambda i: (i, 0)))
    if head_rows:
        outs = [tall(F32), tall(F32), tall(F32), tall(F32), tall(F32), wide(d_seg, F32), wide(d_seg, F32)]
    else:
        outs = [wide(d_seg, BF16), wide(d_seg, BF16), tall(F32), wide(d_seg, BF16), tall(F32),
                wide(2 * d_seg, BF16), wide(d_seg, F32), tall(F32), wide(d_seg, F32)]
    return pl.pallas_call(
        functools.partial(_inproj_kernel, head_rows=head_rows),
        grid=(n // tm,),
        in_specs=[pl.BlockSpec((tm, d_model), lambda i: (i, 0)),
                  full(ng), full(w_bf), full(gq), full(gk), full(gavg)],
        out_specs=[o[1] for o in outs],
        out_shape=[o[0] for o in outs],
        compiler_params=pltpu.CompilerParams(
            dimension_semantics=("arbitrary",), vmem_limit_bytes=VMEM_LIMIT_BYTES),
        name="inproj_samples" if head_rows else "inproj_prompt",
    )(x, ng, w_bf, gq, gk, gavg)


def _diff_epilogue(o1, o2, lam, sg, ga):
    od = o1 - lam * o2
    ms = jnp.mean(od * od, axis=-1, keepdims=True)
    return od * lax.rsqrt(ms + EPS) * sg * _silu(ga)


def _attn_kernel(tasks_ref, lam_ref, q1_ref, q2_ref, k_ref, v_ref, ab_ref, ga_ref, sg_ref, o_ref,
                 s_scr, d_scr, m_scr, acc_scr):
    tq = s_scr.shape[1] // 2
    tk = s_scr.shape[2]
    p = pl.program_id(2)
    n_steps = pl.num_programs(2)
    n_tasks = tasks_ref.shape[1]
    unroll = min(ATTN_UNROLL, n_tasks - 1)

    @pl.when(p == 0)
    def _():
        for t in range(2):
            for a in range(tq // LANES):
                for b in range(tk // LANES):
                    delta = a - b + t * (tk // LANES)
                    if delta in (0, 1):
                        blk = ab_ref[delta]
                    else:
                        blk = jnp.full((LANES, LANES), NEG_INF if delta < 0 else 0.0, F32)
                    d_scr[t, a * LANES:(a + 1) * LANES, b * LANES:(b + 1) * LANES] = blk
        d_scr[2] = jnp.zeros((tq, tk), F32)

    m_scr[...] = jnp.full(m_scr.shape, NEG_INF, F32)
    acc_scr[...] = jnp.zeros(acc_scr.shape, F32)

    def task(kind, i):
        return tasks_ref[kind * n_steps + p, i]

    def produce(i, slot):
        q0 = pl.multiple_of(task(0, i) * tq, tq)
        k0 = pl.multiple_of(task(1, i) * tk, tk)
        qs = jnp.concatenate([q1_ref[pl.ds(q0, tq), :], q2_ref[pl.ds(q0, tq), :]], axis=0)
        s = lax.dot_general(qs, k_ref[pl.ds(k0, tk), :], (((1,), (1,)), ((), ())),
                            preferred_element_type=F32)
        bias = d_scr[task(2, i)]
        s_scr[slot] = s + jnp.concatenate([bias, bias], axis=0)

    def consume(i, slot):
        k0 = pl.multiple_of(task(1, i) * tk, tk)
        a = task(3, i)
        s = s_scr[slot]
        m_old = m_scr[a]
        m_new = jnp.maximum(m_old, jnp.max(s, axis=-1, keepdims=True))
        pr = jnp.exp2(s - jnp.concatenate([m_new] * (tk // LANES), axis=1))
        alpha = jnp.exp2(m_old - m_new)
        pv = jnp.dot(pr.astype(BF16), v_ref[pl.ds(k0, tk), :], preferred_element_type=F32)
        acc_scr[a] = acc_scr[a] * jnp.concatenate([alpha] * (acc_scr.shape[2] // LANES), axis=1) + pv
        m_scr[a] = m_new

    produce(0, 0)

    def body(j, carry):
        for u in range(unroll):
            i = j * unroll + u
            produce(i + 1, (u + 1) % 2)
            consume(i, u % 2)
        return carry

    lax.fori_loop(0, (n_tasks - 1) // unroll, body, 0)
    consume(n_tasks - 1, (n_tasks - 1) % 2)

    for a, i in ((0, 0), (1, n_tasks - 1)):
        q0 = pl.multiple_of(task(0, i) * tq, tq)
        acc = acc_scr[a]
        o = acc[:, :V_DIM] / acc[:, V_DIM:]
        out = _diff_epilogue(o[:tq], o[tq:], lam_ref[0], sg_ref[...], ga_ref[pl.ds(q0, tq), :])
        o_ref[pl.ds(q0, tq), :] = out.astype(o_ref.dtype)


def _attn_tasks(n_q):
    n_steps = n_q // 2
    tab = np.zeros((4, n_steps, n_q + 1), np.int32)
    for p in range(n_steps):
        i = 0
        for acc, qt in enumerate((p, n_q - 1 - p)):
            for t in range(qt + 1):
                tab[:, p, i] = (qt, qt - t, min(t, 2), acc)
                i += 1
        assert i == n_q + 1
    return tab.reshape(4 * n_steps, n_q + 1)


def _prompt_attention(lam, q1, q2, kb, va, dtiles, ga, sg):
    b, l, d_attn = q1.shape
    n_heads = d_attn // V_DIM
    tq, tk = ATTN_TQ, ATTN_TK
    n_q = l // tq
    assert tq == tk and n_q % 2 == 0 and n_q % min(ATTN_UNROLL, n_q) == 0 and min(ATTN_UNROLL, n_q) % 2 == 0
    tasks = jnp.asarray(_attn_tasks(n_q))
    whole = lambda w: pl.BlockSpec((None, l, w), lambda bi, h, pi, tasks: (bi, 0, h))
    grid_spec = pltpu.PrefetchScalarGridSpec(
        num_scalar_prefetch=1,
        grid=(b, n_heads, n_q // 2),
        in_specs=[
            pl.BlockSpec(memory_space=pltpu.SMEM),
            whole(V_DIM), whole(V_DIM), whole(V_DIM), whole(2 * V_DIM),
            pl.BlockSpec((None, 2, LANES, LANES), lambda bi, h, pi, tasks: (h, 0, 0, 0)),
            whole(V_DIM),
            pl.BlockSpec((1, V_DIM), lambda bi, h, pi, tasks: (0, 0)),
        ],
        out_specs=whole(V_DIM),
        scratch_shapes=[pltpu.VMEM((2, 2 * tq, tk), F32), pltpu.VMEM((3, tq, tk), F32),
                        pltpu.VMEM((2, 2 * tq, LANES), F32),
                        pltpu.VMEM((2, 2 * tq, 2 * V_DIM), F32)],
    )
    return pl.pallas_call(
        _attn_kernel,
        grid_spec=grid_spec,
        out_shape=jax.ShapeDtypeStruct((b, l, d_attn), BF16),
        compiler_params=pltpu.CompilerParams(
            dimension_semantics=("arbitrary", "arbitrary", "arbitrary"),
            vmem_limit_bytes=VMEM_LIMIT_BYTES),
        name="prompt_attention",
    )(tasks, lam, q1, q2, kb, va, dtiles, ga, sg)


def _decode_kernel(pt_ref, lam_ref, q1_ref, q2_ref, kn_ref, vn_ref, ga_ref, hm_ref, near_ref, sg_ref,
                   ck_hbm, cv_hbm, o_ref, kbuf, vbuf, knew, vnew, ksem, vsem):
    b = pl.program_id(0)
    nb = pl.num_programs(0)
    n_pages = kbuf.shape[1]
    page_rows = kbuf.shape[2]
    new_rows = q1_ref.shape[0]
    slot = b % 2

    def k_copy(seq, s, j):
        return pltpu.make_async_copy(ck_hbm.at[pt_ref[seq, j]], kbuf.at[s, j], ksem.at[s])

    def v_copy(seq, s, j):
        return pltpu.make_async_copy(cv_hbm.at[pt_ref[seq, j]], vbuf.at[s, j], vsem.at[s])

    def start_fetch(seq, s):
        for j in range(n_pages):
            k_copy(seq, s, j).start()
            v_copy(seq, s, j).start()

    @pl.when(b == 0)
    def _():
        start_fetch(0, 0)
        knew[...] = jnp.zeros(knew.shape, knew.dtype)
        vnew[...] = jnp.zeros(vnew.shape, vnew.dtype)

    @pl.when(b + 1 < nb)
    def _():
        start_fetch(b + 1, 1 - slot)

    qx = jnp.concatenate([q1_ref[...], q2_ref[...]], axis=0).astype(BF16)
    knew[0:new_rows, :] = kn_ref[...].astype(BF16)
    vnew[0:new_rows, :] = vn_ref[...].astype(BF16)

    for j in range(n_pages):
        k_copy(b, slot, j).wait()
        v_copy(b, slot, j).wait()

    nt = (((1,), (1,)), ((), ()))
    s_tiles = [lax.dot_general(qx, kbuf[slot, j].astype(BF16), nt, preferred_element_type=F32)
               for j in range(n_pages)]
    s_tiles.append(lax.dot_general(qx, knew[...], nt, preferred_element_type=F32))
    head_mask = hm_ref[...]
    s = jnp.concatenate([t + head_mask for t in s_tiles[:n_pages - 1]]
                        + [jnp.concatenate(s_tiles[n_pages - 1:], axis=1) + near_ref[...]],
                        axis=1)
    m = jnp.max(s, axis=-1, keepdims=True)
    p = jnp.exp2(s - m)
    l_sum = jnp.sum(p, axis=-1, keepdims=True)
    pb = p.astype(BF16)
    acc = jnp.dot(pb[:, n_pages * page_rows:], vnew[...], preferred_element_type=F32)
    for j in range(n_pages):
        acc = acc + jnp.dot(pb[:, j * page_rows:(j + 1) * page_rows], vbuf[slot, j].astype(BF16),
                            preferred_element_type=F32)
    o = acc / l_sum
    o_ref[...] = _diff_epilogue(o[:new_rows], o[new_rows:], lam_ref[0], sg_ref[...], ga_ref[...])


def _decode_attention(page_table, lam, q1, q2, kn, vn, ga, head_mask, near_bias, sg, cache_k, cache_v,
                      new_rows):
    n_seq, n_pages = page_table.shape
    page_rows = cache_k.shape[1]
    seq_spec = pl.BlockSpec((new_rows, V_DIM), lambda bi, pt: (bi, 0))
    grid_spec = pltpu.PrefetchScalarGridSpec(
        num_scalar_prefetch=1,
        grid=(n_seq,),
        in_specs=[
            pl.BlockSpec(memory_space=pltpu.SMEM),
            seq_spec, seq_spec, seq_spec, seq_spec, seq_spec,
            pl.BlockSpec(head_mask.shape, lambda bi, pt: (0, 0)),
            pl.BlockSpec(near_bias.shape, lambda bi, pt: (0, 0)),
            pl.BlockSpec((1, V_DIM), lambda bi, pt: (0, 0)),
            pl.BlockSpec(memory_space=pl.ANY),
            pl.BlockSpec(memory_space=pl.ANY),
        ],
        out_specs=seq_spec,
        scratch_shapes=[
            pltpu.VMEM((2, n_pages, page_rows, V_DIM), cache_k.dtype),
            pltpu.VMEM((2, n_pages, page_rows, V_DIM), cache_v.dtype),
            pltpu.VMEM((page_rows, V_DIM), BF16),
            pltpu.VMEM((page_rows, V_DIM), BF16),
            pltpu.SemaphoreType.DMA((2,)),
            pltpu.SemaphoreType.DMA((2,)),
        ],
    )
    return pl.pallas_call(
        _decode_kernel,
        grid_spec=grid_spec,
        out_shape=jax.ShapeDtypeStruct((n_seq * new_rows, V_DIM), F32),
        compiler_params=pltpu.CompilerParams(
            dimension_semantics=("arbitrary",), vmem_limit_bytes=VMEM_LIMIT_BYTES),
        name="decode_attention",
    )(page_table, lam, q1, q2, kn, vn, ga, head_mask, near_bias, sg, cache_k, cache_v)


def _ssm_tail(y, u, gs, dskip_ref, wglu_half_ref, bglu_half_ref):
    x = y + dskip_ref[...] * u
    inner = x * (GELU_C1 * (x * x) + GELU_C0)
    z = (0.5 * x) * (1.0 + jnp.tanh(inner))
    t_glu = jnp.tanh(jnp.dot(z.astype(BF16), wglu_half_ref[...], preferred_element_type=F32) + bglu_half_ref[...])
    t_gate = jnp.tanh(0.5 * gs)
    return ((z * gs) * 0.25) * (1.0 + t_glu) * (1.0 + t_gate)


def _ssm_prompt_kernel(u4_ref, gs_ref, w_ref, m_ref, v_ref, pin_re_ref, pin_im_ref, pout_re_ref, pout_im_ref,
                       aux_ref, scan_ref, dskip_ref, wglu_ref, bglu_ref, o_ref, hfin_ref,
                       carry_scr, y4_scr):
    n_q = w_ref.shape[0]
    mic = SSM_MICRO
    sc = SSM_LANE_CHUNK
    ts = gs_ref.shape[0]
    r = ts // mic
    rb = pin_re_ref.shape[0]
    n_blk = r // rb
    tile_rows = lambda a: jnp.concatenate([a] * n_blk, axis=0)
    c = pl.program_id(1)

    @pl.when(c == 0)
    def _():
        carry_scr[...] = jnp.zeros(carry_scr.shape, F32)

    for q in range(n_q):
        re_l = slice(2 * q * sc, (2 * q + 1) * sc)
        im_l = slice((2 * q + 1) * sc, 2 * (q + 1) * sc)
        st_l = slice(q * sc, (q + 1) * sc)
        x = jnp.concatenate([u4_ref[pl.ds(n_q * s + q, r, stride=n_q * mic), :] for s in range(mic)],
                            axis=1).astype(BF16)
        e = jnp.dot(x, w_ref[q], preferred_element_type=F32)
        er, em = e[:, :sc], e[:, sc:]
        pw = lambda ref, i: ref[i:i + 1, st_l]
        pir, pii = tile_rows(pin_re_ref[:, st_l]), tile_rows(pin_im_ref[:, st_l])
        xs = jnp.concatenate([er * pir - em * pii, er * pii + em * pir], axis=1).astype(BF16)
        cs = jnp.dot(scan_ref[...], xs, preferred_element_type=F32)
        tot_r, tot_m = cs[r:r + n_blk, :sc], cs[r:r + n_blk, sc:]
        t_r = tot_r * pw(aux_ref, 2) - tot_m * pw(aux_ref, 3)
        t_m = tot_r * pw(aux_ref, 3) + tot_m * pw(aux_ref, 2)
        ar, am = pw(aux_ref, 0), pw(aux_ref, 1)
        br, bm = pw(aux_ref, 4), pw(aux_ref, 5)
        h_r, h_m = carry_scr[:, re_l], carry_scr[:, im_l]
        base_r, base_m = [], []
        for blk in range(n_blk):
            base_r.append(jnp.broadcast_to(ar * h_r - am * h_m, (rb, sc)))
            base_m.append(jnp.broadcast_to(ar * h_m + am * h_r, (rb, sc)))
            h_r, h_m = (br * h_r - bm * h_m + t_r[blk:blk + 1, :], br * h_m + bm * h_r + t_m[blk:blk + 1, :])
        carry_scr[:, re_l] = h_r
        carry_scr[:, im_l] = h_m
        hfin_ref[:, re_l] = h_r
        hfin_ref[:, im_l] = h_m
        sr = cs[:r, :sc] + jnp.concatenate(base_r, axis=0)
        sm = cs[:r, sc:] + jnp.concatenate(base_m, axis=0)
        por, poi = tile_rows(pout_re_ref[:, st_l]), tile_rows(pout_im_ref[:, st_l])
        hp = jnp.concatenate([sr * por - sm * poi, sr * poi + sm * por], axis=1).astype(BF16)
        y = (jnp.dot(x, m_ref[q], preferred_element_type=F32)
             + jnp.dot(hp, v_ref[q], preferred_element_type=F32))
        for s in range(mic):
            y4_scr[pl.ds(n_q * s + q, r, stride=n_q * mic), :] = y[:, s * LANES:(s + 1) * LANES]
    y = jnp.concatenate([y4_scr[pl.ds(q, ts, stride=n_q), :] for q in range(n_q)], axis=1)
    u = jnp.concatenate([u4_ref[pl.ds(q, ts, stride=n_q), :] for q in range(n_q)], axis=1)
    o_ref[...] = _ssm_tail(y, u, gs_ref[...], dskip_ref, wglu_ref, bglu_ref).astype(o_ref.dtype)


def _ssm_prompt(u4, gs, sp):
    b, l, d_ssm = gs.shape
    n_q = d_ssm // LANES
    ts = SSM_STEP
    n_state2 = 2 * sp["aux"].shape[1]
    row = pl.BlockSpec((None, ts, d_ssm), lambda bi, ci: (bi, ci, 0))
    row4 = pl.BlockSpec((None, ts * n_q, LANES), lambda bi, ci: (bi, ci, 0))
    full = lambda a: pl.BlockSpec(a.shape, lambda bi, ci: (0,) * a.ndim)
    names = ["w", "m", "v", "pin_re", "pin_im", "pout_re", "pout_im", "aux", "scan", "dskip", "wglu", "bglu"]
    return pl.pallas_call(
        _ssm_prompt_kernel,
        grid=(b, l // ts),
        in_specs=[row4, row] + [full(sp[n]) for n in names],
        out_specs=[row, pl.BlockSpec((None, 1, n_state2), lambda bi, ci: (bi, 0, 0))],
        out_shape=[jax.ShapeDtypeStruct((b, l, d_ssm), BF16),
                   jax.ShapeDtypeStruct((b, 1, n_state2), F32)],
        scratch_shapes=[pltpu.VMEM((1, n_state2), F32), pltpu.VMEM((ts * n_q, LANES), F32)],
        compiler_params=pltpu.CompilerParams(
            dimension_semantics=("arbitrary", "arbitrary"), vmem_limit_bytes=VMEM_LIMIT_BYTES),
        name="ssm_prompt",
    )(u4, gs, *[sp[n] for n in names])


def _ssm_sample_kernel(u_ref, gs_ref, h0_ref, bw_ref, cw_ref, abar_ref, dskip_ref, wglu_ref, bglu_ref,
                       o_ref, hfin_ref, bu_scr, h_scr):
    n_seq = h0_ref.shape[0]
    dec_seq = u_ref.shape[0] // n_seq
    n_q = bw_ref.shape[0]
    sc = SSM_LANE_CHUNK
    n_lc = 2 * sc // LANES
    u = u_ref[...]
    ys = []
    for q in range(n_q):
        uq = u[:, q * LANES:(q + 1) * LANES].astype(BF16)
        bu = jnp.dot(uq, bw_ref[q], preferred_element_type=F32)
        for c in range(n_lc):
            bu_scr[c] = bu[:, c * LANES:(c + 1) * LANES]
        lanes = slice(q * sc, (q + 1) * sc)
        ar, am = abar_ref[0:1, lanes], abar_ref[1:2, lanes]
        hr = h0_ref[:, 2 * q * sc:(2 * q + 1) * sc]
        hm = h0_ref[:, (2 * q + 1) * sc:2 * (q + 1) * sc]
        for step in range(dec_seq):
            rows = pl.ds(step, n_seq, stride=dec_seq)
            b_all = jnp.concatenate([bu_scr[c, rows, :] for c in range(n_lc)], axis=1)
            br, bi = b_all[:, :sc], b_all[:, sc:]
            hr, hm = ar * hr - am * hm + br, ar * hm + am * hr + bi
            for c in range(n_lc // 2):
                h_scr[c, rows, :] = hr[:, c * LANES:(c + 1) * LANES]
                h_scr[n_lc // 2 + c, rows, :] = hm[:, c * LANES:(c + 1) * LANES]
        hfin_ref[:, 2 * q * sc:(2 * q + 1) * sc] = hr
        hfin_ref[:, (2 * q + 1) * sc:2 * (q + 1) * sc] = hm
        h_all = jnp.concatenate([h_scr[c] for c in range(n_lc)], axis=1)
        ys.append(jnp.dot(h_all.astype(BF16), cw_ref[q], preferred_element_type=F32))
    y = jnp.concatenate(ys, axis=1)
    o_ref[...] = _ssm_tail(y, u, gs_ref[...], dskip_ref, wglu_ref, bglu_ref).astype(o_ref.dtype)


def _ssm_sample(u, gs, h0, sp):
    n, d_ssm = u.shape
    n_seq, n_state2 = h0.shape
    names = ["bw", "cw", "abar", "dskip", "wglu", "bglu"]
    args = [u, gs, h0] + [sp[k] for k in names]
    full = lambda a: pl.BlockSpec(a.shape, lambda i: (0,) * a.ndim)
    return pl.pallas_call(
        _ssm_sample_kernel,
        grid=(1,),
        in_specs=[full(a) for a in args],
        out_specs=[pl.BlockSpec((n, d_ssm), lambda i: (0, 0)),
                   pl.BlockSpec((n_seq, n_state2), lambda i: (0, 0))],
        out_shape=[jax.ShapeDtypeStruct((n, d_ssm), BF16),
                   jax.ShapeDtypeStruct((n_seq, n_state2), F32)],
        scratch_shapes=[pltpu.VMEM((2 * SSM_LANE_CHUNK // LANES, n, LANES), F32),
                        pltpu.VMEM((2 * SSM_LANE_CHUNK // LANES, n, LANES), F32)],
        compiler_params=pltpu.CompilerParams(
            dimension_semantics=("arbitrary",), vmem_limit_bytes=VMEM_LIMIT_BYTES),
        name="ssm_sample",
    )(*args)


def _outproj_kernel(x_ref, oa_ref, os_ref, w_ref, y_ref):
    d_a = oa_ref.shape[1]
    y = x_ref[...] + jnp.dot(oa_ref[...], w_ref[:d_a, :], preferred_element_type=F32)
    y_ref[...] = y + jnp.dot(os_ref[...], w_ref[d_a:, :], preferred_element_type=F32)


def _outproj(x, oa, os_, w_bf):
    n, d_model = x.shape
    tm = min(ROW_TILE, n)
    row = lambda w: pl.BlockSpec((tm, w), lambda i: (i, 0))
    return pl.pallas_call(
        _outproj_kernel,
        grid=(n // tm,),
        in_specs=[row(d_model), row(oa.shape[1]), row(os_.shape[1]),
                  pl.BlockSpec(w_bf.shape, lambda i: (0, 0))],
        out_specs=row(d_model),
        out_shape=jax.ShapeDtypeStruct((n, d_model), F32),
        compiler_params=pltpu.CompilerParams(
            dimension_semantics=("arbitrary",), vmem_limit_bytes=VMEM_LIMIT_BYTES),
        name="outproj",
    )(x, oa, os_, w_bf)


def _ssm_params(a_re, a_im, log_dt, b_re, b_im, c_re, c_im, d_skip, w_glu, b_glu):
    n_groups, n_state = a_re.shape
    g_per_q = LANES // SSM_GROUP
    n_q = n_groups // g_per_q
    dt = jnp.exp(log_dt.astype(F32))[:, None]
    a_re = a_re.astype(F32)
    a_im = a_im.astype(F32)
    mag = jnp.exp(a_re * dt)
    abar_re = mag * jnp.cos(a_im * dt)
    abar_im = mag * jnp.sin(a_im * dt)
    nr = abar_re - 1.0
    den = a_re * a_re + a_im * a_im
    coef_re = (nr * a_re + abar_im * a_im) / den
    coef_im = (abar_im * a_re - nr * a_im) / den
    b_re = b_re.astype(F32)
    b_im = b_im.astype(F32)
    bbar_re = coef_re[..., None] * b_re - coef_im[..., None] * b_im
    bbar_im = coef_re[..., None] * b_im + coef_im[..., None] * b_re

    same_group = (np.arange(g_per_q * SSM_GROUP)[:, None] // SSM_GROUP
                  == np.arange(g_per_q * n_state)[None, :] // n_state)

    def lane_tile(a, reps):
        w = a.shape[-1]
        return jnp.matmul(a, jnp.asarray(np.tile(np.eye(w, dtype=np.float32), (1, reps))),
                          precision=lax.Precision.HIGHEST)

    def rows_in(t):
        n = t.shape[0]
        t = lane_tile(t.reshape(n, n_q, g_per_q * n_state, SSM_GROUP), g_per_q)
        t = jnp.swapaxes(t * jnp.asarray(same_group.T, F32), -1, -2)
        return jnp.swapaxes(t, 0, 1).reshape(n_q, n * LANES, g_per_q * n_state)

    def cols_out(t):
        n = t.shape[0]
        t = lane_tile(t.reshape(n, n_q, g_per_q * SSM_GROUP, n_state), g_per_q)
        t = jnp.swapaxes(t * jnp.asarray(same_group, F32), -1, -2)
        return jnp.transpose(t, (1, 2, 0, 3)).reshape(n_q, g_per_q * n_state, n * LANES)

    bw = jnp.concatenate([rows_in(bbar_re[None]), rows_in(bbar_im[None])], axis=2).astype(BF16)
    cw = jnp.concatenate([cols_out(c_re.astype(F32)[None]), cols_out(-c_im.astype(F32)[None])],
                         axis=1).astype(BF16)

    ar = abar_re.reshape(1, -1)
    ai = abar_im.reshape(1, -1)
    sp = {
        "bw": bw, "cw": cw,
        "abar": jnp.concatenate([ar, ai], axis=0),
        "dskip": d_skip.astype(F32).reshape(1, -1),
        "wglu": (0.5 * w_glu.astype(F32)).astype(BF16),
        "bglu": 0.5 * b_glu.astype(F32).reshape(1, -1),
    }

    mic = SSM_MICRO
    lr_step = a_re * dt
    th_step = a_im * dt

    def power(t):
        t = jnp.asarray(np.asarray(t, np.float32))[:, None, None]
        mag = jnp.exp(t * lr_step)
        return mag * jnp.cos(t * th_step), mag * jnp.sin(t * th_step)

    pw_r, pw_i = power(np.arange(mic + 1))
    zr, zi = power(np.arange(mic - 1, -1, -1))
    wb_r = zr[..., None] * bbar_re[None] - zi[..., None] * bbar_im[None]
    wb_i = zr[..., None] * bbar_im[None] + zi[..., None] * bbar_re[None]
    w = jnp.concatenate([rows_in(wb_r), rows_in(wb_i)], axis=2).astype(BF16)

    c_re = c_re.astype(F32)
    c_im = c_im.astype(F32)
    pr1, pi1 = pw_r[1:mic + 1][:, :, None, :], pw_i[1:mic + 1][:, :, None, :]
    v_r = c_re[None] * pr1 - c_im[None] * pi1
    v_i = c_re[None] * pi1 + c_im[None] * pr1
    v = jnp.concatenate([cols_out(v_r), cols_out(-v_i)], axis=1).astype(BF16)

    tb_r = pw_r[:mic, :, :, None] * bbar_re[None] - pw_i[:mic, :, :, None] * bbar_im[None]
    tb_i = pw_r[:mic, :, :, None] * bbar_im[None] + pw_i[:mic, :, :, None] * bbar_re[None]
    taps = (jnp.sum(c_re[None, :, None, :, :] * jnp.swapaxes(tb_r, 2, 3)[:, :, :, None, :], axis=-1)
            - jnp.sum(c_im[None, :, None, :, :] * jnp.swapaxes(tb_i, 2, 3)[:, :, :, None, :], axis=-1))
    zero = jnp.zeros_like(taps[0])
    grid = jnp.stack([jnp.stack([taps[t - s] if t >= s else zero for t in range(mic)], axis=0)
                      for s in range(mic)], axis=0)
    grid = lane_tile(grid.reshape(mic, mic, n_q, LANES, SSM_GROUP), g_per_q)
    grid = grid * jnp.asarray(same_group[:, ::n_state // SSM_GROUP], F32)
    m = jnp.transpose(grid, (2, 0, 3, 1, 4)).reshape(n_q, mic * LANES, mic * LANES).astype(BF16)

    rb = SSM_SCAN_ROWS
    k = np.arange(rb)
    in_r, in_i = power(-mic * k)
    out_r, out_i = power(mic * (k - 1))
    flat = lambda a: a.reshape(a.shape[0], -1)
    aux_r, aux_i = power(np.array([mic, mic * (rb - 1), mic * rb]))
    aux = jnp.stack([flat(aux_r), flat(aux_i)], axis=1).reshape(6, -1)
    n_rows = SSM_STEP // mic
    blk = np.arange(n_rows) // rb
    strict = (blk[:, None] == blk[None, :]) & (np.arange(n_rows)[:, None] > np.arange(n_rows)[None, :])
    sums = np.arange(n_rows // rb)[:, None] == blk[None, :]
    pad = np.zeros((-(n_rows + n_rows // rb) % 16, n_rows), bool)
    scan = jnp.asarray(np.concatenate([strict, sums, pad], axis=0).astype(np.float32), BF16)
    sp.update({"w": w, "m": m, "v": v, "aux": aux, "scan": scan,
               "pin_re": flat(in_r), "pin_im": flat(in_i), "pout_re": flat(out_r), "pout_im": flat(out_i)})
    return sp


def _state_to_lanes(h_re, h_im):
    b = h_re.shape[0]
    sc = SSM_LANE_CHUNK
    r = h_re.astype(F32).reshape(b, -1, 1, sc)
    i = h_im.astype(F32).reshape(b, -1, 1, sc)
    return jnp.concatenate([r, i], axis=2).reshape(b, -1)


def _lanes_to_state(h, n_groups, n_state):
    b = h.shape[0]
    h = h.reshape(b, -1, 2, SSM_LANE_CHUNK)
    return (h[:, :, 0, :].reshape(b, n_groups, n_state), h[:, :, 1, :].reshape(b, n_groups, n_state))


def _toeplitz(v, n):
    h = v.shape[0]
    x = jnp.broadcast_to(v[:, None, :], (h, n, 2 * n)).reshape(h, 2 * n * n)
    return x[:, :n * (2 * n - 1)].reshape(h, n, 2 * n - 1)[:, :, :n]


def _prompt_bias_blocks(fvec):
    n = LANES
    h = fvec.shape[0]
    neg = jnp.full((h, n - 1), NEG_INF, F32)
    va = jnp.concatenate([fvec[:, 0:1], neg, jnp.zeros((h, 1), F32), fvec[:, 1:n][:, ::-1]], axis=1)
    vb = jnp.concatenate([fvec[:, 1:n + 1][:, ::-1], jnp.zeros((h, n), F32)], axis=1)
    return jnp.stack([_toeplitz(va, n), _toeplitz(vb, n)], axis=1)


def _decode_bias(fvec, page, dec_seq, n_heads):
    h = fvec.shape[0]
    rows = []
    for i in range(dec_seq):
        last = fvec[:, i + 1:i + 1 + page][:, ::-1]
        new = jnp.concatenate([fvec[:, 0:i + 1][:, ::-1], jnp.full((h, page - i - 1), NEG_INF, F32)], axis=1)
        rows.append(jnp.concatenate([last, new], axis=1))
    per_head = jnp.stack(rows, axis=0)
    same = np.eye(n_heads, dtype=bool)[None, :, None, :]
    near = jnp.where(jnp.asarray(same), per_head[:, :, :, None], NEG_INF)
    near = near.reshape(dec_seq * n_heads, -1)
    mask = np.where(np.broadcast_to(same, (dec_seq, n_heads, page, n_heads)), 0.0, NEG_INF)
    mask = mask.reshape(dec_seq * n_heads, -1).astype(np.float32)
    return jnp.asarray(np.concatenate([mask, mask], axis=0)), jnp.concatenate([near, near], axis=0)


def kernel(x_prompt, x_sample, cache_k, cache_v, state_ssm_re, state_ssm_im, page_table,
           norm_g, w_in, q_norm_g, k_norm_g, lambda_q1, lambda_k1, lambda_q2, lambda_k2,
           subln_g, rel_bias, ssm_a_re, ssm_a_im, ssm_log_dt, ssm_b_re, ssm_b_im,
           ssm_c_re, ssm_c_im, ssm_d, w_glu, b_glu, w_out):
    batch, seq, d_model = x_prompt.shape
    dec_batch, dec_seq, _ = x_sample.shape
    depth, n_pool, page, n_heads, _ = cache_k.shape
    n_pages = page_table.shape[1]
    d_attn = n_heads * V_DIM
    n_groups, n_state = ssm_a_re.shape[1:]
    new_rows = dec_seq * n_heads

    buckets = _bucket_table(2 * LANES)
    far_from = int(np.max(np.nonzero(buckets < N_BUCKETS - 1)[0])) + 1
    assert far_from <= LANES and _bucket_table(seq + page * n_pages)[far_from:].min() == N_BUCKETS - 1
    assert page == LANES and dec_seq < LANES and ATTN_TQ == ATTN_TK

    rel_bias = rel_bias.astype(F32)
    fvec = (rel_bias[buckets].T - rel_bias[N_BUCKETS - 1][:, None]) * LOG2E
    fvec = jnp.where(jnp.asarray(np.arange(2 * LANES) < far_from)[None], fvec, 0.0)
    dtiles = _prompt_bias_blocks(fvec)
    head_mask, near_bias = _decode_bias(fvec, page, dec_seq, n_heads)

    group_avg = jnp.asarray(np.kron(np.eye(d_attn // QK_DIM), np.full((QK_DIM, QK_DIM), 1.0 / QK_DIM)), BF16)
    n_rep = d_attn // QK_DIM
    cache_k_rows = cache_k.reshape(depth * n_pool, page * n_heads, V_DIM)
    cache_v_rows = cache_v.reshape(depth * n_pool, page * n_heads, V_DIM)

    hp = x_prompt.reshape(batch * seq, d_model)
    hs = x_sample.reshape(dec_batch * dec_seq, d_model)
    kp_l, vp_l, ks_l, vs_l = [], [], [], []
    srp_l, sip_l, srs_l, sis_l = [], [], [], []
    for l in range(depth):
        lam_init = _lambda_init(l)
        lam = (jnp.exp(jnp.sum(lambda_q1[l].astype(F32) * lambda_k1[l].astype(F32)))
               - jnp.exp(jnp.sum(lambda_q2[l].astype(F32) * lambda_k2[l].astype(F32))) + lam_init)
        lam = lam.reshape(1).astype(F32)
        ng = norm_g[l].astype(F32).reshape(1, d_model)
        w_bf = w_in[l].astype(BF16)
        gq = jnp.tile(q_norm_g[l].astype(F32), n_rep).reshape(1, d_attn) * (QK_DIM ** -0.5 * LOG2E)
        gk = jnp.tile(k_norm_g[l].astype(F32), n_rep).reshape(1, d_attn)
        sg = (subln_g[l].astype(F32) * (1.0 - lam_init)).reshape(1, V_DIM)
        wo_bf = w_out[l].astype(BF16)
        sp = _ssm_params(ssm_a_re[l], ssm_a_im[l], ssm_log_dt[l], ssm_b_re[l], ssm_b_im[l],
                         ssm_c_re[l], ssm_c_im[l], ssm_d[l], w_glu[l], b_glu[l])

        q1, q2, k4, kb, v4, va, ga, u4, gs = _inproj(hp, ng, w_bf, gq, gk, group_avg, head_rows=False)
        r3 = lambda a: a.reshape(batch, seq, a.shape[-1])
        o_a = _prompt_attention(lam, r3(q1), r3(q2), r3(kb), r3(va), dtiles, r3(ga), sg)
        o_s, hfin = _ssm_prompt(u4.reshape(batch, -1, LANES), r3(gs), sp)
        hp = _outproj(hp, o_a.reshape(batch * seq, d_attn), o_s.reshape(batch * seq, -1), wo_bf)
        kp_l.append(k4.reshape(batch, seq, n_heads, V_DIM).astype(cache_k.dtype))
        vp_l.append(v4.reshape(batch, seq, n_heads, V_DIM).astype(cache_v.dtype))
        hr_p, hi_p = _lanes_to_state(hfin.reshape(batch, -1), n_groups, n_state)
        srp_l.append(hr_p.astype(state_ssm_re.dtype))
        sip_l.append(hi_p.astype(state_ssm_im.dtype))

        q1, q2, k4, v4, ga, u, gs = _inproj(hs, ng, w_bf, gq, gk, group_avg, head_rows=True)
        o_a = _decode_attention(page_table + l * n_pool, lam, q1, q2, k4, v4, ga, head_mask, near_bias, sg,
                                cache_k_rows, cache_v_rows, new_rows)
        h0 = _state_to_lanes(state_ssm_re[l], state_ssm_im[l])
        o_s, hfin = _ssm_sample(u, gs, h0, sp)
        hs = _outproj(hs, o_a.reshape(dec_batch * dec_seq, d_attn).astype(BF16), o_s, wo_bf)
        ks_l.append(k4.reshape(dec_batch, dec_seq, n_heads, V_DIM).astype(cache_k.dtype))
        vs_l.append(v4.reshape(dec_batch, dec_seq, n_heads, V_DIM).astype(cache_v.dtype))
        hr_s, hi_s = _lanes_to_state(hfin, n_groups, n_state)
        srs_l.append(hr_s.astype(state_ssm_re.dtype))
        sis_l.append(hi_s.astype(state_ssm_im.dtype))

    y_prompt = hp.reshape(batch, seq, d_model).astype(x_prompt.dtype)
    y_sample = hs.reshape(dec_batch, dec_seq, d_model).astype(x_sample.dtype)
    return (y_prompt, y_sample, jnp.stack(kp_l), jnp.stack(vp_l), jnp.stack(ks_l), jnp.stack(vs_l),
            jnp.stack(srp_l), jnp.stack(sip_l), jnp.stack(srs_l), jnp.stack(sis_l))
```

```python
import functools
import math

import numpy as np
import jax
import jax.numpy as jnp
from jax import lax
from jax.experimental import pallas as pl
from jax.experimental.pallas import tpu as pltpu

F32 = jnp.float32
BF16 = jnp.bfloat16

QK_DIM = 64
V_DIM = 2 * QK_DIM
N_BUCKETS = 32
MAX_DISTANCE = 128
SSM_GROUP = 16
SSM_STATE = 64
EPS = 1e-6
NEG_INF = -1e30
LOG2E = math.log2(math.e)
GELU_C0 = math.sqrt(2.0 / math.pi)
GELU_C1 = GELU_C0 * 0.044715

LANES = 128
VMEM_LIMIT_BYTES = 56 * 1024 * 1024

ROW_TILE = 512
ATTN_TQ = 512
ATTN_TK = 512
ATTN_UNROLL = 8
DECODE_SEQS_PER_STEP = 2
SSM_MICRO = 4
SSM_STEP = 1024
SSM_SCAN_ROWS = 32
SSM_LANE_CHUNK = 512


def _lambda_init(layer):
    return 0.8 - 0.6 * math.exp(-0.3 * layer)


def _bucket_table(n_max):
    n = np.arange(n_max)
    max_exact = N_BUCKETS // 2
    nf = np.maximum(n, 1).astype(np.float32)
    large = max_exact + (np.log(nf / np.float32(max_exact)) / np.float32(math.log(MAX_DISTANCE / max_exact))
                         * np.float32(N_BUCKETS - max_exact)).astype(np.int32)
    large = np.minimum(large, N_BUCKETS - 1)
    return np.where(n < max_exact, n, large).astype(np.int32)


def _silu(x):
    return (0.5 * x) * (1.0 + jnp.tanh(0.5 * x))


def _store_head_rows(ref, val, n_heads):
    rows = val.shape[0]
    for h in range(n_heads):
        ref[pl.ds(h, rows, stride=n_heads), :] = val[:, h * V_DIM:(h + 1) * V_DIM]


def _inproj_kernel(x_ref, ng_ref, w_ref, gq_ref, gk_ref, gavg_ref, *out_refs, head_rows):
    x = x_ref[...]
    ms = jnp.mean(x * x, axis=-1, keepdims=True)
    xb = (x * lax.rsqrt(ms + EPS) * ng_ref[...]).astype(BF16)
    d_seg = gq_ref.shape[1]
    n_heads = d_seg // V_DIM

    def seg(i):
        return jnp.dot(xb, w_ref[:, i * d_seg:(i + 1) * d_seg], preferred_element_type=F32)

    def group_norm(t, g):
        msq = jnp.dot((t * t).astype(BF16), gavg_ref[...], preferred_element_type=F32)
        return t * lax.rsqrt(msq + EPS) * g

    q = group_norm(seg(0), gq_ref[...])
    lane = lax.broadcasted_iota(jnp.int32, q.shape, 1)
    first = (lane % V_DIM) < QK_DIM
    qa = jnp.where(first, q, 0.0)
    qb = jnp.where(first, 0.0, q)
    k = group_norm(seg(1), gk_ref[...])
    v = seg(2)
    ga = seg(3)
    if head_rows:
        q1_ref, q2_ref, k4_ref, v4_ref, ga_ref, u_ref, gs_ref = out_refs
        _store_head_rows(q1_ref, qa, n_heads)
        _store_head_rows(q2_ref, qb, n_heads)
        _store_head_rows(ga_ref, ga, n_heads)
    else:
        q1_ref, q2_ref, k4_ref, kb_ref, v4_ref, va_ref, ga_ref, u_ref, gs_ref = out_refs
        q1_ref[...] = qa.astype(BF16)
        q2_ref[...] = qb.astype(BF16)
        kb_ref[...] = k.astype(BF16)
        vb = v.astype(BF16)
        ones = jnp.ones((v.shape[0], V_DIM), BF16)
        pieces = []
        for h in range(n_heads):
            pieces += [vb[:, h * V_DIM:(h + 1) * V_DIM], ones]
        va_ref[...] = jnp.concatenate(pieces, axis=1)
        ga_ref[...] = ga
    _store_head_rows(k4_ref, k, n_heads)
    _store_head_rows(v4_ref, v, n_heads)
    if head_rows:
        u_ref[...] = seg(4)
    else:
        _store_head_rows(u_ref, seg(4), d_seg // LANES)
    gs_ref[...] = seg(5)


def _inproj(x, ng, w_bf, gq, gk, gavg, head_rows):
    n, d_model = x.shape
    d_seg = gq.shape[1]
    n_heads = d_seg // V_DIM
    tm = min(ROW_TILE, n)
    full = lambda a: pl.BlockSpec(a.shape, lambda i: (0,) * a.ndim)
    wide = lambda w, dt: (jax.ShapeDtypeStruct((n, w), dt), pl.BlockSpec((tm, w), lambda i: (i, 0)))
    tall = lambda dt: (jax.ShapeDtypeStruct((n * n_heads, V_DIM), dt),
                       pl.BlockSpec((tm * n_heads, V_DIM), lambda i: (i, 0)))
    if head_rows:
        outs = [tall(F32), tall(F32), tall(F32), tall(F32), tall(F32), wide(d_seg, F32), wide(d_seg, F32)]
    else:
        outs = [wide(d_seg, BF16), wide(d_seg, BF16), tall(F32), wide(d_seg, BF16), tall(F32),
                wide(2 * d_seg, BF16), wide(d_seg, F32), tall(F32), wide(d_seg, F32)]
    return pl.pallas_call(
        functools.partial(_inproj_kernel, head_rows=head_rows),
        grid=(n // tm,),
        in_specs=[pl.BlockSpec((tm, d_model), lambda i: (i, 0)),
                  full(ng), full(w_bf), full(gq), full(gk), full(gavg)],
        out_specs=[o[1] for o in outs],
        out_shape=[o[0] for o in outs],
        compiler_params=pltpu.CompilerParams(
            dimension_semantics=("arbitrary",), vmem_limit_bytes=VMEM_LIMIT_BYTES),
        name="inproj_samples" if head_rows else "inproj_prompt",
    )(x, ng, w_bf, gq, gk, gavg)


def _diff_epilogue(o1, o2, lam, sg, ga):
    od = o1 - lam * o2
    ms = jnp.mean(od * od, axis=-1, keepdims=True)
    return od * lax.rsqrt(ms + EPS) * sg * _silu(ga)


def _attention_kernel(tasks_ref, pt_ref, lam_ref,
                      q1a_ref, q2a_ref, q1b_ref, q2b_ref, k_ref, v_ref, ab_ref, gaa_ref, gab_ref, sg_ref,
                      dq1_ref, dq2_ref, dkn_ref, dvn_ref, dga_ref, hm_ref, near_ref, ck_hbm, cv_hbm,
                      o_ref, od_ref,
                      s_scr, d_scr, m_scr, acc_scr, qs_scr, kbuf, vbuf, knew, vnew, ksem, vsem):
    tq = s_scr.shape[1] // 2
    tk = s_scr.shape[2]
    p = pl.program_id(2)
    n_steps = pl.num_programs(2)
    n_tasks = tasks_ref.shape[1]
    unroll = min(ATTN_UNROLL, n_tasks - 1)
    n_trips = (n_tasks - 1) // unroll
    step = (pl.program_id(0) * pl.num_programs(1) + pl.program_id(1)) * n_steps + p
    n_seq, n_pages = pt_ref.shape
    page_rows = kbuf.shape[1]
    lam = lam_ref[0]

    def k_copy(seq, j):
        return pltpu.make_async_copy(ck_hbm.at[pt_ref[seq, j]], kbuf.at[j], ksem.at[0])

    def v_copy(seq, j):
        return pltpu.make_async_copy(cv_hbm.at[pt_ref[seq, j]], vbuf.at[j], vsem.at[0])

    def start_fetch(seq):
        for j in range(n_pages):
            k_copy(seq, j).start()
            v_copy(seq, j).start()

    def decode(local, rows):
        seq = step * DECODE_SEQS_PER_STEP + local
        r0 = local * rows
        qx = jnp.concatenate([dq1_ref[r0:r0 + rows, :], dq2_ref[r0:r0 + rows, :]], axis=0).astype(BF16)
        knew[0:rows, :] = dkn_ref[r0:r0 + rows, :].astype(BF16)
        vnew[0:rows, :] = dvn_ref[r0:r0 + rows, :].astype(BF16)
        for j in range(n_pages):
            k_copy(seq, j).wait()
            v_copy(seq, j).wait()
        nt = (((1,), (1,)), ((), ()))
        s_tiles = [lax.dot_general(qx, kbuf[j].astype(BF16), nt, preferred_element_type=F32)
                   for j in range(n_pages)]
        s_tiles.append(lax.dot_general(qx, knew[...], nt, preferred_element_type=F32))
        head_mask = hm_ref[...]
        s = jnp.concatenate([t + head_mask for t in s_tiles[:n_pages - 1]]
                            + [jnp.concatenate(s_tiles[n_pages - 1:], axis=1) + near_ref[...]], axis=1)
        m = jnp.max(s, axis=-1, keepdims=True)
        pr = jnp.exp2(s - m)
        l_sum = jnp.sum(pr, axis=-1, keepdims=True)
        pb = pr.astype(BF16)
        acc = jnp.dot(pb[:, n_pages * page_rows:], vnew[...], preferred_element_type=F32)
        for j in range(n_pages):
            acc = acc + jnp.dot(pb[:, j * page_rows:(j + 1) * page_rows], vbuf[j].astype(BF16),
                                preferred_element_type=F32)
        o = acc / l_sum
        od_ref[r0:r0 + rows, :] = _diff_epilogue(o[:rows], o[rows:], lam, sg_ref[...], dga_ref[r0:r0 + rows, :])

        @pl.when(seq + 1 < n_seq)
        def _():
            start_fetch(seq + 1)

    @pl.when(step == 0)
    def _():
        start_fetch(0)
        knew[...] = jnp.zeros(knew.shape, knew.dtype)
        vnew[...] = jnp.zeros(vnew.shape, vnew.dtype)

    @pl.when(p == 0)
    def _():
        for t in range(2):
            for a in range(tq // LANES):
                for b in range(tk // LANES):
                    delta = a - b + t * (tk // LANES)
                    if delta in (0, 1):
                        blk = ab_ref[delta]
                    else:
                        blk = jnp.full((LANES, LANES), NEG_INF if delta < 0 else 0.0, F32)
                    d_scr[t, a * LANES:(a + 1) * LANES, b * LANES:(b + 1) * LANES] = blk
        d_scr[2] = jnp.zeros((tq, tk), F32)

    m_scr[...] = jnp.full(m_scr.shape, NEG_INF, F32)
    acc_scr[...] = jnp.zeros(acc_scr.shape, F32)
    qs_scr[0] = jnp.concatenate([q1a_ref[...], q2a_ref[...]], axis=0)
    qs_scr[1] = jnp.concatenate([q1b_ref[...], q2b_ref[...]], axis=0)

    def task(kind, i):
        return tasks_ref[kind * n_steps + p, i]

    def produce(i, slot):
        k0 = pl.multiple_of(task(1, i) * tk, tk)
        s = lax.dot_general(qs_scr[task(3, i)], k_ref[pl.ds(k0, tk), :], (((1,), (1,)), ((), ())),
                            preferred_element_type=F32)
        bias = d_scr[task(2, i)]
        s_scr[slot] = s + jnp.concatenate([bias, bias], axis=0)

    def consume(i, slot):
        k0 = pl.multiple_of(task(1, i) * tk, tk)
        a = task(3, i)
        s = s_scr[slot]
        m_old = m_scr[a]
        m_new = jnp.maximum(m_old, jnp.max(s, axis=-1, keepdims=True))
        pr = jnp.exp2(s - jnp.concatenate([m_new] * (tk // LANES), axis=1))
        alpha = jnp.exp2(m_old - m_new)
        pv = jnp.dot(pr.astype(BF16), v_ref[pl.ds(k0, tk), :], preferred_element_type=F32)
        acc_scr[a] = acc_scr[a] * jnp.concatenate([alpha] * (acc_scr.shape[2] // LANES), axis=1) + pv
        m_scr[a] = m_new

    dec_rows = od_ref.shape[0] // DECODE_SEQS_PER_STEP
    produce(0, 0)
    for trip in range(n_trips):
        if trip < DECODE_SEQS_PER_STEP:
            decode(trip, dec_rows)
        for u in range(unroll):
            i = trip * unroll + u
            produce(i + 1, (u + 1) % 2)
            consume(i, u % 2)
    for local in range(n_trips, DECODE_SEQS_PER_STEP):
        decode(local, dec_rows)
    consume(n_tasks - 1, (n_tasks - 1) % 2)

    for a, (ga_ref, i) in enumerate(((gaa_ref, 0), (gab_ref, n_tasks - 1))):
        q0 = pl.multiple_of(task(0, i) * tq, tq)
        acc = acc_scr[a]
        o = acc[:, :V_DIM] / acc[:, V_DIM:]
        out = _diff_epilogue(o[:tq], o[tq:], lam, sg_ref[...], ga_ref[...])
        o_ref[pl.ds(q0, tq), :] = out.astype(o_ref.dtype)


def _attn_tasks(n_q):
    n_steps = n_q // 2
    tab = np.zeros((4, n_steps, n_q + 1), np.int32)
    for p in range(n_steps):
        i = 0
        for acc, qt in enumerate((p, n_q - 1 - p)):
            for t in range(qt + 1):
                tab[:, p, i] = (qt, qt - t, min(t, 2), acc)
                i += 1
        assert i == n_q + 1
    return tab.reshape(4 * n_steps, n_q + 1)


def _attention(lam, sg, q1, q2, kb, va, dtiles, ga,
               page_table, dq1, dq2, dkn, dvn, dga, head_mask, near_bias, cache_k, cache_v, new_rows):
    b, l, d_attn = q1.shape
    n_heads = d_attn // V_DIM
    tq, tk = ATTN_TQ, ATTN_TK
    n_q = l // tq
    n_steps = n_q // 2
    unroll = min(ATTN_UNROLL, n_q)
    n_seq, n_pages = page_table.shape
    page_rows = cache_k.shape[1]
    total_steps = b * n_heads * n_steps
    assert tq == tk and n_q % 2 == 0 and n_q % unroll == 0 and unroll % 2 == 0
    assert n_seq == total_steps * DECODE_SEQS_PER_STEP
    tasks = jnp.asarray(_attn_tasks(n_q))
    dec_rows = DECODE_SEQS_PER_STEP * new_rows

    def gstep(bi, h, pi):
        return (bi * n_heads + h) * n_steps + pi

    whole = lambda w: pl.BlockSpec((None, l, w), lambda bi, h, pi, *_: (bi, 0, h))
    tile_a = pl.BlockSpec((None, tq, V_DIM), lambda bi, h, pi, *_: (bi, pi, h))
    tile_b = pl.BlockSpec((None, tq, V_DIM), lambda bi, h, pi, *_: (bi, n_q - 1 - pi, h))
    dec = pl.BlockSpec((dec_rows, V_DIM), lambda bi, h, pi, *_: (gstep(bi, h, pi), 0))
    const2 = lambda a: pl.BlockSpec(a.shape, lambda bi, h, pi, *_: (0, 0))
    grid_spec = pltpu.PrefetchScalarGridSpec(
        num_scalar_prefetch=2,
        grid=(b, n_heads, n_steps),
        in_specs=[
            pl.BlockSpec(memory_space=pltpu.SMEM),
            tile_a, tile_a, tile_b, tile_b, whole(V_DIM), whole(2 * V_DIM),
            pl.BlockSpec((None, 2, LANES, LANES), lambda bi, h, pi, *_: (h, 0, 0, 0)),
            tile_a, tile_b, const2(sg),
            dec, dec, dec, dec, dec, const2(head_mask), const2(near_bias),
            pl.BlockSpec(memory_space=pl.ANY), pl.BlockSpec(memory_space=pl.ANY),
        ],
        out_specs=[whole(V_DIM), dec],
        scratch_shapes=[
            pltpu.VMEM((2, 2 * tq, tk), F32), pltpu.VMEM((3, tq, tk), F32),
            pltpu.VMEM((2, 2 * tq, LANES), F32), pltpu.VMEM((2, 2 * tq, 2 * V_DIM), F32),
            pltpu.VMEM((2, 2 * tq, V_DIM), BF16),
            pltpu.VMEM((n_pages, page_rows, V_DIM), cache_k.dtype),
            pltpu.VMEM((n_pages, page_rows, V_DIM), cache_v.dtype),
            pltpu.VMEM((page_rows, V_DIM), BF16), pltpu.VMEM((page_rows, V_DIM), BF16),
            pltpu.SemaphoreType.DMA((1,)), pltpu.SemaphoreType.DMA((1,)),
        ],
    )
    return pl.pallas_call(
        _attention_kernel,
        grid_spec=grid_spec,
        out_shape=[jax.ShapeDtypeStruct((b, l, d_attn), BF16),
                   jax.ShapeDtypeStruct((n_seq * new_rows, V_DIM), F32)],
        compiler_params=pltpu.CompilerParams(
            dimension_semantics=("arbitrary", "arbitrary", "arbitrary"),
            vmem_limit_bytes=VMEM_LIMIT_BYTES),
        name="attention",
    )(tasks, page_table, lam, q1, q2, q1, q2, kb, va, dtiles, ga, ga, sg,
      dq1, dq2, dkn, dvn, dga, head_mask, near_bias, cache_k, cache_v)


def _ssm_tail(y, u, gs, dskip_ref, wglu_half_ref, bglu_half_ref):
    x = y + dskip_ref[...] * u
    inner = x * (GELU_C1 * (x * x) + GELU_C0)
    z = (0.5 * x) * (1.0 + jnp.tanh(inner))
    t_glu = jnp.tanh(jnp.dot(z.astype(BF16), wglu_half_ref[...], preferred_element_type=F32) + bglu_half_ref[...])
    t_gate = jnp.tanh(0.5 * gs)
    return ((z * gs) * 0.25) * (1.0 + t_glu) * (1.0 + t_gate)


def _ssm_prompt_kernel(u4_ref, gs_ref, w_ref, m_ref, v_ref, pin_re_ref, pin_im_ref, pout_re_ref, pout_im_ref,
                       aux_ref, scan_ref, dskip_ref, wglu_ref, bglu_ref, o_ref, hfin_ref,
                       carry_scr, y4_scr):
    n_q = w_ref.shape[0]
    mic = SSM_MICRO
    sc = SSM_LANE_CHUNK
    ts = gs_ref.shape[0]
    r = ts // mic
    rb = pin_re_ref.shape[0]
    n_blk = r // rb
    tile_rows = lambda a: jnp.concatenate([a] * n_blk, axis=0)
    c = pl.program_id(1)

    @pl.when(c == 0)
    def _():
        carry_scr[...] = jnp.zeros(carry_scr.shape, F32)

    for q in range(n_q):
        re_l = slice(2 * q * sc, (2 * q + 1) * sc)
        im_l = slice((2 * q + 1) * sc, 2 * (q + 1) * sc)
        st_l = slice(q * sc, (q + 1) * sc)
        x = jnp.concatenate([u4_ref[pl.ds(n_q * s + q, r, stride=n_q * mic), :] for s in range(mic)],
                            axis=1).astype(BF16)
        e = jnp.dot(x, w_ref[q], preferred_element_type=F32)
        er, em = e[:, :sc], e[:, sc:]
        pw = lambda ref, i: ref[i:i + 1, st_l]
        pir, pii = tile_rows(pin_re_ref[:, st_l]), tile_rows(pin_im_ref[:, st_l])
        xs = jnp.concatenate([er * pir - em * pii, er * pii + em * pir], axis=1).astype(BF16)
        cs = jnp.dot(scan_ref[...], xs, preferred_element_type=F32)
        tot_r, tot_m = cs[r:r + n_blk, :sc], cs[r:r + n_blk, sc:]
        t_r = tot_r * pw(aux_ref, 2) - tot_m * pw(aux_ref, 3)
        t_m = tot_r * pw(aux_ref, 3) + tot_m * pw(aux_ref, 2)
        ar, am = pw(aux_ref, 0), pw(aux_ref, 1)
        br, bm = pw(aux_ref, 4), pw(aux_ref, 5)
        h_r, h_m = carry_scr[:, re_l], carry_scr[:, im_l]
        base_r, base_m = [], []
        for blk in range(n_blk):
            base_r.append(jnp.broadcast_to(ar * h_r - am * h_m, (rb, sc)))
            base_m.append(jnp.broadcast_to(ar * h_m + am * h_r, (rb, sc)))
            h_r, h_m = (br * h_r - bm * h_m + t_r[blk:blk + 1, :], br * h_m + bm * h_r + t_m[blk:blk + 1, :])
        carry_scr[:, re_l] = h_r
        carry_scr[:, im_l] = h_m
        hfin_ref[:, re_l] = h_r
        hfin_ref[:, im_l] = h_m
        sr = cs[:r, :sc] + jnp.concatenate(base_r, axis=0)
        sm = cs[:r, sc:] + jnp.concatenate(base_m, axis=0)
        por, poi = tile_rows(pout_re_ref[:, st_l]), tile_rows(pout_im_ref[:, st_l])
        hp = jnp.concatenate([sr * por - sm * poi, sr * poi + sm * por], axis=1).astype(BF16)
        y = (jnp.dot(x, m_ref[q], preferred_element_type=F32)
             + jnp.dot(hp, v_ref[q], preferred_element_type=F32))
        for s in range(mic):
            y4_scr[pl.ds(n_q * s + q, r, stride=n_q * mic), :] = y[:, s * LANES:(s + 1) * LANES]
    y = jnp.concatenate([y4_scr[pl.ds(q, ts, stride=n_q), :] for q in range(n_q)], axis=1)
    u = jnp.concatenate([u4_ref[pl.ds(q, ts, stride=n_q), :] for q in range(n_q)], axis=1)
    o_ref[...] = _ssm_tail(y, u, gs_ref[...], dskip_ref, wglu_ref, bglu_ref).astype(o_ref.dtype)


def _ssm_prompt(u4, gs, sp):
    b, l, d_ssm = gs.shape
    n_q = d_ssm // LANES
    ts = SSM_STEP
    n_state2 = 2 * sp["aux"].shape[1]
    row = pl.BlockSpec((None, ts, d_ssm), lambda bi, ci: (bi, ci, 0))
    row4 = pl.BlockSpec((None, ts * n_q, LANES), lambda bi, ci: (bi, ci, 0))
    full = lambda a: pl.BlockSpec(a.shape, lambda bi, ci: (0,) * a.ndim)
    names = ["w", "m", "v", "pin_re", "pin_im", "pout_re", "pout_im", "aux", "scan", "dskip", "wglu", "bglu"]
    return pl.pallas_call(
        _ssm_prompt_kernel,
        grid=(b, l // ts),
        in_specs=[row4, row] + [full(sp[n]) for n in names],
        out_specs=[row, pl.BlockSpec((None, 1, n_state2), lambda bi, ci: (bi, 0, 0))],
        out_shape=[jax.ShapeDtypeStruct((b, l, d_ssm), BF16),
                   jax.ShapeDtypeStruct((b, 1, n_state2), F32)],
        scratch_shapes=[pltpu.VMEM((1, n_state2), F32), pltpu.VMEM((ts * n_q, LANES), F32)],
        compiler_params=pltpu.CompilerParams(
            dimension_semantics=("arbitrary", "arbitrary"), vmem_limit_bytes=VMEM_LIMIT_BYTES),
        name="ssm_prompt",
    )(u4, gs, *[sp[n] for n in names])


def _ssm_sample_kernel(u_ref, gs_ref, h0_ref, bw_ref, cw_ref, abar_ref, dskip_ref, wglu_ref, bglu_ref,
                       o_ref, hfin_ref, bu_scr, h_scr):
    n_seq = h0_ref.shape[0]
    dec_seq = u_ref.shape[0] // n_seq
    n_q = bw_ref.shape[0]
    sc = SSM_LANE_CHUNK
    n_lc = 2 * sc // LANES
    u = u_ref[...]
    ys = []
    for q in range(n_q):
        uq = u[:, q * LANES:(q + 1) * LANES].astype(BF16)
        bu = jnp.dot(uq, bw_ref[q], preferred_element_type=F32)
        for c in range(n_lc):
            bu_scr[c] = bu[:, c * LANES:(c + 1) * LANES]
        lanes = slice(q * sc, (q + 1) * sc)
        ar, am = abar_ref[0:1, lanes], abar_ref[1:2, lanes]
        hr = h0_ref[:, 2 * q * sc:(2 * q + 1) * sc]
        hm = h0_ref[:, (2 * q + 1) * sc:2 * (q + 1) * sc]
        for step in range(dec_seq):
            rows = pl.ds(step, n_seq, stride=dec_seq)
            b_all = jnp.concatenate([bu_scr[c, rows, :] for c in range(n_lc)], axis=1)
            br, bi = b_all[:, :sc], b_all[:, sc:]
            hr, hm = ar * hr - am * hm + br, ar * hm + am * hr + bi
            for c in range(n_lc // 2):
                h_scr[c, rows, :] = hr[:, c * LANES:(c + 1) * LANES]
                h_scr[n_lc // 2 + c, rows, :] = hm[:, c * LANES:(c + 1) * LANES]
        hfin_ref[:, 2 * q * sc:(2 * q + 1) * sc] = hr
        hfin_ref[:, (2 * q + 1) * sc:2 * (q + 1) * sc] = hm
        h_all = jnp.concatenate([h_scr[c] for c in range(n_lc)], axis=1)
        ys.append(jnp.dot(h_all.astype(BF16), cw_ref[q], preferred_element_type=F32))
    y = jnp.concatenate(ys, axis=1)
    o_ref[...] = _ssm_tail(y, u, gs_ref[...], dskip_ref, wglu_ref, bglu_ref).astype(o_ref.dtype)


def _ssm_sample(u, gs, h0, sp):
    n, d_ssm = u.shape
    n_seq, n_state2 = h0.shape
    names = ["bw", "cw", "abar", "dskip", "wglu", "bglu"]
    args = [u, gs, h0] + [sp[k] for k in names]
    full = lambda a: pl.BlockSpec(a.shape, lambda i: (0,) * a.ndim)
    return pl.pallas_call(
        _ssm_sample_kernel,
        grid=(1,),
        in_specs=[full(a) for a in args],
        out_specs=[pl.BlockSpec((n, d_ssm), lambda i: (0, 0)),
                   pl.BlockSpec((n_seq, n_state2), lambda i: (0, 0))],
        out_shape=[jax.ShapeDtypeStruct((n, d_ssm), BF16),
                   jax.ShapeDtypeStruct((n_seq, n_state2), F32)],
        scratch_shapes=[pltpu.VMEM((2 * SSM_LANE_CHUNK // LANES, n, LANES), F32),
                        pltpu.VMEM((2 * SSM_LANE_CHUNK // LANES, n, LANES), F32)],
        compiler_params=pltpu.CompilerParams(
            dimension_semantics=("arbitrary",), vmem_limit_bytes=VMEM_LIMIT_BYTES),
        name="ssm_sample",
    )(*args)


def _outproj_kernel(x_ref, oa_ref, os_ref, w_ref, y_ref):
    d_a = oa_ref.shape[1]
    y = x_ref[...] + jnp.dot(oa_ref[...], w_ref[:d_a, :], preferred_element_type=F32)
    y_ref[...] = y + jnp.dot(os_ref[...], w_ref[d_a:, :], preferred_element_type=F32)


def _outproj(x, oa, os_, w_bf):
    n, d_model = x.shape
    tm = min(ROW_TILE, n)
    row = lambda w: pl.BlockSpec((tm, w), lambda i: (i, 0))
    return pl.pallas_call(
        _outproj_kernel,
        grid=(n // tm,),
        in_specs=[row(d_model), row(oa.shape[1]), row(os_.shape[1]),
                  pl.BlockSpec(w_bf.shape, lambda i: (0, 0))],
        out_specs=row(d_model),
        out_shape=jax.ShapeDtypeStruct((n, d_model), F32),
        compiler_params=pltpu.CompilerParams(
            dimension_semantics=("arbitrary",), vmem_limit_bytes=VMEM_LIMIT_BYTES),
        name="outproj",
    )(x, oa, os_, w_bf)


def _ssm_params(a_re, a_im, log_dt, b_re, b_im, c_re, c_im, d_skip, w_glu, b_glu):
    n_groups, n_state = a_re.shape
    g_per_q = LANES // SSM_GROUP
    n_q = n_groups // g_per_q
    dt = jnp.exp(log_dt.astype(F32))[:, None]
    a_re = a_re.astype(F32)
    a_im = a_im.astype(F32)
    mag = jnp.exp(a_re * dt)
    abar_re = mag * jnp.cos(a_im * dt)
    abar_im = mag * jnp.sin(a_im * dt)
    nr = abar_re - 1.0
    den = a_re * a_re + a_im * a_im
    coef_re = (nr * a_re + abar_im * a_im) / den
    coef_im = (abar_im * a_re - nr * a_im) / den
    b_re = b_re.astype(F32)
    b_im = b_im.astype(F32)
    bbar_re = coef_re[..., None] * b_re - coef_im[..., None] * b_im
    bbar_im = coef_re[..., None] * b_im + coef_im[..., None] * b_re

    same_group = (np.arange(g_per_q * SSM_GROUP)[:, None] // SSM_GROUP
                  == np.arange(g_per_q * n_state)[None, :] // n_state)

    def lane_tile(a, reps):
        w = a.shape[-1]
        return jnp.matmul(a, jnp.asarray(np.tile(np.eye(w, dtype=np.float32), (1, reps))),
                          precision=lax.Precision.HIGHEST)

    def rows_in(t):
        n = t.shape[0]
        t = lane_tile(t.reshape(n, n_q, g_per_q * n_state, SSM_GROUP), g_per_q)
        t = jnp.swapaxes(t * jnp.asarray(same_group.T, F32), -1, -2)
        return jnp.swapaxes(t, 0, 1).reshape(n_q, n * LANES, g_per_q * n_state)

    def cols_out(t):
        n = t.shape[0]
        t = lane_tile(t.reshape(n, n_q, g_per_q * SSM_GROUP, n_state), g_per_q)
        t = jnp.swapaxes(t * jnp.asarray(same_group, F32), -1, -2)
        return jnp.transpose(t, (1, 2, 0, 3)).reshape(n_q, g_per_q * n_state, n * LANES)

    bw = jnp.concatenate([rows_in(bbar_re[None]), rows_in(bbar_im[None])], axis=2).astype(BF16)
    cw = jnp.concatenate([cols_out(c_re.astype(F32)[None]), cols_out(-c_im.astype(F32)[None])],
                         axis=1).astype(BF16)

    ar = abar_re.reshape(1, -1)
    ai = abar_im.reshape(1, -1)
    sp = {
        "bw": bw, "cw": cw,
        "abar": jnp.concatenate([ar, ai], axis=0),
        "dskip": d_skip.astype(F32).reshape(1, -1),
        "wglu": (0.5 * w_glu.astype(F32)).astype(BF16),
        "bglu": 0.5 * b_glu.astype(F32).reshape(1, -1),
    }

    mic = SSM_MICRO
    lr_step = a_re * dt
    th_step = a_im * dt

    def power(t):
        t = jnp.asarray(np.asarray(t, np.float32))[:, None, None]
        mag = jnp.exp(t * lr_step)
        return mag * jnp.cos(t * th_step), mag * jnp.sin(t * th_step)

    pw_r, pw_i = power(np.arange(mic + 1))
    zr, zi = power(np.arange(mic - 1, -1, -1))
    wb_r = zr[..., None] * bbar_re[None] - zi[..., None] * bbar_im[None]
    wb_i = zr[..., None] * bbar_im[None] + zi[..., None] * bbar_re[None]
    w = jnp.concatenate([rows_in(wb_r), rows_in(wb_i)], axis=2).astype(BF16)

    c_re = c_re.astype(F32)
    c_im = c_im.astype(F32)
    pr1, pi1 = pw_r[1:mic + 1][:, :, None, :], pw_i[1:mic + 1][:, :, None, :]
    v_r = c_re[None] * pr1 - c_im[None] * pi1
    v_i = c_re[None] * pi1 + c_im[None] * pr1
    v = jnp.concatenate([cols_out(v_r), cols_out(-v_i)], axis=1).astype(BF16)

    tb_r = pw_r[:mic, :, :, None] * bbar_re[None] - pw_i[:mic, :, :, None] * bbar_im[None]
    tb_i = pw_r[:mic, :, :, None] * bbar_im[None] + pw_i[:mic, :, :, None] * bbar_re[None]
    taps = (jnp.sum(c_re[None, :, None, :, :] * jnp.swapaxes(tb_r, 2, 3)[:, :, :, None, :], axis=-1)
            - jnp.sum(c_im[None, :, None, :, :] * jnp.swapaxes(tb_i, 2, 3)[:, :, :, None, :], axis=-1))
    zero = jnp.zeros_like(taps[0])
    grid = jnp.stack([jnp.stack([taps[t - s] if t >= s else zero for t in range(mic)], axis=0)
                      for s in range(mic)], axis=0)
    grid = lane_tile(grid.reshape(mic, mic, n_q, LANES, SSM_GROUP), g_per_q)
    grid = grid * jnp.asarray(same_group[:, ::n_state // SSM_GROUP], F32)
    m = jnp.transpose(grid, (2, 0, 3, 1, 4)).reshape(n_q, mic * LANES, mic * LANES).astype(BF16)

    rb = SSM_SCAN_ROWS
    k = np.arange(rb)
    in_r, in_i = power(-mic * k)
    out_r, out_i = power(mic * (k - 1))
    flat = lambda a: a.reshape(a.shape[0], -1)
    aux_r, aux_i = power(np.array([mic, mic * (rb - 1), mic * rb]))
    aux = jnp.stack([flat(aux_r), flat(aux_i)], axis=1).reshape(6, -1)
    n_rows = SSM_STEP // mic
    blk = np.arange(n_rows) // rb
    strict = (blk[:, None] == blk[None, :]) & (np.arange(n_rows)[:, None] > np.arange(n_rows)[None, :])
    sums = np.arange(n_rows // rb)[:, None] == blk[None, :]
    pad = np.zeros((-(n_rows + n_rows // rb) % 16, n_rows), bool)
    scan = jnp.asarray(np.concatenate([strict, sums, pad], axis=0).astype(np.float32), BF16)
    sp.update({"w": w, "m": m, "v": v, "aux": aux, "scan": scan,
               "pin_re": flat(in_r), "pin_im": flat(in_i), "pout_re": flat(out_r), "pout_im": flat(out_i)})
    return sp


def _state_to_lanes(h_re, h_im):
    b = h_re.shape[0]
    sc = SSM_LANE_CHUNK
    r = h_re.astype(F32).reshape(b, -1, 1, sc)
    i = h_im.astype(F32).reshape(b, -1, 1, sc)
    return jnp.concatenate([r, i], axis=2).reshape(b, -1)


def _lanes_to_state(h, n_groups, n_state):
    b = h.shape[0]
    h = h.reshape(b, -1, 2, SSM_LANE_CHUNK)
    return (h[:, :, 0, :].reshape(b, n_groups, n_state), h[:, :, 1, :].reshape(b, n_groups, n_state))


def _toeplitz(v, n):
    h = v.shape[0]
    x = jnp.broadcast_to(v[:, None, :], (h, n, 2 * n)).reshape(h, 2 * n * n)
    return x[:, :n * (2 * n - 1)].reshape(h, n, 2 * n - 1)[:, :, :n]


def _prompt_bias_blocks(fvec):
    n = LANES
    h = fvec.shape[0]
    neg = jnp.full((h, n - 1), NEG_INF, F32)
    va = jnp.concatenate([fvec[:, 0:1], neg, jnp.zeros((h, 1), F32), fvec[:, 1:n][:, ::-1]], axis=1)
    vb = jnp.concatenate([fvec[:, 1:n + 1][:, ::-1], jnp.zeros((h, n), F32)], axis=1)
    return jnp.stack([_toeplitz(va, n), _toeplitz(vb, n)], axis=1)


def _decode_bias(fvec, page, dec_seq, n_heads):
    h = fvec.shape[0]
    rows = []
    for i in range(dec_seq):
        last = fvec[:, i + 1:i + 1 + page][:, ::-1]
        new = jnp.concatenate([fvec[:, 0:i + 1][:, ::-1], jnp.full((h, page - i - 1), NEG_INF, F32)], axis=1)
        rows.append(jnp.concatenate([last, new], axis=1))
    per_head = jnp.stack(rows, axis=0)
    same = np.eye(n_heads, dtype=bool)[None, :, None, :]
    near = jnp.where(jnp.asarray(same), per_head[:, :, :, None], NEG_INF)
    near = near.reshape(dec_seq * n_heads, -1)
    mask = np.where(np.broadcast_to(same, (dec_seq, n_heads, page, n_heads)), 0.0, NEG_INF)
    mask = mask.reshape(dec_seq * n_heads, -1).astype(np.float32)
    return jnp.asarray(np.concatenate([mask, mask], axis=0)), jnp.concatenate([near, near], axis=0)


def kernel(x_prompt, x_sample, cache_k, cache_v, state_ssm_re, state_ssm_im, page_table,
           norm_g, w_in, q_norm_g, k_norm_g, lambda_q1, lambda_k1, lambda_q2, lambda_k2,
           subln_g, rel_bias, ssm_a_re, ssm_a_im, ssm_log_dt, ssm_b_re, ssm_b_im,
           ssm_c_re, ssm_c_im, ssm_d, w_glu, b_glu, w_out):
    batch, seq, d_model = x_prompt.shape
    dec_batch, dec_seq, _ = x_sample.shape
    depth, n_pool, page, n_heads, _ = cache_k.shape
    n_pages = page_table.shape[1]
    d_attn = n_heads * V_DIM
    n_groups, n_state = ssm_a_re.shape[1:]
    new_rows = dec_seq * n_heads

    buckets = _bucket_table(2 * LANES)
    far_from = int(np.max(np.nonzero(buckets < N_BUCKETS - 1)[0])) + 1
    assert far_from <= LANES and _bucket_table(seq + page * n_pages)[far_from:].min() == N_BUCKETS - 1
    assert page == LANES and dec_seq < LANES and ATTN_TQ == ATTN_TK

    rel_bias = rel_bias.astype(F32)
    fvec = (rel_bias[buckets].T - rel_bias[N_BUCKETS - 1][:, None]) * LOG2E
    fvec = jnp.where(jnp.asarray(np.arange(2 * LANES) < far_from)[None], fvec, 0.0)
    dtiles = _prompt_bias_blocks(fvec)
    head_mask, near_bias = _decode_bias(fvec, page, dec_seq, n_heads)

    group_avg = jnp.asarray(np.kron(np.eye(d_attn // QK_DIM), np.full((QK_DIM, QK_DIM), 1.0 / QK_DIM)), BF16)
    n_rep = d_attn // QK_DIM
    cache_k_rows = cache_k.reshape(depth * n_pool, page * n_heads, V_DIM)
    cache_v_rows = cache_v.reshape(depth * n_pool, page * n_heads, V_DIM)

    hp = x_prompt.reshape(batch * seq, d_model)
    hs = x_sample.reshape(dec_batch * dec_seq, d_model)
    kp_l, vp_l, ks_l, vs_l = [], [], [], []
    srp_l, sip_l, srs_l, sis_l = [], [], [], []
    for l in range(depth):
        lam_init = _lambda_init(l)
        lam = (jnp.exp(jnp.sum(lambda_q1[l].astype(F32) * lambda_k1[l].astype(F32)))
               - jnp.exp(jnp.sum(lambda_q2[l].astype(F32) * lambda_k2[l].astype(F32))) + lam_init)
        lam = lam.reshape(1).astype(F32)
        ng = norm_g[l].astype(F32).reshape(1, d_model)
        w_bf = w_in[l].astype(BF16)
        gq = jnp.tile(q_norm_g[l].astype(F32), n_rep).reshape(1, d_attn) * (QK_DIM ** -0.5 * LOG2E)
        gk = jnp.tile(k_norm_g[l].astype(F32), n_rep).reshape(1, d_attn)
        sg = (subln_g[l].astype(F32) * (1.0 - lam_init)).reshape(1, V_DIM)
        wo_bf = w_out[l].astype(BF16)
        sp = _ssm_params(ssm_a_re[l], ssm_a_im[l], ssm_log_dt[l], ssm_b_re[l], ssm_b_im[l],
                         ssm_c_re[l], ssm_c_im[l], ssm_d[l], w_glu[l], b_glu[l])

        q1, q2, k4, kb, v4, va, ga, u4, gs = _inproj(hp, ng, w_bf, gq, gk, group_avg, head_rows=False)
        sq1, sq2, sk4, sv4, sga, su, sgs = _inproj(hs, ng, w_bf, gq, gk, group_avg, head_rows=True)
        r3 = lambda a: a.reshape(batch, seq, a.shape[-1])
        o_a, o_dec = _attention(lam, sg, r3(q1), r3(q2), r3(kb), r3(va), dtiles, r3(ga),
                                page_table + l * n_pool, sq1, sq2, sk4, sv4, sga, head_mask, near_bias,
                                cache_k_rows, cache_v_rows, new_rows)
        o_s, hfin = _ssm_prompt(u4.reshape(batch, -1, LANES), r3(gs), sp)
        hp = _outproj(hp, o_a.reshape(batch * seq, d_attn), o_s.reshape(batch * seq, -1), wo_bf)
        kp_l.append(k4.reshape(batch, seq, n_heads, V_DIM).astype(cache_k.dtype))
        vp_l.append(v4.reshape(batch, seq, n_heads, V_DIM).astype(cache_v.dtype))
        hr_p, hi_p = _lanes_to_state(hfin.reshape(batch, -1), n_groups, n_state)
        srp_l.append(hr_p.astype(state_ssm_re.dtype))
        sip_l.append(hi_p.astype(state_ssm_im.dtype))

        h0 = _state_to_lanes(state_ssm_re[l], state_ssm_im[l])
        o_s, hfin = _ssm_sample(su, sgs, h0, sp)
        hs = _outproj(hs, o_dec.reshape(dec_batch * dec_seq, d_attn).astype(BF16), o_s, wo_bf)
        ks_l.append(sk4.reshape(dec_batch, dec_seq, n_heads, V_DIM).astype(cache_k.dtype))
        vs_l.append(sv4.reshape(dec_batch, dec_seq, n_heads, V_DIM).astype(cache_v.dtype))
        hr_s, hi_s = _lanes_to_state(hfin, n_groups, n_state)
        srs_l.append(hr_s.astype(state_ssm_re.dtype))
        sis_l.append(hi_s.astype(state_ssm_im.dtype))

    y_prompt = hp.reshape(batch, seq, d_model).astype(x_prompt.dtype)
    y_sample = hs.reshape(dec_batch, dec_seq, d_model).astype(x_sample.dtype)
    return (y_prompt, y_sample, jnp.stack(kp_l), jnp.stack(vp_l), jnp.stack(ks_l), jnp.stack(vs_l),
            jnp.stack(srp_l), jnp.stack(sip_l), jnp.stack(srs_l), jnp.stack(sis_l))
```

```python
import functools
import math

import numpy as np
import jax
import jax.numpy as jnp
from jax import lax
from jax.experimental import pallas as pl
from jax.experimental.pallas import tpu as pltpu

F32 = jnp.float32
BF16 = jnp.bfloat16

QK_DIM = 64
V_DIM = 2 * QK_DIM
N_BUCKETS = 32
MAX_DISTANCE = 128
SSM_GROUP = 16
SSM_STATE = 64
EPS = 1e-6
NEG_INF = -1e30
LOG2E = math.log2(math.e)
GELU_C0 = math.sqrt(2.0 / math.pi)
GELU_C1 = GELU_C0 * 0.044715

LANES = 128
VMEM_LIMIT_BYTES = 56 * 1024 * 1024

ROW_TILE = 512
ATTN_TQ = 512
ATTN_TK = 512
ATTN_UNROLL = 8
DECODE_SEQS_PER_STEP = 2
SSM_MICRO = 4
SSM_STEP = 1024
SSM_SCAN_ROWS = 32
SSM_LANE_CHUNK = 512


def _lambda_init(layer):
    return 0.8 - 0.6 * math.exp(-0.3 * layer)


def _bucket_table(n_max):
    n = np.arange(n_max)
    max_exact = N_BUCKETS // 2
    nf = np.maximum(n, 1).astype(np.float32)
    large = max_exact + (np.log(nf / np.float32(max_exact)) / np.float32(math.log(MAX_DISTANCE / max_exact))
                         * np.float32(N_BUCKETS - max_exact)).astype(np.int32)
    large = np.minimum(large, N_BUCKETS - 1)
    return np.where(n < max_exact, n, large).astype(np.int32)


def _silu(x):
    return (0.5 * x) * (1.0 + jnp.tanh(0.5 * x))


def _store_head_rows(ref, val, n_heads):
    rows = val.shape[0]
    for h in range(n_heads):
        ref[pl.ds(h, rows, stride=n_heads), :] = val[:, h * V_DIM:(h + 1) * V_DIM]


def _inproj_kernel(x_ref, ng_ref, w_ref, gq_ref, gk_ref, gavg_ref, *out_refs, head_rows):
    x = x_ref[...]
    ms = jnp.mean(x * x, axis=-1, keepdims=True)
    xb = (x * lax.rsqrt(ms + EPS) * ng_ref[...]).astype(BF16)
    d_seg = gq_ref.shape[1]
    n_heads = d_seg // V_DIM

    def seg(i):
        return jnp.dot(xb, w_ref[:, i * d_seg:(i + 1) * d_seg], preferred_element_type=F32)

    def group_norm(t, g):
        sq = (t * t).astype(BF16)
        wg = gavg_ref.shape[0]
        msq = jnp.concatenate([jnp.dot(sq[:, c:c + wg], gavg_ref[...], preferred_element_type=F32)
                               for c in range(0, d_seg, wg)], axis=1)
        return t * lax.rsqrt(msq + EPS) * g

    q = group_norm(seg(0), gq_ref[...])
    lane = lax.broadcasted_iota(jnp.int32, q.shape, 1)
    first = (lane % V_DIM) < QK_DIM
    qa = jnp.where(first, q, 0.0)
    qb = jnp.where(first, 0.0, q)
    k = group_norm(seg(1), gk_ref[...])
    v = seg(2)
    ga = seg(3)
    if head_rows:
        q1_ref, q2_ref, k4_ref, v4_ref, ga_ref, u_ref, gs_ref = out_refs
        _store_head_rows(q1_ref, qa, n_heads)
        _store_head_rows(q2_ref, qb, n_heads)
        _store_head_rows(ga_ref, ga, n_heads)
    else:
        q1_ref, q2_ref, k4_ref, kb_ref, v4_ref, va_ref, ga_ref, u_ref, gs_ref = out_refs
        q1_ref[...] = qa.astype(BF16)
        q2_ref[...] = qb.astype(BF16)
        kb_ref[...] = k.astype(BF16)
        vb = v.astype(BF16)
        ones = jnp.ones((v.shape[0], V_DIM), BF16)
        pieces = []
        for h in range(n_heads):
            pieces += [vb[:, h * V_DIM:(h + 1) * V_DIM], ones]
        va_ref[...] = jnp.concatenate(pieces, axis=1)
        ga_ref[...] = ga
    _store_head_rows(k4_ref, k, n_heads)
    _store_head_rows(v4_ref, v, n_heads)
    if head_rows:
        u_ref[...] = seg(4)
    else:
        _store_head_rows(u_ref, seg(4), d_seg // LANES)
    gs_ref[...] = seg(5)


def _inproj(x, ng, w_bf, gq, gk, gavg, head_rows):
    n, d_model = x.shape
    d_seg = gq.shape[1]
    n_heads = d_seg // V_DIM
    tm = min(ROW_TILE, n)
    full = lambda a: pl.BlockSpec(a.shape, lambda i: (0,) * a.ndim)
    wide = lambda w, dt: (jax.ShapeDtypeStruct((n, w), dt), pl.BlockSpec((tm, w), lambda i: (i, 0)))
    tall = lambda dt: (jax.ShapeDtypeStruct((n * n_heads, V_DIM), dt),
                       pl.BlockSpec((tm * n_heads, V_DIM), lambda i: (i, 0)))
    if head_rows:
        outs = [tall(F32), tall(F32), tall(F32), tall(F32), tall(F32), wide(d_seg, F32), wide(d_seg, F32)]
    else:
        outs = [wide(d_seg, BF16), wide(d_seg, BF16), tall(F32), wide(d_seg, BF16), tall(F32),
                wide(2 * d_seg, BF16), wide(d_seg, F32), tall(F32), wide(d_seg, F32)]
    return pl.pallas_call(
        functools.partial(_inproj_kernel, head_rows=head_rows),
        grid=(n // tm,),
        in_specs=[pl.BlockSpec((tm, d_model), lambda i: (i, 0)),
                  full(ng), full(w_bf), full(gq), full(gk), full(gavg)],
        out_specs=[o[1] for o in outs],
        out_shape=[o[0] for o in outs],
        compiler_params=pltpu.CompilerParams(
            dimension_semantics=("arbitrary",), vmem_limit_bytes=VMEM_LIMIT_BYTES),
        name="inproj_samples" if head_rows else "inproj_prompt",
    )(x, ng, w_bf, gq, gk, gavg)


def _diff_epilogue(o1, o2, lam, sg, ga):
    od = o1 - lam * o2
    ms = jnp.mean(od * od, axis=-1, keepdims=True)
    return od * lax.rsqrt(ms + EPS) * sg * _silu(ga)


def _attention_kernel(tasks_ref, pt_ref, lam_ref,
                      q1a_ref, q2a_ref, q1b_ref, q2b_ref, k_ref, v_ref, ab_ref, gaa_ref, gab_ref, sg_ref,
                      dq1_ref, dq2_ref, dkn_ref, dvn_ref, dga_ref, hm_ref, near_ref, ck_hbm, cv_hbm,
                      o_ref, od_ref,
                      s_scr, d_scr, m_scr, acc_scr, qs_scr, kbuf, vbuf, knew, vnew, ksem, vsem):
    tq = s_scr.shape[1] // 2
    tk = s_scr.shape[2]
    p = pl.program_id(2)
    n_steps = pl.num_programs(2)
    n_tasks = tasks_ref.shape[1]
    unroll = min(ATTN_UNROLL, n_tasks - 1)
    n_trips = (n_tasks - 1) // unroll
    step = (pl.program_id(0) * pl.num_programs(1) + pl.program_id(1)) * n_steps + p
    n_pages = kbuf.shape[1]
    page_rows = kbuf.shape[2]
    total_steps = pt_ref.shape[0] // DECODE_SEQS_PER_STEP
    lam = lam_ref[0]

    def k_copy(st, local, j):
        return pltpu.make_async_copy(ck_hbm.at[pt_ref[st * DECODE_SEQS_PER_STEP + local, j]],
                                     kbuf.at[local, j], ksem.at[0])

    def v_copy(st, local, j):
        return pltpu.make_async_copy(cv_hbm.at[pt_ref[st * DECODE_SEQS_PER_STEP + local, j]],
                                     vbuf.at[local, j], vsem.at[0])

    def start_fetch(st):
        for local in range(DECODE_SEQS_PER_STEP):
            for j in range(n_pages):
                k_copy(st, local, j).start()
                v_copy(st, local, j).start()

    def wait_fetch(st):
        for local in range(DECODE_SEQS_PER_STEP):
            for j in range(n_pages):
                k_copy(st, local, j).wait()
                v_copy(st, local, j).wait()

    def decode(local, rows):
        r0 = local * rows
        qx = jnp.concatenate([dq1_ref[r0:r0 + rows, :], dq2_ref[r0:r0 + rows, :]], axis=0).astype(BF16)
        knew[local, 0:rows, :] = dkn_ref[r0:r0 + rows, :].astype(BF16)
        vnew[local, 0:rows, :] = dvn_ref[r0:r0 + rows, :].astype(BF16)
        nt = (((1,), (1,)), ((), ()))
        s_tiles = [lax.dot_general(qx, kbuf[local, j].astype(BF16), nt, preferred_element_type=F32)
                   for j in range(n_pages)]
        s_tiles.append(lax.dot_general(qx, knew[local], nt, preferred_element_type=F32))
        head_mask = hm_ref[...]
        s = jnp.concatenate([t + head_mask for t in s_tiles[:n_pages - 1]]
                            + [jnp.concatenate(s_tiles[n_pages - 1:], axis=1) + near_ref[...]], axis=1)
        m = jnp.max(s, axis=-1, keepdims=True)
        pr = jnp.exp2(s - m)
        l_sum = jnp.sum(pr, axis=-1, keepdims=True)
        pb = pr.astype(BF16)
        acc = jnp.dot(pb[:, n_pages * page_rows:], vnew[local], preferred_element_type=F32)
        for j in range(n_pages):
            acc = acc + jnp.dot(pb[:, j * page_rows:(j + 1) * page_rows], vbuf[local, j].astype(BF16),
                                preferred_element_type=F32)
        o = acc / l_sum
        od_ref[r0:r0 + rows, :] = _diff_epilogue(o[:rows], o[rows:], lam, sg_ref[...], dga_ref[r0:r0 + rows, :])

    @pl.when(step == 0)
    def _():
        start_fetch(0)
        knew[...] = jnp.zeros(knew.shape, knew.dtype)
        vnew[...] = jnp.zeros(vnew.shape, vnew.dtype)

    @pl.when(p == 0)
    def _():
        for t in range(2):
            for a in range(tq // LANES):
                for b in range(tk // LANES):
                    delta = a - b + t * (tk // LANES)
                    if delta in (0, 1):
                        blk = ab_ref[delta]
                    else:
                        blk = jnp.full((LANES, LANES), NEG_INF if delta < 0 else 0.0, F32)
                    d_scr[t, a * LANES:(a + 1) * LANES, b * LANES:(b + 1) * LANES] = blk
        d_scr[2] = jnp.zeros((tq, tk), F32)

    m_scr[...] = jnp.full(m_scr.shape, NEG_INF, F32)
    acc_scr[...] = jnp.zeros(acc_scr.shape, F32)
    qs_scr[0] = jnp.concatenate([q1a_ref[...], q2a_ref[...]], axis=0)
    qs_scr[1] = jnp.concatenate([q1b_ref[...], q2b_ref[...]], axis=0)

    def task(kind, i):
        return tasks_ref[kind * n_steps + p, i]

    def produce(i, slot):
        k0 = pl.multiple_of(task(1, i) * tk, tk)
        s = lax.dot_general(qs_scr[task(3, i)], k_ref[pl.ds(k0, tk), :], (((1,), (1,)), ((), ())),
                            preferred_element_type=F32)
        bias = d_scr[task(2, i)]
        s_scr[slot] = s + jnp.concatenate([bias, bias], axis=0)

    def consume(i, slot):
        k0 = pl.multiple_of(task(1, i) * tk, tk)
        a = task(3, i)
        s = s_scr[slot]
        m_old = m_scr[a]
        m_new = jnp.maximum(m_old, jnp.max(s, axis=-1, keepdims=True))
        pr = jnp.exp2(s - jnp.concatenate([m_new] * (tk // LANES), axis=1))
        alpha = jnp.exp2(m_old - m_new)
        pv = jnp.dot(pr.astype(BF16), v_ref[pl.ds(k0, tk), :], preferred_element_type=F32)
        acc_scr[a] = acc_scr[a] * jnp.concatenate([alpha] * (acc_scr.shape[2] // LANES), axis=1) + pv
        m_scr[a] = m_new

    wait_fetch(step)
    for local in range(DECODE_SEQS_PER_STEP):
        decode(local, od_ref.shape[0] // DECODE_SEQS_PER_STEP)

    @pl.when(step + 1 < total_steps)
    def _():
        start_fetch(step + 1)

    produce(0, 0)

    def body(j, carry):
        for u in range(unroll):
            i = j * unroll + u
            produce(i + 1, (u + 1) % 2)
            consume(i, u % 2)
        return carry

    lax.fori_loop(0, n_trips, body, 0)
    consume(n_tasks - 1, (n_tasks - 1) % 2)

    for a, (ga_ref, i) in enumerate(((gaa_ref, 0), (gab_ref, n_tasks - 1))):
        q0 = pl.multiple_of(task(0, i) * tq, tq)
        acc = acc_scr[a]
        o = acc[:, :V_DIM] / acc[:, V_DIM:]
        out = _diff_epilogue(o[:tq], o[tq:], lam, sg_ref[...], ga_ref[...])
        o_ref[pl.ds(q0, tq), :] = out.astype(o_ref.dtype)


def _attn_tasks(n_q):
    n_steps = n_q // 2
    tab = np.zeros((4, n_steps, n_q + 1), np.int32)
    for p in range(n_steps):
        i = 0
        for acc, qt in enumerate((p, n_q - 1 - p)):
            for t in range(qt + 1):
                tab[:, p, i] = (qt, qt - t, min(t, 2), acc)
                i += 1
        assert i == n_q + 1
    return tab.reshape(4 * n_steps, n_q + 1)


def _attention(lam, sg, q1, q2, kb, va, dtiles, ga,
               page_table, dq1, dq2, dkn, dvn, dga, head_mask, near_bias, cache_k, cache_v, new_rows):
    b, l, d_attn = q1.shape
    n_heads = d_attn // V_DIM
    tq, tk = ATTN_TQ, ATTN_TK
    n_q = l // tq
    n_steps = n_q // 2
    unroll = min(ATTN_UNROLL, n_q)
    n_seq, n_pages = page_table.shape
    page_rows = cache_k.shape[1]
    total_steps = b * n_heads * n_steps
    assert tq == tk and n_q % 2 == 0 and n_q % unroll == 0 and unroll % 2 == 0
    assert n_seq == total_steps * DECODE_SEQS_PER_STEP
    tasks = jnp.asarray(_attn_tasks(n_q))
    dec_rows = DECODE_SEQS_PER_STEP * new_rows

    def gstep(bi, h, pi):
        return (bi * n_heads + h) * n_steps + pi

    whole = lambda w: pl.BlockSpec((None, l, w), lambda bi, h, pi, *_: (bi, 0, h))
    tile_a = pl.BlockSpec((None, tq, V_DIM), lambda bi, h, pi, *_: (bi, pi, h))
    tile_b = pl.BlockSpec((None, tq, V_DIM), lambda bi, h, pi, *_: (bi, n_q - 1 - pi, h))
    dec = pl.BlockSpec((dec_rows, V_DIM), lambda bi, h, pi, *_: (gstep(bi, h, pi), 0))
    const2 = lambda a: pl.BlockSpec(a.shape, lambda bi, h, pi, *_: (0, 0))
    grid_spec = pltpu.PrefetchScalarGridSpec(
        num_scalar_prefetch=2,
        grid=(b, n_heads, n_steps),
        in_specs=[
            pl.BlockSpec(memory_space=pltpu.SMEM),
            tile_a, tile_a, tile_b, tile_b, whole(V_DIM), whole(2 * V_DIM),
            pl.BlockSpec((None, 2, LANES, LANES), lambda bi, h, pi, *_: (h, 0, 0, 0)),
            tile_a, tile_b, const2(sg),
            dec, dec, dec, dec, dec, const2(head_mask), const2(near_bias),
            pl.BlockSpec(memory_space=pl.ANY), pl.BlockSpec(memory_space=pl.ANY),
        ],
        out_specs=[whole(V_DIM), dec],
        scratch_shapes=[
            pltpu.VMEM((2, 2 * tq, tk), F32), pltpu.VMEM((3, tq, tk), F32),
            pltpu.VMEM((2, 2 * tq, LANES), F32), pltpu.VMEM((2, 2 * tq, 2 * V_DIM), F32),
            pltpu.VMEM((2, 2 * tq, V_DIM), BF16),
            pltpu.VMEM((DECODE_SEQS_PER_STEP, n_pages, page_rows, V_DIM), cache_k.dtype),
            pltpu.VMEM((DECODE_SEQS_PER_STEP, n_pages, page_rows, V_DIM), cache_v.dtype),
            pltpu.VMEM((DECODE_SEQS_PER_STEP, page_rows, V_DIM), BF16),
            pltpu.VMEM((DECODE_SEQS_PER_STEP, page_rows, V_DIM), BF16),
            pltpu.SemaphoreType.DMA((1,)), pltpu.SemaphoreType.DMA((1,)),
        ],
    )
    return pl.pallas_call(
        _attention_kernel,
        grid_spec=grid_spec,
        out_shape=[jax.ShapeDtypeStruct((b, l, d_attn), BF16),
                   jax.ShapeDtypeStruct((n_seq * new_rows, V_DIM), F32)],
        compiler_params=pltpu.CompilerParams(
            dimension_semantics=("arbitrary", "arbitrary", "arbitrary"),
            vmem_limit_bytes=VMEM_LIMIT_BYTES),
        name="attention",
    )(tasks, page_table, lam, q1, q2, q1, q2, kb, va, dtiles, ga, ga, sg,
      dq1, dq2, dkn, dvn, dga, head_mask, near_bias, cache_k, cache_v)


def _ssm_tail(y, u, gs, dskip_ref, wglu_half_ref, bglu_half_ref):
    x = y + dskip_ref[...] * u
    inner = x * (GELU_C1 * (x * x) + GELU_C0)
    z = (0.5 * x) * (1.0 + jnp.tanh(inner))
    t_glu = jnp.tanh(jnp.dot(z.astype(BF16), wglu_half_ref[...], preferred_element_type=F32) + bglu_half_ref[...])
    t_gate = jnp.tanh(0.5 * gs)
    return ((z * gs) * 0.25) * (1.0 + t_glu) * (1.0 + t_gate)


def _ssm_prompt_kernel(u4_ref, gs_ref, w_ref, m_ref, v_ref, pin_re_ref, pin_im_ref, pout_re_ref, pout_im_ref,
                       aux_ref, scan_ref, dskip_ref, wglu_ref, bglu_ref, o_ref, hfin_ref,
                       carry_scr, y4_scr):
    n_q = w_ref.shape[0]
    mic = SSM_MICRO
    sc = SSM_LANE_CHUNK
    ts = gs_ref.shape[0]
    r = ts // mic
    rb = pin_re_ref.shape[0]
    n_blk = r // rb
    tile_rows = lambda a: jnp.concatenate([a] * n_blk, axis=0)
    c = pl.program_id(1)

    @pl.when(c == 0)
    def _():
        carry_scr[...] = jnp.zeros(carry_scr.shape, F32)

    for q in range(n_q):
        re_l = slice(2 * q * sc, (2 * q + 1) * sc)
        im_l = slice((2 * q + 1) * sc, 2 * (q + 1) * sc)
        st_l = slice(q * sc, (q + 1) * sc)
        x = jnp.concatenate([u4_ref[pl.ds(n_q * s + q, r, stride=n_q * mic), :] for s in range(mic)],
                            axis=1).astype(BF16)
        e = jnp.dot(x, w_ref[q], preferred_element_type=F32)
        er, em = e[:, :sc], e[:, sc:]
        pw = lambda ref, i: ref[i:i + 1, st_l]
        pir, pii = tile_rows(pin_re_ref[:, st_l]), tile_rows(pin_im_ref[:, st_l])
        xs = jnp.concatenate([er * pir - em * pii, er * pii + em * pir], axis=1).astype(BF16)
        cs = jnp.dot(scan_ref[...], xs, preferred_element_type=F32)
        tot_r, tot_m = cs[r:r + n_blk, :sc], cs[r:r + n_blk, sc:]
        t_r = tot_r * pw(aux_ref, 2) - tot_m * pw(aux_ref, 3)
        t_m = tot_r * pw(aux_ref, 3) + tot_m * pw(aux_ref, 2)
        ar, am = pw(aux_ref, 0), pw(aux_ref, 1)
        br, bm = pw(aux_ref, 4), pw(aux_ref, 5)
        h_r, h_m = carry_scr[:, re_l], carry_scr[:, im_l]
        base_r, base_m = [], []
        for blk in range(n_blk):
            base_r.append(jnp.broadcast_to(ar * h_r - am * h_m, (rb, sc)))
            base_m.append(jnp.broadcast_to(ar * h_m + am * h_r, (rb, sc)))
            h_r, h_m = (br * h_r - bm * h_m + t_r[blk:blk + 1, :], br * h_m + bm * h_r + t_m[blk:blk + 1, :])
        carry_scr[:, re_l] = h_r
        carry_scr[:, im_l] = h_m
        hfin_ref[:, re_l] = h_r
        hfin_ref[:, im_l] = h_m
        sr = cs[:r, :sc] + jnp.concatenate(base_r, axis=0)
        sm = cs[:r, sc:] + jnp.concatenate(base_m, axis=0)
        por, poi = tile_rows(pout_re_ref[:, st_l]), tile_rows(pout_im_ref[:, st_l])
        hp = jnp.concatenate([sr * por - sm * poi, sr * poi + sm * por], axis=1).astype(BF16)
        y = (jnp.dot(x, m_ref[q], preferred_element_type=F32)
             + jnp.dot(hp, v_ref[q], preferred_element_type=F32))
        for s in range(mic):
            y4_scr[pl.ds(n_q * s + q, r, stride=n_q * mic), :] = y[:, s * LANES:(s + 1) * LANES]
    y = jnp.concatenate([y4_scr[pl.ds(q, ts, stride=n_q), :] for q in range(n_q)], axis=1)
    u = jnp.concatenate([u4_ref[pl.ds(q, ts, stride=n_q), :] for q in range(n_q)], axis=1)
    o_ref[...] = _ssm_tail(y, u, gs_ref[...], dskip_ref, wglu_ref, bglu_ref).astype(o_ref.dtype)


def _ssm_prompt(u4, gs, sp):
    b, l, d_ssm = gs.shape
    n_q = d_ssm // LANES
    ts = SSM_STEP
    n_state2 = 2 * sp["aux"].shape[1]
    row = pl.BlockSpec((None, ts, d_ssm), lambda bi, ci: (bi, ci, 0))
    row4 = pl.BlockSpec((None, ts * n_q, LANES), lambda bi, ci: (bi, ci, 0))
    full = lambda a: pl.BlockSpec(a.shape, lambda bi, ci: (0,) * a.ndim)
    names = ["w", "m", "v", "pin_re", "pin_im", "pout_re", "pout_im", "aux", "scan", "dskip", "wglu", "bglu"]
    return pl.pallas_call(
        _ssm_prompt_kernel,
        grid=(b, l // ts),
        in_specs=[row4, row] + [full(sp[n]) for n in names],
        out_specs=[row, pl.BlockSpec((None, 1, n_state2), lambda bi, ci: (bi, 0, 0))],
        out_shape=[jax.ShapeDtypeStruct((b, l, d_ssm), BF16),
                   jax.ShapeDtypeStruct((b, 1, n_state2), F32)],
        scratch_shapes=[pltpu.VMEM((1, n_state2), F32), pltpu.VMEM((ts * n_q, LANES), F32)],
        compiler_params=pltpu.CompilerParams(
            dimension_semantics=("arbitrary", "arbitrary"), vmem_limit_bytes=VMEM_LIMIT_BYTES),
        name="ssm_prompt",
    )(u4, gs, *[sp[n] for n in names])


def _ssm_sample_kernel(u_ref, gs_ref, h0_ref, bw_ref, cw_ref, abar_ref, dskip_ref, wglu_ref, bglu_ref,
                       o_ref, hfin_ref, bu_scr, h_scr):
    n_seq = h0_ref.shape[0]
    dec_seq = u_ref.shape[0] // n_seq
    n_q = bw_ref.shape[0]
    sc = SSM_LANE_CHUNK
    n_lc = 2 * sc // LANES
    u = u_ref[...]
    ys = []
    for q in range(n_q):
        uq = u[:, q * LANES:(q + 1) * LANES].astype(BF16)
        bu = jnp.dot(uq, bw_ref[q], preferred_element_type=F32)
        for c in range(n_lc):
            bu_scr[c] = bu[:, c * LANES:(c + 1) * LANES]
        lanes = slice(q * sc, (q + 1) * sc)
        ar, am = abar_ref[0:1, lanes], abar_ref[1:2, lanes]
        hr = h0_ref[:, 2 * q * sc:(2 * q + 1) * sc]
        hm = h0_ref[:, (2 * q + 1) * sc:2 * (q + 1) * sc]
        for step in range(dec_seq):
            rows = pl.ds(step, n_seq, stride=dec_seq)
            b_all = jnp.concatenate([bu_scr[c, rows, :] for c in range(n_lc)], axis=1)
            br, bi = b_all[:, :sc], b_all[:, sc:]
            hr, hm = ar * hr - am * hm + br, ar * hm + am * hr + bi
            for c in range(n_lc // 2):
                h_scr[c, rows, :] = hr[:, c * LANES:(c + 1) * LANES]
                h_scr[n_lc // 2 + c, rows, :] = hm[:, c * LANES:(c + 1) * LANES]
        hfin_ref[:, 2 * q * sc:(2 * q + 1) * sc] = hr
        hfin_ref[:, (2 * q + 1) * sc:2 * (q + 1) * sc] = hm
        h_all = jnp.concatenate([h_scr[c] for c in range(n_lc)], axis=1)
        ys.append(jnp.dot(h_all.astype(BF16), cw_ref[q], preferred_element_type=F32))
    y = jnp.concatenate(ys, axis=1)
    o_ref[...] = _ssm_tail(y, u, gs_ref[...], dskip_ref, wglu_ref, bglu_ref).astype(o_ref.dtype)


def _ssm_sample(u, gs, h0, sp):
    n, d_ssm = u.shape
    n_seq, n_state2 = h0.shape
    names = ["bw", "cw", "abar", "dskip", "wglu", "bglu"]
    args = [u, gs, h0] + [sp[k] for k in names]
    full = lambda a: pl.BlockSpec(a.shape, lambda i: (0,) * a.ndim)
    return pl.pallas_call(
        _ssm_sample_kernel,
        grid=(1,),
        in_specs=[full(a) for a in args],
        out_specs=[pl.BlockSpec((n, d_ssm), lambda i: (0, 0)),
                   pl.BlockSpec((n_seq, n_state2), lambda i: (0, 0))],
        out_shape=[jax.ShapeDtypeStruct((n, d_ssm), BF16),
                   jax.ShapeDtypeStruct((n_seq, n_state2), F32)],
        scratch_shapes=[pltpu.VMEM((2 * SSM_LANE_CHUNK // LANES, n, LANES), F32),
                        pltpu.VMEM((2 * SSM_LANE_CHUNK // LANES, n, LANES), F32)],
        compiler_params=pltpu.CompilerParams(
            dimension_semantics=("arbitrary",), vmem_limit_bytes=VMEM_LIMIT_BYTES),
        name="ssm_sample",
    )(*args)


def _outproj_kernel(x_ref, oa_ref, os_ref, w_ref, y_ref):
    d_a = oa_ref.shape[1]
    y = x_ref[...] + jnp.dot(oa_ref[...], w_ref[:d_a, :], preferred_element_type=F32)
    y_ref[...] = y + jnp.dot(os_ref[...], w_ref[d_a:, :], preferred_element_type=F32)


def _outproj(x, oa, os_, w_bf):
    n, d_model = x.shape
    tm = min(ROW_TILE, n)
    row = lambda w: pl.BlockSpec((tm, w), lambda i: (i, 0))
    return pl.pallas_call(
        _outproj_kernel,
        grid=(n // tm,),
        in_specs=[row(d_model), row(oa.shape[1]), row(os_.shape[1]),
                  pl.BlockSpec(w_bf.shape, lambda i: (0, 0))],
        out_specs=row(d_model),
        out_shape=jax.ShapeDtypeStruct((n, d_model), F32),
        compiler_params=pltpu.CompilerParams(
            dimension_semantics=("arbitrary",), vmem_limit_bytes=VMEM_LIMIT_BYTES),
        name="outproj",
    )(x, oa, os_, w_bf)


def _ssm_params(a_re, a_im, log_dt, b_re, b_im, c_re, c_im, d_skip, w_glu, b_glu):
    n_groups, n_state = a_re.shape
    g_per_q = LANES // SSM_GROUP
    n_q = n_groups // g_per_q
    dt = jnp.exp(log_dt.astype(F32))[:, None]
    a_re = a_re.astype(F32)
    a_im = a_im.astype(F32)
    mag = jnp.exp(a_re * dt)
    abar_re = mag * jnp.cos(a_im * dt)
    abar_im = mag * jnp.sin(a_im * dt)
    nr = abar_re - 1.0
    den = a_re * a_re + a_im * a_im
    coef_re = (nr * a_re + abar_im * a_im) / den
    coef_im = (abar_im * a_re - nr * a_im) / den
    b_re = b_re.astype(F32)
    b_im = b_im.astype(F32)
    bbar_re = coef_re[..., None] * b_re - coef_im[..., None] * b_im
    bbar_im = coef_re[..., None] * b_im + coef_im[..., None] * b_re

    same_group = (np.arange(g_per_q * SSM_GROUP)[:, None] // SSM_GROUP
                  == np.arange(g_per_q * n_state)[None, :] // n_state)

    def lane_tile(a, reps):
        w = a.shape[-1]
        return jnp.matmul(a, jnp.asarray(np.tile(np.eye(w, dtype=np.float32), (1, reps))),
                          precision=lax.Precision.HIGHEST)

    def rows_in(t):
        n = t.shape[0]
        t = lane_tile(t.reshape(n, n_q, g_per_q * n_state, SSM_GROUP), g_per_q)
        t = jnp.swapaxes(t * jnp.asarray(same_group.T, F32), -1, -2)
        return jnp.swapaxes(t, 0, 1).reshape(n_q, n * LANES, g_per_q * n_state)

    def cols_out(t):
        n = t.shape[0]
        t = lane_tile(t.reshape(n, n_q, g_per_q * SSM_GROUP, n_state), g_per_q)
        t = jnp.swapaxes(t * jnp.asarray(same_group, F32), -1, -2)
        return jnp.transpose(t, (1, 2, 0, 3)).reshape(n_q, g_per_q * n_state, n * LANES)

    bw = jnp.concatenate([rows_in(bbar_re[None]), rows_in(bbar_im[None])], axis=2).astype(BF16)
    cw = jnp.concatenate([cols_out(c_re.astype(F32)[None]), cols_out(-c_im.astype(F32)[None])],
                         axis=1).astype(BF16)

    ar = abar_re.reshape(1, -1)
    ai = abar_im.reshape(1, -1)
    sp = {
        "bw": bw, "cw": cw,
        "abar": jnp.concatenate([ar, ai], axis=0),
        "dskip": d_skip.astype(F32).reshape(1, -1),
        "wglu": (0.5 * w_glu.astype(F32)).astype(BF16),
        "bglu": 0.5 * b_glu.astype(F32).reshape(1, -1),
    }

    mic = SSM_MICRO
    lr_step = a_re * dt
    th_step = a_im * dt

    def power(t):
        t = jnp.asarray(np.asarray(t, np.float32))[:, None, None]
        mag = jnp.exp(t * lr_step)
        return mag * jnp.cos(t * th_step), mag * jnp.sin(t * th_step)

    pw_r, pw_i = power(np.arange(mic + 1))
    zr, zi = power(np.arange(mic - 1, -1, -1))
    wb_r = zr[..., None] * bbar_re[None] - zi[..., None] * bbar_im[None]
    wb_i = zr[..., None] * bbar_im[None] + zi[..., None] * bbar_re[None]
    w = jnp.concatenate([rows_in(wb_r), rows_in(wb_i)], axis=2).astype(BF16)

    c_re = c_re.astype(F32)
    c_im = c_im.astype(F32)
    pr1, pi1 = pw_r[1:mic + 1][:, :, None, :], pw_i[1:mic + 1][:, :, None, :]
    v_r = c_re[None] * pr1 - c_im[None] * pi1
    v_i = c_re[None] * pi1 + c_im[None] * pr1
    v = jnp.concatenate([cols_out(v_r), cols_out(-v_i)], axis=1).astype(BF16)

    tb_r = pw_r[:mic, :, :, None] * bbar_re[None] - pw_i[:mic, :, :, None] * bbar_im[None]
    tb_i = pw_r[:mic, :, :, None] * bbar_im[None] + pw_i[:mic, :, :, None] * bbar_re[None]
    taps = (jnp.sum(c_re[None, :, None, :, :] * jnp.swapaxes(tb_r, 2, 3)[:, :, :, None, :], axis=-1)
            - jnp.sum(c_im[None, :, None, :, :] * jnp.swapaxes(tb_i, 2, 3)[:, :, :, None, :], axis=-1))
    zero = jnp.zeros_like(taps[0])
    grid = jnp.stack([jnp.stack([taps[t - s] if t >= s else zero for t in range(mic)], axis=0)
                      for s in range(mic)], axis=0)
    grid = lane_tile(grid.reshape(mic, mic, n_q, LANES, SSM_GROUP), g_per_q)
    grid = grid * jnp.asarray(same_group[:, ::n_state // SSM_GROUP], F32)
    m = jnp.transpose(grid, (2, 0, 3, 1, 4)).reshape(n_q, mic * LANES, mic * LANES).astype(BF16)

    rb = SSM_SCAN_ROWS
    k = np.arange(rb)
    in_r, in_i = power(-mic * k)
    out_r, out_i = power(mic * (k - 1))
    flat = lambda a: a.reshape(a.shape[0], -1)
    aux_r, aux_i = power(np.array([mic, mic * (rb - 1), mic * rb]))
    aux = jnp.stack([flat(aux_r), flat(aux_i)], axis=1).reshape(6, -1)
    n_rows = SSM_STEP // mic
    blk = np.arange(n_rows) // rb
    strict = (blk[:, None] == blk[None, :]) & (np.arange(n_rows)[:, None] > np.arange(n_rows)[None, :])
    sums = np.arange(n_rows // rb)[:, None] == blk[None, :]
    pad = np.zeros((-(n_rows + n_rows // rb) % 16, n_rows), bool)
    scan = jnp.asarray(np.concatenate([strict, sums, pad], axis=0).astype(np.float32), BF16)
    sp.update({"w": w, "m": m, "v": v, "aux": aux, "scan": scan,
               "pin_re": flat(in_r), "pin_im": flat(in_i), "pout_re": flat(out_r), "pout_im": flat(out_i)})
    return sp


def _state_to_lanes(h_re, h_im):
    b = h_re.shape[0]
    sc = SSM_LANE_CHUNK
    r = h_re.astype(F32).reshape(b, -1, 1, sc)
    i = h_im.astype(F32).reshape(b, -1, 1, sc)
    return jnp.concatenate([r, i], axis=2).reshape(b, -1)


def _lanes_to_state(h, n_groups, n_state):
    b = h.shape[0]
    h = h.reshape(b, -1, 2, SSM_LANE_CHUNK)
    return (h[:, :, 0, :].reshape(b, n_groups, n_state), h[:, :, 1, :].reshape(b, n_groups, n_state))


def _toeplitz(v, n):
    h = v.shape[0]
    x = jnp.broadcast_to(v[:, None, :], (h, n, 2 * n)).reshape(h, 2 * n * n)
    return x[:, :n * (2 * n - 1)].reshape(h, n, 2 * n - 1)[:, :, :n]


def _prompt_bias_blocks(fvec):
    n = LANES
    h = fvec.shape[0]
    neg = jnp.full((h, n - 1), NEG_INF, F32)
    va = jnp.concatenate([fvec[:, 0:1], neg, jnp.zeros((h, 1), F32), fvec[:, 1:n][:, ::-1]], axis=1)
    vb = jnp.concatenate([fvec[:, 1:n + 1][:, ::-1], jnp.zeros((h, n), F32)], axis=1)
    return jnp.stack([_toeplitz(va, n), _toeplitz(vb, n)], axis=1)


def _decode_bias(fvec, page, dec_seq, n_heads):
    h = fvec.shape[0]
    rows = []
    for i in range(dec_seq):
        last = fvec[:, i + 1:i + 1 + page][:, ::-1]
        new = jnp.concatenate([fvec[:, 0:i + 1][:, ::-1], jnp.full((h, page - i - 1), NEG_INF, F32)], axis=1)
        rows.append(jnp.concatenate([last, new], axis=1))
    per_head = jnp.stack(rows, axis=0)
    same = np.eye(n_heads, dtype=bool)[None, :, None, :]
    near = jnp.where(jnp.asarray(same), per_head[:, :, :, None], NEG_INF)
    near = near.reshape(dec_seq * n_heads, -1)
    mask = np.where(np.broadcast_to(same, (dec_seq, n_heads, page, n_heads)), 0.0, NEG_INF)
    mask = mask.reshape(dec_seq * n_heads, -1).astype(np.float32)
    return jnp.asarray(np.concatenate([mask, mask], axis=0)), jnp.concatenate([near, near], axis=0)


def kernel(x_prompt, x_sample, cache_k, cache_v, state_ssm_re, state_ssm_im, page_table,
           norm_g, w_in, q_norm_g, k_norm_g, lambda_q1, lambda_k1, lambda_q2, lambda_k2,
           subln_g, rel_bias, ssm_a_re, ssm_a_im, ssm_log_dt, ssm_b_re, ssm_b_im,
           ssm_c_re, ssm_c_im, ssm_d, w_glu, b_glu, w_out):
    batch, seq, d_model = x_prompt.shape
    dec_batch, dec_seq, _ = x_sample.shape
    depth, n_pool, page, n_heads, _ = cache_k.shape
    n_pages = page_table.shape[1]
    d_attn = n_heads * V_DIM
    n_groups, n_state = ssm_a_re.shape[1:]
    new_rows = dec_seq * n_heads

    buckets = _bucket_table(2 * LANES)
    far_from = int(np.max(np.nonzero(buckets < N_BUCKETS - 1)[0])) + 1
    assert far_from <= LANES and _bucket_table(seq + page * n_pages)[far_from:].min() == N_BUCKETS - 1
    assert page == LANES and dec_seq < LANES and ATTN_TQ == ATTN_TK

    rel_bias = rel_bias.astype(F32)
    fvec = (rel_bias[buckets].T - rel_bias[N_BUCKETS - 1][:, None]) * LOG2E
    fvec = jnp.where(jnp.asarray(np.arange(2 * LANES) < far_from)[None], fvec, 0.0)
    dtiles = _prompt_bias_blocks(fvec)
    head_mask, near_bias = _decode_bias(fvec, page, dec_seq, n_heads)

    group_avg = jnp.asarray(np.kron(np.eye(2 * LANES // QK_DIM), np.full((QK_DIM, QK_DIM), 1.0 / QK_DIM)), BF16)
    n_rep = d_attn // QK_DIM
    cache_k_rows = cache_k.reshape(depth * n_pool, page * n_heads, V_DIM)
    cache_v_rows = cache_v.reshape(depth * n_pool, page * n_heads, V_DIM)

    hp = x_prompt.reshape(batch * seq, d_model)
    hs = x_sample.reshape(dec_batch * dec_seq, d_model)
    kp_l, vp_l, ks_l, vs_l = [], [], [], []
    srp_l, sip_l, srs_l, sis_l = [], [], [], []
    for l in range(depth):
        lam_init = _lambda_init(l)
        lam = (jnp.exp(jnp.sum(lambda_q1[l].astype(F32) * lambda_k1[l].astype(F32)))
               - jnp.exp(jnp.sum(lambda_q2[l].astype(F32) * lambda_k2[l].astype(F32))) + lam_init)
        lam = lam.reshape(1).astype(F32)
        ng = norm_g[l].astype(F32).reshape(1, d_model)
        w_bf = w_in[l].astype(BF16)
        gq = jnp.tile(q_norm_g[l].astype(F32), n_rep).reshape(1, d_attn) * (QK_DIM ** -0.5 * LOG2E)
        gk = jnp.tile(k_norm_g[l].astype(F32), n_rep).reshape(1, d_attn)
        sg = (subln_g[l].astype(F32) * (1.0 - lam_init)).reshape(1, V_DIM)
        wo_bf = w_out[l].astype(BF16)
        sp = _ssm_params(ssm_a_re[l], ssm_a_im[l], ssm_log_dt[l], ssm_b_re[l], ssm_b_im[l],
                         ssm_c_re[l], ssm_c_im[l], ssm_d[l], w_glu[l], b_glu[l])

        q1, q2, k4, kb, v4, va, ga, u4, gs = _inproj(hp, ng, w_bf, gq, gk, group_avg, head_rows=False)
        sq1, sq2, sk4, sv4, sga, su, sgs = _inproj(hs, ng, w_bf, gq, gk, group_avg, head_rows=True)
        r3 = lambda a: a.reshape(batch, seq, a.shape[-1])
        o_a, o_dec = _attention(lam, sg, r3(q1), r3(q2), r3(kb), r3(va), dtiles, r3(ga),
                                page_table + l * n_pool, sq1, sq2, sk4, sv4, sga, head_mask, near_bias,
                                cache_k_rows, cache_v_rows, new_rows)
        o_s, hfin = _ssm_prompt(u4.reshape(batch, -1, LANES), r3(gs), sp)
        hp = _outproj(hp, o_a.reshape(batch * seq, d_attn), o_s.reshape(batch * seq, -1), wo_bf)
        kp_l.append(k4.reshape(batch, seq, n_heads, V_DIM).astype(cache_k.dtype))
        vp_l.append(v4.reshape(batch, seq, n_heads, V_DIM).astype(cache_v.dtype))
        hr_p, hi_p = _lanes_to_state(hfin.reshape(batch, -1), n_groups, n_state)
        srp_l.append(hr_p.astype(state_ssm_re.dtype))
        sip_l.append(hi_p.astype(state_ssm_im.dtype))

        h0 = _state_to_lanes(state_ssm_re[l], state_ssm_im[l])
        o_s, hfin = _ssm_sample(su, sgs, h0, sp)
        hs = _outproj(hs, o_dec.reshape(dec_batch * dec_seq, d_attn).astype(BF16), o_s, wo_bf)
        ks_l.append(sk4.reshape(dec_batch, dec_seq, n_heads, V_DIM).astype(cache_k.dtype))
        vs_l.append(sv4.reshape(dec_batch, dec_seq, n_heads, V_DIM).astype(cache_v.dtype))
        hr_s, hi_s = _lanes_to_state(hfin, n_groups, n_state)
        srs_l.append(hr_s.astype(state_ssm_re.dtype))
        sis_l.append(hi_s.astype(state_ssm_im.dtype))

    y_prompt = hp.reshape(batch, seq, d_model).astype(x_prompt.dtype)
    y_sample = hs.reshape(dec_batch, dec_seq, d_model).astype(x_sample.dtype)
    return (y_prompt, y_sample, jnp.stack(kp_l), jnp.stack(vp_l), jnp.stack(ks_l), jnp.stack(vs_l),
            jnp.stack(srp_l), jnp.stack(sip_l), jnp.stack(srs_l), jnp.stack(sis_l))
```

```python
import functools
import math

import numpy as np
import jax
import jax.numpy as jnp
from jax import lax
from jax.experimental import pallas as pl
from jax.experimental.pallas import tpu as pltpu

F32 = jnp.float32
BF16 = jnp.bfloat16

QK_DIM = 64
V_DIM = 2 * QK_DIM
N_BUCKETS = 32
MAX_DISTANCE = 128
SSM_GROUP = 16
SSM_STATE = 64
EPS = 1e-6
NEG_INF = -1e30
LOG2E = math.log2(math.e)
GELU_C0 = math.sqrt(2.0 / math.pi)
GELU_C1 = GELU_C0 * 0.044715

LANES = 128
VMEM_LIMIT_BYTES = 56 * 1024 * 1024

ROW_TILE = 512
ATTN_TQ = 512
ATTN_TK = 512
ATTN_UNROLL = 8
DECODE_SEQS_PER_STEP = 2
SSM_MICRO = 4
SSM_STEP = 1024
SSM_SCAN_ROWS = 32
SSM_LANE_CHUNK = 512


def _lambda_init(layer):
    return 0.8 - 0.6 * math.exp(-0.3 * layer)


def _bucket_table(n_max):
    n = np.arange(n_max)
    max_exact = N_BUCKETS // 2
    nf = np.maximum(n, 1).astype(np.float32)
    large = max_exact + (np.log(nf / np.float32(max_exact)) / np.float32(math.log(MAX_DISTANCE / max_exact))
                         * np.float32(N_BUCKETS - max_exact)).astype(np.int32)
    large = np.minimum(large, N_BUCKETS - 1)
    return np.where(n < max_exact, n, large).astype(np.int32)


def _silu(x):
    return (0.5 * x) * (1.0 + jnp.tanh(0.5 * x))


def _store_head_rows(ref, val, n_heads):
    rows = val.shape[0]
    for h in range(n_heads):
        ref[pl.ds(h, rows, stride=n_heads), :] = val[:, h * V_DIM:(h + 1) * V_DIM]


def _inproj_kernel(x_ref, ng_ref, w_ref, gq_ref, gk_ref, gavg_ref, *out_refs, head_rows):
    x = x_ref[...]
    ms = jnp.mean(x * x, axis=-1, keepdims=True)
    xb = (x * lax.rsqrt(ms + EPS) * ng_ref[...]).astype(BF16)
    d_seg = gq_ref.shape[1]
    n_heads = d_seg // V_DIM

    def seg(i):
        return jnp.dot(xb, w_ref[:, i * d_seg:(i + 1) * d_seg], preferred_element_type=F32)

    def group_norm(t, g):
        sq = (t * t).astype(BF16)
        wg = gavg_ref.shape[0]
        msq = jnp.concatenate([jnp.dot(sq[:, c:c + wg], gavg_ref[...], preferred_element_type=F32)
                               for c in range(0, d_seg, wg)], axis=1)
        return t * lax.rsqrt(msq + EPS) * g

    q = group_norm(seg(0), gq_ref[...])
    lane = lax.broadcasted_iota(jnp.int32, q.shape, 1)
    first = (lane % V_DIM) < QK_DIM
    qa = jnp.where(first, q, 0.0)
    qb = jnp.where(first, 0.0, q)
    k = group_norm(seg(1), gk_ref[...])
    v = seg(2)
    ga = seg(3)
    if head_rows:
        q1_ref, q2_ref, k4_ref, v4_ref, ga_ref, u_ref, gs_ref = out_refs
        _store_head_rows(q1_ref, qa, n_heads)
        _store_head_rows(q2_ref, qb, n_heads)
        _store_head_rows(ga_ref, ga, n_heads)
    else:
        q1_ref, q2_ref, k4_ref, kb_ref, v4_ref, va_ref, ga_ref, u_ref, gs_ref = out_refs
        q1_ref[...] = qa.astype(BF16)
        q2_ref[...] = qb.astype(BF16)
        kb_ref[...] = k.astype(BF16)
        vb = v.astype(BF16)
        ones = jnp.ones((v.shape[0], V_DIM), BF16)
        pieces = []
        for h in range(n_heads):
            pieces += [vb[:, h * V_DIM:(h + 1) * V_DIM], ones]
        va_ref[...] = jnp.concatenate(pieces, axis=1)
        ga_ref[...] = ga
    _store_head_rows(k4_ref, k, n_heads)
    _store_head_rows(v4_ref, v, n_heads)
    if head_rows:
        u_ref[...] = seg(4)
    else:
        _store_head_rows(u_ref, seg(4), d_seg // LANES)
    gs_ref[...] = seg(5)


def _inproj(x, ng, w_bf, gq, gk, gavg, head_rows):
    n, d_model = x.shape
    d_seg = gq.shape[1]
    n_heads = d_seg // V_DIM
    tm = min(ROW_TILE, n)
    full = lambda a: pl.BlockSpec(a.shape, lambda i: (0,) * a.ndim)
    wide = lambda w, dt: (jax.ShapeDtypeStruct((n, w), dt), pl.BlockSpec((tm, w), lambda i: (i, 0)))
    tall = lambda dt: (jax.ShapeDtypeStruct((n * n_heads, V_DIM), dt),
                       pl.BlockSpec((tm * n_heads, V_DIM), lambda i: (i, 0)))
    if head_rows:
        outs = [tall(F32), tall(F32), tall(F32), tall(F32), tall(F32), wide(d_seg, F32), wide(d_seg, F32)]
    else:
        outs = [wide(d_seg, BF16), wide(d_seg, BF16), tall(F32), wide(d_seg, BF16), tall(F32),
                wide(2 * d_seg, BF16), wide(d_seg, F32), tall(F32), wide(d_seg, F32)]
    return pl.pallas_call(
        functools.partial(_inproj_kernel, head_rows=head_rows),
        grid=(n // tm,),
        in_specs=[pl.BlockSpec((tm, d_model), lambda i: (i, 0)),
                  full(ng), full(w_bf), full(gq), full(gk), full(gavg)],
        out_specs=[o[1] for o in outs],
        out_shape=[o[0] for o in outs],
        compiler_params=pltpu.CompilerParams(
            dimension_semantics=("arbitrary",), vmem_limit_bytes=VMEM_LIMIT_BYTES),
        name="inproj_samples" if head_rows else "inproj_prompt",
    )(x, ng, w_bf, gq, gk, gavg)


def _diff_epilogue(o1, o2, lam, sg, ga):
    od = o1 - lam * o2
    ms = jnp.mean(od * od, axis=-1, keepdims=True)
    return od * lax.rsqrt(ms + EPS) * sg * _silu(ga)


def _attention_kernel(tasks_ref, pt_ref, lam_ref,
                      q1a_ref, q2a_ref, q1b_ref, q2b_ref, k_ref, v_ref, ab_ref, gaa_ref, gab_ref, sg_ref,
                      dq1_ref, dq2_ref, dkn_ref, dvn_ref, dga_ref, hm_ref, near_ref, ck_hbm, cv_hbm,
                      o_ref, od_ref,
                      s_scr, d_scr, m_scr, acc_scr, qs_scr, kbuf, vbuf, knew, vnew, ksem, vsem):
    tq = s_scr.shape[1] // 2
    tk = s_scr.shape[2]
    p = pl.program_id(2)
    n_steps = pl.num_programs(2)
    n_tasks = tasks_ref.shape[1]
    unroll = min(ATTN_UNROLL, n_tasks - 1)
    n_trips = (n_tasks - 1) // unroll
    step = (pl.program_id(0) * pl.num_programs(1) + pl.program_id(1)) * n_steps + p
    n_pages = kbuf.shape[1]
    page_rows = kbuf.shape[2]
    total_steps = pt_ref.shape[0] // DECODE_SEQS_PER_STEP
    lam = lam_ref[0]

    def k_copy(st, local, j):
        return pltpu.make_async_copy(ck_hbm.at[pt_ref[st * DECODE_SEQS_PER_STEP + local, j]],
                                     kbuf.at[local, j], ksem.at[0])

    def v_copy(st, local, j):
        return pltpu.make_async_copy(cv_hbm.at[pt_ref[st * DECODE_SEQS_PER_STEP + local, j]],
                                     vbuf.at[local, j], vsem.at[0])

    def start_fetch(st):
        for local in range(DECODE_SEQS_PER_STEP):
            for j in range(n_pages):
                k_copy(st, local, j).start()
                v_copy(st, local, j).start()

    def wait_fetch(st):
        for local in range(DECODE_SEQS_PER_STEP):
            for j in range(n_pages):
                k_copy(st, local, j).wait()
                v_copy(st, local, j).wait()

    def decode(local, rows):
        r0 = local * rows
        qx = jnp.concatenate([dq1_ref[r0:r0 + rows, :], dq2_ref[r0:r0 + rows, :]], axis=0).astype(BF16)
        knew[local, 0:rows, :] = dkn_ref[r0:r0 + rows, :].astype(BF16)
        vnew[local, 0:rows, :] = dvn_ref[r0:r0 + rows, :].astype(BF16)
        nt = (((1,), (1,)), ((), ()))
        s_tiles = [lax.dot_general(qx, kbuf[local, j].astype(BF16), nt, preferred_element_type=F32)
                   for j in range(n_pages)]
        s_tiles.append(lax.dot_general(qx, knew[local], nt, preferred_element_type=F32))
        head_mask = hm_ref[...]
        s = jnp.concatenate([t + head_mask for t in s_tiles[:n_pages - 1]]
                            + [jnp.concatenate(s_tiles[n_pages - 1:], axis=1) + near_ref[...]], axis=1)
        m = jnp.max(s, axis=-1, keepdims=True)
        pr = jnp.exp2(s - m)
        l_sum = jnp.sum(pr, axis=-1, keepdims=True)
        pb = pr.astype(BF16)
        acc = jnp.dot(pb[:, n_pages * page_rows:], vnew[local], preferred_element_type=F32)
        for j in range(n_pages):
            acc = acc + jnp.dot(pb[:, j * page_rows:(j + 1) * page_rows], vbuf[local, j].astype(BF16),
                                preferred_element_type=F32)
        o = acc / l_sum
        od_ref[r0:r0 + rows, :] = _diff_epilogue(o[:rows], o[rows:], lam, sg_ref[...], dga_ref[r0:r0 + rows, :])

    @pl.when(step == 0)
    def _():
        start_fetch(0)
        knew[...] = jnp.zeros(knew.shape, knew.dtype)
        vnew[...] = jnp.zeros(vnew.shape, vnew.dtype)

    @pl.when(p == 0)
    def _():
        for t in range(2):
            for a in range(tq // LANES):
                for b in range(tk // LANES):
                    delta = a - b + t * (tk // LANES)
                    if delta in (0, 1):
                        blk = ab_ref[delta]
                    else:
                        blk = jnp.full((LANES, LANES), NEG_INF if delta < 0 else 0.0, F32)
                    d_scr[t, a * LANES:(a + 1) * LANES, b * LANES:(b + 1) * LANES] = blk
        d_scr[2] = jnp.zeros((tq, tk), F32)

    m_scr[...] = jnp.full(m_scr.shape, NEG_INF, F32)
    acc_scr[...] = jnp.zeros(acc_scr.shape, F32)
    qs_scr[0] = jnp.concatenate([q1a_ref[...], q2a_ref[...]], axis=0)
    qs_scr[1] = jnp.concatenate([q1b_ref[...], q2b_ref[...]], axis=0)

    def task(kind, i):
        return tasks_ref[kind * n_steps + p, i]

    def produce(i, slot):
        k0 = pl.multiple_of(task(1, i) * tk, tk)
        s = lax.dot_general(qs_scr[task(3, i)], k_ref[pl.ds(k0, tk), :], (((1,), (1,)), ((), ())),
                            preferred_element_type=F32)
        bias = d_scr[task(2, i)]
        s_scr[slot] = s + jnp.concatenate([bias, bias], axis=0)

    def consume(i, slot):
        k0 = pl.multiple_of(task(1, i) * tk, tk)
        a = task(3, i)
        s = s_scr[slot]
        m_old = m_scr[a]
        m_new = jnp.maximum(m_old, jnp.max(s, axis=-1, keepdims=True))
        pr = jnp.exp2(s - jnp.concatenate([m_new] * (tk // LANES), axis=1))
        alpha = jnp.exp2(m_old - m_new)
        pv = jnp.dot(pr.astype(BF16), v_ref[pl.ds(k0, tk), :], preferred_element_type=F32)
        acc_scr[a] = acc_scr[a] * jnp.concatenate([alpha] * (acc_scr.shape[2] // LANES), axis=1) + pv
        m_scr[a] = m_new

    wait_fetch(step)
    for local in range(DECODE_SEQS_PER_STEP):
        decode(local, od_ref.shape[0] // DECODE_SEQS_PER_STEP)

    @pl.when(step + 1 < total_steps)
    def _():
        start_fetch(step + 1)

    produce(0, 0)

    def body(j, carry):
        for u in range(unroll):
            i = j * unroll + u
            produce(i + 1, (u + 1) % 2)
            consume(i, u % 2)
        return carry

    lax.fori_loop(0, n_trips, body, 0)
    consume(n_tasks - 1, (n_tasks - 1) % 2)

    for a, (ga_ref, i) in enumerate(((gaa_ref, 0), (gab_ref, n_tasks - 1))):
        q0 = pl.multiple_of(task(0, i) * tq, tq)
        acc = acc_scr[a]
        o = acc[:, :V_DIM] / acc[:, V_DIM:]
        out = _diff_epilogue(o[:tq], o[tq:], lam, sg_ref[...], ga_ref[...])
        o_ref[pl.ds(q0, tq), :] = out.astype(o_ref.dtype)


def _attn_tasks(n_q):
    n_steps = n_q // 2
    tab = np.zeros((4, n_steps, n_q + 1), np.int32)
    for p in range(n_steps):
        i = 0
        for acc, qt in enumerate((p, n_q - 1 - p)):
            for t in range(qt + 1):
                tab[:, p, i] = (qt, qt - t, min(t, 2), acc)
                i += 1
        assert i == n_q + 1
    return tab.reshape(4 * n_steps, n_q + 1)


def _attention(lam, sg, q1, q2, kb, va, dtiles, ga,
               page_table, dq1, dq2, dkn, dvn, dga, head_mask, near_bias, cache_k, cache_v, new_rows):
    b, l, d_attn = q1.shape
    n_heads = d_attn // V_DIM
    tq, tk = ATTN_TQ, ATTN_TK
    n_q = l // tq
    n_steps = n_q // 2
    unroll = min(ATTN_UNROLL, n_q)
    n_seq, n_pages = page_table.shape
    page_rows = cache_k.shape[1]
    total_steps = b * n_heads * n_steps
    assert tq == tk and n_q % 2 == 0 and n_q % unroll == 0 and unroll % 2 == 0
    assert n_seq == total_steps * DECODE_SEQS_PER_STEP
    tasks = jnp.asarray(_attn_tasks(n_q))
    dec_rows = DECODE_SEQS_PER_STEP * new_rows

    def gstep(bi, h, pi):
        return (bi * n_heads + h) * n_steps + pi

    whole = lambda w: pl.BlockSpec((None, l, w), lambda bi, h, pi, *_: (bi, 0, h))
    tile_a = pl.BlockSpec((None, tq, V_DIM), lambda bi, h, pi, *_: (bi, pi, h))
    tile_b = pl.BlockSpec((None, tq, V_DIM), lambda bi, h, pi, *_: (bi, n_q - 1 - pi, h))
    dec = pl.BlockSpec((dec_rows, V_DIM), lambda bi, h, pi, *_: (gstep(bi, h, pi), 0))
    const2 = lambda a: pl.BlockSpec(a.shape, lambda bi, h, pi, *_: (0, 0))
    grid_spec = pltpu.PrefetchScalarGridSpec(
        num_scalar_prefetch=2,
        grid=(b, n_heads, n_steps),
        in_specs=[
            pl.BlockSpec(memory_space=pltpu.SMEM),
            tile_a, tile_a, tile_b, tile_b, whole(V_DIM), whole(2 * V_DIM),
            pl.BlockSpec((None, 2, LANES, LANES), lambda bi, h, pi, *_: (h, 0, 0, 0)),
            tile_a, tile_b, const2(sg),
            dec, dec, dec, dec, dec, const2(head_mask), const2(near_bias),
            pl.BlockSpec(memory_space=pl.ANY), pl.BlockSpec(memory_space=pl.ANY),
        ],
        out_specs=[whole(V_DIM), dec],
        scratch_shapes=[
            pltpu.VMEM((2, 2 * tq, tk), F32), pltpu.VMEM((3, tq, tk), F32),
            pltpu.VMEM((2, 2 * tq, LANES), F32), pltpu.VMEM((2, 2 * tq, 2 * V_DIM), F32),
            pltpu.VMEM((2, 2 * tq, V_DIM), BF16),
            pltpu.VMEM((DECODE_SEQS_PER_STEP, n_pages, page_rows, V_DIM), cache_k.dtype),
            pltpu.VMEM((DECODE_SEQS_PER_STEP, n_pages, page_rows, V_DIM), cache_v.dtype),
            pltpu.VMEM((DECODE_SEQS_PER_STEP, page_rows, V_DIM), BF16),
            pltpu.VMEM((DECODE_SEQS_PER_STEP, page_rows, V_DIM), BF16),
            pltpu.SemaphoreType.DMA((1,)), pltpu.SemaphoreType.DMA((1,)),
        ],
    )
    return pl.pallas_call(
        _attention_kernel,
        grid_spec=grid_spec,
        out_shape=[jax.ShapeDtypeStruct((b, l, d_attn), BF16),
                   jax.ShapeDtypeStruct((n_seq * new_rows, V_DIM), F32)],
        compiler_params=pltpu.CompilerParams(
            dimension_semantics=("arbitrary", "arbitrary", "arbitrary"),
            vmem_limit_bytes=VMEM_LIMIT_BYTES),
        name="attention",
    )(tasks, page_table, lam, q1, q2, q1, q2, kb, va, dtiles, ga, ga, sg,
      dq1, dq2, dkn, dvn, dga, head_mask, near_bias, cache_k, cache_v)


def _ssm_tail(y, u, gs, dskip_ref, wglu_half_ref, bglu_half_ref):
    x = y + dskip_ref[...] * u
    inner = x * (GELU_C1 * (x * x) + GELU_C0)
    z = (0.5 * x) * (1.0 + jnp.tanh(inner))
    t_glu = jnp.tanh(jnp.dot(z.astype(BF16), wglu_half_ref[...], preferred_element_type=F32) + bglu_half_ref[...])
    t_gate = jnp.tanh(0.5 * gs)
    return ((z * gs) * 0.25) * (1.0 + t_glu) * (1.0 + t_gate)


def _ssm_prompt_kernel(x_ref, oa_ref, u4_ref, gs_ref, w_ref, m_ref, v_ref,
                       pin_re_ref, pin_im_ref, pout_re_ref, pout_im_ref,
                       aux_ref, scan_ref, dskip_ref, wglu_ref, bglu_ref, wo_ref, o_ref, hfin_ref,
                       carry_scr, y4_scr):
    n_q = w_ref.shape[0]
    mic = SSM_MICRO
    sc = SSM_LANE_CHUNK
    ts = gs_ref.shape[0]
    r = ts // mic
    rb = pin_re_ref.shape[0]
    n_blk = r // rb
    tile_rows = lambda a: jnp.concatenate([a] * n_blk, axis=0)
    c = pl.program_id(1)

    @pl.when(c == 0)
    def _():
        carry_scr[...] = jnp.zeros(carry_scr.shape, F32)

    for q in range(n_q):
        re_l = slice(2 * q * sc, (2 * q + 1) * sc)
        im_l = slice((2 * q + 1) * sc, 2 * (q + 1) * sc)
        st_l = slice(q * sc, (q + 1) * sc)
        x = jnp.concatenate([u4_ref[pl.ds(n_q * s + q, r, stride=n_q * mic), :] for s in range(mic)],
                            axis=1).astype(BF16)
        e = jnp.dot(x, w_ref[q], preferred_element_type=F32)
        er, em = e[:, :sc], e[:, sc:]
        pw = lambda ref, i: ref[i:i + 1, st_l]
        pir, pii = tile_rows(pin_re_ref[:, st_l]), tile_rows(pin_im_ref[:, st_l])
        xs = jnp.concatenate([er * pir - em * pii, er * pii + em * pir], axis=1).astype(BF16)
        cs = jnp.dot(scan_ref[...], xs, preferred_element_type=F32)
        tot_r, tot_m = cs[r:r + n_blk, :sc], cs[r:r + n_blk, sc:]
        t_r = tot_r * pw(aux_ref, 2) - tot_m * pw(aux_ref, 3)
        t_m = tot_r * pw(aux_ref, 3) + tot_m * pw(aux_ref, 2)
        ar, am = pw(aux_ref, 0), pw(aux_ref, 1)
        br, bm = pw(aux_ref, 4), pw(aux_ref, 5)
        h_r, h_m = carry_scr[:, re_l], carry_scr[:, im_l]
        base_r, base_m = [], []
        for blk in range(n_blk):
            base_r.append(jnp.broadcast_to(ar * h_r - am * h_m, (rb, sc)))
            base_m.append(jnp.broadcast_to(ar * h_m + am * h_r, (rb, sc)))
            h_r, h_m = (br * h_r - bm * h_m + t_r[blk:blk + 1, :], br * h_m + bm * h_r + t_m[blk:blk + 1, :])
        carry_scr[:, re_l] = h_r
        carry_scr[:, im_l] = h_m
        hfin_ref[:, re_l] = h_r
        hfin_ref[:, im_l] = h_m
        sr = cs[:r, :sc] + jnp.concatenate(base_r, axis=0)
        sm = cs[:r, sc:] + jnp.concatenate(base_m, axis=0)
        por, poi = tile_rows(pout_re_ref[:, st_l]), tile_rows(pout_im_ref[:, st_l])
        hp = jnp.concatenate([sr * por - sm * poi, sr * poi + sm * por], axis=1).astype(BF16)
        y = (jnp.dot(x, m_ref[q], preferred_element_type=F32)
             + jnp.dot(hp, v_ref[q], preferred_element_type=F32))
        for s in range(mic):
            y4_scr[pl.ds(n_q * s + q, r, stride=n_q * mic), :] = y[:, s * LANES:(s + 1) * LANES]
    y = jnp.concatenate([y4_scr[pl.ds(q, ts, stride=n_q), :] for q in range(n_q)], axis=1)
    u = jnp.concatenate([u4_ref[pl.ds(q, ts, stride=n_q), :] for q in range(n_q)], axis=1)
    o_s = _ssm_tail(y, u, gs_ref[...], dskip_ref, wglu_ref, bglu_ref).astype(BF16)
    d_a = oa_ref.shape[1]
    out = x_ref[...] + jnp.dot(oa_ref[...], wo_ref[:d_a, :], preferred_element_type=F32)
    o_ref[...] = out + jnp.dot(o_s, wo_ref[d_a:, :], preferred_element_type=F32)


def _ssm_prompt_outproj(x, oa, u4, gs, sp, wo_bf):
    b, l, d_ssm = gs.shape
    n_q = d_ssm // LANES
    ts = SSM_STEP
    n_state2 = 2 * sp["aux"].shape[1]
    row = lambda w: pl.BlockSpec((None, ts, w), lambda bi, ci: (bi, ci, 0))
    row4 = pl.BlockSpec((None, ts * n_q, LANES), lambda bi, ci: (bi, ci, 0))
    full = lambda a: pl.BlockSpec(a.shape, lambda bi, ci: (0,) * a.ndim)
    names = ["w", "m", "v", "pin_re", "pin_im", "pout_re", "pout_im", "aux", "scan", "dskip", "wglu", "bglu"]
    return pl.pallas_call(
        _ssm_prompt_kernel,
        grid=(b, l // ts),
        in_specs=[row(x.shape[2]), row(oa.shape[2]), row4, row(d_ssm)] + [full(sp[n]) for n in names]
        + [full(wo_bf)],
        out_specs=[row(x.shape[2]), pl.BlockSpec((None, 1, n_state2), lambda bi, ci: (bi, 0, 0))],
        out_shape=[jax.ShapeDtypeStruct(x.shape, F32),
                   jax.ShapeDtypeStruct((b, 1, n_state2), F32)],
        scratch_shapes=[pltpu.VMEM((1, n_state2), F32), pltpu.VMEM((ts * n_q, LANES), F32)],
        compiler_params=pltpu.CompilerParams(
            dimension_semantics=("arbitrary", "arbitrary"), vmem_limit_bytes=VMEM_LIMIT_BYTES),
        name="ssm_prompt_outproj",
    )(x, oa, u4, gs, *[sp[n] for n in names], wo_bf)


def _ssm_sample_kernel(u_ref, gs_ref, h0_ref, bw_ref, cw_ref, abar_ref, dskip_ref, wglu_ref, bglu_ref,
                       o_ref, hfin_ref, bu_scr, h_scr):
    n_seq = h0_ref.shape[0]
    dec_seq = u_ref.shape[0] // n_seq
    n_q = bw_ref.shape[0]
    sc = SSM_LANE_CHUNK
    n_lc = 2 * sc // LANES
    u = u_ref[...]
    ys = []
    for q in range(n_q):
        uq = u[:, q * LANES:(q + 1) * LANES].astype(BF16)
        bu = jnp.dot(uq, bw_ref[q], preferred_element_type=F32)
        for c in range(n_lc):
            bu_scr[c] = bu[:, c * LANES:(c + 1) * LANES]
        lanes = slice(q * sc, (q + 1) * sc)
        ar, am = abar_ref[0:1, lanes], abar_ref[1:2, lanes]
        hr = h0_ref[:, 2 * q * sc:(2 * q + 1) * sc]
        hm = h0_ref[:, (2 * q + 1) * sc:2 * (q + 1) * sc]
        for step in range(dec_seq):
            rows = pl.ds(step, n_seq, stride=dec_seq)
            b_all = jnp.concatenate([bu_scr[c, rows, :] for c in range(n_lc)], axis=1)
            br, bi = b_all[:, :sc], b_all[:, sc:]
            hr, hm = ar * hr - am * hm + br, ar * hm + am * hr + bi
            for c in range(n_lc // 2):
                h_scr[c, rows, :] = hr[:, c * LANES:(c + 1) * LANES]
                h_scr[n_lc // 2 + c, rows, :] = hm[:, c * LANES:(c + 1) * LANES]
        hfin_ref[:, 2 * q * sc:(2 * q + 1) * sc] = hr
        hfin_ref[:, (2 * q + 1) * sc:2 * (q + 1) * sc] = hm
        h_all = jnp.concatenate([h_scr[c] for c in range(n_lc)], axis=1)
        ys.append(jnp.dot(h_all.astype(BF16), cw_ref[q], preferred_element_type=F32))
    y = jnp.concatenate(ys, axis=1)
    o_ref[...] = _ssm_tail(y, u, gs_ref[...], dskip_ref, wglu_ref, bglu_ref).astype(o_ref.dtype)


def _ssm_sample(u, gs, h0, sp):
    n, d_ssm = u.shape
    n_seq, n_state2 = h0.shape
    names = ["bw", "cw", "abar", "dskip", "wglu", "bglu"]
    args = [u, gs, h0] + [sp[k] for k in names]
    full = lambda a: pl.BlockSpec(a.shape, lambda i: (0,) * a.ndim)
    return pl.pallas_call(
        _ssm_sample_kernel,
        grid=(1,),
        in_specs=[full(a) for a in args],
        out_specs=[pl.BlockSpec((n, d_ssm), lambda i: (0, 0)),
                   pl.BlockSpec((n_seq, n_state2), lambda i: (0, 0))],
        out_shape=[jax.ShapeDtypeStruct((n, d_ssm), BF16),
                   jax.ShapeDtypeStruct((n_seq, n_state2), F32)],
        scratch_shapes=[pltpu.VMEM((2 * SSM_LANE_CHUNK // LANES, n, LANES), F32),
                        pltpu.VMEM((2 * SSM_LANE_CHUNK // LANES, n, LANES), F32)],
        compiler_params=pltpu.CompilerParams(
            dimension_semantics=("arbitrary",), vmem_limit_bytes=VMEM_LIMIT_BYTES),
        name="ssm_sample",
    )(*args)


def _outproj_kernel(x_ref, oa_ref, os_ref, w_ref, y_ref):
    d_a = oa_ref.shape[1]
    y = x_ref[...] + jnp.dot(oa_ref[...], w_ref[:d_a, :], preferred_element_type=F32)
    y_ref[...] = y + jnp.dot(os_ref[...], w_ref[d_a:, :], preferred_element_type=F32)


def _outproj(x, oa, os_, w_bf):
    n, d_model = x.shape
    tm = min(ROW_TILE, n)
    row = lambda w: pl.BlockSpec((tm, w), lambda i: (i, 0))
    return pl.pallas_call(
        _outproj_kernel,
        grid=(n // tm,),
        in_specs=[row(d_model), row(oa.shape[1]), row(os_.shape[1]),
                  pl.BlockSpec(w_bf.shape, lambda i: (0, 0))],
        out_specs=row(d_model),
        out_shape=jax.ShapeDtypeStruct((n, d_model), F32),
        compiler_params=pltpu.CompilerParams(
            dimension_semantics=("arbitrary",), vmem_limit_bytes=VMEM_LIMIT_BYTES),
        name="outproj",
    )(x, oa, os_, w_bf)


def _ssm_params(a_re, a_im, log_dt, b_re, b_im, c_re, c_im, d_skip, w_glu, b_glu):
    n_groups, n_state = a_re.shape
    g_per_q = LANES // SSM_GROUP
    n_q = n_groups // g_per_q
    dt = jnp.exp(log_dt.astype(F32))[:, None]
    a_re = a_re.astype(F32)
    a_im = a_im.astype(F32)
    mag = jnp.exp(a_re * dt)
    abar_re = mag * jnp.cos(a_im * dt)
    abar_im = mag * jnp.sin(a_im * dt)
    nr = abar_re - 1.0
    den = a_re * a_re + a_im * a_im
    coef_re = (nr * a_re + abar_im * a_im) / den
    coef_im = (abar_im * a_re - nr * a_im) / den
    b_re = b_re.astype(F32)
    b_im = b_im.astype(F32)
    bbar_re = coef_re[..., None] * b_re - coef_im[..., None] * b_im
    bbar_im = coef_re[..., None] * b_im + coef_im[..., None] * b_re

    same_group = (np.arange(g_per_q * SSM_GROUP)[:, None] // SSM_GROUP
                  == np.arange(g_per_q * n_state)[None, :] // n_state)

    def lane_tile(a, reps):
        w = a.shape[-1]
        return jnp.matmul(a, jnp.asarray(np.tile(np.eye(w, dtype=np.float32), (1, reps))),
                          precision=lax.Precision.HIGHEST)

    def rows_in(t):
        n = t.shape[0]
        t = lane_tile(t.reshape(n, n_q, g_per_q * n_state, SSM_GROUP), g_per_q)
        t = jnp.swapaxes(t * jnp.asarray(same_group.T, F32), -1, -2)
        return jnp.swapaxes(t, 0, 1).reshape(n_q, n * LANES, g_per_q * n_state)

    def cols_out(t):
        n = t.shape[0]
        t = lane_tile(t.reshape(n, n_q, g_per_q * SSM_GROUP, n_state), g_per_q)
        t = jnp.swapaxes(t * jnp.asarray(same_group, F32), -1, -2)
        return jnp.transpose(t, (1, 2, 0, 3)).reshape(n_q, g_per_q * n_state, n * LANES)

    bw = jnp.concatenate([rows_in(bbar_re[None]), rows_in(bbar_im[None])], axis=2).astype(BF16)
    cw = jnp.concatenate([cols_out(c_re.astype(F32)[None]), cols_out(-c_im.astype(F32)[None])],
                         axis=1).astype(BF16)

    ar = abar_re.reshape(1, -1)
    ai = abar_im.reshape(1, -1)
    sp = {
        "bw": bw, "cw": cw,
        "abar": jnp.concatenate([ar, ai], axis=0),
        "dskip": d_skip.astype(F32).reshape(1, -1),
        "wglu": (0.5 * w_glu.astype(F32)).astype(BF16),
        "bglu": 0.5 * b_glu.astype(F32).reshape(1, -1),
    }

    mic = SSM_MICRO
    lr_step = a_re * dt
    th_step = a_im * dt

    def power(t):
        t = jnp.asarray(np.asarray(t, np.float32))[:, None, None]
        mag = jnp.exp(t * lr_step)
        return mag * jnp.cos(t * th_step), mag * jnp.sin(t * th_step)

    pw_r, pw_i = power(np.arange(mic + 1))
    zr, zi = power(np.arange(mic - 1, -1, -1))
    wb_r = zr[..., None] * bbar_re[None] - zi[..., None] * bbar_im[None]
    wb_i = zr[..., None] * bbar_im[None] + zi[..., None] * bbar_re[None]
    w = jnp.concatenate([rows_in(wb_r), rows_in(wb_i)], axis=2).astype(BF16)

    c_re = c_re.astype(F32)
    c_im = c_im.astype(F32)
    pr1, pi1 = pw_r[1:mic + 1][:, :, None, :], pw_i[1:mic + 1][:, :, None, :]
    v_r = c_re[None] * pr1 - c_im[None] * pi1
    v_i = c_re[None] * pi1 + c_im[None] * pr1
    v = jnp.concatenate([cols_out(v_r), cols_out(-v_i)], axis=1).astype(BF16)

    tb_r = pw_r[:mic, :, :, None] * bbar_re[None] - pw_i[:mic, :, :, None] * bbar_im[None]
    tb_i = pw_r[:mic, :, :, None] * bbar_im[None] + pw_i[:mic, :, :, None] * bbar_re[None]
    taps = (jnp.sum(c_re[None, :, None, :, :] * jnp.swapaxes(tb_r, 2, 3)[:, :, :, None, :], axis=-1)
            - jnp.sum(c_im[None, :, None, :, :] * jnp.swapaxes(tb_i, 2, 3)[:, :, :, None, :], axis=-1))
    zero = jnp.zeros_like(taps[0])
    grid = jnp.stack([jnp.stack([taps[t - s] if t >= s else zero for t in range(mic)], axis=0)
                      for s in range(mic)], axis=0)
    grid = lane_tile(grid.reshape(mic, mic, n_q, LANES, SSM_GROUP), g_per_q)
    grid = grid * jnp.asarray(same_group[:, ::n_state // SSM_GROUP], F32)
    m = jnp.transpose(grid, (2, 0, 3, 1, 4)).reshape(n_q, mic * LANES, mic * LANES).astype(BF16)

    rb = SSM_SCAN_ROWS
    k = np.arange(rb)
    in_r, in_i = power(-mic * k)
    out_r, out_i = power(mic * (k - 1))
    flat = lambda a: a.reshape(a.shape[0], -1)
    aux_r, aux_i = power(np.array([mic, mic * (rb - 1), mic * rb]))
    aux = jnp.stack([flat(aux_r), flat(aux_i)], axis=1).reshape(6, -1)
    n_rows = SSM_STEP // mic
    blk = np.arange(n_rows) // rb
    strict = (blk[:, None] == blk[None, :]) & (np.arange(n_rows)[:, None] > np.arange(n_rows)[None, :])
    sums = np.arange(n_rows // rb)[:, None] == blk[None, :]
    pad = np.zeros((-(n_rows + n_rows // rb) % 16, n_rows), bool)
    scan = jnp.asarray(np.concatenate([strict, sums, pad], axis=0).astype(np.float32), BF16)
    sp.update({"w": w, "m": m, "v": v, "aux": aux, "scan": scan,
               "pin_re": flat(in_r), "pin_im": flat(in_i), "pout_re": flat(out_r), "pout_im": flat(out_i)})
    return sp


def _state_to_lanes(h_re, h_im):
    b = h_re.shape[0]
    sc = SSM_LANE_CHUNK
    r = h_re.astype(F32).reshape(b, -1, 1, sc)
    i = h_im.astype(F32).reshape(b, -1, 1, sc)
    return jnp.concatenate([r, i], axis=2).reshape(b, -1)


def _lanes_to_state(h, n_groups, n_state):
    b = h.shape[0]
    h = h.reshape(b, -1, 2, SSM_LANE_CHUNK)
    return (h[:, :, 0, :].reshape(b, n_groups, n_state), h[:, :, 1, :].reshape(b, n_groups, n_state))


def _toeplitz(v, n):
    h = v.shape[0]
    x = jnp.broadcast_to(v[:, None, :], (h, n, 2 * n)).reshape(h, 2 * n * n)
    return x[:, :n * (2 * n - 1)].reshape(h, n, 2 * n - 1)[:, :, :n]


def _prompt_bias_blocks(fvec):
    n = LANES
    h = fvec.shape[0]
    neg = jnp.full((h, n - 1), NEG_INF, F32)
    va = jnp.concatenate([fvec[:, 0:1], neg, jnp.zeros((h, 1), F32), fvec[:, 1:n][:, ::-1]], axis=1)
    vb = jnp.concatenate([fvec[:, 1:n + 1][:, ::-1], jnp.zeros((h, n), F32)], axis=1)
    return jnp.stack([_toeplitz(va, n), _toeplitz(vb, n)], axis=1)


def _decode_bias(fvec, page, dec_seq, n_heads):
    h = fvec.shape[0]
    rows = []
    for i in range(dec_seq):
        last = fvec[:, i + 1:i + 1 + page][:, ::-1]
        new = jnp.concatenate([fvec[:, 0:i + 1][:, ::-1], jnp.full((h, page - i - 1), NEG_INF, F32)], axis=1)
        rows.append(jnp.concatenate([last, new], axis=1))
    per_head = jnp.stack(rows, axis=0)
    same = np.eye(n_heads, dtype=bool)[None, :, None, :]
    near = jnp.where(jnp.asarray(same), per_head[:, :, :, None], NEG_INF)
    near = near.reshape(dec_seq * n_heads, -1)
    mask = np.where(np.broadcast_to(same, (dec_seq, n_heads, page, n_heads)), 0.0, NEG_INF)
    mask = mask.reshape(dec_seq * n_heads, -1).astype(np.float32)
    return jnp.asarray(np.concatenate([mask, mask], axis=0)), jnp.concatenate([near, near], axis=0)


def kernel(x_prompt, x_sample, cache_k, cache_v, state_ssm_re, state_ssm_im, page_table,
           norm_g, w_in, q_norm_g, k_norm_g, lambda_q1, lambda_k1, lambda_q2, lambda_k2,
           subln_g, rel_bias, ssm_a_re, ssm_a_im, ssm_log_dt, ssm_b_re, ssm_b_im,
           ssm_c_re, ssm_c_im, ssm_d, w_glu, b_glu, w_out):
    batch, seq, d_model = x_prompt.shape
    dec_batch, dec_seq, _ = x_sample.shape
    depth, n_pool, page, n_heads, _ = cache_k.shape
    n_pages = page_table.shape[1]
    d_attn = n_heads * V_DIM
    n_groups, n_state = ssm_a_re.shape[1:]
    new_rows = dec_seq * n_heads

    buckets = _bucket_table(2 * LANES)
    far_from = int(np.max(np.nonzero(buckets < N_BUCKETS - 1)[0])) + 1
    assert far_from <= LANES and _bucket_table(seq + page * n_pages)[far_from:].min() == N_BUCKETS - 1
    assert page == LANES and dec_seq < LANES and ATTN_TQ == ATTN_TK

    rel_bias = rel_bias.astype(F32)
    fvec = (rel_bias[buckets].T - rel_bias[N_BUCKETS - 1][:, None]) * LOG2E
    fvec = jnp.where(jnp.asarray(np.arange(2 * LANES) < far_from)[None], fvec, 0.0)
    dtiles = _prompt_bias_blocks(fvec)
    head_mask, near_bias = _decode_bias(fvec, page, dec_seq, n_heads)

    group_avg = jnp.asarray(np.kron(np.eye(2 * LANES // QK_DIM), np.full((QK_DIM, QK_DIM), 1.0 / QK_DIM)), BF16)
    n_rep = d_attn // QK_DIM
    cache_k_rows = cache_k.reshape(depth * n_pool, page * n_heads, V_DIM)
    cache_v_rows = cache_v.reshape(depth * n_pool, page * n_heads, V_DIM)

    hp = x_prompt.reshape(batch * seq, d_model)
    hs = x_sample.reshape(dec_batch * dec_seq, d_model)
    kp_l, vp_l, ks_l, vs_l = [], [], [], []
    srp_l, sip_l, srs_l, sis_l = [], [], [], []
    for l in range(depth):
        lam_init = _lambda_init(l)
        lam = (jnp.exp(jnp.sum(lambda_q1[l].astype(F32) * lambda_k1[l].astype(F32)))
               - jnp.exp(jnp.sum(lambda_q2[l].astype(F32) * lambda_k2[l].astype(F32))) + lam_init)
        lam = lam.reshape(1).astype(F32)
        ng = norm_g[l].astype(F32).reshape(1, d_model)
        w_bf = w_in[l].astype(BF16)
        gq = jnp.tile(q_norm_g[l].astype(F32), n_rep).reshape(1, d_attn) * (QK_DIM ** -0.5 * LOG2E)
        gk = jnp.tile(k_norm_g[l].astype(F32), n_rep).reshape(1, d_attn)
        sg = (subln_g[l].astype(F32) * (1.0 - lam_init)).reshape(1, V_DIM)
        wo_bf = w_out[l].astype(BF16)
        sp = _ssm_params(ssm_a_re[l], ssm_a_im[l], ssm_log_dt[l], ssm_b_re[l], ssm_b_im[l],
                         ssm_c_re[l], ssm_c_im[l], ssm_d[l], w_glu[l], b_glu[l])

        q1, q2, k4, kb, v4, va, ga, u4, gs = _inproj(hp, ng, w_bf, gq, gk, group_avg, head_rows=False)
        sq1, sq2, sk4, sv4, sga, su, sgs = _inproj(hs, ng, w_bf, gq, gk, group_avg, head_rows=True)
        r3 = lambda a: a.reshape(batch, seq, a.shape[-1])
        o_a, o_dec = _attention(lam, sg, r3(q1), r3(q2), r3(kb), r3(va), dtiles, r3(ga),
                                page_table + l * n_pool, sq1, sq2, sk4, sv4, sga, head_mask, near_bias,
                                cache_k_rows, cache_v_rows, new_rows)
        hp3, hfin = _ssm_prompt_outproj(r3(hp), o_a, u4.reshape(batch, -1, LANES), r3(gs), sp, wo_bf)
        hp = hp3.reshape(batch * seq, d_model)
        kp_l.append(k4.reshape(batch, seq, n_heads, V_DIM).astype(cache_k.dtype))
        vp_l.append(v4.reshape(batch, seq, n_heads, V_DIM).astype(cache_v.dtype))
        hr_p, hi_p = _lanes_to_state(hfin.reshape(batch, -1), n_groups, n_state)
        srp_l.append(hr_p.astype(state_ssm_re.dtype))
        sip_l.append(hi_p.astype(state_ssm_im.dtype))

        h0 = _state_to_lanes(state_ssm_re[l], state_ssm_im[l])
        o_s, hfin = _ssm_sample(su, sgs, h0, sp)
        hs = _outproj(hs, o_dec.reshape(dec_batch * dec_seq, d_attn).astype(BF16), o_s, wo_bf)
        ks_l.append(sk4.reshape(dec_batch, dec_seq, n_heads, V_DIM).astype(cache_k.dtype))
        vs_l.append(sv4.reshape(dec_batch, dec_seq, n_heads, V_DIM).astype(cache_v.dtype))
        hr_s, hi_s = _lanes_to_state(hfin, n_groups, n_state)
        srs_l.append(hr_s.astype(state_ssm_re.dtype))
        sis_l.append(hi_s.astype(state_ssm_im.dtype))

    y_prompt = hp.reshape(batch, seq, d_model).astype(x_prompt.dtype)
    y_sample = hs.reshape(dec_batch, dec_seq, d_model).astype(x_sample.dtype)
    return (y_prompt, y_sample, jnp.stack(kp_l), jnp.stack(vp_l), jnp.stack(ks_l), jnp.stack(vs_l),
            jnp.stack(srp_l), jnp.stack(sip_l), jnp.stack(srs_l), jnp.stack(sis_l))
```

```python
import functools
import math

import numpy as np
import jax
import jax.numpy as jnp
from jax import lax
from jax.experimental import pallas as pl
from jax.experimental.pallas import tpu as pltpu

F32 = jnp.float32
BF16 = jnp.bfloat16

QK_DIM = 64
V_DIM = 2 * QK_DIM
N_BUCKETS = 32
MAX_DISTANCE = 128
SSM_GROUP = 16
SSM_STATE = 64
EPS = 1e-6
NEG_INF = -1e30
LOG2E = math.log2(math.e)
GELU_C0 = math.sqrt(2.0 / math.pi)
GELU_C1 = GELU_C0 * 0.044715

LANES = 128
VMEM_LIMIT_BYTES = 56 * 1024 * 1024

ROW_TILE = 512
ATTN_TQ = 512
ATTN_TK = 512
ATTN_UNROLL = 8
ATTN_NEAR_TASKS = 4
DECODE_SEQS_PER_STEP = 2
SSM_MICRO = 4
SSM_STEP = 1024
SSM_SCAN_ROWS = 32
SSM_LANE_CHUNK = 512


def _lambda_init(layer):
    return 0.8 - 0.6 * math.exp(-0.3 * layer)


def _bucket_table(n_max):
    n = np.arange(n_max)
    max_exact = N_BUCKETS // 2
    nf = np.maximum(n, 1).astype(np.float32)
    large = max_exact + (np.log(nf / np.float32(max_exact)) / np.float32(math.log(MAX_DISTANCE / max_exact))
                         * np.float32(N_BUCKETS - max_exact)).astype(np.int32)
    large = np.minimum(large, N_BUCKETS - 1)
    return np.where(n < max_exact, n, large).astype(np.int32)


def _silu(x):
    return (0.5 * x) * (1.0 + jnp.tanh(0.5 * x))


def _store_head_rows(ref, val, n_heads):
    rows = val.shape[0]
    for h in range(n_heads):
        ref[pl.ds(h, rows, stride=n_heads), :] = val[:, h * V_DIM:(h + 1) * V_DIM]


def _inproj_kernel(x_ref, ng_ref, w_ref, gq_ref, gk_ref, gavg_ref, *out_refs, head_rows):
    x = x_ref[...]
    ms = jnp.mean(x * x, axis=-1, keepdims=True)
    xb = (x * lax.rsqrt(ms + EPS) * ng_ref[...]).astype(BF16)
    d_seg = gq_ref.shape[1]
    n_heads = d_seg // V_DIM

    def seg(i):
        return jnp.dot(xb, w_ref[:, i * d_seg:(i + 1) * d_seg], preferred_element_type=F32)

    def group_norm(t, g):
        sq = (t * t).astype(BF16)
        wg = gavg_ref.shape[0]
        msq = jnp.concatenate([jnp.dot(sq[:, c:c + wg], gavg_ref[...], preferred_element_type=F32)
                               for c in range(0, d_seg, wg)], axis=1)
        return t * lax.rsqrt(msq + EPS) * g

    q = group_norm(seg(0), gq_ref[...])
    lane = lax.broadcasted_iota(jnp.int32, q.shape, 1)
    first = (lane % V_DIM) < QK_DIM
    qa = jnp.where(first, q, 0.0)
    qb = jnp.where(first, 0.0, q)
    k = group_norm(seg(1), gk_ref[...])
    v = seg(2)
    ga = seg(3)
    if head_rows:
        q1_ref, q2_ref, k4_ref, v4_ref, ga_ref, u_ref, gs_ref = out_refs
        _store_head_rows(q1_ref, qa, n_heads)
        _store_head_rows(q2_ref, qb, n_heads)
        _store_head_rows(ga_ref, ga, n_heads)
    else:
        q1_ref, q2_ref, k4_ref, kb_ref, v4_ref, va_ref, ga_ref, u_ref, gs_ref = out_refs
        q1_ref[...] = qa.astype(BF16)
        q2_ref[...] = qb.astype(BF16)
        kb_ref[...] = k.astype(BF16)
        vb = v.astype(BF16)
        ones = jnp.ones((v.shape[0], V_DIM), BF16)
        pieces = []
        for h in range(n_heads):
            pieces += [vb[:, h * V_DIM:(h + 1) * V_DIM], ones]
        va_ref[...] = jnp.concatenate(pieces, axis=1)
        ga_ref[...] = ga
    _store_head_rows(k4_ref, k, n_heads)
    _store_head_rows(v4_ref, v, n_heads)
    if head_rows:
        u_ref[...] = seg(4)
    else:
        _store_head_rows(u_ref, seg(4), d_seg // LANES)
    gs_ref[...] = seg(5)


def _inproj(x, ng, w_bf, gq, gk, gavg, head_rows):
    n, d_model = x.shape
    d_seg = gq.shape[1]
    n_heads = d_seg // V_DIM
    tm = min(ROW_TILE, n)
    full = lambda a: pl.BlockSpec(a.shape, lambda i: (0,) * a.ndim)
    wide = lambda w, dt: (jax.ShapeDtypeStruct((n, w), dt), pl.BlockSpec((tm, w), lambda i: (i, 0)))
    tall = lambda dt: (jax.ShapeDtypeStruct((n * n_heads, V_DIM), dt),
                       pl.BlockSpec((tm * n_heads, V_DIM), lambda i: (i, 0)))
    if head_rows:
        outs = [tall(F32), tall(F32), tall(F32), tall(F32), tall(F32), wide(d_seg, F32), wide(d_seg, F32)]
    else:
        outs = [wide(d_seg, BF16), wide(d_seg, BF16), tall(F32), wide(d_seg, BF16), tall(F32),
                wide(2 * d_seg, BF16), wide(d_seg, F32), tall(F32), wide(d_seg, F32)]
    return pl.pallas_call(
        functools.partial(_inproj_kernel, head_rows=head_rows),
        grid=(n // tm,),
        in_specs=[pl.BlockSpec((tm, d_model), lambda i: (i, 0)),
                  full(ng), full(w_bf), full(gq), full(gk), full(gavg)],
        out_specs=[o[1] for o in outs],
        out_shape=[o[0] for o in outs],
        compiler_params=pltpu.CompilerParams(
            dimension_semantics=("arbitrary",), vmem_limit_bytes=VMEM_LIMIT_BYTES),
        name="inproj_samples" if head_rows else "inproj_prompt",
    )(x, ng, w_bf, gq, gk, gavg)


def _diff_epilogue(o1, o2, lam, sg, ga):
    od = o1 - lam * o2
    ms = jnp.mean(od * od, axis=-1, keepdims=True)
    return od * lax.rsqrt(ms + EPS) * sg * _silu(ga)


def _attention_kernel(tasks_ref, pt_ref, lam_ref,
                      q1a_ref, q2a_ref, q1b_ref, q2b_ref, k_ref, v_ref, ab_ref, gaa_ref, gab_ref, sg_ref,
                      dq1_ref, dq2_ref, dkn_ref, dvn_ref, dga_ref, hm_ref, near_ref, ck_hbm, cv_hbm,
                      o_ref, od_ref,
                      s_scr, d_scr, m_scr, acc_scr, qs_scr, kbuf, vbuf, knew, vnew, ksem, vsem):
    tq = s_scr.shape[1] // 2
    tk = s_scr.shape[2]
    p = pl.program_id(2)
    n_steps = pl.num_programs(2)
    n_tasks = tasks_ref.shape[1]
    n_far = n_tasks - 1 - ATTN_NEAR_TASKS
    unroll = max([u for u in range(2, ATTN_UNROLL + 1, 2) if n_far % u == 0], default=0)
    step = (pl.program_id(0) * pl.num_programs(1) + pl.program_id(1)) * n_steps + p
    n_pages = kbuf.shape[1]
    page_rows = kbuf.shape[2]
    total_steps = pt_ref.shape[0] // DECODE_SEQS_PER_STEP
    lam = lam_ref[0]

    def k_copy(st, local, j):
        return pltpu.make_async_copy(ck_hbm.at[pt_ref[st * DECODE_SEQS_PER_STEP + local, j]],
                                     kbuf.at[local, j], ksem.at[0])

    def v_copy(st, local, j):
        return pltpu.make_async_copy(cv_hbm.at[pt_ref[st * DECODE_SEQS_PER_STEP + local, j]],
                                     vbuf.at[local, j], vsem.at[0])

    def start_fetch(st):
        for local in range(DECODE_SEQS_PER_STEP):
            for j in range(n_pages):
                k_copy(st, local, j).start()
                v_copy(st, local, j).start()

    def wait_fetch(st):
        for local in range(DECODE_SEQS_PER_STEP):
            for j in range(n_pages):
                k_copy(st, local, j).wait()
                v_copy(st, local, j).wait()

    def decode(local, rows):
        r0 = local * rows
        qx = jnp.concatenate([dq1_ref[r0:r0 + rows, :], dq2_ref[r0:r0 + rows, :]], axis=0).astype(BF16)
        knew[local, 0:rows, :] = dkn_ref[r0:r0 + rows, :].astype(BF16)
        vnew[local, 0:rows, :] = dvn_ref[r0:r0 + rows, :].astype(BF16)
        nt = (((1,), (1,)), ((), ()))
        s_tiles = [lax.dot_general(qx, kbuf[local, j].astype(BF16), nt, preferred_element_type=F32)
                   for j in range(n_pages)]
        s_tiles.append(lax.dot_general(qx, knew[local], nt, preferred_element_type=F32))
        head_mask = hm_ref[...]
        s = jnp.concatenate([t + head_mask for t in s_tiles[:n_pages - 1]]
                            + [jnp.concatenate(s_tiles[n_pages - 1:], axis=1) + near_ref[...]], axis=1)
        m = jnp.max(s, axis=-1, keepdims=True)
        pr = jnp.exp2(s - m)
        l_sum = jnp.sum(pr, axis=-1, keepdims=True)
        pb = pr.astype(BF16)
        acc = jnp.dot(pb[:, n_pages * page_rows:], vnew[local], preferred_element_type=F32)
        for j in range(n_pages):
            acc = acc + jnp.dot(pb[:, j * page_rows:(j + 1) * page_rows], vbuf[local, j].astype(BF16),
                                preferred_element_type=F32)
        o = acc / l_sum
        od_ref[r0:r0 + rows, :] = _diff_epilogue(o[:rows], o[rows:], lam, sg_ref[...], dga_ref[r0:r0 + rows, :])

    @pl.when(step == 0)
    def _():
        start_fetch(0)
        knew[...] = jnp.zeros(knew.shape, knew.dtype)
        vnew[...] = jnp.zeros(vnew.shape, vnew.dtype)

    @pl.when(p == 0)
    def _():
        for t in range(2):
            for a in range(tq // LANES):
                for b in range(tk // LANES):
                    delta = a - b + t * (tk // LANES)
                    if delta in (0, 1):
                        blk = ab_ref[delta]
                    else:
                        blk = jnp.full((LANES, LANES), NEG_INF if delta < 0 else 0.0, F32)
                    d_scr[t, a * LANES:(a + 1) * LANES, b * LANES:(b + 1) * LANES] = blk
        d_scr[2] = jnp.zeros((tq, tk), F32)

    m_scr[...] = jnp.full(m_scr.shape, NEG_INF, F32)
    acc_scr[...] = jnp.zeros(acc_scr.shape, F32)
    qs_scr[0] = jnp.concatenate([q1a_ref[...], q2a_ref[...]], axis=0)
    qs_scr[1] = jnp.concatenate([q1b_ref[...], q2b_ref[...]], axis=0)

    def task(kind, i):
        return tasks_ref[kind * n_steps + p, i]

    def produce(i, slot, near):
        k0 = pl.multiple_of(task(1, i) * tk, tk)
        s = lax.dot_general(qs_scr[task(3, i)], k_ref[pl.ds(k0, tk), :], (((1,), (1,)), ((), ())),
                            preferred_element_type=F32)
        if near:
            bias = d_scr[task(2, i)]
            s = s + jnp.concatenate([bias, bias], axis=0)
        s_scr[slot] = s

    def consume(i, slot):
        k0 = pl.multiple_of(task(1, i) * tk, tk)
        a = task(3, i)
        s = s_scr[slot]
        m_old = m_scr[a]
        m_new = jnp.maximum(m_old, jnp.max(s, axis=-1, keepdims=True))
        pr = jnp.exp2(s - jnp.concatenate([m_new] * (tk // LANES), axis=1))
        alpha = jnp.exp2(m_old - m_new)
        pv = jnp.dot(pr.astype(BF16), v_ref[pl.ds(k0, tk), :], preferred_element_type=F32)
        acc_scr[a] = acc_scr[a] * jnp.concatenate([alpha] * (acc_scr.shape[2] // LANES), axis=1) + pv
        m_scr[a] = m_new

    wait_fetch(step)
    for local in range(DECODE_SEQS_PER_STEP):
        decode(local, od_ref.shape[0] // DECODE_SEQS_PER_STEP)

    @pl.when(step + 1 < total_steps)
    def _():
        start_fetch(step + 1)

    produce(0, 0, True)
    for i in range(ATTN_NEAR_TASKS):
        produce(i + 1, (i + 1) % 2, i + 1 < ATTN_NEAR_TASKS)
        consume(i, i % 2)

    def body(j, carry):
        for u in range(unroll):
            i = ATTN_NEAR_TASKS + j * unroll + u
            produce(i + 1, (u + 1) % 2, False)
            consume(i, u % 2)
        return carry

    if n_far:
        lax.fori_loop(0, n_far // unroll, body, 0)
    consume(n_tasks - 1, (n_tasks - 1) % 2)

    for a, ga_ref in enumerate((gaa_ref, gab_ref)):
        q0 = pl.multiple_of(task(0, ATTN_NEAR_TASKS // 2 * a) * tq, tq)
        acc = acc_scr[a]
        o = acc[:, :V_DIM] / acc[:, V_DIM:]
        out = _diff_epilogue(o[:tq], o[tq:], lam, sg_ref[...], ga_ref[...])
        o_ref[pl.ds(q0, tq), :] = out.astype(o_ref.dtype)


def _attn_tasks(n_q):
    n_steps = n_q // 2
    tab = np.zeros((4, n_steps, n_q + 1), np.int32)
    for p in range(n_steps):
        near, far = [], []
        for acc, qt in enumerate((p, n_q - 1 - p)):
            tasks = [(qt, qt - t, min(t, 2), acc) for t in range(qt + 1)]
            near.append(tasks[:2])
            far += tasks[2:]
        while len(near[0]) < 2:
            near[0].append(far.pop(0))
        order = near[0] + near[1] + far
        assert len(order) == n_q + 1 and len(near[1]) == 2 and ATTN_NEAR_TASKS == 4
        tab[:, p, :] = np.array(order).T
    return tab.reshape(4 * n_steps, n_q + 1)


def _attention(lam, sg, q1, q2, kb, va, dtiles, ga,
               page_table, dq1, dq2, dkn, dvn, dga, head_mask, near_bias, cache_k, cache_v, new_rows):
    b, l, d_attn = q1.shape
    n_heads = d_attn // V_DIM
    tq, tk = ATTN_TQ, ATTN_TK
    n_q = l // tq
    n_steps = n_q // 2
    n_seq, n_pages = page_table.shape
    page_rows = cache_k.shape[1]
    total_steps = b * n_heads * n_steps
    assert tq == tk and n_q % 2 == 0 and n_q >= ATTN_NEAR_TASKS
    assert n_seq == total_steps * DECODE_SEQS_PER_STEP
    tasks = jnp.asarray(_attn_tasks(n_q))
    dec_rows = DECODE_SEQS_PER_STEP * new_rows

    def gstep(bi, h, pi):
        return (bi * n_heads + h) * n_steps + pi

    whole = lambda w: pl.BlockSpec((None, l, w), lambda bi, h, pi, *_: (bi, 0, h))
    tile_a = pl.BlockSpec((None, tq, V_DIM), lambda bi, h, pi, *_: (bi, pi, h))
    tile_b = pl.BlockSpec((None, tq, V_DIM), lambda bi, h, pi, *_: (bi, n_q - 1 - pi, h))
    dec = pl.BlockSpec((dec_rows, V_DIM), lambda bi, h, pi, *_: (gstep(bi, h, pi), 0))
    const2 = lambda a: pl.BlockSpec(a.shape, lambda bi, h, pi, *_: (0, 0))
    grid_spec = pltpu.PrefetchScalarGridSpec(
        num_scalar_prefetch=2,
        grid=(b, n_heads, n_steps),
        in_specs=[
            pl.BlockSpec(memory_space=pltpu.SMEM),
            tile_a, tile_a, tile_b, tile_b, whole(V_DIM), whole(2 * V_DIM),
            pl.BlockSpec((None, 2, LANES, LANES), lambda bi, h, pi, *_: (h, 0, 0, 0)),
            tile_a, tile_b, const2(sg),
            dec, dec, dec, dec, dec, const2(head_mask), const2(near_bias),
            pl.BlockSpec(memory_space=pl.ANY), pl.BlockSpec(memory_space=pl.ANY),
        ],
        out_specs=[whole(V_DIM), dec],
        scratch_shapes=[
            pltpu.VMEM((2, 2 * tq, tk), F32), pltpu.VMEM((3, tq, tk), F32),
            pltpu.VMEM((2, 2 * tq, LANES), F32), pltpu.VMEM((2, 2 * tq, 2 * V_DIM), F32),
            pltpu.VMEM((2, 2 * tq, V_DIM), BF16),
            pltpu.VMEM((DECODE_SEQS_PER_STEP, n_pages, page_rows, V_DIM), cache_k.dtype),
            pltpu.VMEM((DECODE_SEQS_PER_STEP, n_pages, page_rows, V_DIM), cache_v.dtype),
            pltpu.VMEM((DECODE_SEQS_PER_STEP, page_rows, V_DIM), BF16),
            pltpu.VMEM((DECODE_SEQS_PER_STEP, page_rows, V_DIM), BF16),
            pltpu.SemaphoreType.DMA((1,)), pltpu.SemaphoreType.DMA((1,)),
        ],
    )
    return pl.pallas_call(
        _attention_kernel,
        grid_spec=grid_spec,
        out_shape=[jax.ShapeDtypeStruct((b, l, d_attn), BF16),
                   jax.ShapeDtypeStruct((n_seq * new_rows, V_DIM), F32)],
        compiler_params=pltpu.CompilerParams(
            dimension_semantics=("arbitrary", "arbitrary", "arbitrary"),
            vmem_limit_bytes=VMEM_LIMIT_BYTES),
        name="attention",
    )(tasks, page_table, lam, q1, q2, q1, q2, kb, va, dtiles, ga, ga, sg,
      dq1, dq2, dkn, dvn, dga, head_mask, near_bias, cache_k, cache_v)


def _ssm_tail(y, u, gs, dskip_ref, wglu_half_ref, bglu_half_ref):
    x = y + dskip_ref[...] * u
    inner = x * (GELU_C1 * (x * x) + GELU_C0)
    z = (0.5 * x) * (1.0 + jnp.tanh(inner))
    t_glu = jnp.tanh(jnp.dot(z.astype(BF16), wglu_half_ref[...], preferred_element_type=F32) + bglu_half_ref[...])
    t_gate = jnp.tanh(0.5 * gs)
    return ((z * gs) * 0.25) * (1.0 + t_glu) * (1.0 + t_gate)


def _ssm_prompt_kernel(x_ref, oa_ref, u4_ref, gs_ref, w_ref, m_ref, v_ref,
                       pin_re_ref, pin_im_ref, pout_re_ref, pout_im_ref,
                       aux_ref, scan_ref, dskip_ref, wglu_ref, bglu_ref, wo_ref, o_ref, hfin_ref,
                       carry_scr, y4_scr):
    n_q = w_ref.shape[0]
    mic = SSM_MICRO
    sc = SSM_LANE_CHUNK
    ts = gs_ref.shape[0]
    r = ts // mic
    rb = pin_re_ref.shape[0]
    n_blk = r // rb
    tile_rows = lambda a: jnp.concatenate([a] * n_blk, axis=0)
    c = pl.program_id(1)

    @pl.when(c == 0)
    def _():
        carry_scr[...] = jnp.zeros(carry_scr.shape, F32)

    for q in range(n_q):
        re_l = slice(2 * q * sc, (2 * q + 1) * sc)
        im_l = slice((2 * q + 1) * sc, 2 * (q + 1) * sc)
        st_l = slice(q * sc, (q + 1) * sc)
        x = jnp.concatenate([u4_ref[pl.ds(n_q * s + q, r, stride=n_q * mic), :] for s in range(mic)],
                            axis=1).astype(BF16)
        e = jnp.dot(x, w_ref[q], preferred_element_type=F32)
        er, em = e[:, :sc], e[:, sc:]
        pw = lambda ref, i: ref[i:i + 1, st_l]
        pir, pii = tile_rows(pin_re_ref[:, st_l]), tile_rows(pin_im_ref[:, st_l])
        xs = jnp.concatenate([er * pir - em * pii, er * pii + em * pir], axis=1).astype(BF16)
        cs = jnp.dot(scan_ref[...], xs, preferred_element_type=F32)
        tot_r, tot_m = cs[r:r + n_blk, :sc], cs[r:r + n_blk, sc:]
        t_r = tot_r * pw(aux_ref, 2) - tot_m * pw(aux_ref, 3)
        t_m = tot_r * pw(aux_ref, 3) + tot_m * pw(aux_ref, 2)
        ar, am = pw(aux_ref, 0), pw(aux_ref, 1)
        br, bm = pw(aux_ref, 4), pw(aux_ref, 5)
        h_r, h_m = carry_scr[:, re_l], carry_scr[:, im_l]
        base_r, base_m = [], []
        for blk in range(n_blk):
            base_r.append(jnp.broadcast_to(ar * h_r - am * h_m, (rb, sc)))
            base_m.append(jnp.broadcast_to(ar * h_m + am * h_r, (rb, sc)))
            h_r, h_m = (br * h_r - bm * h_m + t_r[blk:blk + 1, :], br * h_m + bm * h_r + t_m[blk:blk + 1, :])
        carry_scr[:, re_l] = h_r
        carry_scr[:, im_l] = h_m
        hfin_ref[:, re_l] = h_r
        hfin_ref[:, im_l] = h_m
        sr = cs[:r, :sc] + jnp.concatenate(base_r, axis=0)
        sm = cs[:r, sc:] + jnp.concatenate(base_m, axis=0)
        por, poi = tile_rows(pout_re_ref[:, st_l]), tile_rows(pout_im_ref[:, st_l])
        hp = jnp.concatenate([sr * por - sm * poi, sr * poi + sm * por], axis=1).astype(BF16)
        y = (jnp.dot(x, m_ref[q], preferred_element_type=F32)
             + jnp.dot(hp, v_ref[q], preferred_element_type=F32))
        for s in range(mic):
            y4_scr[pl.ds(n_q * s + q, r, stride=n_q * mic), :] = y[:, s * LANES:(s + 1) * LANES]
    y = jnp.concatenate([y4_scr[pl.ds(q, ts, stride=n_q), :] for q in range(n_q)], axis=1)
    u = jnp.concatenate([u4_ref[pl.ds(q, ts, stride=n_q), :] for q in range(n_q)], axis=1)
    o_s = _ssm_tail(y, u, gs_ref[...], dskip_ref, wglu_ref, bglu_ref).astype(BF16)
    d_a = oa_ref.shape[1]
    out = x_ref[...] + jnp.dot(oa_ref[...], wo_ref[:d_a, :], preferred_element_type=F32)
    o_ref[...] = out + jnp.dot(o_s, wo_ref[d_a:, :], preferred_element_type=F32)


def _ssm_prompt_outproj(x, oa, u4, gs, sp, wo_bf):
    b, l, d_ssm = gs.shape
    n_q = d_ssm // LANES
    ts = SSM_STEP
    n_state2 = 2 * sp["aux"].shape[1]
    row = lambda w: pl.BlockSpec((None, ts, w), lambda bi, ci: (bi, ci, 0))
    row4 = pl.BlockSpec((None, ts * n_q, LANES), lambda bi, ci: (bi, ci, 0))
    full = lambda a: pl.BlockSpec(a.shape, lambda bi, ci: (0,) * a.ndim)
    names = ["w", "m", "v", "pin_re", "pin_im", "pout_re", "pout_im", "aux", "scan", "dskip", "wglu", "bglu"]
    return pl.pallas_call(
        _ssm_prompt_kernel,
        grid=(b, l // ts),
        in_specs=[row(x.shape[2]), row(oa.shape[2]), row4, row(d_ssm)] + [full(sp[n]) for n in names]
        + [full(wo_bf)],
        out_specs=[row(x.shape[2]), pl.BlockSpec((None, 1, n_state2), lambda bi, ci: (bi, 0, 0))],
        out_shape=[jax.ShapeDtypeStruct(x.shape, F32),
                   jax.ShapeDtypeStruct((b, 1, n_state2), F32)],
        scratch_shapes=[pltpu.VMEM((1, n_state2), F32), pltpu.VMEM((ts * n_q, LANES), F32)],
        compiler_params=pltpu.CompilerParams(
            dimension_semantics=("arbitrary", "arbitrary"), vmem_limit_bytes=VMEM_LIMIT_BYTES),
        name="ssm_prompt_outproj",
    )(x, oa, u4, gs, *[sp[n] for n in names], wo_bf)


def _ssm_sample_kernel(u_ref, gs_ref, h0_ref, bw_ref, cw_ref, abar_ref, dskip_ref, wglu_ref, bglu_ref,
                       o_ref, hfin_ref, bu_scr, h_scr):
    n_seq = h0_ref.shape[0]
    dec_seq = u_ref.shape[0] // n_seq
    n_q = bw_ref.shape[0]
    sc = SSM_LANE_CHUNK
    n_lc = 2 * sc // LANES
    u = u_ref[...]
    ys = []
    for q in range(n_q):
        uq = u[:, q * LANES:(q + 1) * LANES].astype(BF16)
        bu = jnp.dot(uq, bw_ref[q], preferred_element_type=F32)
        for c in range(n_lc):
            bu_scr[c] = bu[:, c * LANES:(c + 1) * LANES]
        lanes = slice(q * sc, (q + 1) * sc)
        ar, am = abar_ref[0:1, lanes], abar_ref[1:2, lanes]
        hr = h0_ref[:, 2 * q * sc:(2 * q + 1) * sc]
        hm = h0_ref[:, (2 * q + 1) * sc:2 * (q + 1) * sc]
        for step in range(dec_seq):
            rows = pl.ds(step, n_seq, stride=dec_seq)
            b_all = jnp.concatenate([bu_scr[c, rows, :] for c in range(n_lc)], axis=1)
            br, bi = b_all[:, :sc], b_all[:, sc:]
            hr, hm = ar * hr - am * hm + br, ar * hm + am * hr + bi
            for c in range(n_lc // 2):
                h_scr[c, rows, :] = hr[:, c * LANES:(c + 1) * LANES]
                h_scr[n_lc // 2 + c, rows, :] = hm[:, c * LANES:(c + 1) * LANES]
        hfin_ref[:, 2 * q * sc:(2 * q + 1) * sc] = hr
        hfin_ref[:, (2 * q + 1) * sc:2 * (q + 1) * sc] = hm
        h_all = jnp.concatenate([h_scr[c] for c in range(n_lc)], axis=1)
        ys.append(jnp.dot(h_all.astype(BF16), cw_ref[q], preferred_element_type=F32))
    y = jnp.concatenate(ys, axis=1)
    o_ref[...] = _ssm_tail(y, u, gs_ref[...], dskip_ref, wglu_ref, bglu_ref).astype(o_ref.dtype)


def _ssm_sample(u, gs, h0, sp):
    n, d_ssm = u.shape
    n_seq, n_state2 = h0.shape
    names = ["bw", "cw", "abar", "dskip", "wglu", "bglu"]
    args = [u, gs, h0] + [sp[k] for k in names]
    full = lambda a: pl.BlockSpec(a.shape, lambda i: (0,) * a.ndim)
    return pl.pallas_call(
        _ssm_sample_kernel,
        grid=(1,),
        in_specs=[full(a) for a in args],
        out_specs=[pl.BlockSpec((n, d_ssm), lambda i: (0, 0)),
                   pl.BlockSpec((n_seq, n_state2), lambda i: (0, 0))],
        out_shape=[jax.ShapeDtypeStruct((n, d_ssm), BF16),
                   jax.ShapeDtypeStruct((n_seq, n_state2), F32)],
        scratch_shapes=[pltpu.VMEM((2 * SSM_LANE_CHUNK // LANES, n, LANES), F32),
                        pltpu.VMEM((2 * SSM_LANE_CHUNK // LANES, n, LANES), F32)],
        compiler_params=pltpu.CompilerParams(
            dimension_semantics=("arbitrary",), vmem_limit_bytes=VMEM_LIMIT_BYTES),
        name="ssm_sample",
    )(*args)


def _outproj_kernel(x_ref, oa_ref, os_ref, w_ref, y_ref):
    d_a = oa_ref.shape[1]
    y = x_ref[...] + jnp.dot(oa_ref[...], w_ref[:d_a, :], preferred_element_type=F32)
    y_ref[...] = y + jnp.dot(os_ref[...], w_ref[d_a:, :], preferred_element_type=F32)


def _outproj(x, oa, os_, w_bf):
    n, d_model = x.shape
    tm = min(ROW_TILE, n)
    row = lambda w: pl.BlockSpec((tm, w), lambda i: (i, 0))
    return pl.pallas_call(
        _outproj_kernel,
        grid=(n // tm,),
        in_specs=[row(d_model), row(oa.shape[1]), row(os_.shape[1]),
                  pl.BlockSpec(w_bf.shape, lambda i: (0, 0))],
        out_specs=row(d_model),
        out_shape=jax.ShapeDtypeStruct((n, d_model), F32),
        compiler_params=pltpu.CompilerParams(
            dimension_semantics=("arbitrary",), vmem_limit_bytes=VMEM_LIMIT_BYTES),
        name="outproj",
    )(x, oa, os_, w_bf)


def _ssm_params(a_re, a_im, log_dt, b_re, b_im, c_re, c_im, d_skip, w_glu, b_glu):
    n_groups, n_state = a_re.shape
    g_per_q = LANES // SSM_GROUP
    n_q = n_groups // g_per_q
    dt = jnp.exp(log_dt.astype(F32))[:, None]
    a_re = a_re.astype(F32)
    a_im = a_im.astype(F32)
    mag = jnp.exp(a_re * dt)
    abar_re = mag * jnp.cos(a_im * dt)
    abar_im = mag * jnp.sin(a_im * dt)
    nr = abar_re - 1.0
    den = a_re * a_re + a_im * a_im
    coef_re = (nr * a_re + abar_im * a_im) / den
    coef_im = (abar_im * a_re - nr * a_im) / den
    b_re = b_re.astype(F32)
    b_im = b_im.astype(F32)
    bbar_re = coef_re[..., None] * b_re - coef_im[..., None] * b_im
    bbar_im = coef_re[..., None] * b_im + coef_im[..., None] * b_re

    same_group = (np.arange(g_per_q * SSM_GROUP)[:, None] // SSM_GROUP
                  == np.arange(g_per_q * n_state)[None, :] // n_state)

    def lane_tile(a, reps):
        w = a.shape[-1]
        return jnp.matmul(a, jnp.asarray(np.tile(np.eye(w, dtype=np.float32), (1, reps))),
                          precision=lax.Precision.HIGHEST)

    def rows_in(t):
        n = t.shape[0]
        t = lane_tile(t.reshape(n, n_q, g_per_q * n_state, SSM_GROUP), g_per_q)
        t = jnp.swapaxes(t * jnp.asarray(same_group.T, F32), -1, -2)
        return jnp.swapaxes(t, 0, 1).reshape(n_q, n * LANES, g_per_q * n_state)

    def cols_out(t):
        n = t.shape[0]
        t = lane_tile(t.reshape(n, n_q, g_per_q * SSM_GROUP, n_state), g_per_q)
        t = jnp.swapaxes(t * jnp.asarray(same_group, F32), -1, -2)
        return jnp.transpose(t, (1, 2, 0, 3)).reshape(n_q, g_per_q * n_state, n * LANES)

    bw = jnp.concatenate([rows_in(bbar_re[None]), rows_in(bbar_im[None])], axis=2).astype(BF16)
    cw = jnp.concatenate([cols_out(c_re.astype(F32)[None]), cols_out(-c_im.astype(F32)[None])],
                         axis=1).astype(BF16)

    ar = abar_re.reshape(1, -1)
    ai = abar_im.reshape(1, -1)
    sp = {
        "bw": bw, "cw": cw,
        "abar": jnp.concatenate([ar, ai], axis=0),
        "dskip": d_skip.astype(F32).reshape(1, -1),
        "wglu": (0.5 * w_glu.astype(F32)).astype(BF16),
        "bglu": 0.5 * b_glu.astype(F32).reshape(1, -1),
    }

    mic = SSM_MICRO
    lr_step = a_re * dt
    th_step = a_im * dt

    def power(t):
        t = jnp.asarray(np.asarray(t, np.float32))[:, None, None]
        mag = jnp.exp(t * lr_step)
        return mag * jnp.cos(t * th_step), mag * jnp.sin(t * th_step)

    pw_r, pw_i = power(np.arange(mic + 1))
    zr, zi = power(np.arange(mic - 1, -1, -1))
    wb_r = zr[..., None] * bbar_re[None] - zi[..., None] * bbar_im[None]
    wb_i = zr[..., None] * bbar_im[None] + zi[..., None] * bbar_re[None]
    w = jnp.concatenate([rows_in(wb_r), rows_in(wb_i)], axis=2).astype(BF16)

    c_re = c_re.astype(F32)
    c_im = c_im.astype(F32)
    pr1, pi1 = pw_r[1:mic + 1][:, :, None, :], pw_i[1:mic + 1][:, :, None, :]
    v_r = c_re[None] * pr1 - c_im[None] * pi1
    v_i = c_re[None] * pi1 + c_im[None] * pr1
    v = jnp.concatenate([cols_out(v_r), cols_out(-v_i)], axis=1).astype(BF16)

    tb_r = pw_r[:mic, :, :, None] * bbar_re[None] - pw_i[:mic, :, :, None] * bbar_im[None]
    tb_i = pw_r[:mic, :, :, None] * bbar_im[None] + pw_i[:mic, :, :, None] * bbar_re[None]
    taps = (jnp.sum(c_re[None, :, None, :, :] * jnp.swapaxes(tb_r, 2, 3)[:, :, :, None, :], axis=-1)
            - jnp.sum(c_im[None, :, None, :, :] * jnp.swapaxes(tb_i, 2, 3)[:, :, :, None, :], axis=-1))
    zero = jnp.zeros_like(taps[0])
    grid = jnp.stack([jnp.stack([taps[t - s] if t >= s else zero for t in range(mic)], axis=0)
                      for s in range(mic)], axis=0)
    grid = lane_tile(grid.reshape(mic, mic, n_q, LANES, SSM_GROUP), g_per_q)
    grid = grid * jnp.asarray(same_group[:, ::n_state // SSM_GROUP], F32)
    m = jnp.transpose(grid, (2, 0, 3, 1, 4)).reshape(n_q, mic * LANES, mic * LANES).astype(BF16)

    rb = SSM_SCAN_ROWS
    k = np.arange(rb)
    in_r, in_i = power(-mic * k)
    out_r, out_i = power(mic * (k - 1))
    flat = lambda a: a.reshape(a.shape[0], -1)
    aux_r, aux_i = power(np.array([mic, mic * (rb - 1), mic * rb]))
    aux = jnp.stack([flat(aux_r), flat(aux_i)], axis=1).reshape(6, -1)
    n_rows = SSM_STEP // mic
    blk = np.arange(n_rows) // rb
    strict = (blk[:, None] == blk[None, :]) & (np.arange(n_rows)[:, None] > np.arange(n_rows)[None, :])
    sums = np.arange(n_rows // rb)[:, None] == blk[None, :]
    pad = np.zeros((-(n_rows + n_rows // rb) % 16, n_rows), bool)
    scan = jnp.asarray(np.concatenate([strict, sums, pad], axis=0).astype(np.float32), BF16)
    sp.update({"w": w, "m": m, "v": v, "aux": aux, "scan": scan,
               "pin_re": flat(in_r), "pin_im": flat(in_i), "pout_re": flat(out_r), "pout_im": flat(out_i)})
    return sp


def _state_to_lanes(h_re, h_im):
    b = h_re.shape[0]
    sc = SSM_LANE_CHUNK
    r = h_re.astype(F32).reshape(b, -1, 1, sc)
    i = h_im.astype(F32).reshape(b, -1, 1, sc)
    return jnp.concatenate([r, i], axis=2).reshape(b, -1)


def _lanes_to_state(h, n_groups, n_state):
    b = h.shape[0]
    h = h.reshape(b, -1, 2, SSM_LANE_CHUNK)
    return (h[:, :, 0, :].reshape(b, n_groups, n_state), h[:, :, 1, :].reshape(b, n_groups, n_state))


def _toeplitz(v, n):
    h = v.shape[0]
    x = jnp.broadcast_to(v[:, None, :], (h, n, 2 * n)).reshape(h, 2 * n * n)
    return x[:, :n * (2 * n - 1)].reshape(h, n, 2 * n - 1)[:, :, :n]


def _prompt_bias_blocks(fvec):
    n = LANES
    h = fvec.shape[0]
    neg = jnp.full((h, n - 1), NEG_INF, F32)
    va = jnp.concatenate([fvec[:, 0:1], neg, jnp.zeros((h, 1), F32), fvec[:, 1:n][:, ::-1]], axis=1)
    vb = jnp.concatenate([fvec[:, 1:n + 1][:, ::-1], jnp.zeros((h, n), F32)], axis=1)
    return jnp.stack([_toeplitz(va, n), _toeplitz(vb, n)], axis=1)


def _decode_bias(fvec, page, dec_seq, n_heads):
    h = fvec.shape[0]
    rows = []
    for i in range(dec_seq):
        last = fvec[:, i + 1:i + 1 + page][:, ::-1]
        new = jnp.concatenate([fvec[:, 0:i + 1][:, ::-1], jnp.full((h, page - i - 1), NEG_INF, F32)], axis=1)
        rows.append(jnp.concatenate([last, new], axis=1))
    per_head = jnp.stack(rows, axis=0)
    same = np.eye(n_heads, dtype=bool)[None, :, None, :]
    near = jnp.where(jnp.asarray(same), per_head[:, :, :, None], NEG_INF)
    near = near.reshape(dec_seq * n_heads, -1)
    mask = np.where(np.broadcast_to(same, (dec_seq, n_heads, page, n_heads)), 0.0, NEG_INF)
    mask = mask.reshape(dec_seq * n_heads, -1).astype(np.float32)
    return jnp.asarray(np.concatenate([mask, mask], axis=0)), jnp.concatenate([near, near], axis=0)


def kernel(x_prompt, x_sample, cache_k, cache_v, state_ssm_re, state_ssm_im, page_table,
           norm_g, w_in, q_norm_g, k_norm_g, lambda_q1, lambda_k1, lambda_q2, lambda_k2,
           subln_g, rel_bias, ssm_a_re, ssm_a_im, ssm_log_dt, ssm_b_re, ssm_b_im,
           ssm_c_re, ssm_c_im, ssm_d, w_glu, b_glu, w_out):
    batch, seq, d_model = x_prompt.shape
    dec_batch, dec_seq, _ = x_sample.shape
    depth, n_pool, page, n_heads, _ = cache_k.shape
    n_pages = page_table.shape[1]
    d_attn = n_heads * V_DIM
    n_groups, n_state = ssm_a_re.shape[1:]
    new_rows = dec_seq * n_heads

    buckets = _bucket_table(2 * LANES)
    far_from = int(np.max(np.nonzero(buckets < N_BUCKETS - 1)[0])) + 1
    assert far_from <= LANES and _bucket_table(seq + page * n_pages)[far_from:].min() == N_BUCKETS - 1
    assert page == LANES and dec_seq < LANES and ATTN_TQ == ATTN_TK

    rel_bias = rel_bias.astype(F32)
    fvec = (rel_bias[buckets].T - rel_bias[N_BUCKETS - 1][:, None]) * LOG2E
    fvec = jnp.where(jnp.asarray(np.arange(2 * LANES) < far_from)[None], fvec, 0.0)
    dtiles = _prompt_bias_blocks(fvec)
    head_mask, near_bias = _decode_bias(fvec, page, dec_seq, n_heads)

    group_avg = jnp.asarray(np.kron(np.eye(2 * LANES // QK_DIM), np.full((QK_DIM, QK_DIM), 1.0 / QK_DIM)), BF16)
    n_rep = d_attn // QK_DIM
    cache_k_rows = cache_k.reshape(depth * n_pool, page * n_heads, V_DIM)
    cache_v_rows = cache_v.reshape(depth * n_pool, page * n_heads, V_DIM)

    hp = x_prompt.reshape(batch * seq, d_model)
    hs = x_sample.reshape(dec_batch * dec_seq, d_model)
    kp_l, vp_l, ks_l, vs_l = [], [], [], []
    srp_l, sip_l, srs_l, sis_l = [], [], [], []
    for l in range(depth):
        lam_init = _lambda_init(l)
        lam = (jnp.exp(jnp.sum(lambda_q1[l].astype(F32) * lambda_k1[l].astype(F32)))
               - jnp.exp(jnp.sum(lambda_q2[l].astype(F32) * lambda_k2[l].astype(F32))) + lam_init)
        lam = lam.reshape(1).astype(F32)
        ng = norm_g[l].astype(F32).reshape(1, d_model)
        w_bf = w_in[l].astype(BF16)
        gq = jnp.tile(q_norm_g[l].astype(F32), n_rep).reshape(1, d_attn) * (QK_DIM ** -0.5 * LOG2E)
        gk = jnp.tile(k_norm_g[l].astype(F32), n_rep).reshape(1, d_attn)
        sg = (subln_g[l].astype(F32) * (1.0 - lam_init)).reshape(1, V_DIM)
        wo_bf = w_out[l].astype(BF16)
        sp = _ssm_params(ssm_a_re[l], ssm_a_im[l], ssm_log_dt[l], ssm_b_re[l], ssm_b_im[l],
                         ssm_c_re[l], ssm_c_im[l], ssm_d[l], w_glu[l], b_glu[l])

        q1, q2, k4, kb, v4, va, ga, u4, gs = _inproj(hp, ng, w_bf, gq, gk, group_avg, head_rows=False)
        sq1, sq2, sk4, sv4, sga, su, sgs = _inproj(hs, ng, w_bf, gq, gk, group_avg, head_rows=True)
        r3 = lambda a: a.reshape(batch, seq, a.shape[-1])
        o_a, o_dec = _attention(lam, sg, r3(q1), r3(q2), r3(kb), r3(va), dtiles, r3(ga),
                                page_table + l * n_pool, sq1, sq2, sk4, sv4, sga, head_mask, near_bias,
                                cache_k_rows, cache_v_rows, new_rows)
        hp3, hfin = _ssm_prompt_outproj(r3(hp), o_a, u4.reshape(batch, -1, LANES), r3(gs), sp, wo_bf)
        hp = hp3.reshape(batch * seq, d_model)
        kp_l.append(k4.reshape(batch, seq, n_heads, V_DIM).astype(cache_k.dtype))
        vp_l.append(v4.reshape(batch, seq, n_heads, V_DIM).astype(cache_v.dtype))
        hr_p, hi_p = _lanes_to_state(hfin.reshape(batch, -1), n_groups, n_state)
        srp_l.append(hr_p.astype(state_ssm_re.dtype))
        sip_l.append(hi_p.astype(state_ssm_im.dtype))

        h0 = _state_to_lanes(state_ssm_re[l], state_ssm_im[l])
        o_s, hfin = _ssm_sample(su, sgs, h0, sp)
        hs = _outproj(hs, o_dec.reshape(dec_batch * dec_seq, d_attn).astype(BF16), o_s, wo_bf)
        ks_l.append(sk4.reshape(dec_batch, dec_seq, n_heads, V_DIM).astype(cache_k.dtype))
        vs_l.append(sv4.reshape(dec_batch, dec_seq, n_heads, V_DIM).astype(cache_v.dtype))
        hr_s, hi_s = _lanes_to_state(hfin, n_groups, n_state)
        srs_l.append(hr_s.astype(state_ssm_re.dtype))
        sis_l.append(hi_s.astype(state_ssm_im.dtype))

    y_prompt = hp.reshape(batch, seq, d_model).astype(x_prompt.dtype)
    y_sample = hs.reshape(dec_batch, dec_seq, d_model).astype(x_sample.dtype)
    return (y_prompt, y_sample, jnp.stack(kp_l), jnp.stack(vp_l), jnp.stack(ks_l), jnp.stack(vs_l),
            jnp.stack(srp_l), jnp.stack(sip_l), jnp.stack(srs_l), jnp.stack(sis_l))
```

```python
import functools
import math

import numpy as np
import jax
import jax.numpy as jnp
from jax import lax
from jax.experimental import pallas as pl
from jax.experimental.pallas import tpu as pltpu

F32 = jnp.float32
BF16 = jnp.bfloat16

QK_DIM = 64
V_DIM = 2 * QK_DIM
N_BUCKETS = 32
MAX_DISTANCE = 128
SSM_GROUP = 16
SSM_STATE = 64
EPS = 1e-6
NEG_INF = -1e30
LOG2E = math.log2(math.e)
GELU_C0 = math.sqrt(2.0 / math.pi)
GELU_C1 = GELU_C0 * 0.044715

LANES = 128
VMEM_LIMIT_BYTES = 56 * 1024 * 1024

ROW_TILE = 512
ATTN_TQ = 512
ATTN_TK = 512
ATTN_UNROLL = 8
ATTN_NEAR_TASKS = 4
DECODE_SEQS_PER_STEP = 2
SSM_MICRO = 4
SSM_STEP = 1024
SSM_SCAN_ROWS = 32
SSM_LANE_CHUNK = 512


def _lambda_init(layer):
    return 0.8 - 0.6 * math.exp(-0.3 * layer)


def _bucket_table(n_max):
    n = np.arange(n_max)
    max_exact = N_BUCKETS // 2
    nf = np.maximum(n, 1).astype(np.float32)
    large = max_exact + (np.log(nf / np.float32(max_exact)) / np.float32(math.log(MAX_DISTANCE / max_exact))
                         * np.float32(N_BUCKETS - max_exact)).astype(np.int32)
    large = np.minimum(large, N_BUCKETS - 1)
    return np.where(n < max_exact, n, large).astype(np.int32)


def _silu(x):
    return (0.5 * x) * (1.0 + jnp.tanh(0.5 * x))


def _store_head_rows(ref, val, n_heads):
    rows = val.shape[0]
    for h in range(n_heads):
        ref[pl.ds(h, rows, stride=n_heads), :] = val[:, h * V_DIM:(h + 1) * V_DIM]


def _inproj_kernel(x_ref, ng_ref, w_ref, gq_ref, gk_ref, gavg_ref, *out_refs, head_rows):
    x = x_ref[...]
    ms = jnp.mean(x * x, axis=-1, keepdims=True)
    xb = (x * lax.rsqrt(ms + EPS) * ng_ref[...]).astype(BF16)
    d_seg = gq_ref.shape[1]
    n_heads = d_seg // V_DIM

    def seg(i):
        return jnp.dot(xb, w_ref[:, i * d_seg:(i + 1) * d_seg], preferred_element_type=F32)

    def group_norm(t, g):
        sq = (t * t).astype(BF16)
        wg = gavg_ref.shape[0]
        msq = jnp.concatenate([jnp.dot(sq[:, c:c + wg], gavg_ref[...], preferred_element_type=F32)
                               for c in range(0, d_seg, wg)], axis=1)
        return t * lax.rsqrt(msq + EPS) * g

    q = group_norm(seg(0), gq_ref[...])
    lane = lax.broadcasted_iota(jnp.int32, q.shape, 1)
    first = (lane % V_DIM) < QK_DIM
    qa = jnp.where(first, q, 0.0)
    qb = jnp.where(first, 0.0, q)
    k = group_norm(seg(1), gk_ref[...])
    v = seg(2)
    ga = seg(3)
    if head_rows:
        q1_ref, q2_ref, k4_ref, v4_ref, ga_ref, u_ref, gs_ref = out_refs
        _store_head_rows(q1_ref, qa, n_heads)
        _store_head_rows(q2_ref, qb, n_heads)
        _store_head_rows(ga_ref, ga, n_heads)
    else:
        q1_ref, q2_ref, k4_ref, kb_ref, v4_ref, va_ref, ga_ref, u_ref, gs_ref = out_refs
        q1_ref[...] = qa.astype(BF16)
        q2_ref[...] = qb.astype(BF16)
        kb_ref[...] = k.astype(BF16)
        vb = v.astype(BF16)
        ones = jnp.ones((v.shape[0], V_DIM), BF16)
        pieces = []
        for h in range(n_heads):
            pieces += [vb[:, h * V_DIM:(h + 1) * V_DIM], ones]
        va_ref[...] = jnp.concatenate(pieces, axis=1)
        ga_ref[...] = ga
    _store_head_rows(k4_ref, k, n_heads)
    _store_head_rows(v4_ref, v, n_heads)
    if head_rows:
        u_ref[...] = seg(4)
    else:
        _store_head_rows(u_ref, seg(4), d_seg // LANES)
    gs_ref[...] = seg(5)


def _inproj(x, ng, w_bf, gq, gk, gavg, head_rows):
    n, d_model = x.shape
    d_seg = gq.shape[1]
    n_heads = d_seg // V_DIM
    tm = min(ROW_TILE, n)
    full = lambda a: pl.BlockSpec(a.shape, lambda i: (0,) * a.ndim)
    wide = lambda w, dt: (jax.ShapeDtypeStruct((n, w), dt), pl.BlockSpec((tm, w), lambda i: (i, 0)))
    tall = lambda dt: (jax.ShapeDtypeStruct((n * n_heads, V_DIM), dt),
                       pl.BlockSpec((tm * n_heads, V_DIM), lambda i: (i, 0)))
    if head_rows:
        outs = [tall(F32), tall(F32), tall(F32), tall(F32), tall(F32), wide(d_seg, F32), wide(d_seg, F32)]
    else:
        outs = [wide(d_seg, BF16), wide(d_seg, BF16), tall(F32), wide(d_seg, BF16), tall(F32),
                wide(2 * d_seg, BF16), wide(d_seg, F32), tall(F32), wide(d_seg, F32)]
    return pl.pallas_call(
        functools.partial(_inproj_kernel, head_rows=head_rows),
        grid=(n // tm,),
        in_specs=[pl.BlockSpec((tm, d_model), lambda i: (i, 0)),
                  full(ng), full(w_bf), full(gq), full(gk), full(gavg)],
        out_specs=[o[1] for o in outs],
        out_shape=[o[0] for o in outs],
        compiler_params=pltpu.CompilerParams(
            dimension_semantics=("arbitrary",), vmem_limit_bytes=VMEM_LIMIT_BYTES),
        name="inproj_samples" if head_rows else "inproj_prompt",
    )(x, ng, w_bf, gq, gk, gavg)


def _diff_epilogue(o1, o2, lam, sg, ga):
    od = o1 - lam * o2
    ms = jnp.mean(od * od, axis=-1, keepdims=True)
    return od * lax.rsqrt(ms + EPS) * sg * _silu(ga)


def _attention_kernel(tasks_ref, pt_ref, lam_ref,
                      q1a_ref, q2a_ref, q1b_ref, q2b_ref, k_ref, v_ref, ab_ref, gaa_ref, gab_ref, sg_ref,
                      dq1_ref, dq2_ref, dkn_ref, dvn_ref, dga_ref, hm_ref, near_ref, ck_hbm, cv_hbm,
                      o_ref, od_ref,
                      s_scr, d_scr, m_scr, acc_scr, qs_scr, kbuf, vbuf, knew, vnew, ksem, vsem):
    tq = s_scr.shape[1] // 2
    tk = s_scr.shape[2]
    p = pl.program_id(2)
    n_steps = pl.num_programs(2)
    n_tasks = tasks_ref.shape[1]
    n_far = n_tasks - 1 - ATTN_NEAR_TASKS
    unroll = max([u for u in range(2, ATTN_UNROLL + 1, 2) if n_far % u == 0], default=0)
    step = (pl.program_id(0) * pl.num_programs(1) + pl.program_id(1)) * n_steps + p
    n_pages = kbuf.shape[1]
    page_rows = kbuf.shape[2]
    total_steps = pt_ref.shape[0] // DECODE_SEQS_PER_STEP
    lam = lam_ref[0]

    def k_copy(st, local, j):
        return pltpu.make_async_copy(ck_hbm.at[pt_ref[st * DECODE_SEQS_PER_STEP + local, j]],
                                     kbuf.at[local, j], ksem.at[0])

    def v_copy(st, local, j):
        return pltpu.make_async_copy(cv_hbm.at[pt_ref[st * DECODE_SEQS_PER_STEP + local, j]],
                                     vbuf.at[local, j], vsem.at[0])

    def start_fetch(st):
        for local in range(DECODE_SEQS_PER_STEP):
            for j in range(n_pages):
                k_copy(st, local, j).start()
                v_copy(st, local, j).start()

    def wait_fetch(st):
        for local in range(DECODE_SEQS_PER_STEP):
            for j in range(n_pages):
                k_copy(st, local, j).wait()
                v_copy(st, local, j).wait()

    def decode(local, rows):
        r0 = local * rows
        qx = jnp.concatenate([dq1_ref[r0:r0 + rows, :], dq2_ref[r0:r0 + rows, :]], axis=0).astype(BF16)
        knew[local, 0:rows, :] = dkn_ref[r0:r0 + rows, :].astype(BF16)
        vnew[local, 0:rows, :] = dvn_ref[r0:r0 + rows, :].astype(BF16)
        nt = (((1,), (1,)), ((), ()))
        s_tiles = [lax.dot_general(qx, kbuf[local, j].astype(BF16), nt, preferred_element_type=F32)
                   for j in range(n_pages)]
        s_tiles.append(lax.dot_general(qx, knew[local], nt, preferred_element_type=F32))
        head_mask = hm_ref[...]
        s = jnp.concatenate([t + head_mask for t in s_tiles[:n_pages - 1]]
                            + [jnp.concatenate(s_tiles[n_pages - 1:], axis=1) + near_ref[...]], axis=1)
        m = jnp.max(s, axis=-1, keepdims=True)
        pr = jnp.exp2(s - m)
        l_sum = jnp.sum(pr, axis=-1, keepdims=True)
        pb = pr.astype(BF16)
        acc = jnp.dot(pb[:, n_pages * page_rows:], vnew[local], preferred_element_type=F32)
        for j in range(n_pages):
            acc = acc + jnp.dot(pb[:, j * page_rows:(j + 1) * page_rows], vbuf[local, j].astype(BF16),
                                preferred_element_type=F32)
        o = acc / l_sum
        od_ref[r0:r0 + rows, :] = _diff_epilogue(o[:rows], o[rows:], lam, sg_ref[...], dga_ref[r0:r0 + rows, :])

    @pl.when(step == 0)
    def _():
        start_fetch(0)
        knew[...] = jnp.zeros(knew.shape, knew.dtype)
        vnew[...] = jnp.zeros(vnew.shape, vnew.dtype)

    @pl.when(p == 0)
    def _():
        for t in range(2):
            for a in range(tq // LANES):
                for b in range(tk // LANES):
                    delta = a - b + t * (tk // LANES)
                    if delta in (0, 1):
                        blk = ab_ref[delta]
                    else:
                        blk = jnp.full((LANES, LANES), NEG_INF if delta < 0 else 0.0, F32)
                    d_scr[t, a * LANES:(a + 1) * LANES, b * LANES:(b + 1) * LANES] = blk
        d_scr[2] = jnp.zeros((tq, tk), F32)

    m_scr[...] = jnp.full(m_scr.shape, NEG_INF, F32)
    acc_scr[...] = jnp.zeros(acc_scr.shape, F32)
    qs_scr[0] = jnp.concatenate([q1a_ref[...], q2a_ref[...]], axis=0)
    qs_scr[1] = jnp.concatenate([q1b_ref[...], q2b_ref[...]], axis=0)

    def task(kind, i):
        return tasks_ref[kind * n_steps + p, i]

    def produce(i, slot, near):
        k0 = pl.multiple_of(task(1, i) * tk, tk)
        s = lax.dot_general(qs_scr[task(3, i)], k_ref[pl.ds(k0, tk), :], (((1,), (1,)), ((), ())),
                            preferred_element_type=F32)
        if near:
            bias = d_scr[task(2, i)]
            s = s + jnp.concatenate([bias, bias], axis=0)
        s_scr[slot] = s

    def consume(i, slot):
        k0 = pl.multiple_of(task(1, i) * tk, tk)
        a = task(3, i)
        s = s_scr[slot]
        m_old = m_scr[a]
        m_new = jnp.maximum(m_old, jnp.max(s, axis=-1, keepdims=True))
        pr = jnp.exp2(s - jnp.concatenate([m_new] * (tk // LANES), axis=1))
        alpha = jnp.exp2(m_old - m_new)
        pv = jnp.dot(pr.astype(BF16), v_ref[pl.ds(k0, tk), :], preferred_element_type=F32)
        acc_scr[a] = acc_scr[a] * jnp.concatenate([alpha] * (acc_scr.shape[2] // LANES), axis=1) + pv
        m_scr[a] = m_new

    wait_fetch(step)
    for local in range(DECODE_SEQS_PER_STEP):
        decode(local, od_ref.shape[0] // DECODE_SEQS_PER_STEP)

    @pl.when(step + 1 < total_steps)
    def _():
        start_fetch(step + 1)

    produce(0, 0, True)
    for i in range(ATTN_NEAR_TASKS):
        produce(i + 1, (i + 1) % 2, i + 1 < ATTN_NEAR_TASKS)
        consume(i, i % 2)

    def body(j, carry):
        for u in range(unroll):
            i = ATTN_NEAR_TASKS + j * unroll + u
            produce(i + 1, (u + 1) % 2, False)
            consume(i, u % 2)
        return carry

    if n_far:
        lax.fori_loop(0, n_far // unroll, body, 0)
    consume(n_tasks - 1, (n_tasks - 1) % 2)

    for a, ga_ref in enumerate((gaa_ref, gab_ref)):
        q0 = pl.multiple_of(task(0, ATTN_NEAR_TASKS // 2 * a) * tq, tq)
        acc = acc_scr[a]
        o = acc[:, :V_DIM] / acc[:, V_DIM:]
        out = _diff_epilogue(o[:tq], o[tq:], lam, sg_ref[...], ga_ref[...])
        o_ref[pl.ds(q0, tq), :] = out.astype(o_ref.dtype)


def _attn_tasks(n_q):
    n_steps = n_q // 2
    tab = np.zeros((4, n_steps, n_q + 1), np.int32)
    for p in range(n_steps):
        near, far = [], []
        for acc, qt in enumerate((p, n_q - 1 - p)):
            tasks = [(qt, qt - t, min(t, 2), acc) for t in range(qt + 1)]
            near.append(tasks[:2])
            far += tasks[2:]
        while len(near[0]) < 2:
            near[0].append(far.pop(0))
        order = near[0] + near[1] + far
        assert len(order) == n_q + 1 and len(near[1]) == 2 and ATTN_NEAR_TASKS == 4
        tab[:, p, :] = np.array(order).T
    return tab.reshape(4 * n_steps, n_q + 1)


def _attention(lam, sg, q1, q2, kb, va, dtiles, ga,
               page_table, dq1, dq2, dkn, dvn, dga, head_mask, near_bias, cache_k, cache_v, new_rows):
    b, l, d_attn = q1.shape
    n_heads = d_attn // V_DIM
    tq, tk = ATTN_TQ, ATTN_TK
    n_q = l // tq
    n_steps = n_q // 2
    n_seq, n_pages = page_table.shape
    page_rows = cache_k.shape[1]
    total_steps = b * n_heads * n_steps
    assert tq == tk and n_q % 2 == 0 and n_q >= ATTN_NEAR_TASKS
    assert n_seq == total_steps * DECODE_SEQS_PER_STEP
    tasks = jnp.asarray(_attn_tasks(n_q))
    dec_rows = DECODE_SEQS_PER_STEP * new_rows

    def gstep(bi, h, pi):
        return (bi * n_heads + h) * n_steps + pi

    whole = lambda w: pl.BlockSpec((None, l, w), lambda bi, h, pi, *_: (bi, 0, h))
    tile_a = pl.BlockSpec((None, tq, V_DIM), lambda bi, h, pi, *_: (bi, pi, h))
    tile_b = pl.BlockSpec((None, tq, V_DIM), lambda bi, h, pi, *_: (bi, n_q - 1 - pi, h))
    dec = pl.BlockSpec((dec_rows, V_DIM), lambda bi, h, pi, *_: (gstep(bi, h, pi), 0))
    const2 = lambda a: pl.BlockSpec(a.shape, lambda bi, h, pi, *_: (0, 0))
    grid_spec = pltpu.PrefetchScalarGridSpec(
        num_scalar_prefetch=2,
        grid=(b, n_heads, n_steps),
        in_specs=[
            pl.BlockSpec(memory_space=pltpu.SMEM),
            tile_a, tile_a, tile_b, tile_b, whole(V_DIM), whole(2 * V_DIM),
            pl.BlockSpec((None, 2, LANES, LANES), lambda bi, h, pi, *_: (h, 0, 0, 0)),
            tile_a, tile_b, const2(sg),
            dec, dec, dec, dec, dec, const2(head_mask), const2(near_bias),
            pl.BlockSpec(memory_space=pl.ANY), pl.BlockSpec(memory_space=pl.ANY),
        ],
        out_specs=[whole(V_DIM), dec],
        scratch_shapes=[
            pltpu.VMEM((2, 2 * tq, tk), F32), pltpu.VMEM((3, tq, tk), F32),
            pltpu.VMEM((2, 2 * tq, LANES), F32), pltpu.VMEM((2, 2 * tq, 2 * V_DIM), F32),
            pltpu.VMEM((2, 2 * tq, V_DIM), BF16),
            pltpu.VMEM((DECODE_SEQS_PER_STEP, n_pages, page_rows, V_DIM), cache_k.dtype),
            pltpu.VMEM((DECODE_SEQS_PER_STEP, n_pages, page_rows, V_DIM), cache_v.dtype),
            pltpu.VMEM((DECODE_SEQS_PER_STEP, page_rows, V_DIM), BF16),
            pltpu.VMEM((DECODE_SEQS_PER_STEP, page_rows, V_DIM), BF16),
            pltpu.SemaphoreType.DMA((1,)), pltpu.SemaphoreType.DMA((1,)),
        ],
    )
    return pl.pallas_call(
        _attention_kernel,
        grid_spec=grid_spec,
        out_shape=[jax.ShapeDtypeStruct((b, l, d_attn), BF16),
                   jax.ShapeDtypeStruct((n_seq * new_rows, V_DIM), F32)],
        compiler_params=pltpu.CompilerParams(
            dimension_semantics=("arbitrary", "arbitrary", "arbitrary"),
            vmem_limit_bytes=VMEM_LIMIT_BYTES),
        name="attention",
    )(tasks, page_table, lam, q1, q2, q1, q2, kb, va, dtiles, ga, ga, sg,
      dq1, dq2, dkn, dvn, dga, head_mask, near_bias, cache_k, cache_v)


def _ssm_tail(y, u, gs, dskip_ref, wglu_half_ref, bglu_half_ref):
    x = y + dskip_ref[...] * u
    inner = x * (GELU_C1 * (x * x) + GELU_C0)
    z = (0.5 * x) * (1.0 + jnp.tanh(inner))
    t_glu = jnp.tanh(jnp.dot(z.astype(BF16), wglu_half_ref[...], preferred_element_type=F32) + bglu_half_ref[...])
    t_gate = jnp.tanh(0.5 * gs)
    return ((z * gs) * 0.25) * (1.0 + t_glu) * (1.0 + t_gate)


def _ssm_prompt_kernel(x_ref, oa_ref, u4_ref, gs_ref, w_ref, m_ref, v_ref,
                       pin_re_ref, pin_im_ref, pout_re_ref, pout_im_ref,
                       aux_ref, scan_ref, dskip_ref, wglu_ref, bglu_ref, wo_ref, o_ref, hfin_ref,
                       carry_scr, y4_scr):
    n_q = w_ref.shape[0]
    mic = SSM_MICRO
    sc = SSM_LANE_CHUNK
    ts = gs_ref.shape[0]
    r = ts // mic
    rb = pin_re_ref.shape[0]
    n_blk = r // rb
    tile_rows = lambda a: jnp.concatenate([a] * n_blk, axis=0)
    c = pl.program_id(1)

    @pl.when(c == 0)
    def _():
        carry_scr[...] = jnp.zeros(carry_scr.shape, F32)

    for q in range(n_q):
        re_l = slice(2 * q * sc, (2 * q + 1) * sc)
        im_l = slice((2 * q + 1) * sc, 2 * (q + 1) * sc)
        st_l = slice(q * sc, (q + 1) * sc)
        x = jnp.concatenate([u4_ref[pl.ds(n_q * s + q, r, stride=n_q * mic), :] for s in range(mic)],
                            axis=1).astype(BF16)
        e = jnp.dot(x, w_ref[q], preferred_element_type=F32)
        er, em = e[:, :sc], e[:, sc:]
        pw = lambda ref, i: ref[i:i + 1, st_l]
        pir, pii = tile_rows(pin_re_ref[:, st_l]), tile_rows(pin_im_ref[:, st_l])
        xs = jnp.concatenate([er * pir - em * pii, er * pii + em * pir], axis=1).astype(BF16)
        cs = jnp.dot(scan_ref[...], xs, preferred_element_type=F32)
        tot_r, tot_m = cs[r:r + n_blk, :sc], cs[r:r + n_blk, sc:]
        t_r = tot_r * pw(aux_ref, 2) - tot_m * pw(aux_ref, 3)
        t_m = tot_r * pw(aux_ref, 3) + tot_m * pw(aux_ref, 2)
        ar, am = pw(aux_ref, 0), pw(aux_ref, 1)
        br, bm = pw(aux_ref, 4), pw(aux_ref, 5)
        h_r, h_m = carry_scr[:, re_l], carry_scr[:, im_l]
        base_r, base_m = [], []
        for blk in range(n_blk):
            base_r.append(jnp.broadcast_to(ar * h_r - am * h_m, (rb, sc)))
            base_m.append(jnp.broadcast_to(ar * h_m + am * h_r, (rb, sc)))
            h_r, h_m = (br * h_r - bm * h_m + t_r[blk:blk + 1, :], br * h_m + bm * h_r + t_m[blk:blk + 1, :])
        carry_scr[:, re_l] = h_r
        carry_scr[:, im_l] = h_m
        hfin_ref[:, re_l] = h_r
        hfin_ref[:, im_l] = h_m
        sr = cs[:r, :sc] + jnp.concatenate(base_r, axis=0)
        sm = cs[:r, sc:] + jnp.concatenate(base_m, axis=0)
        por, poi = tile_rows(pout_re_ref[:, st_l]), tile_rows(pout_im_ref[:, st_l])
        hp = jnp.concatenate([sr * por - sm * poi, sr * poi + sm * por], axis=1).astype(BF16)
        y = (jnp.dot(x, m_ref[q], preferred_element_type=F32)
             + jnp.dot(hp, v_ref[q], preferred_element_type=F32))
        for s in range(mic):
            y4_scr[pl.ds(n_q * s + q, r, stride=n_q * mic), :] = y[:, s * LANES:(s + 1) * LANES]
    y = jnp.concatenate([y4_scr[pl.ds(q, ts, stride=n_q), :] for q in range(n_q)], axis=1)
    u = jnp.concatenate([u4_ref[pl.ds(q, ts, stride=n_q), :] for q in range(n_q)], axis=1)
    o_s = _ssm_tail(y, u, gs_ref[...], dskip_ref, wglu_ref, bglu_ref).astype(BF16)
    d_a = oa_ref.shape[1]
    out = x_ref[...] + jnp.dot(oa_ref[...], wo_ref[:d_a, :], preferred_element_type=F32)
    o_ref[...] = out + jnp.dot(o_s, wo_ref[d_a:, :], preferred_element_type=F32)


def _ssm_prompt_outproj(x, oa, u4, gs, sp, wo_bf):
    b, l, d_ssm = gs.shape
    n_q = d_ssm // LANES
    ts = SSM_STEP
    n_state2 = 2 * sp["aux"].shape[1]
    row = lambda w: pl.BlockSpec((None, ts, w), lambda bi, ci: (bi, ci, 0))
    row4 = pl.BlockSpec((None, ts * n_q, LANES), lambda bi, ci: (bi, ci, 0))
    full = lambda a: pl.BlockSpec(a.shape, lambda bi, ci: (0,) * a.ndim)
    names = ["w", "m", "v", "pin_re", "pin_im", "pout_re", "pout_im", "aux", "scan", "dskip", "wglu", "bglu"]
    return pl.pallas_call(
        _ssm_prompt_kernel,
        grid=(b, l // ts),
        in_specs=[row(x.shape[2]), row(oa.shape[2]), row4, row(d_ssm)] + [full(sp[n]) for n in names]
        + [full(wo_bf)],
        out_specs=[row(x.shape[2]), pl.BlockSpec((None, 1, n_state2), lambda bi, ci: (bi, 0, 0))],
        out_shape=[jax.ShapeDtypeStruct(x.shape, F32),
                   jax.ShapeDtypeStruct((b, 1, n_state2), F32)],
        scratch_shapes=[pltpu.VMEM((1, n_state2), F32), pltpu.VMEM((ts * n_q, LANES), F32)],
        compiler_params=pltpu.CompilerParams(
            dimension_semantics=("arbitrary", "arbitrary"), vmem_limit_bytes=VMEM_LIMIT_BYTES),
        name="ssm_prompt_outproj",
    )(x, oa, u4, gs, *[sp[n] for n in names], wo_bf)


def _ssm_sample_kernel(x_ref, oa_ref, u_ref, gs_ref, h0r_ref, h0m_ref, bw_ref, cw_ref, abar_ref,
                       dskip_ref, wglu_ref, bglu_ref, wo_ref, o_ref, hr_ref, hm_ref, bu_scr, h_scr):
    n_seq = h0r_ref.shape[0]
    dec_seq = u_ref.shape[0] // n_seq
    n_q = bw_ref.shape[0]
    sc = SSM_LANE_CHUNK
    n_lc = 2 * sc // LANES
    u = u_ref[...]
    ys = []
    for q in range(n_q):
        uq = u[:, q * LANES:(q + 1) * LANES].astype(BF16)
        bu = jnp.dot(uq, bw_ref[q], preferred_element_type=F32)
        for c in range(n_lc):
            bu_scr[c] = bu[:, c * LANES:(c + 1) * LANES]
        lanes = slice(q * sc, (q + 1) * sc)
        ar, am = abar_ref[0:1, lanes], abar_ref[1:2, lanes]
        hr = h0r_ref[:, lanes]
        hm = h0m_ref[:, lanes]
        for step in range(dec_seq):
            rows = pl.ds(step, n_seq, stride=dec_seq)
            b_all = jnp.concatenate([bu_scr[c, rows, :] for c in range(n_lc)], axis=1)
            br, bi = b_all[:, :sc], b_all[:, sc:]
            hr, hm = ar * hr - am * hm + br, ar * hm + am * hr + bi
            for c in range(n_lc // 2):
                h_scr[c, rows, :] = hr[:, c * LANES:(c + 1) * LANES]
                h_scr[n_lc // 2 + c, rows, :] = hm[:, c * LANES:(c + 1) * LANES]
        hr_ref[:, lanes] = hr
        hm_ref[:, lanes] = hm
        h_all = jnp.concatenate([h_scr[c] for c in range(n_lc)], axis=1)
        ys.append(jnp.dot(h_all.astype(BF16), cw_ref[q], preferred_element_type=F32))
    y = jnp.concatenate(ys, axis=1)
    o_s = _ssm_tail(y, u, gs_ref[...], dskip_ref, wglu_ref, bglu_ref).astype(BF16)
    d_a = oa_ref.shape[1]
    out = x_ref[...] + jnp.dot(oa_ref[...].astype(BF16), wo_ref[:d_a, :], preferred_element_type=F32)
    o_ref[...] = out + jnp.dot(o_s, wo_ref[d_a:, :], preferred_element_type=F32)


def _ssm_sample_outproj(x, oa, u, gs, h0_re, h0_im, sp, wo_bf):
    n = u.shape[0]
    names = ["bw", "cw", "abar", "dskip", "wglu", "bglu"]
    args = [x, oa, u, gs, h0_re, h0_im] + [sp[k] for k in names] + [wo_bf]
    full = lambda a: pl.BlockSpec(a.shape, lambda i: (0,) * a.ndim)
    state = jax.ShapeDtypeStruct(h0_re.shape, F32)
    return pl.pallas_call(
        _ssm_sample_kernel,
        grid=(1,),
        in_specs=[full(a) for a in args],
        out_specs=[full(x), full(h0_re), full(h0_im)],
        out_shape=[jax.ShapeDtypeStruct(x.shape, F32), state, state],
        scratch_shapes=[pltpu.VMEM((2 * SSM_LANE_CHUNK // LANES, n, LANES), F32),
                        pltpu.VMEM((2 * SSM_LANE_CHUNK // LANES, n, LANES), F32)],
        compiler_params=pltpu.CompilerParams(
            dimension_semantics=("arbitrary",), vmem_limit_bytes=VMEM_LIMIT_BYTES),
        name="ssm_sample_outproj",
    )(*args)


def _ssm_params(a_re, a_im, log_dt, b_re, b_im, c_re, c_im, d_skip, w_glu, b_glu):
    n_groups, n_state = a_re.shape
    g_per_q = LANES // SSM_GROUP
    n_q = n_groups // g_per_q
    dt = jnp.exp(log_dt.astype(F32))[:, None]
    a_re = a_re.astype(F32)
    a_im = a_im.astype(F32)
    mag = jnp.exp(a_re * dt)
    abar_re = mag * jnp.cos(a_im * dt)
    abar_im = mag * jnp.sin(a_im * dt)
    nr = abar_re - 1.0
    den = a_re * a_re + a_im * a_im
    coef_re = (nr * a_re + abar_im * a_im) / den
    coef_im = (abar_im * a_re - nr * a_im) / den
    b_re = b_re.astype(F32)
    b_im = b_im.astype(F32)
    bbar_re = coef_re[..., None] * b_re - coef_im[..., None] * b_im
    bbar_im = coef_re[..., None] * b_im + coef_im[..., None] * b_re

    same_group = (np.arange(g_per_q * SSM_GROUP)[:, None] // SSM_GROUP
                  == np.arange(g_per_q * n_state)[None, :] // n_state)

    def lane_tile(a, reps):
        w = a.shape[-1]
        return jnp.matmul(a, jnp.asarray(np.tile(np.eye(w, dtype=np.float32), (1, reps))),
                          precision=lax.Precision.HIGHEST)

    def rows_in(t):
        n = t.shape[0]
        t = lane_tile(t.reshape(n, n_q, g_per_q * n_state, SSM_GROUP), g_per_q)
        t = jnp.swapaxes(t * jnp.asarray(same_group.T, F32), -1, -2)
        return jnp.swapaxes(t, 0, 1).reshape(n_q, n * LANES, g_per_q * n_state)

    def cols_out(t):
        n = t.shape[0]
        t = lane_tile(t.reshape(n, n_q, g_per_q * SSM_GROUP, n_state), g_per_q)
        t = jnp.swapaxes(t * jnp.asarray(same_group, F32), -1, -2)
        return jnp.transpose(t, (1, 2, 0, 3)).reshape(n_q, g_per_q * n_state, n * LANES)

    bw = jnp.concatenate([rows_in(bbar_re[None]), rows_in(bbar_im[None])], axis=2).astype(BF16)
    cw = jnp.concatenate([cols_out(c_re.astype(F32)[None]), cols_out(-c_im.astype(F32)[None])],
                         axis=1).astype(BF16)

    ar = abar_re.reshape(1, -1)
    ai = abar_im.reshape(1, -1)
    sp = {
        "bw": bw, "cw": cw,
        "abar": jnp.concatenate([ar, ai], axis=0),
        "dskip": d_skip.astype(F32).reshape(1, -1),
        "wglu": (0.5 * w_glu.astype(F32)).astype(BF16),
        "bglu": 0.5 * b_glu.astype(F32).reshape(1, -1),
    }

    mic = SSM_MICRO
    lr_step = a_re * dt
    th_step = a_im * dt

    def power(t):
        t = jnp.asarray(np.asarray(t, np.float32))[:, None, None]
        mag = jnp.exp(t * lr_step)
        return mag * jnp.cos(t * th_step), mag * jnp.sin(t * th_step)

    pw_r, pw_i = power(np.arange(mic + 1))
    zr, zi = power(np.arange(mic - 1, -1, -1))
    wb_r = zr[..., None] * bbar_re[None] - zi[..., None] * bbar_im[None]
    wb_i = zr[..., None] * bbar_im[None] + zi[..., None] * bbar_re[None]
    w = jnp.concatenate([rows_in(wb_r), rows_in(wb_i)], axis=2).astype(BF16)

    c_re = c_re.astype(F32)
    c_im = c_im.astype(F32)
    pr1, pi1 = pw_r[1:mic + 1][:, :, None, :], pw_i[1:mic + 1][:, :, None, :]
    v_r = c_re[None] * pr1 - c_im[None] * pi1
    v_i = c_re[None] * pi1 + c_im[None] * pr1
    v = jnp.concatenate([cols_out(v_r), cols_out(-v_i)], axis=1).astype(BF16)

    tb_r = pw_r[:mic, :, :, None] * bbar_re[None] - pw_i[:mic, :, :, None] * bbar_im[None]
    tb_i = pw_r[:mic, :, :, None] * bbar_im[None] + pw_i[:mic, :, :, None] * bbar_re[None]
    taps = (jnp.sum(c_re[None, :, None, :, :] * jnp.swapaxes(tb_r, 2, 3)[:, :, :, None, :], axis=-1)
            - jnp.sum(c_im[None, :, None, :, :] * jnp.swapaxes(tb_i, 2, 3)[:, :, :, None, :], axis=-1))
    zero = jnp.zeros_like(taps[0])
    grid = jnp.stack([jnp.stack([taps[t - s] if t >= s else zero for t in range(mic)], axis=0)
                      for s in range(mic)], axis=0)
    grid = lane_tile(grid.reshape(mic, mic, n_q, LANES, SSM_GROUP), g_per_q)
    grid = grid * jnp.asarray(same_group[:, ::n_state // SSM_GROUP], F32)
    m = jnp.transpose(grid, (2, 0, 3, 1, 4)).reshape(n_q, mic * LANES, mic * LANES).astype(BF16)

    rb = SSM_SCAN_ROWS
    k = np.arange(rb)
    in_r, in_i = power(-mic * k)
    out_r, out_i = power(mic * (k - 1))
    flat = lambda a: a.reshape(a.shape[0], -1)
    aux_r, aux_i = power(np.array([mic, mic * (rb - 1), mic * rb]))
    aux = jnp.stack([flat(aux_r), flat(aux_i)], axis=1).reshape(6, -1)
    n_rows = SSM_STEP // mic
    blk = np.arange(n_rows) // rb
    strict = (blk[:, None] == blk[None, :]) & (np.arange(n_rows)[:, None] > np.arange(n_rows)[None, :])
    sums = np.arange(n_rows // rb)[:, None] == blk[None, :]
    pad = np.zeros((-(n_rows + n_rows // rb) % 16, n_rows), bool)
    scan = jnp.asarray(np.concatenate([strict, sums, pad], axis=0).astype(np.float32), BF16)
    sp.update({"w": w, "m": m, "v": v, "aux": aux, "scan": scan,
               "pin_re": flat(in_r), "pin_im": flat(in_i), "pout_re": flat(out_r), "pout_im": flat(out_i)})
    return sp


def _lanes_to_state(h, n_groups, n_state):
    b = h.shape[0]
    h = h.reshape(b, -1, 2, SSM_LANE_CHUNK)
    return (h[:, :, 0, :].reshape(b, n_groups, n_state), h[:, :, 1, :].reshape(b, n_groups, n_state))


def _toeplitz(v, n):
    h = v.shape[0]
    x = jnp.broadcast_to(v[:, None, :], (h, n, 2 * n)).reshape(h, 2 * n * n)
    return x[:, :n * (2 * n - 1)].reshape(h, n, 2 * n - 1)[:, :, :n]


def _prompt_bias_blocks(fvec):
    n = LANES
    h = fvec.shape[0]
    neg = jnp.full((h, n - 1), NEG_INF, F32)
    va = jnp.concatenate([fvec[:, 0:1], neg, jnp.zeros((h, 1), F32), fvec[:, 1:n][:, ::-1]], axis=1)
    vb = jnp.concatenate([fvec[:, 1:n + 1][:, ::-1], jnp.zeros((h, n), F32)], axis=1)
    return jnp.stack([_toeplitz(va, n), _toeplitz(vb, n)], axis=1)


def _decode_bias(fvec, page, dec_seq, n_heads):
    h = fvec.shape[0]
    rows = []
    for i in range(dec_seq):
        last = fvec[:, i + 1:i + 1 + page][:, ::-1]
        new = jnp.concatenate([fvec[:, 0:i + 1][:, ::-1], jnp.full((h, page - i - 1), NEG_INF, F32)], axis=1)
        rows.append(jnp.concatenate([last, new], axis=1))
    per_head = jnp.stack(rows, axis=0)
    same = np.eye(n_heads, dtype=bool)[None, :, None, :]
    near = jnp.where(jnp.asarray(same), per_head[:, :, :, None], NEG_INF)
    near = near.reshape(dec_seq * n_heads, -1)
    mask = np.where(np.broadcast_to(same, (dec_seq, n_heads, page, n_heads)), 0.0, NEG_INF)
    mask = mask.reshape(dec_seq * n_heads, -1).astype(np.float32)
    return jnp.asarray(np.concatenate([mask, mask], axis=0)), jnp.concatenate([near, near], axis=0)


def kernel(x_prompt, x_sample, cache_k, cache_v, state_ssm_re, state_ssm_im, page_table,
           norm_g, w_in, q_norm_g, k_norm_g, lambda_q1, lambda_k1, lambda_q2, lambda_k2,
           subln_g, rel_bias, ssm_a_re, ssm_a_im, ssm_log_dt, ssm_b_re, ssm_b_im,
           ssm_c_re, ssm_c_im, ssm_d, w_glu, b_glu, w_out):
    batch, seq, d_model = x_prompt.shape
    dec_batch, dec_seq, _ = x_sample.shape
    depth, n_pool, page, n_heads, _ = cache_k.shape
    n_pages = page_table.shape[1]
    d_attn = n_heads * V_DIM
    n_groups, n_state = ssm_a_re.shape[1:]
    new_rows = dec_seq * n_heads

    buckets = _bucket_table(2 * LANES)
    far_from = int(np.max(np.nonzero(buckets < N_BUCKETS - 1)[0])) + 1
    assert far_from <= LANES and _bucket_table(seq + page * n_pages)[far_from:].min() == N_BUCKETS - 1
    assert page == LANES and dec_seq < LANES and ATTN_TQ == ATTN_TK

    rel_bias = rel_bias.astype(F32)
    fvec = (rel_bias[buckets].T - rel_bias[N_BUCKETS - 1][:, None]) * LOG2E
    fvec = jnp.where(jnp.asarray(np.arange(2 * LANES) < far_from)[None], fvec, 0.0)
    dtiles = _prompt_bias_blocks(fvec)
    head_mask, near_bias = _decode_bias(fvec, page, dec_seq, n_heads)

    group_avg = jnp.asarray(np.kron(np.eye(2 * LANES // QK_DIM), np.full((QK_DIM, QK_DIM), 1.0 / QK_DIM)), BF16)
    n_rep = d_attn // QK_DIM
    cache_k_rows = cache_k.reshape(depth * n_pool, page * n_heads, V_DIM)
    cache_v_rows = cache_v.reshape(depth * n_pool, page * n_heads, V_DIM)

    hp = x_prompt.reshape(batch * seq, d_model)
    hs = x_sample.reshape(dec_batch * dec_seq, d_model)
    kp_l, vp_l, ks_l, vs_l = [], [], [], []
    srp_l, sip_l, srs_l, sis_l = [], [], [], []
    for l in range(depth):
        lam_init = _lambda_init(l)
        lam = (jnp.exp(jnp.sum(lambda_q1[l].astype(F32) * lambda_k1[l].astype(F32)))
               - jnp.exp(jnp.sum(lambda_q2[l].astype(F32) * lambda_k2[l].astype(F32))) + lam_init)
        lam = lam.reshape(1).astype(F32)
        ng = norm_g[l].astype(F32).reshape(1, d_model)
        w_bf = w_in[l].astype(BF16)
        gq = jnp.tile(q_norm_g[l].astype(F32), n_rep).reshape(1, d_attn) * (QK_DIM ** -0.5 * LOG2E)
        gk = jnp.tile(k_norm_g[l].astype(F32), n_rep).reshape(1, d_attn)
        sg = (subln_g[l].astype(F32) * (1.0 - lam_init)).reshape(1, V_DIM)
        wo_bf = w_out[l].astype(BF16)
        sp = _ssm_params(ssm_a_re[l], ssm_a_im[l], ssm_log_dt[l], ssm_b_re[l], ssm_b_im[l],
                         ssm_c_re[l], ssm_c_im[l], ssm_d[l], w_glu[l], b_glu[l])

        q1, q2, k4, kb, v4, va, ga, u4, gs = _inproj(hp, ng, w_bf, gq, gk, group_avg, head_rows=False)
        sq1, sq2, sk4, sv4, sga, su, sgs = _inproj(hs, ng, w_bf, gq, gk, group_avg, head_rows=True)
        r3 = lambda a: a.reshape(batch, seq, a.shape[-1])
        o_a, o_dec = _attention(lam, sg, r3(q1), r3(q2), r3(kb), r3(va), dtiles, r3(ga),
                                page_table + l * n_pool, sq1, sq2, sk4, sv4, sga, head_mask, near_bias,
                                cache_k_rows, cache_v_rows, new_rows)
        hp3, hfin = _ssm_prompt_outproj(r3(hp), o_a, u4.reshape(batch, -1, LANES), r3(gs), sp, wo_bf)
        hp = hp3.reshape(batch * seq, d_model)
        kp_l.append(k4.reshape(batch, seq, n_heads, V_DIM).astype(cache_k.dtype))
        vp_l.append(v4.reshape(batch, seq, n_heads, V_DIM).astype(cache_v.dtype))
        hr_p, hi_p = _lanes_to_state(hfin.reshape(batch, -1), n_groups, n_state)
        srp_l.append(hr_p.astype(state_ssm_re.dtype))
        sip_l.append(hi_p.astype(state_ssm_im.dtype))

        hs, hr_s, hi_s = _ssm_sample_outproj(
            hs, o_dec.reshape(dec_batch * dec_seq, d_attn), su, sgs,
            state_ssm_re[l].astype(F32).reshape(dec_batch, -1), state_ssm_im[l].astype(F32).reshape(dec_batch, -1),
            sp, wo_bf)
        ks_l.append(sk4.reshape(dec_batch, dec_seq, n_heads, V_DIM).astype(cache_k.dtype))
        vs_l.append(sv4.reshape(dec_batch, dec_seq, n_heads, V_DIM).astype(cache_v.dtype))
        srs_l.append(hr_s.reshape(dec_batch, n_groups, n_state).astype(state_ssm_re.dtype))
        sis_l.append(hi_s.reshape(dec_batch, n_groups, n_state).astype(state_ssm_im.dtype))

    y_prompt = hp.reshape(batch, seq, d_model).astype(x_prompt.dtype)
    y_sample = hs.reshape(dec_batch, dec_seq, d_model).astype(x_sample.dtype)
    return (y_prompt, y_sample, jnp.stack(kp_l), jnp.stack(vp_l), jnp.stack(ks_l), jnp.stack(vs_l),
            jnp.stack(srp_l), jnp.stack(sip_l), jnp.stack(srs_l), jnp.stack(sis_l))
```

```python
import functools
import math

import numpy as np
import jax
import jax.numpy as jnp
from jax import lax
from jax.experimental import pallas as pl
from jax.experimental.pallas import tpu as pltpu

F32 = jnp.float32
BF16 = jnp.bfloat16

QK_DIM = 64
V_DIM = 2 * QK_DIM
N_BUCKETS = 32
MAX_DISTANCE = 128
SSM_GROUP = 16
SSM_STATE = 64
EPS = 1e-6
NEG_INF = -1e30
LOG2E = math.log2(math.e)
GELU_C0 = math.sqrt(2.0 / math.pi)
GELU_C1 = GELU_C0 * 0.044715

LANES = 128
VMEM_LIMIT_BYTES = 56 * 1024 * 1024

ROW_TILE = 512
ATTN_TQ = 512
ATTN_TK = 512
ATTN_UNROLL = 8
ATTN_NEAR_TASKS = 4
DECODE_SEQS_PER_STEP = 2
SSM_MICRO = 4
SSM_STEP = 1024
SSM_SCAN_ROWS = 32
SSM_LANE_CHUNK = 512


def _lambda_init(layer):
    return 0.8 - 0.6 * math.exp(-0.3 * layer)


def _bucket_table(n_max):
    n = np.arange(n_max)
    max_exact = N_BUCKETS // 2
    nf = np.maximum(n, 1).astype(np.float32)
    large = max_exact + (np.log(nf / np.float32(max_exact)) / np.float32(math.log(MAX_DISTANCE / max_exact))
                         * np.float32(N_BUCKETS - max_exact)).astype(np.int32)
    large = np.minimum(large, N_BUCKETS - 1)
    return np.where(n < max_exact, n, large).astype(np.int32)


def _silu(x):
    return (0.5 * x) * (1.0 + jnp.tanh(0.5 * x))


def _store_head_rows(ref, val, n_heads):
    rows = val.shape[0]
    for h in range(n_heads):
        ref[pl.ds(h, rows, stride=n_heads), :] = val[:, h * V_DIM:(h + 1) * V_DIM]


def _inproj_kernel(x_ref, ng_ref, w_ref, gq_ref, gk_ref, gavg_ref, *out_refs, head_rows):
    x = x_ref[...]
    ms = jnp.mean(x * x, axis=-1, keepdims=True)
    xb = (x * lax.rsqrt(ms + EPS) * ng_ref[...]).astype(BF16)
    d_seg = gq_ref.shape[1]
    n_heads = d_seg // V_DIM

    def seg(i):
        return jnp.dot(xb, w_ref[:, i * d_seg:(i + 1) * d_seg], preferred_element_type=F32)

    def group_norm(t, g):
        sq = (t * t).astype(BF16)
        wg = gavg_ref.shape[0]
        msq = jnp.concatenate([jnp.dot(sq[:, c:c + wg], gavg_ref[...], preferred_element_type=F32)
                               for c in range(0, d_seg, wg)], axis=1)
        return t * lax.rsqrt(msq + EPS) * g

    q = group_norm(seg(0), gq_ref[...])
    lane = lax.broadcasted_iota(jnp.int32, q.shape, 1)
    first = (lane % V_DIM) < QK_DIM
    qa = jnp.where(first, q, 0.0)
    qb = jnp.where(first, 0.0, q)
    k = group_norm(seg(1), gk_ref[...])
    v = seg(2)
    ga = seg(3)
    if head_rows:
        q1_ref, q2_ref, k4_ref, v4_ref, ga_ref, u_ref, gs_ref = out_refs
        _store_head_rows(q1_ref, qa, n_heads)
        _store_head_rows(q2_ref, qb, n_heads)
        _store_head_rows(ga_ref, ga, n_heads)
    else:
        q1_ref, q2_ref, k4_ref, kb_ref, v4_ref, va_ref, ga_ref, u_ref, gs_ref = out_refs
        q1_ref[...] = qa.astype(BF16)
        q2_ref[...] = qb.astype(BF16)
        kb_ref[...] = k.astype(BF16)
        vb = v.astype(BF16)
        ones = jnp.ones((v.shape[0], V_DIM), BF16)
        pieces = []
        for h in range(n_heads):
            pieces += [vb[:, h * V_DIM:(h + 1) * V_DIM], ones]
        va_ref[...] = jnp.concatenate(pieces, axis=1)
        ga_ref[...] = ga
    _store_head_rows(k4_ref, k, n_heads)
    _store_head_rows(v4_ref, v, n_heads)
    if head_rows:
        u_ref[...] = seg(4)
    else:
        _store_head_rows(u_ref, seg(4), d_seg // LANES)
    gs_ref[...] = seg(5)


def _inproj(x, ng, w_bf, gq, gk, gavg, head_rows):
    n, d_model = x.shape
    d_seg = gq.shape[1]
    n_heads = d_seg // V_DIM
    tm = min(ROW_TILE, n)
    full = lambda a: pl.BlockSpec(a.shape, lambda i: (0,) * a.ndim)
    wide = lambda w, dt: (jax.ShapeDtypeStruct((n, w), dt), pl.BlockSpec((tm, w), lambda i: (i, 0)))
    tall = lambda dt: (jax.ShapeDtypeStruct((n * n_heads, V_DIM), dt),
                       pl.BlockSpec((tm * n_heads, V_DIM), lambda i: (i, 0)))
    if head_rows:
        outs = [tall(F32), tall(F32), tall(F32), tall(F32), tall(F32), wide(d_seg, F32), wide(d_seg, F32)]
    else:
        outs = [wide(d_seg, BF16), wide(d_seg, BF16), tall(F32), wide(d_seg, BF16), tall(F32),
                wide(2 * d_seg, BF16), wide(d_seg, F32), tall(F32), wide(d_seg, F32)]
    return pl.pallas_call(
        functools.partial(_inproj_kernel, head_rows=head_rows),
        grid=(n // tm,),
        in_specs=[pl.BlockSpec((tm, d_model), lambda i: (i, 0)),
                  full(ng), full(w_bf), full(gq), full(gk), full(gavg)],
        out_specs=[o[1] for o in outs],
        out_shape=[o[0] for o in outs],
        compiler_params=pltpu.CompilerParams(
            dimension_semantics=("arbitrary",), vmem_limit_bytes=VMEM_LIMIT_BYTES),
        name="inproj_samples" if head_rows else "inproj_prompt",
    )(x, ng, w_bf, gq, gk, gavg)


def _diff_epilogue(o1, o2, lam, sg, ga):
    od = o1 - lam * o2
    ms = jnp.mean(od * od, axis=-1, keepdims=True)
    return od * lax.rsqrt(ms + EPS) * sg * _silu(ga)


def _attention_kernel(tasks_ref, pt_ref, lam_ref,
                      q1a_ref, q2a_ref, q1b_ref, q2b_ref, k_ref, v_ref, ab_ref, gaa_ref, gab_ref, sg_ref,
                      dq1_ref, dq2_ref, dkn_ref, dvn_ref, dga_ref, hm_ref, near_ref, ck_hbm, cv_hbm,
                      o_ref, od_ref,
                      s_scr, d_scr, m_scr, acc_scr, qs_scr, kbuf, vbuf, knew, vnew, ksem, vsem):
    tq = s_scr.shape[1] // 2
    tk = s_scr.shape[2]
    p = pl.program_id(2)
    n_steps = pl.num_programs(2)
    n_tasks = tasks_ref.shape[1]
    n_far = n_tasks - 1 - ATTN_NEAR_TASKS
    unroll = max([u for u in range(2, ATTN_UNROLL + 1, 2) if n_far % u == 0], default=0)
    step = (pl.program_id(0) * pl.num_programs(1) + pl.program_id(1)) * n_steps + p
    n_pages = kbuf.shape[1]
    page_rows = kbuf.shape[2]
    total_steps = pt_ref.shape[0] // DECODE_SEQS_PER_STEP
    lam = lam_ref[0]

    def k_copy(st, local, j):
        return pltpu.make_async_copy(ck_hbm.at[pt_ref[st * DECODE_SEQS_PER_STEP + local, j]],
                                     kbuf.at[local, j], ksem.at[0])

    def v_copy(st, local, j):
        return pltpu.make_async_copy(cv_hbm.at[pt_ref[st * DECODE_SEQS_PER_STEP + local, j]],
                                     vbuf.at[local, j], vsem.at[0])

    def start_fetch(st):
        for local in range(DECODE_SEQS_PER_STEP):
            for j in range(n_pages):
                k_copy(st, local, j).start()
                v_copy(st, local, j).start()

    def wait_fetch(st):
        for local in range(DECODE_SEQS_PER_STEP):
            for j in range(n_pages):
                k_copy(st, local, j).wait()
                v_copy(st, local, j).wait()

    def decode(local, rows):
        r0 = local * rows
        qx = jnp.concatenate([dq1_ref[r0:r0 + rows, :], dq2_ref[r0:r0 + rows, :]], axis=0).astype(BF16)
        knew[local, 0:rows, :] = dkn_ref[r0:r0 + rows, :].astype(BF16)
        vnew[local, 0:rows, :] = dvn_ref[r0:r0 + rows, :].astype(BF16)
        nt = (((1,), (1,)), ((), ()))
        s_tiles = [lax.dot_general(qx, kbuf[local, j].astype(BF16), nt, preferred_element_type=F32)
                   for j in range(n_pages)]
        s_tiles.append(lax.dot_general(qx, knew[local], nt, preferred_element_type=F32))
        head_mask = hm_ref[...]
        s = jnp.concatenate([t + head_mask for t in s_tiles[:n_pages - 1]]
                            + [jnp.concatenate(s_tiles[n_pages - 1:], axis=1) + near_ref[...]], axis=1)
        m = jnp.max(s, axis=-1, keepdims=True)
        pr = jnp.exp2(s - m)
        l_sum = jnp.sum(pr, axis=-1, keepdims=True)
        pb = pr.astype(BF16)
        acc = jnp.dot(pb[:, n_pages * page_rows:], vnew[local], preferred_element_type=F32)
        for j in range(n_pages):
            acc = acc + jnp.dot(pb[:, j * page_rows:(j + 1) * page_rows], vbuf[local, j].astype(BF16),
                                preferred_element_type=F32)
        o = acc / l_sum
        od_ref[r0:r0 + rows, :] = _diff_epilogue(o[:rows], o[rows:], lam, sg_ref[...], dga_ref[r0:r0 + rows, :])

    @pl.when(step == 0)
    def _():
        start_fetch(0)
        knew[...] = jnp.zeros(knew.shape, knew.dtype)
        vnew[...] = jnp.zeros(vnew.shape, vnew.dtype)

    @pl.when(p == 0)
    def _():
        for t in range(2):
            for a in range(tq // LANES):
                for b in range(tk // LANES):
                    delta = a - b + t * (tk // LANES)
                    if delta in (0, 1):
                        blk = ab_ref[delta]
                    else:
                        blk = jnp.full((LANES, LANES), NEG_INF if delta < 0 else 0.0, F32)
                    d_scr[t, a * LANES:(a + 1) * LANES, b * LANES:(b + 1) * LANES] = blk
        d_scr[2] = jnp.zeros((tq, tk), F32)

    m_scr[...] = jnp.full(m_scr.shape, NEG_INF, F32)
    acc_scr[...] = jnp.zeros(acc_scr.shape, F32)
    qs_scr[0] = jnp.concatenate([q1a_ref[...], q2a_ref[...]], axis=0)
    qs_scr[1] = jnp.concatenate([q1b_ref[...], q2b_ref[...]], axis=0)

    def task(kind, i):
        return tasks_ref[kind * n_steps + p, i]

    def produce(i, slot, near):
        k0 = pl.multiple_of(task(1, i) * tk, tk)
        s = lax.dot_general(qs_scr[task(3, i)], k_ref[pl.ds(k0, tk), :], (((1,), (1,)), ((), ())),
                            preferred_element_type=F32)
        if near:
            bias = d_scr[task(2, i)]
            s = s + jnp.concatenate([bias, bias], axis=0)
        s_scr[slot] = s

    def consume(i, slot):
        k0 = pl.multiple_of(task(1, i) * tk, tk)
        a = task(3, i)
        s = s_scr[slot]
        m_old = m_scr[a]
        m_new = jnp.maximum(m_old, jnp.max(s, axis=-1, keepdims=True))
        pr = jnp.exp2(s - jnp.concatenate([m_new] * (tk // LANES), axis=1))
        alpha = jnp.exp2(m_old - m_new)
        pv = jnp.dot(pr.astype(BF16), v_ref[pl.ds(k0, tk), :], preferred_element_type=F32)
        acc_scr[a] = acc_scr[a] * jnp.concatenate([alpha] * (acc_scr.shape[2] // LANES), axis=1) + pv
        m_scr[a] = m_new

    wait_fetch(step)
    for local in range(DECODE_SEQS_PER_STEP):
        decode(local, od_ref.shape[0] // DECODE_SEQS_PER_STEP)
    produce(0, 0, True)
    for i in range(ATTN_NEAR_TASKS):
        produce(i + 1, (i + 1) % 2, i + 1 < ATTN_NEAR_TASKS)
        consume(i, i % 2)

    @pl.when(step + 1 < total_steps)
    def _():
        start_fetch(step + 1)

    def body(j, carry):
        for u in range(unroll):
            i = ATTN_NEAR_TASKS + j * unroll + u
            produce(i + 1, (u + 1) % 2, False)
            consume(i, u % 2)
        return carry

    if n_far:
        lax.fori_loop(0, n_far // unroll, body, 0)
    consume(n_tasks - 1, (n_tasks - 1) % 2)

    for a, ga_ref in enumerate((gaa_ref, gab_ref)):
        q0 = pl.multiple_of(task(0, ATTN_NEAR_TASKS // 2 * a) * tq, tq)
        acc = acc_scr[a]
        o = acc[:, :V_DIM] / acc[:, V_DIM:]
        out = _diff_epilogue(o[:tq], o[tq:], lam, sg_ref[...], ga_ref[...])
        o_ref[pl.ds(q0, tq), :] = out.astype(o_ref.dtype)


def _attn_tasks(n_q):
    n_steps = n_q // 2
    tab = np.zeros((4, n_steps, n_q + 1), np.int32)
    for p in range(n_steps):
        near, far = [], []
        for acc, qt in enumerate((p, n_q - 1 - p)):
            tasks = [(qt, qt - t, min(t, 2), acc) for t in range(qt + 1)]
            near.append(tasks[:2])
            far += tasks[2:]
        while len(near[0]) < 2:
            near[0].append(far.pop(0))
        order = near[0] + near[1] + far
        assert len(order) == n_q + 1 and len(near[1]) == 2 and ATTN_NEAR_TASKS == 4
        tab[:, p, :] = np.array(order).T
    return tab.reshape(4 * n_steps, n_q + 1)


def _attention(lam, sg, q1, q2, kb, va, dtiles, ga,
               page_table, dq1, dq2, dkn, dvn, dga, head_mask, near_bias, cache_k, cache_v, new_rows):
    b, l, d_attn = q1.shape
    n_heads = d_attn // V_DIM
    tq, tk = ATTN_TQ, ATTN_TK
    n_q = l // tq
    n_steps = n_q // 2
    n_seq, n_pages = page_table.shape
    page_rows = cache_k.shape[1]
    total_steps = b * n_heads * n_steps
    assert tq == tk and n_q % 2 == 0 and n_q >= ATTN_NEAR_TASKS
    assert n_seq == total_steps * DECODE_SEQS_PER_STEP
    tasks = jnp.asarray(_attn_tasks(n_q))
    dec_rows = DECODE_SEQS_PER_STEP * new_rows

    def gstep(bi, h, pi):
        return (bi * n_heads + h) * n_steps + pi

    whole = lambda w: pl.BlockSpec((None, l, w), lambda bi, h, pi, *_: (bi, 0, h))
    tile_a = pl.BlockSpec((None, tq, V_DIM), lambda bi, h, pi, *_: (bi, pi, h))
    tile_b = pl.BlockSpec((None, tq, V_DIM), lambda bi, h, pi, *_: (bi, n_q - 1 - pi, h))
    dec = pl.BlockSpec((dec_rows, V_DIM), lambda bi, h, pi, *_: (gstep(bi, h, pi), 0))
    const2 = lambda a: pl.BlockSpec(a.shape, lambda bi, h, pi, *_: (0, 0))
    grid_spec = pltpu.PrefetchScalarGridSpec(
        num_scalar_prefetch=2,
        grid=(b, n_heads, n_steps),
        in_specs=[
            pl.BlockSpec(memory_space=pltpu.SMEM),
            tile_a, tile_a, tile_b, tile_b, whole(V_DIM), whole(2 * V_DIM),
            pl.BlockSpec((None, 2, LANES, LANES), lambda bi, h, pi, *_: (h, 0, 0, 0)),
            tile_a, tile_b, const2(sg),
            dec, dec, dec, dec, dec, const2(head_mask), const2(near_bias),
            pl.BlockSpec(memory_space=pl.ANY), pl.BlockSpec(memory_space=pl.ANY),
        ],
        out_specs=[whole(V_DIM), dec],
        scratch_shapes=[
            pltpu.VMEM((2, 2 * tq, tk), F32), pltpu.VMEM((3, tq, tk), F32),
            pltpu.VMEM((2, 2 * tq, LANES), F32), pltpu.VMEM((2, 2 * tq, 2 * V_DIM), F32),
            pltpu.VMEM((2, 2 * tq, V_DIM), BF16),
            pltpu.VMEM((DECODE_SEQS_PER_STEP, n_pages, page_rows, V_DIM), cache_k.dtype),
            pltpu.VMEM((DECODE_SEQS_PER_STEP, n_pages, page_rows, V_DIM), cache_v.dtype),
            pltpu.VMEM((DECODE_SEQS_PER_STEP, page_rows, V_DIM), BF16),
            pltpu.VMEM((DECODE_SEQS_PER_STEP, page_rows, V_DIM), BF16),
            pltpu.SemaphoreType.DMA((1,)), pltpu.SemaphoreType.DMA((1,)),
        ],
    )
    return pl.pallas_call(
        _attention_kernel,
        grid_spec=grid_spec,
        out_shape=[jax.ShapeDtypeStruct((b, l, d_attn), BF16),
                   jax.ShapeDtypeStruct((n_seq * new_rows, V_DIM), F32)],
        compiler_params=pltpu.CompilerParams(
            dimension_semantics=("arbitrary", "arbitrary", "arbitrary"),
            vmem_limit_bytes=VMEM_LIMIT_BYTES),
        name="attention",
    )(tasks, page_table, lam, q1, q2, q1, q2, kb, va, dtiles, ga, ga, sg,
      dq1, dq2, dkn, dvn, dga, head_mask, near_bias, cache_k, cache_v)


def _ssm_tail(y, u, gs, dskip_ref, wglu_half_ref, bglu_half_ref):
    x = y + dskip_ref[...] * u
    inner = x * (GELU_C1 * (x * x) + GELU_C0)
    z = (0.5 * x) * (1.0 + jnp.tanh(inner))
    t_glu = jnp.tanh(jnp.dot(z.astype(BF16), wglu_half_ref[...], preferred_element_type=F32) + bglu_half_ref[...])
    t_gate = jnp.tanh(0.5 * gs)
    return ((z * gs) * 0.25) * (1.0 + t_glu) * (1.0 + t_gate)


def _ssm_prompt_kernel(x_ref, oa_ref, u4_ref, gs_ref, w_ref, m_ref, v_ref,
                       pin_re_ref, pin_im_ref, pout_re_ref, pout_im_ref,
                       aux_ref, scan_ref, dskip_ref, wglu_ref, bglu_ref, wo_ref, o_ref, hfin_ref,
                       carry_scr, y4_scr):
    n_q = w_ref.shape[0]
    mic = SSM_MICRO
    sc = SSM_LANE_CHUNK
    ts = gs_ref.shape[0]
    r = ts // mic
    rb = pin_re_ref.shape[0]
    n_blk = r // rb
    tile_rows = lambda a: jnp.concatenate([a] * n_blk, axis=0)
    c = pl.program_id(1)

    @pl.when(c == 0)
    def _():
        carry_scr[...] = jnp.zeros(carry_scr.shape, F32)

    for q in range(n_q):
        re_l = slice(2 * q * sc, (2 * q + 1) * sc)
        im_l = slice((2 * q + 1) * sc, 2 * (q + 1) * sc)
        st_l = slice(q * sc, (q + 1) * sc)
        x = jnp.concatenate([u4_ref[pl.ds(n_q * s + q, r, stride=n_q * mic), :] for s in range(mic)],
                            axis=1).astype(BF16)
        e = jnp.dot(x, w_ref[q], preferred_element_type=F32)
        er, em = e[:, :sc], e[:, sc:]
        pw = lambda ref, i: ref[i:i + 1, st_l]
        pir, pii = tile_rows(pin_re_ref[:, st_l]), tile_rows(pin_im_ref[:, st_l])
        xs = jnp.concatenate([er * pir - em * pii, er * pii + em * pir], axis=1).astype(BF16)
        cs = jnp.dot(scan_ref[...], xs, preferred_element_type=F32)
        tot_r, tot_m = cs[r:r + n_blk, :sc], cs[r:r + n_blk, sc:]
        t_r = tot_r * pw(aux_ref, 2) - tot_m * pw(aux_ref, 3)
        t_m = tot_r * pw(aux_ref, 3) + tot_m * pw(aux_ref, 2)
        ar, am = pw(aux_ref, 0), pw(aux_ref, 1)
        br, bm = pw(aux_ref, 4), pw(aux_ref, 5)
        h_r, h_m = carry_scr[:, re_l], carry_scr[:, im_l]
        base_r, base_m = [], []
        for blk in range(n_blk):
            base_r.append(jnp.broadcast_to(ar * h_r - am * h_m, (rb, sc)))
            base_m.append(jnp.broadcast_to(ar * h_m + am * h_r, (rb, sc)))
            h_r, h_m = (br * h_r - bm * h_m + t_r[blk:blk + 1, :], br * h_m + bm * h_r + t_m[blk:blk + 1, :])
        carry_scr[:, re_l] = h_r
        carry_scr[:, im_l] = h_m
        hfin_ref[:, re_l] = h_r
        hfin_ref[:, im_l] = h_m
        sr = cs[:r, :sc] + jnp.concatenate(base_r, axis=0)
        sm = cs[:r, sc:] + jnp.concatenate(base_m, axis=0)
        por, poi = tile_rows(pout_re_ref[:, st_l]), tile_rows(pout_im_ref[:, st_l])
        hp = jnp.concatenate([sr * por - sm * poi, sr * poi + sm * por], axis=1).astype(BF16)
        y = (jnp.dot(x, m_ref[q], preferred_element_type=F32)
             + jnp.dot(hp, v_ref[q], preferred_element_type=F32))
        for s in range(mic):
            y4_scr[pl.ds(n_q * s + q, r, stride=n_q * mic), :] = y[:, s * LANES:(s + 1) * LANES]
    y = jnp.concatenate([y4_scr[pl.ds(q, ts, stride=n_q), :] for q in range(n_q)], axis=1)
    u = jnp.concatenate([u4_ref[pl.ds(q, ts, stride=n_q), :] for q in range(n_q)], axis=1)
    o_s = _ssm_tail(y, u, gs_ref[...], dskip_ref, wglu_ref, bglu_ref).astype(BF16)
    d_a = oa_ref.shape[1]
    out = x_ref[...] + jnp.dot(oa_ref[...], wo_ref[:d_a, :], preferred_element_type=F32)
    o_ref[...] = out + jnp.dot(o_s, wo_ref[d_a:, :], preferred_element_type=F32)


def _ssm_prompt_outproj(x, oa, u4, gs, sp, wo_bf):
    b, l, d_ssm = gs.shape
    n_q = d_ssm // LANES
    ts = SSM_STEP
    n_state2 = 2 * sp["aux"].shape[1]
    row = lambda w: pl.BlockSpec((None, ts, w), lambda bi, ci: (bi, ci, 0))
    row4 = pl.BlockSpec((None, ts * n_q, LANES), lambda bi, ci: (bi, ci, 0))
    full = lambda a: pl.BlockSpec(a.shape, lambda bi, ci: (0,) * a.ndim)
    names = ["w", "m", "v", "pin_re", "pin_im", "pout_re", "pout_im", "aux", "scan", "dskip", "wglu", "bglu"]
    return pl.pallas_call(
        _ssm_prompt_kernel,
        grid=(b, l // ts),
        in_specs=[row(x.shape[2]), row(oa.shape[2]), row4, row(d_ssm)] + [full(sp[n]) for n in names]
        + [full(wo_bf)],
        out_specs=[row(x.shape[2]), pl.BlockSpec((None, 1, n_state2), lambda bi, ci: (bi, 0, 0))],
        out_shape=[jax.ShapeDtypeStruct(x.shape, F32),
                   jax.ShapeDtypeStruct((b, 1, n_state2), F32)],
        scratch_shapes=[pltpu.VMEM((1, n_state2), F32), pltpu.VMEM((ts * n_q, LANES), F32)],
        compiler_params=pltpu.CompilerParams(
            dimension_semantics=("arbitrary", "arbitrary"), vmem_limit_bytes=VMEM_LIMIT_BYTES),
        name="ssm_prompt_outproj",
    )(x, oa, u4, gs, *[sp[n] for n in names], wo_bf)


def _ssm_sample_kernel(x_ref, oa_ref, u_ref, gs_ref, h0r_ref, h0m_ref, bw_ref, cw_ref, abar_ref,
                       dskip_ref, wglu_ref, bglu_ref, wo_ref, o_ref, hr_ref, hm_ref, bu_scr, h_scr):
    n_seq = h0r_ref.shape[0]
    dec_seq = u_ref.shape[0] // n_seq
    n_q = bw_ref.shape[0]
    sc = SSM_LANE_CHUNK
    n_lc = 2 * sc // LANES
    u = u_ref[...]
    ys = []
    for q in range(n_q):
        uq = u[:, q * LANES:(q + 1) * LANES].astype(BF16)
        bu = jnp.dot(uq, bw_ref[q], preferred_element_type=F32)
        for c in range(n_lc):
            bu_scr[c] = bu[:, c * LANES:(c + 1) * LANES]
        lanes = slice(q * sc, (q + 1) * sc)
        ar, am = abar_ref[0:1, lanes], abar_ref[1:2, lanes]
        hr = h0r_ref[:, lanes]
        hm = h0m_ref[:, lanes]
        for step in range(dec_seq):
            rows = pl.ds(step, n_seq, stride=dec_seq)
            b_all = jnp.concatenate([bu_scr[c, rows, :] for c in range(n_lc)], axis=1)
            br, bi = b_all[:, :sc], b_all[:, sc:]
            hr, hm = ar * hr - am * hm + br, ar * hm + am * hr + bi
            for c in range(n_lc // 2):
                h_scr[c, rows, :] = hr[:, c * LANES:(c + 1) * LANES]
                h_scr[n_lc // 2 + c, rows, :] = hm[:, c * LANES:(c + 1) * LANES]
        hr_ref[:, lanes] = hr
        hm_ref[:, lanes] = hm
        h_all = jnp.concatenate([h_scr[c] for c in range(n_lc)], axis=1)
        ys.append(jnp.dot(h_all.astype(BF16), cw_ref[q], preferred_element_type=F32))
    y = jnp.concatenate(ys, axis=1)
    o_s = _ssm_tail(y, u, gs_ref[...], dskip_ref, wglu_ref, bglu_ref).astype(BF16)
    d_a = oa_ref.shape[1]
    out = x_ref[...] + jnp.dot(oa_ref[...].astype(BF16), wo_ref[:d_a, :], preferred_element_type=F32)
    o_ref[...] = out + jnp.dot(o_s, wo_ref[d_a:, :], preferred_element_type=F32)


def _ssm_sample_outproj(x, oa, u, gs, h0_re, h0_im, sp, wo_bf):
    n = u.shape[0]
    names = ["bw", "cw", "abar", "dskip", "wglu", "bglu"]
    args = [x, oa, u, gs, h0_re, h0_im] + [sp[k] for k in names] + [wo_bf]
    full = lambda a: pl.BlockSpec(a.shape, lambda i: (0,) * a.ndim)
    state = jax.ShapeDtypeStruct(h0_re.shape, F32)
    return pl.pallas_call(
        _ssm_sample_kernel,
        grid=(1,),
        in_specs=[full(a) for a in args],
        out_specs=[full(x), full(h0_re), full(h0_im)],
        out_shape=[jax.ShapeDtypeStruct(x.shape, F32), state, state],
        scratch_shapes=[pltpu.VMEM((2 * SSM_LANE_CHUNK // LANES, n, LANES), F32),
                        pltpu.VMEM((2 * SSM_LANE_CHUNK // LANES, n, LANES), F32)],
        compiler_params=pltpu.CompilerParams(
            dimension_semantics=("arbitrary",), vmem_limit_bytes=VMEM_LIMIT_BYTES),
        name="ssm_sample_outproj",
    )(*args)


def _ssm_params(a_re, a_im, log_dt, b_re, b_im, c_re, c_im, d_skip, w_glu, b_glu):
    n_groups, n_state = a_re.shape
    g_per_q = LANES // SSM_GROUP
    n_q = n_groups // g_per_q
    dt = jnp.exp(log_dt.astype(F32))[:, None]
    a_re = a_re.astype(F32)
    a_im = a_im.astype(F32)
    mag = jnp.exp(a_re * dt)
    abar_re = mag * jnp.cos(a_im * dt)
    abar_im = mag * jnp.sin(a_im * dt)
    nr = abar_re - 1.0
    den = a_re * a_re + a_im * a_im
    coef_re = (nr * a_re + abar_im * a_im) / den
    coef_im = (abar_im * a_re - nr * a_im) / den
    b_re = b_re.astype(F32)
    b_im = b_im.astype(F32)
    bbar_re = coef_re[..., None] * b_re - coef_im[..., None] * b_im
    bbar_im = coef_re[..., None] * b_im + coef_im[..., None] * b_re

    same_group = (np.arange(g_per_q * SSM_GROUP)[:, None] // SSM_GROUP
                  == np.arange(g_per_q * n_state)[None, :] // n_state)

    def lane_tile(a, reps):
        w = a.shape[-1]
        return jnp.matmul(a, jnp.asarray(np.tile(np.eye(w, dtype=np.float32), (1, reps))),
                          precision=lax.Precision.HIGHEST)

    def rows_in(t):
        n = t.shape[0]
        t = lane_tile(t.reshape(n, n_q, g_per_q * n_state, SSM_GROUP), g_per_q)
        t = jnp.swapaxes(t * jnp.asarray(same_group.T, F32), -1, -2)
        return jnp.swapaxes(t, 0, 1).reshape(n_q, n * LANES, g_per_q * n_state)

    def cols_out(t):
        n = t.shape[0]
        t = lane_tile(t.reshape(n, n_q, g_per_q * SSM_GROUP, n_state), g_per_q)
        t = jnp.swapaxes(t * jnp.asarray(same_group, F32), -1, -2)
        return jnp.transpose(t, (1, 2, 0, 3)).reshape(n_q, g_per_q * n_state, n * LANES)

    bw = jnp.concatenate([rows_in(bbar_re[None]), rows_in(bbar_im[None])], axis=2).astype(BF16)
    cw = jnp.concatenate([cols_out(c_re.astype(F32)[None]), cols_out(-c_im.astype(F32)[None])],
                         axis=1).astype(BF16)

    ar = abar_re.reshape(1, -1)
    ai = abar_im.reshape(1, -1)
    sp = {
        "bw": bw, "cw": cw,
        "abar": jnp.concatenate([ar, ai], axis=0),
        "dskip": d_skip.astype(F32).reshape(1, -1),
        "wglu": (0.5 * w_glu.astype(F32)).astype(BF16),
        "bglu": 0.5 * b_glu.astype(F32).reshape(1, -1),
    }

    mic = SSM_MICRO
    lr_step = a_re * dt
    th_step = a_im * dt

    def power(t):
        t = jnp.asarray(np.asarray(t, np.float32))[:, None, None]
        mag = jnp.exp(t * lr_step)
        return mag * jnp.cos(t * th_step), mag * jnp.sin(t * th_step)

    pw_r, pw_i = power(np.arange(mic + 1))
    zr, zi = power(np.arange(mic - 1, -1, -1))
    wb_r = zr[..., None] * bbar_re[None] - zi[..., None] * bbar_im[None]
    wb_i = zr[..., None] * bbar_im[None] + zi[..., None] * bbar_re[None]
    w = jnp.concatenate([rows_in(wb_r), rows_in(wb_i)], axis=2).astype(BF16)

    c_re = c_re.astype(F32)
    c_im = c_im.astype(F32)
    pr1, pi1 = pw_r[1:mic + 1][:, :, None, :], pw_i[1:mic + 1][:, :, None, :]
    v_r = c_re[None] * pr1 - c_im[None] * pi1
    v_i = c_re[None] * pi1 + c_im[None] * pr1
    v = jnp.concatenate([cols_out(v_r), cols_out(-v_i)], axis=1).astype(BF16)

    tb_r = pw_r[:mic, :, :, None] * bbar_re[None] - pw_i[:mic, :, :, None] * bbar_im[None]
    tb_i = pw_r[:mic, :, :, None] * bbar_im[None] + pw_i[:mic, :, :, None] * bbar_re[None]
    taps = (jnp.sum(c_re[None, :, None, :, :] * jnp.swapaxes(tb_r, 2, 3)[:, :, :, None, :], axis=-1)
            - jnp.sum(c_im[None, :, None, :, :] * jnp.swapaxes(tb_i, 2, 3)[:, :, :, None, :], axis=-1))
    zero = jnp.zeros_like(taps[0])
    grid = jnp.stack([jnp.stack([taps[t - s] if t >= s else zero for t in range(mic)], axis=0)
                      for s in range(mic)], axis=0)
    grid = lane_tile(grid.reshape(mic, mic, n_q, LANES, SSM_GROUP), g_per_q)
    grid = grid * jnp.asarray(same_group[:, ::n_state // SSM_GROUP], F32)
    m = jnp.transpose(grid, (2, 0, 3, 1, 4)).reshape(n_q, mic * LANES, mic * LANES).astype(BF16)

    rb = SSM_SCAN_ROWS
    k = np.arange(rb)
    in_r, in_i = power(-mic * k)
    out_r, out_i = power(mic * (k - 1))
    flat = lambda a: a.reshape(a.shape[0], -1)
    aux_r, aux_i = power(np.array([mic, mic * (rb - 1), mic * rb]))
    aux = jnp.stack([flat(aux_r), flat(aux_i)], axis=1).reshape(6, -1)
    n_rows = SSM_STEP // mic
    blk = np.arange(n_rows) // rb
    strict = (blk[:, None] == blk[None, :]) & (np.arange(n_rows)[:, None] > np.arange(n_rows)[None, :])
    sums = np.arange(n_rows // rb)[:, None] == blk[None, :]
    pad = np.zeros((-(n_rows + n_rows // rb) % 16, n_rows), bool)
    scan = jnp.asarray(np.concatenate([strict, sums, pad], axis=0).astype(np.float32), BF16)
    sp.update({"w": w, "m": m, "v": v, "aux": aux, "scan": scan,
               "pin_re": flat(in_r), "pin_im": flat(in_i), "pout_re": flat(out_r), "pout_im": flat(out_i)})
    return sp


def _lanes_to_state(h, n_groups, n_state):
    b = h.shape[0]
    h = h.reshape(b, -1, 2, SSM_LANE_CHUNK)
    return (h[:, :, 0, :].reshape(b, n_groups, n_state), h[:, :, 1, :].reshape(b, n_groups, n_state))


def _toeplitz(v, n):
    h = v.shape[0]
    x = jnp.broadcast_to(v[:, None, :], (h, n, 2 * n)).reshape(h, 2 * n * n)
    return x[:, :n * (2 * n - 1)].reshape(h, n, 2 * n - 1)[:, :, :n]


def _prompt_bias_blocks(fvec):
    n = LANES
    h = fvec.shape[0]
    neg = jnp.full((h, n - 1), NEG_INF, F32)
    va = jnp.concatenate([fvec[:, 0:1], neg, jnp.zeros((h, 1), F32), fvec[:, 1:n][:, ::-1]], axis=1)
    vb = jnp.concatenate([fvec[:, 1:n + 1][:, ::-1], jnp.zeros((h, n), F32)], axis=1)
    return jnp.stack([_toeplitz(va, n), _toeplitz(vb, n)], axis=1)


def _decode_bias(fvec, page, dec_seq, n_heads):
    h = fvec.shape[0]
    rows = []
    for i in range(dec_seq):
        last = fvec[:, i + 1:i + 1 + page][:, ::-1]
        new = jnp.concatenate([fvec[:, 0:i + 1][:, ::-1], jnp.full((h, page - i - 1), NEG_INF, F32)], axis=1)
        rows.append(jnp.concatenate([last, new], axis=1))
    per_head = jnp.stack(rows, axis=0)
    same = np.eye(n_heads, dtype=bool)[None, :, None, :]
    near = jnp.where(jnp.asarray(same), per_head[:, :, :, None], NEG_INF)
    near = near.reshape(dec_seq * n_heads, -1)
    mask = np.where(np.broadcast_to(same, (dec_seq, n_heads, page, n_heads)), 0.0, NEG_INF)
    mask = mask.reshape(dec_seq * n_heads, -1).astype(np.float32)
    return jnp.asarray(np.concatenate([mask, mask], axis=0)), jnp.concatenate([near, near], axis=0)


def kernel(x_prompt, x_sample, cache_k, cache_v, state_ssm_re, state_ssm_im, page_table,
           norm_g, w_in, q_norm_g, k_norm_g, lambda_q1, lambda_k1, lambda_q2, lambda_k2,
           subln_g, rel_bias, ssm_a_re, ssm_a_im, ssm_log_dt, ssm_b_re, ssm_b_im,
           ssm_c_re, ssm_c_im, ssm_d, w_glu, b_glu, w_out):
    batch, seq, d_model = x_prompt.shape
    dec_batch, dec_seq, _ = x_sample.shape
    depth, n_pool, page, n_heads, _ = cache_k.shape
    n_pages = page_table.shape[1]
    d_attn = n_heads * V_DIM
    n_groups, n_state = ssm_a_re.shape[1:]
    new_rows = dec_seq * n_heads

    buckets = _bucket_table(2 * LANES)
    far_from = int(np.max(np.nonzero(buckets < N_BUCKETS - 1)[0])) + 1
    assert far_from <= LANES and _bucket_table(seq + page * n_pages)[far_from:].min() == N_BUCKETS - 1
    assert page == LANES and dec_seq < LANES and ATTN_TQ == ATTN_TK

    rel_bias = rel_bias.astype(F32)
    fvec = (rel_bias[buckets].T - rel_bias[N_BUCKETS - 1][:, None]) * LOG2E
    fvec = jnp.where(jnp.asarray(np.arange(2 * LANES) < far_from)[None], fvec, 0.0)
    dtiles = _prompt_bias_blocks(fvec)
    head_mask, near_bias = _decode_bias(fvec, page, dec_seq, n_heads)

    group_avg = jnp.asarray(np.kron(np.eye(2 * LANES // QK_DIM), np.full((QK_DIM, QK_DIM), 1.0 / QK_DIM)), BF16)
    n_rep = d_attn // QK_DIM
    cache_k_rows = cache_k.reshape(depth * n_pool, page * n_heads, V_DIM)
    cache_v_rows = cache_v.reshape(depth * n_pool, page * n_heads, V_DIM)

    hp = x_prompt.reshape(batch * seq, d_model)
    hs = x_sample.reshape(dec_batch * dec_seq, d_model)
    kp_l, vp_l, ks_l, vs_l = [], [], [], []
    srp_l, sip_l, srs_l, sis_l = [], [], [], []
    for l in range(depth):
        lam_init = _lambda_init(l)
        lam = (jnp.exp(jnp.sum(lambda_q1[l].astype(F32) * lambda_k1[l].astype(F32)))
               - jnp.exp(jnp.sum(lambda_q2[l].astype(F32) * lambda_k2[l].astype(F32))) + lam_init)
        lam = lam.reshape(1).astype(F32)
        ng = norm_g[l].astype(F32).reshape(1, d_model)
        w_bf = w_in[l].astype(BF16)
        gq = jnp.tile(q_norm_g[l].astype(F32), n_rep).reshape(1, d_attn) * (QK_DIM ** -0.5 * LOG2E)
        gk = jnp.tile(k_norm_g[l].astype(F32), n_rep).reshape(1, d_attn)
        sg = (subln_g[l].astype(F32) * (1.0 - lam_init)).reshape(1, V_DIM)
        wo_bf = w_out[l].astype(BF16)
        sp = _ssm_params(ssm_a_re[l], ssm_a_im[l], ssm_log_dt[l], ssm_b_re[l], ssm_b_im[l],
                         ssm_c_re[l], ssm_c_im[l], ssm_d[l], w_glu[l], b_glu[l])

        q1, q2, k4, kb, v4, va, ga, u4, gs = _inproj(hp, ng, w_bf, gq, gk, group_avg, head_rows=False)
        sq1, sq2, sk4, sv4, sga, su, sgs = _inproj(hs, ng, w_bf, gq, gk, group_avg, head_rows=True)
        r3 = lambda a: a.reshape(batch, seq, a.shape[-1])
        o_a, o_dec = _attention(lam, sg, r3(q1), r3(q2), r3(kb), r3(va), dtiles, r3(ga),
                                page_table + l * n_pool, sq1, sq2, sk4, sv4, sga, head_mask, near_bias,
                                cache_k_rows, cache_v_rows, new_rows)
        hp3, hfin = _ssm_prompt_outproj(r3(hp), o_a, u4.reshape(batch, -1, LANES), r3(gs), sp, wo_bf)
        hp = hp3.reshape(batch * seq, d_model)
        kp_l.append(k4.reshape(batch, seq, n_heads, V_DIM).astype(cache_k.dtype))
        vp_l.append(v4.reshape(batch, seq, n_heads, V_DIM).astype(cache_v.dtype))
        hr_p, hi_p = _lanes_to_state(hfin.reshape(batch, -1), n_groups, n_state)
        srp_l.append(hr_p.astype(state_ssm_re.dtype))
        sip_l.append(hi_p.astype(state_ssm_im.dtype))

        hs, hr_s, hi_s = _ssm_sample_outproj(
            hs, o_dec.reshape(dec_batch * dec_seq, d_attn), su, sgs,
            state_ssm_re[l].astype(F32).reshape(dec_batch, -1), state_ssm_im[l].astype(F32).reshape(dec_batch, -1),
            sp, wo_bf)
        ks_l.append(sk4.reshape(dec_batch, dec_seq, n_heads, V_DIM).astype(cache_k.dtype))
        vs_l.append(sv4.reshape(dec_batch, dec_seq, n_heads, V_DIM).astype(cache_v.dtype))
        srs_l.append(hr_s.reshape(dec_batch, n_groups, n_state).astype(state_ssm_re.dtype))
        sis_l.append(hi_s.reshape(dec_batch, n_groups, n_state).astype(state_ssm_im.dtype))

    y_prompt = hp.reshape(batch, seq, d_model).astype(x_prompt.dtype)
    y_sample = hs.reshape(dec_batch, dec_seq, d_model).astype(x_sample.dtype)
    return (y_prompt, y_sample, jnp.stack(kp_l), jnp.stack(vp_l), jnp.stack(ks_l), jnp.stack(vs_l),
            jnp.stack(srp_l), jnp.stack(sip_l), jnp.stack(srs_l), jnp.stack(sis_l))
```

```python
import functools
import math

import numpy as np
import jax
import jax.numpy as jnp
from jax import lax
from jax.experimental import pallas as pl
from jax.experimental.pallas import tpu as pltpu

F32 = jnp.float32
BF16 = jnp.bfloat16

QK_DIM = 64
V_DIM = 2 * QK_DIM
N_BUCKETS = 32
MAX_DISTANCE = 128
SSM_GROUP = 16
SSM_STATE = 64
EPS = 1e-6
NEG_INF = -1e30
LOG2E = math.log2(math.e)
GELU_C0 = math.sqrt(2.0 / math.pi)
GELU_C1 = GELU_C0 * 0.044715

LANES = 128
VMEM_LIMIT_BYTES = 56 * 1024 * 1024

ROW_TILE = 512
ATTN_TQ = 512
ATTN_TK = 512
ATTN_UNROLL = 8
ATTN_NEAR_TASKS = 4
DECODE_SEQS_PER_STEP = 2
SSM_MICRO = 4
SSM_STEP = 1024
SSM_SCAN_ROWS = 32
SSM_LANE_CHUNK = 512


def _lambda_init(layer):
    return 0.8 - 0.6 * math.exp(-0.3 * layer)


def _bucket_table(n_max):
    n = np.arange(n_max)
    max_exact = N_BUCKETS // 2
    nf = np.maximum(n, 1).astype(np.float32)
    large = max_exact + (np.log(nf / np.float32(max_exact)) / np.float32(math.log(MAX_DISTANCE / max_exact))
                         * np.float32(N_BUCKETS - max_exact)).astype(np.int32)
    large = np.minimum(large, N_BUCKETS - 1)
    return np.where(n < max_exact, n, large).astype(np.int32)


def _silu(x):
    return (0.5 * x) * (1.0 + jnp.tanh(0.5 * x))


def _store_head_rows(ref, val, n_heads):
    rows = val.shape[0]
    for h in range(n_heads):
        ref[pl.ds(h, rows, stride=n_heads), :] = val[:, h * V_DIM:(h + 1) * V_DIM]


def _inproj_kernel(x_ref, ng_ref, w_ref, gq_ref, gk_ref, gavg_ref, *out_refs, head_rows):
    x = x_ref[...]
    ms = jnp.mean(x * x, axis=-1, keepdims=True)
    xb = (x * lax.rsqrt(ms + EPS) * ng_ref[...]).astype(BF16)
    d_seg = gq_ref.shape[1]
    n_heads = d_seg // V_DIM

    def seg(i):
        return jnp.dot(xb, w_ref[:, i * d_seg:(i + 1) * d_seg], preferred_element_type=F32)

    def group_norm(t, g):
        sq = (t * t).astype(BF16)
        wg = gavg_ref.shape[0]
        msq = jnp.concatenate([jnp.dot(sq[:, c:c + wg], gavg_ref[...], preferred_element_type=F32)
                               for c in range(0, d_seg, wg)], axis=1)
        return t * lax.rsqrt(msq + EPS) * g

    q = group_norm(seg(0), gq_ref[...])
    lane = lax.broadcasted_iota(jnp.int32, q.shape, 1)
    first = (lane % V_DIM) < QK_DIM
    qa = jnp.where(first, q, 0.0)
    qb = jnp.where(first, 0.0, q)
    k = group_norm(seg(1), gk_ref[...])
    v = seg(2)
    ga = seg(3)
    if head_rows:
        q1_ref, q2_ref, k4_ref, v4_ref, ga_ref, u_ref, gs_ref = out_refs
        _store_head_rows(q1_ref, qa, n_heads)
        _store_head_rows(q2_ref, qb, n_heads)
        _store_head_rows(ga_ref, ga, n_heads)
    else:
        q1_ref, q2_ref, k4_ref, kb_ref, v4_ref, va_ref, ga_ref, u_ref, gs_ref = out_refs
        q1_ref[...] = qa.astype(BF16)
        q2_ref[...] = qb.astype(BF16)
        kb_ref[...] = k.astype(BF16)
        vb = v.astype(BF16)
        ones = jnp.ones((v.shape[0], V_DIM), BF16)
        pieces = []
        for h in range(n_heads):
            pieces += [vb[:, h * V_DIM:(h + 1) * V_DIM], ones]
        va_ref[...] = jnp.concatenate(pieces, axis=1)
        ga_ref[...] = ga
    _store_head_rows(k4_ref, k, n_heads)
    _store_head_rows(v4_ref, v, n_heads)
    _store_head_rows(u_ref, seg(4), d_seg // LANES)
    gs_ref[...] = seg(5)


def _inproj(x, ng, w_bf, gq, gk, gavg, head_rows):
    n, d_model = x.shape
    d_seg = gq.shape[1]
    n_heads = d_seg // V_DIM
    tm = min(ROW_TILE, n)
    full = lambda a: pl.BlockSpec(a.shape, lambda i: (0,) * a.ndim)
    wide = lambda w, dt: (jax.ShapeDtypeStruct((n, w), dt), pl.BlockSpec((tm, w), lambda i: (i, 0)))
    tall = lambda dt: (jax.ShapeDtypeStruct((n * n_heads, V_DIM), dt),
                       pl.BlockSpec((tm * n_heads, V_DIM), lambda i: (i, 0)))
    if head_rows:
        outs = [tall(F32), tall(F32), tall(F32), tall(F32), tall(F32), tall(F32), wide(d_seg, F32)]
    else:
        outs = [wide(d_seg, BF16), wide(d_seg, BF16), tall(F32), wide(d_seg, BF16), tall(F32),
                wide(2 * d_seg, BF16), wide(d_seg, F32), tall(F32), wide(d_seg, F32)]
    return pl.pallas_call(
        functools.partial(_inproj_kernel, head_rows=head_rows),
        grid=(n // tm,),
        in_specs=[pl.BlockSpec((tm, d_model), lambda i: (i, 0)),
                  full(ng), full(w_bf), full(gq), full(gk), full(gavg)],
        out_specs=[o[1] for o in outs],
        out_shape=[o[0] for o in outs],
        compiler_params=pltpu.CompilerParams(
            dimension_semantics=("arbitrary",), vmem_limit_bytes=VMEM_LIMIT_BYTES),
        name="inproj_samples" if head_rows else "inproj_prompt",
    )(x, ng, w_bf, gq, gk, gavg)


def _diff_epilogue(o1, o2, lam, sg, ga):
    od = o1 - lam * o2
    ms = jnp.mean(od * od, axis=-1, keepdims=True)
    return od * lax.rsqrt(ms + EPS) * sg * _silu(ga)


def _attention_kernel(tasks_ref, pt_ref, lam_ref,
                      q1a_ref, q2a_ref, q1b_ref, q2b_ref, k_ref, v_ref, ab_ref, gaa_ref, gab_ref, sg_ref,
                      dq1_ref, dq2_ref, dkn_ref, dvn_ref, dga_ref, hm_ref, near_ref, ck_hbm, cv_hbm,
                      o_ref, od_ref,
                      s_scr, d_scr, m_scr, acc_scr, qs_scr, kbuf, vbuf, knew, vnew, ksem, vsem):
    tq = s_scr.shape[1] // 2
    tk = s_scr.shape[2]
    p = pl.program_id(2)
    n_steps = pl.num_programs(2)
    n_tasks = tasks_ref.shape[1]
    n_far = n_tasks - 1 - ATTN_NEAR_TASKS
    unroll = max([u for u in range(2, ATTN_UNROLL + 1, 2) if n_far % u == 0], default=0)
    step = (pl.program_id(0) * pl.num_programs(1) + pl.program_id(1)) * n_steps + p
    n_pages = kbuf.shape[1]
    page_rows = kbuf.shape[2]
    total_steps = pt_ref.shape[0] // DECODE_SEQS_PER_STEP
    lam = lam_ref[0]

    def k_copy(st, local, j):
        return pltpu.make_async_copy(ck_hbm.at[pt_ref[st * DECODE_SEQS_PER_STEP + local, j]],
                                     kbuf.at[local, j], ksem.at[0])

    def v_copy(st, local, j):
        return pltpu.make_async_copy(cv_hbm.at[pt_ref[st * DECODE_SEQS_PER_STEP + local, j]],
                                     vbuf.at[local, j], vsem.at[0])

    def start_fetch(st):
        for local in range(DECODE_SEQS_PER_STEP):
            for j in range(n_pages):
                k_copy(st, local, j).start()
                v_copy(st, local, j).start()

    def wait_fetch(st):
        for local in range(DECODE_SEQS_PER_STEP):
            for j in range(n_pages):
                k_copy(st, local, j).wait()
                v_copy(st, local, j).wait()

    def decode(local, rows):
        r0 = local * rows
        qx = jnp.concatenate([dq1_ref[r0:r0 + rows, :], dq2_ref[r0:r0 + rows, :]], axis=0).astype(BF16)
        knew[local, 0:rows, :] = dkn_ref[r0:r0 + rows, :].astype(BF16)
        vnew[local, 0:rows, :] = dvn_ref[r0:r0 + rows, :].astype(BF16)
        nt = (((1,), (1,)), ((), ()))
        s_tiles = [lax.dot_general(qx, kbuf[local, j].astype(BF16), nt, preferred_element_type=F32)
                   for j in range(n_pages)]
        s_tiles.append(lax.dot_general(qx, knew[local], nt, preferred_element_type=F32))
        head_mask = hm_ref[...]
        s = jnp.concatenate([t + head_mask for t in s_tiles[:n_pages - 1]]
                            + [jnp.concatenate(s_tiles[n_pages - 1:], axis=1) + near_ref[...]], axis=1)
        m = jnp.max(s, axis=-1, keepdims=True)
        pr = jnp.exp2(s - m)
        l_sum = jnp.sum(pr, axis=-1, keepdims=True)
        pb = pr.astype(BF16)
        acc = jnp.dot(pb[:, n_pages * page_rows:], vnew[local], preferred_element_type=F32)
        for j in range(n_pages):
            acc = acc + jnp.dot(pb[:, j * page_rows:(j + 1) * page_rows], vbuf[local, j].astype(BF16),
                                preferred_element_type=F32)
        o = acc / l_sum
        od_ref[r0:r0 + rows, :] = _diff_epilogue(o[:rows], o[rows:], lam, sg_ref[...], dga_ref[r0:r0 + rows, :])

    @pl.when(step == 0)
    def _():
        start_fetch(0)
        knew[...] = jnp.zeros(knew.shape, knew.dtype)
        vnew[...] = jnp.zeros(vnew.shape, vnew.dtype)

    @pl.when(p == 0)
    def _():
        for t in range(2):
            for a in range(tq // LANES):
                for b in range(tk // LANES):
                    delta = a - b + t * (tk // LANES)
                    if delta in (0, 1):
                        blk = ab_ref[delta]
                    else:
                        blk = jnp.full((LANES, LANES), NEG_INF if delta < 0 else 0.0, F32)
                    d_scr[t, a * LANES:(a + 1) * LANES, b * LANES:(b + 1) * LANES] = blk
        d_scr[2] = jnp.zeros((tq, tk), F32)

    m_scr[...] = jnp.full(m_scr.shape, NEG_INF, F32)
    acc_scr[...] = jnp.zeros(acc_scr.shape, F32)
    qs_scr[0] = jnp.concatenate([q1a_ref[...], q2a_ref[...]], axis=0)
    qs_scr[1] = jnp.concatenate([q1b_ref[...], q2b_ref[...]], axis=0)

    def task(kind, i):
        return tasks_ref[kind * n_steps + p, i]

    def produce(i, slot, near):
        k0 = pl.multiple_of(task(1, i) * tk, tk)
        s = lax.dot_general(qs_scr[task(3, i)], k_ref[pl.ds(k0, tk), :], (((1,), (1,)), ((), ())),
                            preferred_element_type=F32)
        if near:
            bias = d_scr[task(2, i)]
            s = s + jnp.concatenate([bias, bias], axis=0)
        s_scr[slot] = s

    def consume(i, slot):
        k0 = pl.multiple_of(task(1, i) * tk, tk)
        a = task(3, i)
        s = s_scr[slot]
        m_old = m_scr[a]
        m_new = jnp.maximum(m_old, jnp.max(s, axis=-1, keepdims=True))
        pr = jnp.exp2(s - jnp.concatenate([m_new] * (tk // LANES), axis=1))
        alpha = jnp.exp2(m_old - m_new)
        pv = jnp.dot(pr.astype(BF16), v_ref[pl.ds(k0, tk), :], preferred_element_type=F32)
        acc_scr[a] = acc_scr[a] * jnp.concatenate([alpha] * (acc_scr.shape[2] // LANES), axis=1) + pv
        m_scr[a] = m_new

    wait_fetch(step)
    for local in range(DECODE_SEQS_PER_STEP):
        decode(local, od_ref.shape[0] // DECODE_SEQS_PER_STEP)
    produce(0, 0, True)
    for i in range(ATTN_NEAR_TASKS):
        produce(i + 1, (i + 1) % 2, i + 1 < ATTN_NEAR_TASKS)
        consume(i, i % 2)

    @pl.when(step + 1 < total_steps)
    def _():
        start_fetch(step + 1)

    def body(j, carry):
        for u in range(unroll):
            i = ATTN_NEAR_TASKS + j * unroll + u
            produce(i + 1, (u + 1) % 2, False)
            consume(i, u % 2)
        return carry

    if n_far:
        lax.fori_loop(0, n_far // unroll, body, 0)
    consume(n_tasks - 1, (n_tasks - 1) % 2)

    for a, ga_ref in enumerate((gaa_ref, gab_ref)):
        q0 = pl.multiple_of(task(0, ATTN_NEAR_TASKS // 2 * a) * tq, tq)
        acc = acc_scr[a]
        o = acc[:, :V_DIM] / acc[:, V_DIM:]
        out = _diff_epilogue(o[:tq], o[tq:], lam, sg_ref[...], ga_ref[...])
        o_ref[pl.ds(q0, tq), :] = out.astype(o_ref.dtype)


def _attn_tasks(n_q):
    n_steps = n_q // 2
    tab = np.zeros((4, n_steps, n_q + 1), np.int32)
    for p in range(n_steps):
        near, far = [], []
        for acc, qt in enumerate((p, n_q - 1 - p)):
            tasks = [(qt, qt - t, min(t, 2), acc) for t in range(qt + 1)]
            near.append(tasks[:2])
            far += tasks[2:]
        while len(near[0]) < 2:
            near[0].append(far.pop(0))
        order = near[0] + near[1] + far
        assert len(order) == n_q + 1 and len(near[1]) == 2 and ATTN_NEAR_TASKS == 4
        tab[:, p, :] = np.array(order).T
    return tab.reshape(4 * n_steps, n_q + 1)


def _attention(lam, sg, q1, q2, kb, va, dtiles, ga,
               page_table, dq1, dq2, dkn, dvn, dga, head_mask, near_bias, cache_k, cache_v, new_rows):
    b, l, d_attn = q1.shape
    n_heads = d_attn // V_DIM
    tq, tk = ATTN_TQ, ATTN_TK
    n_q = l // tq
    n_steps = n_q // 2
    n_seq, n_pages = page_table.shape
    page_rows = cache_k.shape[1]
    total_steps = b * n_heads * n_steps
    assert tq == tk and n_q % 2 == 0 and n_q >= ATTN_NEAR_TASKS
    assert n_seq == total_steps * DECODE_SEQS_PER_STEP
    tasks = jnp.asarray(_attn_tasks(n_q))
    dec_rows = DECODE_SEQS_PER_STEP * new_rows

    def gstep(bi, h, pi):
        return (bi * n_heads + h) * n_steps + pi

    whole = lambda w: pl.BlockSpec((None, l, w), lambda bi, h, pi, *_: (bi, 0, h))
    tile_a = pl.BlockSpec((None, tq, V_DIM), lambda bi, h, pi, *_: (bi, pi, h))
    tile_b = pl.BlockSpec((None, tq, V_DIM), lambda bi, h, pi, *_: (bi, n_q - 1 - pi, h))
    dec = pl.BlockSpec((dec_rows, V_DIM), lambda bi, h, pi, *_: (gstep(bi, h, pi), 0))
    const2 = lambda a: pl.BlockSpec(a.shape, lambda bi, h, pi, *_: (0, 0))
    grid_spec = pltpu.PrefetchScalarGridSpec(
        num_scalar_prefetch=2,
        grid=(b, n_heads, n_steps),
        in_specs=[
            pl.BlockSpec(memory_space=pltpu.SMEM),
            tile_a, tile_a, tile_b, tile_b, whole(V_DIM), whole(2 * V_DIM),
            pl.BlockSpec((None, 2, LANES, LANES), lambda bi, h, pi, *_: (h, 0, 0, 0)),
            tile_a, tile_b, const2(sg),
            dec, dec, dec, dec, dec, const2(head_mask), const2(near_bias),
            pl.BlockSpec(memory_space=pl.ANY), pl.BlockSpec(memory_space=pl.ANY),
        ],
        out_specs=[whole(V_DIM), dec],
        scratch_shapes=[
            pltpu.VMEM((2, 2 * tq, tk), F32), pltpu.VMEM((3, tq, tk), F32),
            pltpu.VMEM((2, 2 * tq, LANES), F32), pltpu.VMEM((2, 2 * tq, 2 * V_DIM), F32),
            pltpu.VMEM((2, 2 * tq, V_DIM), BF16),
            pltpu.VMEM((DECODE_SEQS_PER_STEP, n_pages, page_rows, V_DIM), cache_k.dtype),
            pltpu.VMEM((DECODE_SEQS_PER_STEP, n_pages, page_rows, V_DIM), cache_v.dtype),
            pltpu.VMEM((DECODE_SEQS_PER_STEP, page_rows, V_DIM), BF16),
            pltpu.VMEM((DECODE_SEQS_PER_STEP, page_rows, V_DIM), BF16),
            pltpu.SemaphoreType.DMA((1,)), pltpu.SemaphoreType.DMA((1,)),
        ],
    )
    return pl.pallas_call(
        _attention_kernel,
        grid_spec=grid_spec,
        out_shape=[jax.ShapeDtypeStruct((b, l, d_attn), BF16),
                   jax.ShapeDtypeStruct((n_seq * new_rows, V_DIM), F32)],
        compiler_params=pltpu.CompilerParams(
            dimension_semantics=("arbitrary", "arbitrary", "arbitrary"),
            vmem_limit_bytes=VMEM_LIMIT_BYTES),
        name="attention",
    )(tasks, page_table, lam, q1, q2, q1, q2, kb, va, dtiles, ga, ga, sg,
      dq1, dq2, dkn, dvn, dga, head_mask, near_bias, cache_k, cache_v)


def _ssm_tail(y, u, gs, dskip_ref, wglu_half_ref, bglu_half_ref):
    x = y + dskip_ref[...] * u
    inner = x * (GELU_C1 * (x * x) + GELU_C0)
    z = (0.5 * x) * (1.0 + jnp.tanh(inner))
    t_glu = jnp.tanh(jnp.dot(z.astype(BF16), wglu_half_ref[...], preferred_element_type=F32) + bglu_half_ref[...])
    t_gate = jnp.tanh(0.5 * gs)
    return ((z * gs) * 0.25) * (1.0 + t_glu) * (1.0 + t_gate)


def _ssm_prompt_kernel(x_ref, oa_ref, u4_ref, gs_ref, w_ref, m_ref, v_ref,
                       pin_re_ref, pin_im_ref, pout_re_ref, pout_im_ref,
                       aux_ref, scan_ref, dskip_ref, wglu_ref, bglu_ref, wo_ref, o_ref, hfin_ref,
                       carry_scr, y4_scr):
    n_q = w_ref.shape[0]
    mic = SSM_MICRO
    sc = SSM_LANE_CHUNK
    ts = gs_ref.shape[0]
    r = ts // mic
    rb = pin_re_ref.shape[0]
    n_blk = r // rb
    tile_rows = lambda a: jnp.concatenate([a] * n_blk, axis=0)
    c = pl.program_id(1)

    @pl.when(c == 0)
    def _():
        carry_scr[...] = jnp.zeros(carry_scr.shape, F32)

    for q in range(n_q):
        re_l = slice(2 * q * sc, (2 * q + 1) * sc)
        im_l = slice((2 * q + 1) * sc, 2 * (q + 1) * sc)
        st_l = slice(q * sc, (q + 1) * sc)
        x = jnp.concatenate([u4_ref[pl.ds(n_q * s + q, r, stride=n_q * mic), :] for s in range(mic)],
                            axis=1).astype(BF16)
        e = jnp.dot(x, w_ref[q], preferred_element_type=F32)
        er, em = e[:, :sc], e[:, sc:]
        pw = lambda ref, i: ref[i:i + 1, st_l]
        pir, pii = tile_rows(pin_re_ref[:, st_l]), tile_rows(pin_im_ref[:, st_l])
        xs = jnp.concatenate([er * pir - em * pii, er * pii + em * pir], axis=1).astype(BF16)
        cs = jnp.dot(scan_ref[...], xs, preferred_element_type=F32)
        tot_r, tot_m = cs[r:r + n_blk, :sc], cs[r:r + n_blk, sc:]
        t_r = tot_r * pw(aux_ref, 2) - tot_m * pw(aux_ref, 3)
        t_m = tot_r * pw(aux_ref, 3) + tot_m * pw(aux_ref, 2)
        ar, am = pw(aux_ref, 0), pw(aux_ref, 1)
        br, bm = pw(aux_ref, 4), pw(aux_ref, 5)
        h_r, h_m = carry_scr[:, re_l], carry_scr[:, im_l]
        base_r, base_m = [], []
        for blk in range(n_blk):
            base_r.append(jnp.broadcast_to(ar * h_r - am * h_m, (rb, sc)))
            base_m.append(jnp.broadcast_to(ar * h_m + am * h_r, (rb, sc)))
            h_r, h_m = (br * h_r - bm * h_m + t_r[blk:blk + 1, :], br * h_m + bm * h_r + t_m[blk:blk + 1, :])
        carry_scr[:, re_l] = h_r
        carry_scr[:, im_l] = h_m
        hfin_ref[:, re_l] = h_r
        hfin_ref[:, im_l] = h_m
        sr = cs[:r, :sc] + jnp.concatenate(base_r, axis=0)
        sm = cs[:r, sc:] + jnp.concatenate(base_m, axis=0)
        por, poi = tile_rows(pout_re_ref[:, st_l]), tile_rows(pout_im_ref[:, st_l])
        hp = jnp.concatenate([sr * por - sm * poi, sr * poi + sm * por], axis=1).astype(BF16)
        y = (jnp.dot(x, m_ref[q], preferred_element_type=F32)
             + jnp.dot(hp, v_ref[q], preferred_element_type=F32))
        for s in range(mic):
            y4_scr[pl.ds(n_q * s + q, r, stride=n_q * mic), :] = y[:, s * LANES:(s + 1) * LANES]
    y = jnp.concatenate([y4_scr[pl.ds(q, ts, stride=n_q), :] for q in range(n_q)], axis=1)
    u = jnp.concatenate([u4_ref[pl.ds(q, ts, stride=n_q), :] for q in range(n_q)], axis=1)
    o_s = _ssm_tail(y, u, gs_ref[...], dskip_ref, wglu_ref, bglu_ref).astype(BF16)
    d_a = oa_ref.shape[1]
    out = x_ref[...] + jnp.dot(oa_ref[...], wo_ref[:d_a, :], preferred_element_type=F32)
    o_ref[...] = out + jnp.dot(o_s, wo_ref[d_a:, :], preferred_element_type=F32)


def _ssm_prompt_outproj(x, oa, u4, gs, sp, wo_bf):
    b, l, d_ssm = gs.shape
    n_q = d_ssm // LANES
    ts = SSM_STEP
    n_state2 = 2 * sp["aux"].shape[1]
    row = lambda w: pl.BlockSpec((None, ts, w), lambda bi, ci: (bi, ci, 0))
    row4 = pl.BlockSpec((None, ts * n_q, LANES), lambda bi, ci: (bi, ci, 0))
    full = lambda a: pl.BlockSpec(a.shape, lambda bi, ci: (0,) * a.ndim)
    names = ["w", "m", "v", "pin_re", "pin_im", "pout_re", "pout_im", "aux", "scan", "dskip", "wglu", "bglu"]
    return pl.pallas_call(
        _ssm_prompt_kernel,
        grid=(b, l // ts),
        in_specs=[row(x.shape[2]), row(oa.shape[2]), row4, row(d_ssm)] + [full(sp[n]) for n in names]
        + [full(wo_bf)],
        out_specs=[row(x.shape[2]), pl.BlockSpec((None, 1, n_state2), lambda bi, ci: (bi, 0, 0))],
        out_shape=[jax.ShapeDtypeStruct(x.shape, F32),
                   jax.ShapeDtypeStruct((b, 1, n_state2), F32)],
        scratch_shapes=[pltpu.VMEM((1, n_state2), F32), pltpu.VMEM((ts * n_q, LANES), F32)],
        compiler_params=pltpu.CompilerParams(
            dimension_semantics=("arbitrary", "arbitrary"), vmem_limit_bytes=VMEM_LIMIT_BYTES),
        name="ssm_prompt_outproj",
    )(x, oa, u4, gs, *[sp[n] for n in names], wo_bf)


def _ssm_sample_kernel(x_ref, oa_ref, u4_ref, gs_ref, h0r_ref, h0m_ref, w_ref, m_ref, v_ref, aux_ref,
                       dskip_ref, wglu_ref, bglu_ref, wo_ref, o_ref, hr_ref, hm_ref, y4_scr):
    n_seq = h0r_ref.shape[0]
    n_q = w_ref.shape[0]
    mic = SSM_MICRO
    sc = SSM_LANE_CHUNK
    n = gs_ref.shape[0]
    for q in range(n_q):
        st_l = slice(q * sc, (q + 1) * sc)
        x = jnp.concatenate([u4_ref[pl.ds(n_q * s + q, n_seq, stride=n_q * mic), :] for s in range(mic)],
                            axis=1).astype(BF16)
        e = jnp.dot(x, w_ref[q], preferred_element_type=F32)
        h_r, h_m = h0r_ref[:, st_l], h0m_ref[:, st_l]
        ar, am = aux_ref[0:1, st_l], aux_ref[1:2, st_l]
        hr_ref[:, st_l] = ar * h_r - am * h_m + e[:, :sc]
        hm_ref[:, st_l] = ar * h_m + am * h_r + e[:, sc:]
        hp = jnp.concatenate([h_r, h_m], axis=1).astype(BF16)
        y = (jnp.dot(x, m_ref[q], preferred_element_type=F32)
             + jnp.dot(hp, v_ref[q], preferred_element_type=F32))
        for s in range(mic):
            y4_scr[pl.ds(n_q * s + q, n_seq, stride=n_q * mic), :] = y[:, s * LANES:(s + 1) * LANES]
    y = jnp.concatenate([y4_scr[pl.ds(q, n, stride=n_q), :] for q in range(n_q)], axis=1)
    u = jnp.concatenate([u4_ref[pl.ds(q, n, stride=n_q), :] for q in range(n_q)], axis=1)
    o_s = _ssm_tail(y, u, gs_ref[...], dskip_ref, wglu_ref, bglu_ref).astype(BF16)
    d_a = oa_ref.shape[1]
    out = x_ref[...] + jnp.dot(oa_ref[...].astype(BF16), wo_ref[:d_a, :], preferred_element_type=F32)
    o_ref[...] = out + jnp.dot(o_s, wo_ref[d_a:, :], preferred_element_type=F32)


def _ssm_sample_outproj(x, oa, u4, gs, h0_re, h0_im, sp, wo_bf):
    n = gs.shape[0]
    assert n == h0_re.shape[0] * SSM_MICRO, "one micro-chunk of new steps per sequence"
    names = ["w", "m", "v", "aux", "dskip", "wglu", "bglu"]
    args = [x, oa, u4, gs, h0_re, h0_im] + [sp[k] for k in names] + [wo_bf]
    full = lambda a: pl.BlockSpec(a.shape, lambda i: (0,) * a.ndim)
    state = jax.ShapeDtypeStruct(h0_re.shape, F32)
    return pl.pallas_call(
        _ssm_sample_kernel,
        grid=(1,),
        in_specs=[full(a) for a in args],
        out_specs=[full(x), full(h0_re), full(h0_im)],
        out_shape=[jax.ShapeDtypeStruct(x.shape, F32), state, state],
        scratch_shapes=[pltpu.VMEM(u4.shape, F32)],
        compiler_params=pltpu.CompilerParams(
            dimension_semantics=("arbitrary",), vmem_limit_bytes=VMEM_LIMIT_BYTES),
        name="ssm_sample_outproj",
    )(*args)


def _ssm_params(a_re, a_im, log_dt, b_re, b_im, c_re, c_im, d_skip, w_glu, b_glu):
    n_groups, n_state = a_re.shape
    g_per_q = LANES // SSM_GROUP
    n_q = n_groups // g_per_q
    dt = jnp.exp(log_dt.astype(F32))[:, None]
    a_re = a_re.astype(F32)
    a_im = a_im.astype(F32)
    mag = jnp.exp(a_re * dt)
    abar_re = mag * jnp.cos(a_im * dt)
    abar_im = mag * jnp.sin(a_im * dt)
    nr = abar_re - 1.0
    den = a_re * a_re + a_im * a_im
    coef_re = (nr * a_re + abar_im * a_im) / den
    coef_im = (abar_im * a_re - nr * a_im) / den
    b_re = b_re.astype(F32)
    b_im = b_im.astype(F32)
    bbar_re = coef_re[..., None] * b_re - coef_im[..., None] * b_im
    bbar_im = coef_re[..., None] * b_im + coef_im[..., None] * b_re

    same_group = (np.arange(g_per_q * SSM_GROUP)[:, None] // SSM_GROUP
                  == np.arange(g_per_q * n_state)[None, :] // n_state)

    def lane_tile(a, reps):
        w = a.shape[-1]
        return jnp.matmul(a, jnp.asarray(np.tile(np.eye(w, dtype=np.float32), (1, reps))),
                          precision=lax.Precision.HIGHEST)

    def rows_in(t):
        n = t.shape[0]
        t = lane_tile(t.reshape(n, n_q, g_per_q * n_state, SSM_GROUP), g_per_q)
        t = jnp.swapaxes(t * jnp.asarray(same_group.T, F32), -1, -2)
        return jnp.swapaxes(t, 0, 1).reshape(n_q, n * LANES, g_per_q * n_state)

    def cols_out(t):
        n = t.shape[0]
        t = lane_tile(t.reshape(n, n_q, g_per_q * SSM_GROUP, n_state), g_per_q)
        t = jnp.swapaxes(t * jnp.asarray(same_group, F32), -1, -2)
        return jnp.transpose(t, (1, 2, 0, 3)).reshape(n_q, g_per_q * n_state, n * LANES)

    sp = {
        "dskip": d_skip.astype(F32).reshape(1, -1),
        "wglu": (0.5 * w_glu.astype(F32)).astype(BF16),
        "bglu": 0.5 * b_glu.astype(F32).reshape(1, -1),
    }

    mic = SSM_MICRO
    lr_step = a_re * dt
    th_step = a_im * dt

    def power(t):
        t = jnp.asarray(np.asarray(t, np.float32))[:, None, None]
        mag = jnp.exp(t * lr_step)
        return mag * jnp.cos(t * th_step), mag * jnp.sin(t * th_step)

    pw_r, pw_i = power(np.arange(mic + 1))
    zr, zi = power(np.arange(mic - 1, -1, -1))
    wb_r = zr[..., None] * bbar_re[None] - zi[..., None] * bbar_im[None]
    wb_i = zr[..., None] * bbar_im[None] + zi[..., None] * bbar_re[None]
    w = jnp.concatenate([rows_in(wb_r), rows_in(wb_i)], axis=2).astype(BF16)

    c_re = c_re.astype(F32)
    c_im = c_im.astype(F32)
    pr1, pi1 = pw_r[1:mic + 1][:, :, None, :], pw_i[1:mic + 1][:, :, None, :]
    v_r = c_re[None] * pr1 - c_im[None] * pi1
    v_i = c_re[None] * pi1 + c_im[None] * pr1
    v = jnp.concatenate([cols_out(v_r), cols_out(-v_i)], axis=1).astype(BF16)

    tb_r = pw_r[:mic, :, :, None] * bbar_re[None] - pw_i[:mic, :, :, None] * bbar_im[None]
    tb_i = pw_r[:mic, :, :, None] * bbar_im[None] + pw_i[:mic, :, :, None] * bbar_re[None]
    taps = (jnp.sum(c_re[None, :, None, :, :] * jnp.swapaxes(tb_r, 2, 3)[:, :, :, None, :], axis=-1)
            - jnp.sum(c_im[None, :, None, :, :] * jnp.swapaxes(tb_i, 2, 3)[:, :, :, None, :], axis=-1))
    zero = jnp.zeros_like(taps[0])
    grid = jnp.stack([jnp.stack([taps[t - s] if t >= s else zero for t in range(mic)], axis=0)
                      for s in range(mic)], axis=0)
    grid = lane_tile(grid.reshape(mic, mic, n_q, LANES, SSM_GROUP), g_per_q)
    grid = grid * jnp.asarray(same_group[:, ::n_state // SSM_GROUP], F32)
    m = jnp.transpose(grid, (2, 0, 3, 1, 4)).reshape(n_q, mic * LANES, mic * LANES).astype(BF16)

    rb = SSM_SCAN_ROWS
    k = np.arange(rb)
    in_r, in_i = power(-mic * k)
    out_r, out_i = power(mic * (k - 1))
    flat = lambda a: a.reshape(a.shape[0], -1)
    aux_r, aux_i = power(np.array([mic, mic * (rb - 1), mic * rb]))
    aux = jnp.stack([flat(aux_r), flat(aux_i)], axis=1).reshape(6, -1)
    n_rows = SSM_STEP // mic
    blk = np.arange(n_rows) // rb
    strict = (blk[:, None] == blk[None, :]) & (np.arange(n_rows)[:, None] > np.arange(n_rows)[None, :])
    sums = np.arange(n_rows // rb)[:, None] == blk[None, :]
    pad = np.zeros((-(n_rows + n_rows // rb) % 16, n_rows), bool)
    scan = jnp.asarray(np.concatenate([strict, sums, pad], axis=0).astype(np.float32), BF16)
    sp.update({"w": w, "m": m, "v": v, "aux": aux, "scan": scan,
               "pin_re": flat(in_r), "pin_im": flat(in_i), "pout_re": flat(out_r), "pout_im": flat(out_i)})
    return sp


def _lanes_to_state(h, n_groups, n_state):
    b = h.shape[0]
    h = h.reshape(b, -1, 2, SSM_LANE_CHUNK)
    return (h[:, :, 0, :].reshape(b, n_groups, n_state), h[:, :, 1, :].reshape(b, n_groups, n_state))


def _toeplitz(v, n):
    h = v.shape[0]
    x = jnp.broadcast_to(v[:, None, :], (h, n, 2 * n)).reshape(h, 2 * n * n)
    return x[:, :n * (2 * n - 1)].reshape(h, n, 2 * n - 1)[:, :, :n]


def _prompt_bias_blocks(fvec):
    n = LANES
    h = fvec.shape[0]
    neg = jnp.full((h, n - 1), NEG_INF, F32)
    va = jnp.concatenate([fvec[:, 0:1], neg, jnp.zeros((h, 1), F32), fvec[:, 1:n][:, ::-1]], axis=1)
    vb = jnp.concatenate([fvec[:, 1:n + 1][:, ::-1], jnp.zeros((h, n), F32)], axis=1)
    return jnp.stack([_toeplitz(va, n), _toeplitz(vb, n)], axis=1)


def _decode_bias(fvec, page, dec_seq, n_heads):
    h = fvec.shape[0]
    rows = []
    for i in range(dec_seq):
        last = fvec[:, i + 1:i + 1 + page][:, ::-1]
        new = jnp.concatenate([fvec[:, 0:i + 1][:, ::-1], jnp.full((h, page - i - 1), NEG_INF, F32)], axis=1)
        rows.append(jnp.concatenate([last, new], axis=1))
    per_head = jnp.stack(rows, axis=0)
    same = np.eye(n_heads, dtype=bool)[None, :, None, :]
    near = jnp.where(jnp.asarray(same), per_head[:, :, :, None], NEG_INF)
    near = near.reshape(dec_seq * n_heads, -1)
    mask = np.where(np.broadcast_to(same, (dec_seq, n_heads, page, n_heads)), 0.0, NEG_INF)
    mask = mask.reshape(dec_seq * n_heads, -1).astype(np.float32)
    return jnp.asarray(np.concatenate([mask, mask], axis=0)), jnp.concatenate([near, near], axis=0)


def kernel(x_prompt, x_sample, cache_k, cache_v, state_ssm_re, state_ssm_im, page_table,
           norm_g, w_in, q_norm_g, k_norm_g, lambda_q1, lambda_k1, lambda_q2, lambda_k2,
           subln_g, rel_bias, ssm_a_re, ssm_a_im, ssm_log_dt, ssm_b_re, ssm_b_im,
           ssm_c_re, ssm_c_im, ssm_d, w_glu, b_glu, w_out):
    batch, seq, d_model = x_prompt.shape
    dec_batch, dec_seq, _ = x_sample.shape
    depth, n_pool, page, n_heads, _ = cache_k.shape
    n_pages = page_table.shape[1]
    d_attn = n_heads * V_DIM
    n_groups, n_state = ssm_a_re.shape[1:]
    new_rows = dec_seq * n_heads

    buckets = _bucket_table(2 * LANES)
    far_from = int(np.max(np.nonzero(buckets < N_BUCKETS - 1)[0])) + 1
    assert far_from <= LANES and _bucket_table(seq + page * n_pages)[far_from:].min() == N_BUCKETS - 1
    assert page == LANES and dec_seq < LANES and ATTN_TQ == ATTN_TK

    rel_bias = rel_bias.astype(F32)
    fvec = (rel_bias[buckets].T - rel_bias[N_BUCKETS - 1][:, None]) * LOG2E
    fvec = jnp.where(jnp.asarray(np.arange(2 * LANES) < far_from)[None], fvec, 0.0)
    dtiles = _prompt_bias_blocks(fvec)
    head_mask, near_bias = _decode_bias(fvec, page, dec_seq, n_heads)

    group_avg = jnp.asarray(np.kron(np.eye(2 * LANES // QK_DIM), np.full((QK_DIM, QK_DIM), 1.0 / QK_DIM)), BF16)
    n_rep = d_attn // QK_DIM
    cache_k_rows = cache_k.reshape(depth * n_pool, page * n_heads, V_DIM)
    cache_v_rows = cache_v.reshape(depth * n_pool, page * n_heads, V_DIM)

    hp = x_prompt.reshape(batch * seq, d_model)
    hs = x_sample.reshape(dec_batch * dec_seq, d_model)
    kp_l, vp_l, ks_l, vs_l = [], [], [], []
    srp_l, sip_l, srs_l, sis_l = [], [], [], []
    for l in range(depth):
        lam_init = _lambda_init(l)
        lam = (jnp.exp(jnp.sum(lambda_q1[l].astype(F32) * lambda_k1[l].astype(F32)))
               - jnp.exp(jnp.sum(lambda_q2[l].astype(F32) * lambda_k2[l].astype(F32))) + lam_init)
        lam = lam.reshape(1).astype(F32)
        ng = norm_g[l].astype(F32).reshape(1, d_model)
        w_bf = w_in[l].astype(BF16)
        gq = jnp.tile(q_norm_g[l].astype(F32), n_rep).reshape(1, d_attn) * (QK_DIM ** -0.5 * LOG2E)
        gk = jnp.tile(k_norm_g[l].astype(F32), n_rep).reshape(1, d_attn)
        sg = (subln_g[l].astype(F32) * (1.0 - lam_init)).reshape(1, V_DIM)
        wo_bf = w_out[l].astype(BF16)
        sp = _ssm_params(ssm_a_re[l], ssm_a_im[l], ssm_log_dt[l], ssm_b_re[l], ssm_b_im[l],
                         ssm_c_re[l], ssm_c_im[l], ssm_d[l], w_glu[l], b_glu[l])

        q1, q2, k4, kb, v4, va, ga, u4, gs = _inproj(hp, ng, w_bf, gq, gk, group_avg, head_rows=False)
        sq1, sq2, sk4, sv4, sga, su4, sgs = _inproj(hs, ng, w_bf, gq, gk, group_avg, head_rows=True)
        r3 = lambda a: a.reshape(batch, seq, a.shape[-1])
        o_a, o_dec = _attention(lam, sg, r3(q1), r3(q2), r3(kb), r3(va), dtiles, r3(ga),
                                page_table + l * n_pool, sq1, sq2, sk4, sv4, sga, head_mask, near_bias,
                                cache_k_rows, cache_v_rows, new_rows)
        hp3, hfin = _ssm_prompt_outproj(r3(hp), o_a, u4.reshape(batch, -1, LANES), r3(gs), sp, wo_bf)
        hp = hp3.reshape(batch * seq, d_model)
        kp_l.append(k4.reshape(batch, seq, n_heads, V_DIM).astype(cache_k.dtype))
        vp_l.append(v4.reshape(batch, seq, n_heads, V_DIM).astype(cache_v.dtype))
        hr_p, hi_p = _lanes_to_state(hfin.reshape(batch, -1), n_groups, n_state)
        srp_l.append(hr_p.astype(state_ssm_re.dtype))
        sip_l.append(hi_p.astype(state_ssm_im.dtype))

        hs, hr_s, hi_s = _ssm_sample_outproj(
            hs, o_dec.reshape(dec_batch * dec_seq, d_attn), su4, sgs,
            state_ssm_re[l].astype(F32).reshape(dec_batch, -1), state_ssm_im[l].astype(F32).reshape(dec_batch, -1),
            sp, wo_bf)
        ks_l.append(sk4.reshape(dec_batch, dec_seq, n_heads, V_DIM).astype(cache_k.dtype))
        vs_l.append(sv4.reshape(dec_batch, dec_seq, n_heads, V_DIM).astype(cache_v.dtype))
        srs_l.append(hr_s.reshape(dec_batch, n_groups, n_state).astype(state_ssm_re.dtype))
        sis_l.append(hi_s.reshape(dec_batch, n_groups, n_state).astype(state_ssm_im.dtype))

    y_prompt = hp.reshape(batch, seq, d_model).astype(x_prompt.dtype)
    y_sample = hs.reshape(dec_batch, dec_seq, d_model).astype(x_sample.dtype)
    return (y_prompt, y_sample, jnp.stack(kp_l), jnp.stack(vp_l), jnp.stack(ks_l), jnp.stack(vs_l),
            jnp.stack(srp_l), jnp.stack(sip_l), jnp.stack(srs_l), jnp.stack(sis_l))
```

```python
import functools
import math

import numpy as np
import jax
import jax.numpy as jnp
from jax import lax
from jax.experimental import pallas as pl
from jax.experimental.pallas import tpu as pltpu

F32 = jnp.float32
BF16 = jnp.bfloat16

QK_DIM = 64
V_DIM = 2 * QK_DIM
N_BUCKETS = 32
MAX_DISTANCE = 128
SSM_GROUP = 16
SSM_STATE = 64
EPS = 1e-6
NEG_INF = -1e30
LOG2E = math.log2(math.e)
GELU_C0 = math.sqrt(2.0 / math.pi)
GELU_C1 = GELU_C0 * 0.044715

LANES = 128
VMEM_LIMIT_BYTES = 56 * 1024 * 1024

ROW_TILE = 1024
ATTN_TQ = 512
ATTN_TK = 512
ATTN_UNROLL = 8
ATTN_NEAR_TASKS = 4
DECODE_SEQS_PER_STEP = 2
SSM_MICRO = 4
SSM_STEP = 1024
SSM_SCAN_ROWS = 32
SSM_LANE_CHUNK = 512


def _lambda_init(layer):
    return 0.8 - 0.6 * math.exp(-0.3 * layer)


def _bucket_table(n_max):
    n = np.arange(n_max)
    max_exact = N_BUCKETS // 2
    nf = np.maximum(n, 1).astype(np.float32)
    large = max_exact + (np.log(nf / np.float32(max_exact)) / np.float32(math.log(MAX_DISTANCE / max_exact))
                         * np.float32(N_BUCKETS - max_exact)).astype(np.int32)
    large = np.minimum(large, N_BUCKETS - 1)
    return np.where(n < max_exact, n, large).astype(np.int32)


def _silu(x):
    return (0.5 * x) * (1.0 + jnp.tanh(0.5 * x))


def _store_head_rows(ref, val, n_heads):
    rows = val.shape[0]
    for h in range(n_heads):
        ref[pl.ds(h, rows, stride=n_heads), :] = val[:, h * V_DIM:(h + 1) * V_DIM]


def _inproj_kernel(x_ref, ng_ref, w_ref, gq_ref, gk_ref, gavg_ref, *out_refs, head_rows):
    x = x_ref[...]
    ms = jnp.mean(x * x, axis=-1, keepdims=True)
    xb = (x * lax.rsqrt(ms + EPS) * ng_ref[...]).astype(BF16)
    d_seg = gq_ref.shape[1]
    n_heads = d_seg // V_DIM

    def seg(i):
        return jnp.dot(xb, w_ref[:, i * d_seg:(i + 1) * d_seg], preferred_element_type=F32)

    def group_norm(t, g):
        sq = (t * t).astype(BF16)
        wg = gavg_ref.shape[0]
        msq = jnp.concatenate([jnp.dot(sq[:, c:c + wg], gavg_ref[...], preferred_element_type=F32)
                               for c in range(0, d_seg, wg)], axis=1)
        return t * lax.rsqrt(msq + EPS) * g

    q = group_norm(seg(0), gq_ref[...])
    lane = lax.broadcasted_iota(jnp.int32, q.shape, 1)
    first = (lane % V_DIM) < QK_DIM
    qa = jnp.where(first, q, 0.0)
    qb = jnp.where(first, 0.0, q)
    k = group_norm(seg(1), gk_ref[...])
    v = seg(2)
    ga = seg(3)
    if head_rows:
        q1_ref, q2_ref, k4_ref, v4_ref, ga_ref, u_ref, gs_ref = out_refs
        _store_head_rows(q1_ref, qa, n_heads)
        _store_head_rows(q2_ref, qb, n_heads)
        _store_head_rows(ga_ref, ga, n_heads)
    else:
        q1_ref, q2_ref, k4_ref, kb_ref, v4_ref, va_ref, ga_ref, u_ref, gs_ref = out_refs
        q1_ref[...] = qa.astype(BF16)
        q2_ref[...] = qb.astype(BF16)
        kb_ref[...] = k.astype(BF16)
        vb = v.astype(BF16)
        ones = jnp.ones((v.shape[0], V_DIM), BF16)
        pieces = []
        for h in range(n_heads):
            pieces += [vb[:, h * V_DIM:(h + 1) * V_DIM], ones]
        va_ref[...] = jnp.concatenate(pieces, axis=1)
        ga_ref[...] = ga
    _store_head_rows(k4_ref, k, n_heads)
    _store_head_rows(v4_ref, v, n_heads)
    _store_head_rows(u_ref, seg(4), d_seg // LANES)
    gs_ref[...] = seg(5)


def _inproj(x, ng, w_bf, gq, gk, gavg, head_rows):
    n, d_model = x.shape
    d_seg = gq.shape[1]
    n_heads = d_seg // V_DIM
    tm = min(ROW_TILE, n)
    full = lambda a: pl.BlockSpec(a.shape, lambda i: (0,) * a.ndim)
    wide = lambda w, dt: (jax.ShapeDtypeStruct((n, w), dt), pl.BlockSpec((tm, w), lambda i: (i, 0)))
    tall = lambda dt: (jax.ShapeDtypeStruct((n * n_heads, V_DIM), dt),
                       pl.BlockSpec((tm * n_heads, V_DIM), lambda i: (i, 0)))
    if head_rows:
        outs = [tall(F32), tall(F32), tall(F32), tall(F32), tall(F32), tall(F32), wide(d_seg, F32)]
    else:
        outs = [wide(d_seg, BF16), wide(d_seg, BF16), tall(F32), wide(d_seg, BF16), tall(F32),
                wide(2 * d_seg, BF16), wide(d_seg, F32), tall(F32), wide(d_seg, F32)]
    return pl.pallas_call(
        functools.partial(_inproj_kernel, head_rows=head_rows),
        grid=(n // tm,),
        in_specs=[pl.BlockSpec((tm, d_model), lambda i: (i, 0)),
                  full(ng), full(w_bf), full(gq), full(gk), full(gavg)],
        out_specs=[o[1] for o in outs],
        out_shape=[o[0] for o in outs],
        compiler_params=pltpu.CompilerParams(
            dimension_semantics=("arbitrary",), vmem_limit_bytes=VMEM_LIMIT_BYTES),
        name="inproj_samples" if head_rows else "inproj_prompt",
    )(x, ng, w_bf, gq, gk, gavg)


def _diff_epilogue(o1, o2, lam, sg, ga):
    od = o1 - lam * o2
    ms = jnp.mean(od * od, axis=-1, keepdims=True)
    return od * lax.rsqrt(ms + EPS) * sg * _silu(ga)


def _attention_kernel(tasks_ref, pt_ref, lam_ref,
                      q1a_ref, q2a_ref, q1b_ref, q2b_ref, k_ref, v_ref, ab_ref, gaa_ref, gab_ref, sg_ref,
                      dq1_ref, dq2_ref, dkn_ref, dvn_ref, dga_ref, hm_ref, near_ref, ck_hbm, cv_hbm,
                      o_ref, od_ref,
                      s_scr, d_scr, m_scr, acc_scr, qs_scr, kbuf, vbuf, knew, vnew, ksem, vsem):
    tq = s_scr.shape[1] // 2
    tk = s_scr.shape[2]
    p = pl.program_id(2)
    n_steps = pl.num_programs(2)
    n_tasks = tasks_ref.shape[1]
    n_far = n_tasks - 1 - ATTN_NEAR_TASKS
    unroll = max([u for u in range(2, ATTN_UNROLL + 1, 2) if n_far % u == 0], default=0)
    step = (pl.program_id(0) * pl.num_programs(1) + pl.program_id(1)) * n_steps + p
    n_pages = kbuf.shape[1]
    page_rows = kbuf.shape[2]
    total_steps = pt_ref.shape[0] // DECODE_SEQS_PER_STEP
    lam = lam_ref[0]

    def k_copy(st, local, j):
        return pltpu.make_async_copy(ck_hbm.at[pt_ref[st * DECODE_SEQS_PER_STEP + local, j]],
                                     kbuf.at[local, j], ksem.at[0])

    def v_copy(st, local, j):
        return pltpu.make_async_copy(cv_hbm.at[pt_ref[st * DECODE_SEQS_PER_STEP + local, j]],
                                     vbuf.at[local, j], vsem.at[0])

    def start_fetch(st):
        for local in range(DECODE_SEQS_PER_STEP):
            for j in range(n_pages):
                k_copy(st, local, j).start()
                v_copy(st, local, j).start()

    def wait_fetch(st):
        for local in range(DECODE_SEQS_PER_STEP):
            for j in range(n_pages):
                k_copy(st, local, j).wait()
                v_copy(st, local, j).wait()

    def decode(local, rows):
        r0 = local * rows
        qx = jnp.concatenate([dq1_ref[r0:r0 + rows, :], dq2_ref[r0:r0 + rows, :]], axis=0).astype(BF16)
        knew[local, 0:rows, :] = dkn_ref[r0:r0 + rows, :].astype(BF16)
        vnew[local, 0:rows, :] = dvn_ref[r0:r0 + rows, :].astype(BF16)
        nt = (((1,), (1,)), ((), ()))
        s_tiles = [lax.dot_general(qx, kbuf[local, j].astype(BF16), nt, preferred_element_type=F32)
                   for j in range(n_pages)]
        s_tiles.append(lax.dot_general(qx, knew[local], nt, preferred_element_type=F32))
        head_mask = hm_ref[...]
        s = jnp.concatenate([t + head_mask for t in s_tiles[:n_pages - 1]]
                            + [jnp.concatenate(s_tiles[n_pages - 1:], axis=1) + near_ref[...]], axis=1)
        m = jnp.max(s, axis=-1, keepdims=True)
        pr = jnp.exp2(s - m)
        l_sum = jnp.sum(pr, axis=-1, keepdims=True)
        pb = pr.astype(BF16)
        acc = jnp.dot(pb[:, n_pages * page_rows:], vnew[local], preferred_element_type=F32)
        for j in range(n_pages):
            acc = acc + jnp.dot(pb[:, j * page_rows:(j + 1) * page_rows], vbuf[local, j].astype(BF16),
                                preferred_element_type=F32)
        o = acc / l_sum
        od_ref[r0:r0 + rows, :] = _diff_epilogue(o[:rows], o[rows:], lam, sg_ref[...], dga_ref[r0:r0 + rows, :])

    @pl.when(step == 0)
    def _():
        start_fetch(0)
        knew[...] = jnp.zeros(knew.shape, knew.dtype)
        vnew[...] = jnp.zeros(vnew.shape, vnew.dtype)

    @pl.when(p == 0)
    def _():
        for t in range(2):
            for a in range(tq // LANES):
                for b in range(tk // LANES):
                    delta = a - b + t * (tk // LANES)
                    if delta in (0, 1):
                        blk = ab_ref[delta]
                    else:
                        blk = jnp.full((LANES, LANES), NEG_INF if delta < 0 else 0.0, F32)
                    d_scr[t, a * LANES:(a + 1) * LANES, b * LANES:(b + 1) * LANES] = blk
        d_scr[2] = jnp.zeros((tq, tk), F32)

    qs_scr[0] = jnp.concatenate([q1a_ref[...], q2a_ref[...]], axis=0)
    qs_scr[1] = jnp.concatenate([q1b_ref[...], q2b_ref[...]], axis=0)

    def task(kind, i):
        return tasks_ref[kind * n_steps + p, i]

    def produce(i, slot, near):
        k0 = pl.multiple_of(task(1, i) * tk, tk)
        s = lax.dot_general(qs_scr[task(3, i)], k_ref[pl.ds(k0, tk), :], (((1,), (1,)), ((), ())),
                            preferred_element_type=F32)
        if near:
            bias = d_scr[task(2, i)]
            s = s + jnp.concatenate([bias, bias], axis=0)
        s_scr[slot] = s

    def consume(i, slot, first=False):
        k0 = pl.multiple_of(task(1, i) * tk, tk)
        a = task(3, i)
        s = s_scr[slot]
        row_max = jnp.max(s, axis=-1, keepdims=True)
        if first:
            m_new = jnp.broadcast_to(row_max, (s.shape[0], LANES))
        else:
            m_old = m_scr[a]
            m_new = jnp.maximum(m_old, row_max)
        pr = jnp.exp2(s - jnp.concatenate([m_new] * (tk // LANES), axis=1))
        pv = jnp.dot(pr.astype(BF16), v_ref[pl.ds(k0, tk), :], preferred_element_type=F32)
        if first:
            acc_scr[a] = pv
        else:
            alpha = jnp.exp2(m_old - m_new)
            acc_scr[a] = acc_scr[a] * jnp.concatenate([alpha] * (acc_scr.shape[2] // LANES), axis=1) + pv
        m_scr[a] = m_new

    wait_fetch(step)
    for local in range(DECODE_SEQS_PER_STEP):
        decode(local, od_ref.shape[0] // DECODE_SEQS_PER_STEP)
    produce(0, 0, True)
    for i in range(ATTN_NEAR_TASKS):
        produce(i + 1, (i + 1) % 2, i + 1 < ATTN_NEAR_TASKS)
        consume(i, i % 2, first=i < 2)

    @pl.when(step + 1 < total_steps)
    def _():
        start_fetch(step + 1)

    def body(j, carry):
        for u in range(unroll):
            i = ATTN_NEAR_TASKS + j * unroll + u
            produce(i + 1, (u + 1) % 2, False)
            consume(i, u % 2)
        return carry

    if n_far:
        lax.fori_loop(0, n_far // unroll, body, 0)
    consume(n_tasks - 1, (n_tasks - 1) % 2)

    for a, ga_ref in enumerate((gaa_ref, gab_ref)):
        q0 = pl.multiple_of(task(0, a) * tq, tq)
        acc = acc_scr[a]
        o = acc[:, :V_DIM] / acc[:, V_DIM:]
        out = _diff_epilogue(o[:tq], o[tq:], lam, sg_ref[...], ga_ref[...])
        o_ref[pl.ds(q0, tq), :] = out.astype(o_ref.dtype)


def _attn_tasks(n_q):
    n_steps = n_q // 2
    tab = np.zeros((4, n_steps, n_q + 1), np.int32)
    for p in range(n_steps):
        near, far = [], []
        for acc, qt in enumerate((p, n_q - 1 - p)):
            tasks = [(qt, qt - t, min(t, 2), acc) for t in range(qt + 1)]
            near.append(tasks[:2])
            far += tasks[2:]
        while len(near[0]) < 2:
            near[0].append(far.pop(0))
        order = [near[0][0], near[1][0], near[0][1], near[1][1]] + far
        assert len(order) == n_q + 1 and len(near[1]) == 2 and ATTN_NEAR_TASKS == 4
        tab[:, p, :] = np.array(order).T
    return tab.reshape(4 * n_steps, n_q + 1)


def _attention(lam, sg, q1, q2, kb, va, dtiles, ga,
               page_table, dq1, dq2, dkn, dvn, dga, head_mask, near_bias, cache_k, cache_v, new_rows):
    b, l, d_attn = q1.shape
    n_heads = d_attn // V_DIM
    tq, tk = ATTN_TQ, ATTN_TK
    n_q = l // tq
    n_steps = n_q // 2
    n_seq, n_pages = page_table.shape
    page_rows = cache_k.shape[1]
    total_steps = b * n_heads * n_steps
    assert tq == tk and n_q % 2 == 0 and n_q >= ATTN_NEAR_TASKS
    assert n_seq == total_steps * DECODE_SEQS_PER_STEP
    tasks = jnp.asarray(_attn_tasks(n_q))
    dec_rows = DECODE_SEQS_PER_STEP * new_rows

    def gstep(bi, h, pi):
        return (bi * n_heads + h) * n_steps + pi

    whole = lambda w: pl.BlockSpec((None, l, w), lambda bi, h, pi, *_: (bi, 0, h))
    tile_a = pl.BlockSpec((None, tq, V_DIM), lambda bi, h, pi, *_: (bi, pi, h))
    tile_b = pl.BlockSpec((None, tq, V_DIM), lambda bi, h, pi, *_: (bi, n_q - 1 - pi, h))
    dec = pl.BlockSpec((dec_rows, V_DIM), lambda bi, h, pi, *_: (gstep(bi, h, pi), 0))
    const2 = lambda a: pl.BlockSpec(a.shape, lambda bi, h, pi, *_: (0, 0))
    grid_spec = pltpu.PrefetchScalarGridSpec(
        num_scalar_prefetch=2,
        grid=(b, n_heads, n_steps),
        in_specs=[
            pl.BlockSpec(memory_space=pltpu.SMEM),
            tile_a, tile_a, tile_b, tile_b, whole(V_DIM), whole(2 * V_DIM),
            pl.BlockSpec((None, 2, LANES, LANES), lambda bi, h, pi, *_: (h, 0, 0, 0)),
            tile_a, tile_b, const2(sg),
            dec, dec, dec, dec, dec, const2(head_mask), const2(near_bias),
            pl.BlockSpec(memory_space=pl.ANY), pl.BlockSpec(memory_space=pl.ANY),
        ],
        out_specs=[whole(V_DIM), dec],
        scratch_shapes=[
            pltpu.VMEM((2, 2 * tq, tk), F32), pltpu.VMEM((3, tq, tk), F32),
            pltpu.VMEM((2, 2 * tq, LANES), F32), pltpu.VMEM((2, 2 * tq, 2 * V_DIM), F32),
            pltpu.VMEM((2, 2 * tq, V_DIM), BF16),
            pltpu.VMEM((DECODE_SEQS_PER_STEP, n_pages, page_rows, V_DIM), cache_k.dtype),
            pltpu.VMEM((DECODE_SEQS_PER_STEP, n_pages, page_rows, V_DIM), cache_v.dtype),
            pltpu.VMEM((DECODE_SEQS_PER_STEP, page_rows, V_DIM), BF16),
            pltpu.VMEM((DECODE_SEQS_PER_STEP, page_rows, V_DIM), BF16),
            pltpu.SemaphoreType.DMA((1,)), pltpu.SemaphoreType.DMA((1,)),
        ],
    )
    return pl.pallas_call(
        _attention_kernel,
        grid_spec=grid_spec,
        out_shape=[jax.ShapeDtypeStruct((b, l, d_attn), BF16),
                   jax.ShapeDtypeStruct((n_seq * new_rows, V_DIM), F32)],
        compiler_params=pltpu.CompilerParams(
            dimension_semantics=("arbitrary", "arbitrary", "arbitrary"),
            vmem_limit_bytes=VMEM_LIMIT_BYTES),
        name="attention",
    )(tasks, page_table, lam, q1, q2, q1, q2, kb, va, dtiles, ga, ga, sg,
      dq1, dq2, dkn, dvn, dga, head_mask, near_bias, cache_k, cache_v)


def _ssm_tail(y, u, gs, dskip_ref, wglu_half_ref, bglu_half_ref):
    x = y + dskip_ref[...] * u
    inner = x * (GELU_C1 * (x * x) + GELU_C0)
    z = (0.5 * x) * (1.0 + jnp.tanh(inner))
    t_glu = jnp.tanh(jnp.dot(z.astype(BF16), wglu_half_ref[...], preferred_element_type=F32) + bglu_half_ref[...])
    t_gate = jnp.tanh(0.5 * gs)
    return ((z * gs) * 0.25) * (1.0 + t_glu) * (1.0 + t_gate)


def _ssm_prompt_kernel(x_ref, oa_ref, u4_ref, gs_ref, w_ref, m_ref, v_ref,
                       pin_re_ref, pin_im_ref, pout_re_ref, pout_im_ref,
                       aux_ref, scan_ref, dskip_ref, wglu_ref, bglu_ref, wo_ref, o_ref, hfin_ref,
                       carry_scr, y4_scr):
    n_q = w_ref.shape[0]
    mic = SSM_MICRO
    sc = SSM_LANE_CHUNK
    ts = gs_ref.shape[0]
    r = ts // mic
    rb = pin_re_ref.shape[0]
    n_blk = r // rb
    tile_rows = lambda a: jnp.concatenate([a] * n_blk, axis=0)
    c = pl.program_id(1)

    @pl.when(c == 0)
    def _():
        carry_scr[...] = jnp.zeros(carry_scr.shape, F32)

    for q in range(n_q):
        re_l = slice(2 * q * sc, (2 * q + 1) * sc)
        im_l = slice((2 * q + 1) * sc, 2 * (q + 1) * sc)
        st_l = slice(q * sc, (q + 1) * sc)
        x = jnp.concatenate([u4_ref[pl.ds(n_q * s + q, r, stride=n_q * mic), :] for s in range(mic)],
                            axis=1).astype(BF16)
        e = jnp.dot(x, w_ref[q], preferred_element_type=F32)
        er, em = e[:, :sc], e[:, sc:]
        pw = lambda ref, i: ref[i:i + 1, st_l]
        pir, pii = tile_rows(pin_re_ref[:, st_l]), tile_rows(pin_im_ref[:, st_l])
        xs = jnp.concatenate([er * pir - em * pii, er * pii + em * pir], axis=1).astype(BF16)
        cs = jnp.dot(scan_ref[...], xs, preferred_element_type=F32)
        tot_r, tot_m = cs[r:r + n_blk, :sc], cs[r:r + n_blk, sc:]
        t_r = tot_r * pw(aux_ref, 2) - tot_m * pw(aux_ref, 3)
        t_m = tot_r * pw(aux_ref, 3) + tot_m * pw(aux_ref, 2)
        ar, am = pw(aux_ref, 0), pw(aux_ref, 1)
        br, bm = pw(aux_ref, 4), pw(aux_ref, 5)
        h_r, h_m = carry_scr[:, re_l], carry_scr[:, im_l]
        base_r, base_m = [], []
        for blk in range(n_blk):
            base_r.append(jnp.broadcast_to(ar * h_r - am * h_m, (rb, sc)))
            base_m.append(jnp.broadcast_to(ar * h_m + am * h_r, (rb, sc)))
            h_r, h_m = (br * h_r - bm * h_m + t_r[blk:blk + 1, :], br * h_m + bm * h_r + t_m[blk:blk + 1, :])
        carry_scr[:, re_l] = h_r
        carry_scr[:, im_l] = h_m
        hfin_ref[:, re_l] = h_r
        hfin_ref[:, im_l] = h_m
        sr = cs[:r, :sc] + jnp.concatenate(base_r, axis=0)
        sm = cs[:r, sc:] + jnp.concatenate(base_m, axis=0)
        por, poi = tile_rows(pout_re_ref[:, st_l]), tile_rows(pout_im_ref[:, st_l])
        hp = jnp.concatenate([sr * por - sm * poi, sr * poi + sm * por], axis=1).astype(BF16)
        y = (jnp.dot(x, m_ref[q], preferred_element_type=F32)
             + jnp.dot(hp, v_ref[q], preferred_element_type=F32))
        for s in range(mic):
            y4_scr[pl.ds(n_q * s + q, r, stride=n_q * mic), :] = y[:, s * LANES:(s + 1) * LANES]
    y = jnp.concatenate([y4_scr[pl.ds(q, ts, stride=n_q), :] for q in range(n_q)], axis=1)
    u = jnp.concatenate([u4_ref[pl.ds(q, ts, stride=n_q), :] for q in range(n_q)], axis=1)
    o_s = _ssm_tail(y, u, gs_ref[...], dskip_ref, wglu_ref, bglu_ref).astype(BF16)
    d_a = oa_ref.shape[1]
    out = x_ref[...] + jnp.dot(oa_ref[...], wo_ref[:d_a, :], preferred_element_type=F32)
    o_ref[...] = out + jnp.dot(o_s, wo_ref[d_a:, :], preferred_element_type=F32)


def _ssm_prompt_outproj(x, oa, u4, gs, sp, wo_bf):
    b, l, d_ssm = gs.shape
    n_q = d_ssm // LANES
    ts = SSM_STEP
    n_state2 = 2 * sp["aux"].shape[1]
    row = lambda w: pl.BlockSpec((None, ts, w), lambda bi, ci: (bi, ci, 0))
    row4 = pl.BlockSpec((None, ts * n_q, LANES), lambda bi, ci: (bi, ci, 0))
    full = lambda a: pl.BlockSpec(a.shape, lambda bi, ci: (0,) * a.ndim)
    names = ["w", "m", "v", "pin_re", "pin_im", "pout_re", "pout_im", "aux", "scan", "dskip", "wglu", "bglu"]
    return pl.pallas_call(
        _ssm_prompt_kernel,
        grid=(b, l // ts),
        in_specs=[row(x.shape[2]), row(oa.shape[2]), row4, row(d_ssm)] + [full(sp[n]) for n in names]
        + [full(wo_bf)],
        out_specs=[row(x.shape[2]), pl.BlockSpec((None, 1, n_state2), lambda bi, ci: (bi, 0, 0))],
        out_shape=[jax.ShapeDtypeStruct(x.shape, F32),
                   jax.ShapeDtypeStruct((b, 1, n_state2), F32)],
        scratch_shapes=[pltpu.VMEM((1, n_state2), F32), pltpu.VMEM((ts * n_q, LANES), F32)],
        compiler_params=pltpu.CompilerParams(
            dimension_semantics=("arbitrary", "arbitrary"), vmem_limit_bytes=VMEM_LIMIT_BYTES),
        name="ssm_prompt_outproj",
    )(x, oa, u4, gs, *[sp[n] for n in names], wo_bf)


def _ssm_sample_kernel(x_ref, oa_ref, u4_ref, gs_ref, h0r_ref, h0m_ref, w_ref, m_ref, v_ref, aux_ref,
                       dskip_ref, wglu_ref, bglu_ref, wo_ref, o_ref, hr_ref, hm_ref, y4_scr):
    n_seq = h0r_ref.shape[0]
    n_q = w_ref.shape[0]
    mic = SSM_MICRO
    sc = SSM_LANE_CHUNK
    n = gs_ref.shape[0]
    for q in range(n_q):
        st_l = slice(q * sc, (q + 1) * sc)
        x = jnp.concatenate([u4_ref[pl.ds(n_q * s + q, n_seq, stride=n_q * mic), :] for s in range(mic)],
                            axis=1).astype(BF16)
        e = jnp.dot(x, w_ref[q], preferred_element_type=F32)
        h_r, h_m = h0r_ref[:, st_l], h0m_ref[:, st_l]
        ar, am = aux_ref[0:1, st_l], aux_ref[1:2, st_l]
        hr_ref[:, st_l] = ar * h_r - am * h_m + e[:, :sc]
        hm_ref[:, st_l] = ar * h_m + am * h_r + e[:, sc:]
        hp = jnp.concatenate([h_r, h_m], axis=1).astype(BF16)
        y = (jnp.dot(x, m_ref[q], preferred_element_type=F32)
             + jnp.dot(hp, v_ref[q], preferred_element_type=F32))
        for s in range(mic):
            y4_scr[pl.ds(n_q * s + q, n_seq, stride=n_q * mic), :] = y[:, s * LANES:(s + 1) * LANES]
    y = jnp.concatenate([y4_scr[pl.ds(q, n, stride=n_q), :] for q in range(n_q)], axis=1)
    u = jnp.concatenate([u4_ref[pl.ds(q, n, stride=n_q), :] for q in range(n_q)], axis=1)
    o_s = _ssm_tail(y, u, gs_ref[...], dskip_ref, wglu_ref, bglu_ref).astype(BF16)
    d_a = oa_ref.shape[1]
    out = x_ref[...] + jnp.dot(oa_ref[...].astype(BF16), wo_ref[:d_a, :], preferred_element_type=F32)
    o_ref[...] = out + jnp.dot(o_s, wo_ref[d_a:, :], preferred_element_type=F32)


def _ssm_sample_outproj(x, oa, u4, gs, h0_re, h0_im, sp, wo_bf):
    n = gs.shape[0]
    assert n == h0_re.shape[0] * SSM_MICRO, "one micro-chunk of new steps per sequence"
    names = ["w", "m", "v", "aux", "dskip", "wglu", "bglu"]
    args = [x, oa, u4, gs, h0_re, h0_im] + [sp[k] for k in names] + [wo_bf]
    full = lambda a: pl.BlockSpec(a.shape, lambda i: (0,) * a.ndim)
    state = jax.ShapeDtypeStruct(h0_re.shape, F32)
    return pl.pallas_call(
        _ssm_sample_kernel,
        grid=(1,),
        in_specs=[full(a) for a in args],
        out_specs=[full(x), full(h0_re), full(h0_im)],
        out_shape=[jax.ShapeDtypeStruct(x.shape, F32), state, state],
        scratch_shapes=[pltpu.VMEM(u4.shape, F32)],
        compiler_params=pltpu.CompilerParams(
            dimension_semantics=("arbitrary",), vmem_limit_bytes=VMEM_LIMIT_BYTES),
        name="ssm_sample_outproj",
    )(*args)


def _ssm_params(a_re, a_im, log_dt, b_re, b_im, c_re, c_im, d_skip, w_glu, b_glu):
    n_groups, n_state = a_re.shape
    g_per_q = LANES // SSM_GROUP
    n_q = n_groups // g_per_q
    dt = jnp.exp(log_dt.astype(F32))[:, None]
    a_re = a_re.astype(F32)
    a_im = a_im.astype(F32)
    mag = jnp.exp(a_re * dt)
    abar_re = mag * jnp.cos(a_im * dt)
    abar_im = mag * jnp.sin(a_im * dt)
    nr = abar_re - 1.0
    den = a_re * a_re + a_im * a_im
    coef_re = (nr * a_re + abar_im * a_im) / den
    coef_im = (abar_im * a_re - nr * a_im) / den
    b_re = b_re.astype(F32)
    b_im = b_im.astype(F32)
    bbar_re = coef_re[..., None] * b_re - coef_im[..., None] * b_im
    bbar_im = coef_re[..., None] * b_im + coef_im[..., None] * b_re

    same_group = (np.arange(g_per_q * SSM_GROUP)[:, None] // SSM_GROUP
                  == np.arange(g_per_q * n_state)[None, :] // n_state)

    def lane_tile(a, reps):
        w = a.shape[-1]
        return jnp.matmul(a, jnp.asarray(np.tile(np.eye(w, dtype=np.float32), (1, reps))),
                          precision=lax.Precision.HIGHEST)

    def rows_in(t):
        n = t.shape[0]
        t = lane_tile(t.reshape(n, n_q, g_per_q * n_state, SSM_GROUP), g_per_q)
        t = jnp.swapaxes(t * jnp.asarray(same_group.T, F32), -1, -2)
        return jnp.swapaxes(t, 0, 1).reshape(n_q, n * LANES, g_per_q * n_state)

    def cols_out(t):
        n = t.shape[0]
        t = lane_tile(t.reshape(n, n_q, g_per_q * SSM_GROUP, n_state), g_per_q)
        t = jnp.swapaxes(t * jnp.asarray(same_group, F32), -1, -2)
        return jnp.transpose(t, (1, 2, 0, 3)).reshape(n_q, g_per_q * n_state, n * LANES)

    sp = {
        "dskip": d_skip.astype(F32).reshape(1, -1),
        "wglu": (0.5 * w_glu.astype(F32)).astype(BF16),
        "bglu": 0.5 * b_glu.astype(F32).reshape(1, -1),
    }

    mic = SSM_MICRO
    lr_step = a_re * dt
    th_step = a_im * dt

    def power(t):
        t = jnp.asarray(np.asarray(t, np.float32))[:, None, None]
        mag = jnp.exp(t * lr_step)
        return mag * jnp.cos(t * th_step), mag * jnp.sin(t * th_step)

    pw_r, pw_i = power(np.arange(mic + 1))
    zr, zi = power(np.arange(mic - 1, -1, -1))
    wb_r = zr[..., None] * bbar_re[None] - zi[..., None] * bbar_im[None]
    wb_i = zr[..., None] * bbar_im[None] + zi[..., None] * bbar_re[None]
    w = jnp.concatenate([rows_in(wb_r), rows_in(wb_i)], axis=2).astype(BF16)

    c_re = c_re.astype(F32)
    c_im = c_im.astype(F32)
    pr1, pi1 = pw_r[1:mic + 1][:, :, None, :], pw_i[1:mic + 1][:, :, None, :]
    v_r = c_re[None] * pr1 - c_im[None] * pi1
    v_i = c_re[None] * pi1 + c_im[None] * pr1
    v = jnp.concatenate([cols_out(v_r), cols_out(-v_i)], axis=1).astype(BF16)

    tb_r = pw_r[:mic, :, :, None] * bbar_re[None] - pw_i[:mic, :, :, None] * bbar_im[None]
    tb_i = pw_r[:mic, :, :, None] * bbar_im[None] + pw_i[:mic, :, :, None] * bbar_re[None]
    taps = (jnp.sum(c_re[None, :, None, :, :] * jnp.swapaxes(tb_r, 2, 3)[:, :, :, None, :], axis=-1)
            - jnp.sum(c_im[None, :, None, :, :] * jnp.swapaxes(tb_i, 2, 3)[:, :, :, None, :], axis=-1))
    zero = jnp.zeros_like(taps[0])
    grid = jnp.stack([jnp.stack([taps[t - s] if t >= s else zero for t in range(mic)], axis=0)
                      for s in range(mic)], axis=0)
    grid = lane_tile(grid.reshape(mic, mic, n_q, LANES, SSM_GROUP), g_per_q)
    grid = grid * jnp.asarray(same_group[:, ::n_state // SSM_GROUP], F32)
    m = jnp.transpose(grid, (2, 0, 3, 1, 4)).reshape(n_q, mic * LANES, mic * LANES).astype(BF16)

    rb = SSM_SCAN_ROWS
    k = np.arange(rb)
    in_r, in_i = power(-mic * k)
    out_r, out_i = power(mic * (k - 1))
    flat = lambda a: a.reshape(a.shape[0], -1)
    aux_r, aux_i = power(np.array([mic, mic * (rb - 1), mic * rb]))
    aux = jnp.stack([flat(aux_r), flat(aux_i)], axis=1).reshape(6, -1)
    n_rows = SSM_STEP // mic
    blk = np.arange(n_rows) // rb
    strict = (blk[:, None] == blk[None, :]) & (np.arange(n_rows)[:, None] > np.arange(n_rows)[None, :])
    sums = np.arange(n_rows // rb)[:, None] == blk[None, :]
    pad = np.zeros((-(n_rows + n_rows // rb) % 16, n_rows), bool)
    scan = jnp.asarray(np.concatenate([strict, sums, pad], axis=0).astype(np.float32), BF16)
    sp.update({"w": w, "m": m, "v": v, "aux": aux, "scan": scan,
               "pin_re": flat(in_r), "pin_im": flat(in_i), "pout_re": flat(out_r), "pout_im": flat(out_i)})
    return sp


def _lanes_to_state(h, n_groups, n_state):
    b = h.shape[0]
    h = h.reshape(b, -1, 2, SSM_LANE_CHUNK)
    return (h[:, :, 0, :].reshape(b, n_groups, n_state), h[:, :, 1, :].reshape(b, n_groups, n_state))


def _toeplitz(v, n):
    h = v.shape[0]
    x = jnp.broadcast_to(v[:, None, :], (h, n, 2 * n)).reshape(h, 2 * n * n)
    return x[:, :n * (2 * n - 1)].reshape(h, n, 2 * n - 1)[:, :, :n]


def _prompt_bias_blocks(fvec):
    n = LANES
    h = fvec.shape[0]
    neg = jnp.full((h, n - 1), NEG_INF, F32)
    va = jnp.concatenate([fvec[:, 0:1], neg, jnp.zeros((h, 1), F32), fvec[:, 1:n][:, ::-1]], axis=1)
    vb = jnp.concatenate([fvec[:, 1:n + 1][:, ::-1], jnp.zeros((h, n), F32)], axis=1)
    return jnp.stack([_toeplitz(va, n), _toeplitz(vb, n)], axis=1)


def _decode_bias(fvec, page, dec_seq, n_heads):
    h = fvec.shape[0]
    rows = []
    for i in range(dec_seq):
        last = fvec[:, i + 1:i + 1 + page][:, ::-1]
        new = jnp.concatenate([fvec[:, 0:i + 1][:, ::-1], jnp.full((h, page - i - 1), NEG_INF, F32)], axis=1)
        rows.append(jnp.concatenate([last, new], axis=1))
    per_head = jnp.stack(rows, axis=0)
    same = np.eye(n_heads, dtype=bool)[None, :, None, :]
    near = jnp.where(jnp.asarray(same), per_head[:, :, :, None], NEG_INF)
    near = near.reshape(dec_seq * n_heads, -1)
    mask = np.where(np.broadcast_to(same, (dec_seq, n_heads, page, n_heads)), 0.0, NEG_INF)
    mask = mask.reshape(dec_seq * n_heads, -1).astype(np.float32)
    return jnp.asarray(np.concatenate([mask, mask], axis=0)), jnp.concatenate([near, near], axis=0)


def kernel(x_prompt, x_sample, cache_k, cache_v, state_ssm_re, state_ssm_im, page_table,
           norm_g, w_in, q_norm_g, k_norm_g, lambda_q1, lambda_k1, lambda_q2, lambda_k2,
           subln_g, rel_bias, ssm_a_re, ssm_a_im, ssm_log_dt, ssm_b_re, ssm_b_im,
           ssm_c_re, ssm_c_im, ssm_d, w_glu, b_glu, w_out):
    batch, seq, d_model = x_prompt.shape
    dec_batch, dec_seq, _ = x_sample.shape
    depth, n_pool, page, n_heads, _ = cache_k.shape
    n_pages = page_table.shape[1]
    d_attn = n_heads * V_DIM
    n_groups, n_state = ssm_a_re.shape[1:]
    new_rows = dec_seq * n_heads

    buckets = _bucket_table(2 * LANES)
    far_from = int(np.max(np.nonzero(buckets < N_BUCKETS - 1)[0])) + 1
    assert far_from <= LANES and _bucket_table(seq + page * n_pages)[far_from:].min() == N_BUCKETS - 1
    assert page == LANES and dec_seq < LANES and ATTN_TQ == ATTN_TK

    rel_bias = rel_bias.astype(F32)
    fvec = (rel_bias[buckets].T - rel_bias[N_BUCKETS - 1][:, None]) * LOG2E
    fvec = jnp.where(jnp.asarray(np.arange(2 * LANES) < far_from)[None], fvec, 0.0)
    dtiles = _prompt_bias_blocks(fvec)
    head_mask, near_bias = _decode_bias(fvec, page, dec_seq, n_heads)

    group_avg = jnp.asarray(np.kron(np.eye(2 * LANES // QK_DIM), np.full((QK_DIM, QK_DIM), 1.0 / QK_DIM)), BF16)
    n_rep = d_attn // QK_DIM
    cache_k_rows = cache_k.reshape(depth * n_pool, page * n_heads, V_DIM)
    cache_v_rows = cache_v.reshape(depth * n_pool, page * n_heads, V_DIM)

    hp = x_prompt.reshape(batch * seq, d_model)
    hs = x_sample.reshape(dec_batch * dec_seq, d_model)
    kp_l, vp_l, ks_l, vs_l = [], [], [], []
    srp_l, sip_l, srs_l, sis_l = [], [], [], []
    for l in range(depth):
        lam_init = _lambda_init(l)
        lam = (jnp.exp(jnp.sum(lambda_q1[l].astype(F32) * lambda_k1[l].astype(F32)))
               - jnp.exp(jnp.sum(lambda_q2[l].astype(F32) * lambda_k2[l].astype(F32))) + lam_init)
        lam = lam.reshape(1).astype(F32)
        ng = norm_g[l].astype(F32).reshape(1, d_model)
        w_bf = w_in[l].astype(BF16)
        gq = jnp.tile(q_norm_g[l].astype(F32), n_rep).reshape(1, d_attn) * (QK_DIM ** -0.5 * LOG2E)
        gk = jnp.tile(k_norm_g[l].astype(F32), n_rep).reshape(1, d_attn)
        sg = (subln_g[l].astype(F32) * (1.0 - lam_init)).reshape(1, V_DIM)
        wo_bf = w_out[l].astype(BF16)
        sp = _ssm_params(ssm_a_re[l], ssm_a_im[l], ssm_log_dt[l], ssm_b_re[l], ssm_b_im[l],
                         ssm_c_re[l], ssm_c_im[l], ssm_d[l], w_glu[l], b_glu[l])

        q1, q2, k4, kb, v4, va, ga, u4, gs = _inproj(hp, ng, w_bf, gq, gk, group_avg, head_rows=False)
        sq1, sq2, sk4, sv4, sga, su4, sgs = _inproj(hs, ng, w_bf, gq, gk, group_avg, head_rows=True)
        r3 = lambda a: a.reshape(batch, seq, a.shape[-1])
        o_a, o_dec = _attention(lam, sg, r3(q1), r3(q2), r3(kb), r3(va), dtiles, r3(ga),
                                page_table + l * n_pool, sq1, sq2, sk4, sv4, sga, head_mask, near_bias,
                                cache_k_rows, cache_v_rows, new_rows)
        hp3, hfin = _ssm_prompt_outproj(r3(hp), o_a, u4.reshape(batch, -1, LANES), r3(gs), sp, wo_bf)
        hp = hp3.reshape(batch * seq, d_model)
        kp_l.append(k4.reshape(batch, seq, n_heads, V_DIM).astype(cache_k.dtype))
        vp_l.append(v4.reshape(batch, seq, n_heads, V_DIM).astype(cache_v.dtype))
        hr_p, hi_p = _lanes_to_state(hfin.reshape(batch, -1), n_groups, n_state)
        srp_l.append(hr_p.astype(state_ssm_re.dtype))
        sip_l.append(hi_p.astype(state_ssm_im.dtype))

        hs, hr_s, hi_s = _ssm_sample_outproj(
            hs, o_dec.reshape(dec_batch * dec_seq, d_attn), su4, sgs,
            state_ssm_re[l].astype(F32).reshape(dec_batch, -1), state_ssm_im[l].astype(F32).reshape(dec_batch, -1),
            sp, wo_bf)
        ks_l.append(sk4.reshape(dec_batch, dec_seq, n_heads, V_DIM).astype(cache_k.dtype))
        vs_l.append(sv4.reshape(dec_batch, dec_seq, n_heads, V_DIM).astype(cache_v.dtype))
        srs_l.append(hr_s.reshape(dec_batch, n_groups, n_state).astype(state_ssm_re.dtype))
        sis_l.append(hi_s.reshape(dec_batch, n_groups, n_state).astype(state_ssm_im.dtype))

    y_prompt = hp.reshape(batch, seq, d_model).astype(x_prompt.dtype)
    y_sample = hs.reshape(dec_batch, dec_seq, d_model).astype(x_sample.dtype)
    return (y_prompt, y_sample, jnp.stack(kp_l), jnp.stack(vp_l), jnp.stack(ks_l), jnp.stack(vs_l),
            jnp.stack(srp_l), jnp.stack(sip_l), jnp.stack(srs_l), jnp.stack(sis_l))
```
